```python
import jax, jax.numpy as jnp
from jax import lax
import numpy as np

D_MODEL = 1024
BATCH = 8
SEQ = 4096
DEPTH = 1
DEC_BATCH = 1
DEC_SEQ = 16384
PAST_LEN = 128

HEAD_DIM = 64
ROT_DIM = HEAD_DIM // 4
ROPE_THETA = 500000.0
DIL_CONFIGS = ((128, 1), (512, 4), (2048, 16))
N_DIL_GROUPS = len(DIL_CONFIGS)
A_HEADS = 8
A_BLOCK = 64
A_WIDTH = A_HEADS * HEAD_DIM
B_Q_HEADS = 8
B_KV_HEADS = 2
B_REP = B_Q_HEADS // B_KV_HEADS
B_WINDOW = 128
B_BLOCK = 128
B_Q_WIDTH = B_Q_HEADS * HEAD_DIM
B_KV_WIDTH = B_KV_HEADS * HEAD_DIM
A_QKV_WIDTH = 3 * N_DIL_GROUPS * A_WIDTH
IN_WIDTH = A_QKV_WIDTH + B_Q_WIDTH + 2 * B_KV_WIDTH
N_GROUPS = 4
EXPERTS_PER_GROUP = 8
N_EXPERTS = N_GROUPS * EXPERTS_PER_GROUP
TOP_K_IN_GROUP = 2
D_FF_EXPERT = 256
RMS_EPS = 1e-6

kernel_name = "hybrid_dilated_window_hmoe_encoder"


def rms_norm(x, g):
    xf = x.astype(jnp.float32)
    y = xf * lax.rsqrt(jnp.mean(xf * xf, axis=-1, keepdims=True) + RMS_EPS)
    return (y * g.astype(jnp.float32)).astype(x.dtype)


def rope_partial(x, pos):
    half = ROT_DIM // 2
    inv = jnp.power(jnp.float32(ROPE_THETA), -jnp.arange(half, dtype=jnp.float32) / half)
    ang = pos.astype(jnp.float32)[:, None] * inv[None, :]
    cos = jnp.cos(ang)[None, :, None, :]
    sin = jnp.sin(ang)[None, :, None, :]
    xr = x[..., :ROT_DIM].astype(jnp.float32)
    x1, x2 = xr[..., :half], xr[..., half:]
    rot = jnp.concatenate([x1 * cos - x2 * sin, x2 * cos + x1 * sin], axis=-1).astype(x.dtype)
    return jnp.concatenate([rot, x[..., ROT_DIM:]], axis=-1)


def banded_attention(q, k, v, half_window, block, sink=None):
    n, l, g, r, hd = q.shape
    nb = -(-l // block)
    lp = nb * block
    pad = lp - l
    qb = jnp.pad(q, ((0, 0), (0, pad), (0, 0), (0, 0), (0, 0))).reshape(n, nb, block, g, r, hd)
    kb = jnp.pad(k, ((0, 0), (block, pad + block), (0, 0), (0, 0))).reshape(n, nb + 2, block, g, hd)
    vb = jnp.pad(v, ((0, 0), (block, pad + block), (0, 0), (0, 0))).reshape(n, nb + 2, block, g, hd)
    kn = jnp.concatenate([kb[:, :-2], kb[:, 1:-1], kb[:, 2:]], axis=2)
    vn = jnp.concatenate([vb[:, :-2], vb[:, 1:-1], vb[:, 2:]], axis=2)
    s = jnp.einsum("nbqgrd,nbkgd->nbgrqk", qb, kn, preferred_element_type=jnp.float32)
    qpos = jnp.arange(nb)[:, None] * block + jnp.arange(block)[None, :]
    kpos = jnp.arange(nb)[:, None] * block - block + jnp.arange(3 * block)[None, :]
    rel = kpos[:, None, :] - qpos[:, :, None]
    valid = (jnp.abs(rel) <= half_window) & (kpos[:, None, :] >= 0) & (kpos[:, None, :] < l)
    s = jnp.where(valid[None, :, None, None], s, -jnp.inf)
    m = jnp.max(s, axis=-1)
    if sink is not None:
        sk = sink.astype(jnp.float32)[None, None, :, :, None]
        m = jnp.maximum(m, sk)
    p = jnp.exp(s - m[..., None])
    denom = jnp.sum(p, axis=-1)
    if sink is not None:
        denom = denom + jnp.exp(sk - m)
    o = jnp.einsum("nbgrqk,nbkgd->nbqgrd", p.astype(v.dtype), vn, preferred_element_type=jnp.float32)
    o = o / jnp.transpose(denom, (0, 1, 4, 2, 3))[..., None]
    lse = jnp.transpose(m + jnp.log(denom), (0, 1, 4, 2, 3))
    o = o.reshape(n, lp, g, r, hd)[:, :l].astype(v.dtype)
    lse = lse.reshape(n, lp, g, r)[:, :l]
    return o, lse


def to_strided(t, dil):
    b, s, h, d = t.shape
    return t.reshape(b, s // dil, dil, h, d).transpose(0, 2, 1, 3, 4).reshape(b * dil, s // dil, h, d)


def from_strided(t, dil, b):
    rest = t.shape[2:]
    l = t.shape[1]
    t = t.reshape((b, dil, l) + rest)
    return jnp.swapaxes(t, 1, 2).reshape((b, l * dil) + rest)


def dilated_mixture_attention(q, k, v):
    b = q.shape[0]
    outs, lses = [], []
    for gi, (window, dil) in enumerate(DIL_CONFIGS):
        half = window // (2 * dil)
        qs = to_strided(q[:, :, gi], dil)[:, :, :, None, :]
        ks = to_strided(k[:, :, gi], dil)
        vs = to_strided(v[:, :, gi], dil)
        o, lse = banded_attention(qs, ks, vs, half, A_BLOCK)
        outs.append(from_strided(o[:, :, :, 0], dil, b))
        lses.append(from_strided(lse[:, :, :, 0], dil, b))
    w = jax.nn.softmax(jnp.stack(lses, axis=0), axis=0)
    o = jnp.sum(w[..., None] * jnp.stack(outs, axis=0).astype(jnp.float32), axis=0)
    return o.astype(q.dtype)


def mixer_block(x, norm_g, w_in, sink, w_branch_a, w_branch_b, w_gate, w_out):
    b, s, _ = x.shape
    h = rms_norm(x, norm_g)
    proj = h @ w_in
    a_qkv = proj[..., :A_QKV_WIDTH]
    bq = proj[..., A_QKV_WIDTH:A_QKV_WIDTH + B_Q_WIDTH]
    bk = proj[..., A_QKV_WIDTH + B_Q_WIDTH:A_QKV_WIDTH + B_Q_WIDTH + B_KV_WIDTH]
    bv = proj[..., A_QKV_WIDTH + B_Q_WIDTH + B_KV_WIDTH:]
    pos = jnp.arange(s)
    scale = HEAD_DIM ** -0.5
    a_qkv = a_qkv.reshape(b, s, 3, N_DIL_GROUPS * A_HEADS, HEAD_DIM)
    aq = (rope_partial(a_qkv[:, :, 0], pos) * scale).reshape(b, s, N_DIL_GROUPS, A_HEADS, HEAD_DIM)
    ak = rope_partial(a_qkv[:, :, 1], pos).reshape(b, s, N_DIL_GROUPS, A_HEADS, HEAD_DIM)
    av = a_qkv[:, :, 2].reshape(b, s, N_DIL_GROUPS, A_HEADS, HEAD_DIM)
    oa = dilated_mixture_attention(aq, ak, av).reshape(b, s, A_WIDTH)
    bq = (rope_partial(bq.reshape(b, s, B_Q_HEADS, HEAD_DIM), pos) * scale).reshape(b, s, B_KV_HEADS, B_REP, HEAD_DIM)
    bk = rope_partial(bk.reshape(b, s, B_KV_HEADS, HEAD_DIM), pos)
    bv = bv.reshape(b, s, B_KV_HEADS, HEAD_DIM)
    ob, _ = banded_attention(bq, bk, bv, B_WINDOW, B_BLOCK, sink.reshape(B_KV_HEADS, B_REP))
    ob = ob.reshape(b, s, B_Q_WIDTH)
    ya = oa @ w_branch_a
    yb = ob @ w_branch_b
    gates = jax.nn.sigmoid(h @ w_gate)
    mix = gates[..., :D_MODEL] * ya + gates[..., D_MODEL:] * yb
    return x + mix @ w_out


def hier_moe_block(x, norm_g, w_router_group, b_router_group, w_router_expert, b_router_expert, w1, w3, w2):
    b, s, d = x.shape
    t = b * s
    h = rms_norm(x, norm_g).reshape(t, d)
    g_logits = (h @ w_router_group).astype(jnp.float32) + b_router_group.astype(jnp.float32)
    g_prob = jax.nn.softmax(g_logits, axis=-1)
    g_idx = jnp.argmax(g_logits, axis=-1)
    g_w = jnp.take_along_axis(g_prob, g_idx[:, None], axis=-1)
    e_logits = (h @ w_router_expert).astype(jnp.float32).reshape(t, N_GROUPS, EXPERTS_PER_GROUP)
    e_logits = e_logits + b_router_expert.astype(jnp.float32)[None]
    e_sel = jnp.take_along_axis(e_logits, g_idx[:, None, None], axis=1)[:, 0]
    top_v, top_i = lax.top_k(e_sel, TOP_K_IN_GROUP)
    top_w = jax.nn.softmax(top_v, axis=-1) * g_w
    expert_id = g_idx[:, None] * EXPERTS_PER_GROUP + top_i
    gate = jnp.sum(jax.nn.one_hot(expert_id, N_EXPERTS, dtype=jnp.float32) * top_w[..., None], axis=1)

    def expert_step(acc, params):
        e_w1, e_w3, e_w2, e_gate = params
        hid = jax.nn.silu(h @ e_w1) * (h @ e_w3)
        return acc + e_gate[:, None] * (hid @ e_w2).astype(jnp.float32), None

    acc, _ = lax.scan(expert_step, jnp.zeros((t, d), jnp.float32), (w1, w3, w2, gate.T))
    return x + acc.reshape(b, s, d).astype(x.dtype)


def trunk(x, norm_mix, w_in, attn_sink, w_branch_a, w_branch_b, w_gate, w_out, norm_moe,
          w_router_group, b_router_group, w_router_expert, b_router_expert, w1, w3, w2, norm_final):
    for l in range(DEPTH):
        x = mixer_block(x, norm_mix[l], w_in[l], attn_sink[l], w_branch_a[l], w_branch_b[l], w_gate[l], w_out[l])
        x = hier_moe_block(x, norm_moe[l], w_router_group[l], b_router_group[l], w_router_expert[l],
                           b_router_expert[l], w1[l], w3[l], w2[l])
    return rms_norm(x, norm_final)


def setup_inputs(seed: int = 0) -> dict:
    key = jax.random.key(seed)
    ks = jax.random.split(key, 18)
    f32 = jnp.float32
    nrm = lambda k, shape, scale: jax.random.normal(k, shape, f32) * scale
    return {
        "x_prompt": nrm(ks[0], (BATCH, SEQ, D_MODEL), 1.0),
        "x_sample": nrm(ks[1], (DEC_BATCH, DEC_SEQ, D_MODEL), 1.0),
        "norm_mix": 1.0 + nrm(ks[2], (DEPTH, D_MODEL), 0.05),
        "w_in": nrm(ks[3], (DEPTH, D_MODEL, IN_WIDTH), D_MODEL ** -0.5),
        "attn_sink": nrm(ks[4], (DEPTH, B_Q_HEADS), 0.5),
        "w_branch_a": nrm(ks[5], (DEPTH, A_WIDTH, D_MODEL), A_WIDTH ** -0.5),
        "w_branch_b": nrm(ks[6], (DEPTH, B_Q_WIDTH, D_MODEL), B_Q_WIDTH ** -0.5),
        "w_gate": nrm(ks[7], (DEPTH, D_MODEL, 2 * D_MODEL), D_MODEL ** -0.5),
        "w_out": nrm(ks[8], (DEPTH, D_MODEL, D_MODEL), D_MODEL ** -0.5),
        "norm_moe": 1.0 + nrm(ks[9], (DEPTH, D_MODEL), 0.05),
        "w_router_group": nrm(ks[10], (DEPTH, D_MODEL, N_GROUPS), D_MODEL ** -0.5),
        "b_router_group": nrm(ks[11], (DEPTH, N_GROUPS), 0.01),
        "w_router_expert": nrm(ks[12], (DEPTH, D_MODEL, N_GROUPS * EXPERTS_PER_GROUP), D_MODEL ** -0.5),
        "b_router_expert": nrm(ks[13], (DEPTH, N_GROUPS, EXPERTS_PER_GROUP), 0.01),
        "w1": nrm(ks[14], (DEPTH, N_EXPERTS, D_MODEL, D_FF_EXPERT), D_MODEL ** -0.5),
        "w3": nrm(ks[15], (DEPTH, N_EXPERTS, D_MODEL, D_FF_EXPERT), D_MODEL ** -0.5),
        "w2": nrm(ks[16], (DEPTH, N_EXPERTS, D_FF_EXPERT, D_MODEL), D_FF_EXPERT ** -0.5),
        "norm_final": 1.0 + nrm(ks[17], (D_MODEL,), 0.05),
    }


def reference(x_prompt, x_sample, norm_mix, w_in, attn_sink, w_branch_a, w_branch_b, w_gate, w_out, norm_moe,
              w_router_group, b_router_group, w_router_expert, b_router_expert, w1, w3, w2, norm_final):
    y_prompt = trunk(x_prompt, norm_mix, w_in, attn_sink, w_branch_a, w_branch_b, w_gate, w_out, norm_moe,
                     w_router_group, b_router_group, w_router_expert, b_router_expert, w1, w3, w2, norm_final)
    y_sample = trunk(x_sample, norm_mix, w_in, attn_sink, w_branch_a, w_branch_b, w_gate, w_out, norm_moe,
                     w_router_group, b_router_group, w_router_expert, b_router_expert, w1, w3, w2, norm_final)
    return (y_prompt, y_sample)
```

```python
import functools

import jax
import jax.numpy as jnp
from jax import lax
from jax.experimental import pallas as pl
from jax.experimental.pallas import tpu as pltpu

D_MODEL = 1024
HEAD_DIM = 64
ROT_DIM = HEAD_DIM // 4
ROT_HALF = ROT_DIM // 2
ROPE_THETA = 500000.0
DILATIONS = (1, 4, 16)
A_HALF = 64
A_BLOCK = 64
A_HEADS = 8
A_WIDTH = A_HEADS * HEAD_DIM
B_Q_HEADS = 8
B_KV_HEADS = 2
B_REP = B_Q_HEADS // B_KV_HEADS
B_WINDOW = 128
B_BLOCK = 128
B_Q_WIDTH = B_Q_HEADS * HEAD_DIM
B_KV_WIDTH = B_KV_HEADS * HEAD_DIM
N_GROUPS = 4
EXPERTS_PER_GROUP = 8
N_EXPERTS = N_GROUPS * EXPERTS_PER_GROUP
D_FF = 256
RMS_EPS = 1e-6
Q_SCALE = HEAD_DIM ** -0.5

LANES = 128
NEG_BIG = -1e30
VMEM_LIMIT = 56 * 1024 * 1024

IN_TILE = 256
ATT_TILE = 1024
POST_TILE = 512
MOE_TILE = 1024

F32 = jnp.float32
BF16 = jnp.bfloat16


def _rms(x, g):
    ms = jnp.mean(x * x, axis=-1, keepdims=True)
    return x * lax.rsqrt(ms + RMS_EPS) * g


def _rope(a, cos, sin, low_half):
    outs = []
    for k in range(a.shape[1] // LANES):
        ak = a[:, k * LANES:(k + 1) * LANES]
        up = pltpu.roll(ak, LANES - ROT_HALF, axis=1)
        dn = pltpu.roll(ak, ROT_HALF, axis=1)
        outs.append(ak * cos + jnp.where(low_half, up, dn) * sin)
    return outs[0] if len(outs) == 1 else jnp.concatenate(outs, axis=1)


def _in_proj_kernel(x_ref, g_ref, cos_ref, sin_ref, wn_ref, w1_ref, w2_ref,
                    q0, k0, v0, q1, k1, v1, q2, k2, v2, bq, bk, bv,
                    xs, h1_s, h2_s, c1_s, s1_s, c2_s, s2_s):
    tm = x_ref.shape[1]
    n_chunks = D_MODEL // LANES
    for c in range(n_chunks):
        xs[c] = x_ref[0, :, c * LANES:(c + 1) * LANES]
    g = g_ref[...]
    low_half = (lax.broadcasted_iota(jnp.int32, (tm, LANES), 1) % HEAD_DIM) < ROT_HALF

    def mm(h, w_ref, c0, n):
        return jnp.dot(h, w_ref[:, c0:c0 + n], preferred_element_type=F32)

    h0 = _rms(x_ref[0], g).astype(BF16)
    cos0, sin0 = cos_ref[...], sin_ref[...]
    q0[0, 0] = (_rope(mm(h0, wn_ref, 0, A_WIDTH), cos0, sin0, low_half) * Q_SCALE).astype(BF16)
    k0[0, 0] = _rope(mm(h0, wn_ref, A_WIDTH, A_WIDTH), cos0, sin0, low_half).astype(BF16)
    v0[0, 0] = mm(h0, wn_ref, 2 * A_WIDTH, A_WIDTH).astype(BF16)
    c = 3 * A_WIDTH
    bq[0] = (_rope(mm(h0, wn_ref, c, B_Q_WIDTH), cos0, sin0, low_half) * Q_SCALE).astype(BF16)
    bk[0] = _rope(mm(h0, wn_ref, c + B_Q_WIDTH, B_KV_WIDTH), cos0, sin0, low_half).astype(BF16)
    bv[0] = mm(h0, wn_ref, c + B_Q_WIDTH + B_KV_WIDTH, B_KV_WIDTH).astype(BF16)

    for d, hs, cs, ss, w_ref, qo, ko, vo in ((DILATIONS[1], h1_s, c1_s, s1_s, w1_ref, q1, k1, v1),
                                             (DILATIONS[2], h2_s, c2_s, s2_s, w2_ref, q2, k2, v2)):
        n = tm // d
        for r in range(d):
            rows = pl.ds(r, n, stride=d)
            xr = jnp.concatenate([xs[c, rows, :] for c in range(n_chunks)], axis=1)
            hs[r * n:(r + 1) * n, :] = _rms(xr, g).astype(BF16)
            cs[r * n:(r + 1) * n, :] = cos_ref[rows, :]
            ss[r * n:(r + 1) * n, :] = sin_ref[rows, :]
        h = hs[...]
        cos, sin = cs[...], ss[...]
        qv = (_rope(mm(h, w_ref, 0, A_WIDTH), cos, sin, low_half) * Q_SCALE).astype(BF16)
        kv = _rope(mm(h, w_ref, A_WIDTH, A_WIDTH), cos, sin, low_half).astype(BF16)
        vv = mm(h, w_ref, 2 * A_WIDTH, A_WIDTH).astype(BF16)
        for r in range(d):
            qo[0, r] = qv[r * n:(r + 1) * n]
            ko[0, r] = kv[r * n:(r + 1) * n]
            vo[0, r] = vv[r * n:(r + 1) * n]


def _rope_tables(s):
    inv = jnp.power(jnp.float32(ROPE_THETA), -jnp.arange(ROT_HALF, dtype=F32) / ROT_HALF)
    ang = jnp.arange(s).astype(F32)[:, None] * inv[None, :]
    cos, sin = jnp.cos(ang), jnp.sin(ang)
    pad_one = jnp.ones((s, HEAD_DIM - ROT_DIM), F32)
    pad_zero = jnp.zeros((s, HEAD_DIM - ROT_DIM), F32)
    cos_h = jnp.concatenate([cos, cos, pad_one], axis=1)
    sin_h = jnp.concatenate([-sin, sin, pad_zero], axis=1)
    return jnp.concatenate([cos_h, cos_h], axis=1), jnp.concatenate([sin_h, sin_h], axis=1)


def _in_proj(x, norm_g, w_nat, w_g1, w_g2):
    b, s, _ = x.shape
    tm = IN_TILE
    cos_t, sin_t = _rope_tables(s)
    const = lambda t, bb: (0, 0)
    one_buf = pl.Buffered(1)
    in_specs = [
        pl.BlockSpec((1, tm, D_MODEL), lambda t, bb: (bb, t, 0)),
        pl.BlockSpec((1, D_MODEL), const),
        pl.BlockSpec((tm, LANES), lambda t, bb: (t, 0)),
        pl.BlockSpec((tm, LANES), lambda t, bb: (t, 0)),
        pl.BlockSpec(w_nat.shape, const, pipeline_mode=one_buf),
        pl.BlockSpec(w_g1.shape, const, pipeline_mode=one_buf),
        pl.BlockSpec(w_g2.shape, const, pipeline_mode=one_buf),
    ]
    out_shape, out_specs = [], []
    for d in DILATIONS:
        for _ in range(3):
            out_shape.append(jax.ShapeDtypeStruct((b, d, s // d, A_WIDTH), BF16))
            out_specs.append(pl.BlockSpec((1, d, tm // d, A_WIDTH), lambda t, bb: (bb, 0, t, 0)))
    for w in (B_Q_WIDTH, B_KV_WIDTH, B_KV_WIDTH):
        out_shape.append(jax.ShapeDtypeStruct((b, s, w), BF16))
        out_specs.append(pl.BlockSpec((1, tm, w), lambda t, bb: (bb, t, 0)))
    scratch = [pltpu.VMEM((D_MODEL // LANES, tm, LANES), F32),
               pltpu.VMEM((tm, D_MODEL), BF16), pltpu.VMEM((tm, D_MODEL), BF16)] + \
              [pltpu.VMEM((tm, LANES), F32) for _ in range(4)]
    return pl.pallas_call(
        _in_proj_kernel,
        grid=(s // tm, b),
        in_specs=in_specs,
        out_specs=out_specs,
        out_shape=out_shape,
        scratch_shapes=scratch,
        compiler_params=pltpu.CompilerParams(
            dimension_semantics=("arbitrary", "arbitrary"), vmem_limit_bytes=VMEM_LIMIT),
        name="in_proj",
    )(x, norm_g.reshape(1, D_MODEL), cos_t, sin_t, w_nat, w_g1, w_g2)


def _mixer_a_kernel(*refs, seq):
    ins, (o_ref, kext, vext, acc_o, acc_m, acc_l) = refs[:21], refs[21:]
    t = pl.program_id(1)
    tq = ATT_TILE
    blk = A_BLOCK
    win = 3 * blk
    acc_m[...] = jnp.full(acc_m.shape, NEG_BIG, F32)
    acc_l[...] = jnp.zeros(acc_l.shape, F32)
    acc_o[...] = jnp.zeros(acc_o.shape, F32)
    col = lax.broadcasted_iota(jnp.int32, (blk, win), 1)
    row = lax.broadcasted_iota(jnp.int32, (blk, win), 0)
    diff = col - row

    for gi, d in enumerate(DILATIONS):
        q, kc, kp, kn, vc, vp, vn = ins[7 * gi:7 * gi + 7]
        n = tq // d
        nblk = n // blk
        ld = seq // d

        def residue_body(r, carry, q=q, kc=kc, kp=kp, kn=kn, vc=vc, vp=vp, vn=vn, d=d, n=n, nblk=nblk, ld=ld):
            kext[0:blk, :] = kp[0, r]
            kext[blk:blk + n, :] = kc[0, r]
            kext[blk + n:2 * blk + n, :] = kn[0, r]
            vext[0:blk, :] = vp[0, r]
            vext[blk:blk + n, :] = vc[0, r]
            vext[blk + n:2 * blk + n, :] = vn[0, r]

            def block_body(j, carry2):
                j0 = pl.multiple_of(j * blk, blk)
                qb = q[0, r, pl.ds(j0, blk), :]
                kw = kext[pl.ds(j0, win), :]
                vw = vext[pl.ds(j0, win), :]
                qbase = t * n + j * blk
                valid = ((diff >= 0) & (diff <= 2 * A_HALF)
                         & (col >= blk - qbase) & (col < ld - qbase + blk))
                if d == 1:
                    rows = pl.ds(j0, blk)
                else:
                    rows = pl.ds(j * (blk * d) + r, blk, stride=d)
                for hp in range(A_HEADS // 2):
                    o_parts, m_parts, l_parts = [], [], []
                    for h in (2 * hp, 2 * hp + 1):
                        hs = slice(h * HEAD_DIM, (h + 1) * HEAD_DIM)
                        sc = lax.dot_general(qb[:, hs], kw[:, hs], (((1,), (1,)), ((), ())),
                                             preferred_element_type=F32)
                        sc = jnp.where(valid, sc, NEG_BIG)
                        m = jnp.max(sc, axis=-1, keepdims=True)
                        p = jnp.exp(sc - m)
                        l = jnp.sum(p, axis=-1, keepdims=True)
                        o_parts.append(jnp.dot(p.astype(BF16), vw[:, hs], preferred_element_type=F32))
                        m_parts.append(jnp.broadcast_to(m, (blk, HEAD_DIM)))
                        l_parts.append(jnp.broadcast_to(l, (blk, HEAD_DIM)))
                    o_b = jnp.concatenate(o_parts, axis=1)
                    m_b = jnp.concatenate(m_parts, axis=1)
                    l_b = jnp.concatenate(l_parts, axis=1)
                    m_old = acc_m[hp, rows, :]
                    m_new = jnp.maximum(m_old, m_b)
                    a_old = jnp.exp(m_old - m_new)
                    a_new = jnp.exp(m_b - m_new)
                    acc_o[hp, rows, :] = acc_o[hp, rows, :] * a_old + o_b * a_new
                    acc_l[hp, rows, :] = acc_l[hp, rows, :] * a_old + l_b * a_new
                    acc_m[hp, rows, :] = m_new
                return carry2

            lax.fori_loop(0, nblk, block_body, 0)
            return carry

        lax.fori_loop(0, d, residue_body, 0)

    o_ref[0] = jnp.concatenate([acc_o[hp] / acc_l[hp] for hp in range(A_HEADS // 2)], axis=1).astype(BF16)


def _mixer_a(qkv, seq):
    b = qkv[0].shape[0]
    tq = ATT_TILE
    blk = A_BLOCK
    args, in_specs = [], []
    for gi, d in enumerate(DILATIONS):
        q, k, v = qkv[3 * gi:3 * gi + 3]
        n = tq // d
        nblk = n // blk
        last = seq // d // blk - 1
        cur = pl.BlockSpec((1, d, n, A_WIDTH), lambda bb, t: (bb, 0, t, 0))
        prev = pl.BlockSpec((1, d, blk, A_WIDTH),
                            lambda bb, t, nblk=nblk: (bb, 0, jnp.maximum(t * nblk - 1, 0), 0))
        nxt = pl.BlockSpec((1, d, blk, A_WIDTH),
                           lambda bb, t, nblk=nblk, last=last: (bb, 0, jnp.minimum((t + 1) * nblk, last), 0))
        args += [q, k, k, k, v, v, v]
        in_specs += [cur, cur, prev, nxt, cur, prev, nxt]
    return pl.pallas_call(
        functools.partial(_mixer_a_kernel, seq=seq),
        grid=(b, seq // tq),
        in_specs=in_specs,
        out_specs=pl.BlockSpec((1, tq, A_WIDTH), lambda bb, t: (bb, t, 0)),
        out_shape=jax.ShapeDtypeStruct((b, seq, A_WIDTH), BF16),
        scratch_shapes=[pltpu.VMEM((tq + 2 * blk, A_WIDTH), BF16), pltpu.VMEM((tq + 2 * blk, A_WIDTH), BF16),
                        pltpu.VMEM((A_WIDTH // LANES, tq, LANES), F32),
                        pltpu.VMEM((A_WIDTH // LANES, tq, LANES), F32),
                        pltpu.VMEM((A_WIDTH // LANES, tq, LANES), F32)],
        compiler_params=pltpu.CompilerParams(
            dimension_semantics=("arbitrary", "arbitrary"), vmem_limit_bytes=VMEM_LIMIT),
        name="mixer_a",
    )(*args)


def _mixer_b_kernel(sink_ref, q, kc, kp, kn, vc, vp, vn, o_ref, kext, vext, *, seq):
    t = pl.program_id(1)
    tq = ATT_TILE
    blk = B_BLOCK
    win = 3 * blk
    kext[0:blk, :] = kp[0]
    kext[blk:blk + tq, :] = kc[0]
    kext[blk + tq:2 * blk + tq, :] = kn[0]
    vext[0:blk, :] = vp[0]
    vext[blk:blk + tq, :] = vc[0]
    vext[blk + tq:2 * blk + tq, :] = vn[0]
    col = lax.broadcasted_iota(jnp.int32, (blk, win), 1)
    row = lax.broadcasted_iota(jnp.int32, (blk, win), 0)
    diff = col - row

    def block_body(j, carry):
        j0 = pl.multiple_of(j * blk, blk)
        qb = q[0, pl.ds(j0, blk), :]
        kw = kext[pl.ds(j0, win), :]
        vw = vext[pl.ds(j0, win), :]
        qbase = t * tq + j * blk
        valid = ((diff >= 0) & (diff <= 2 * B_WINDOW)
                 & (col >= blk - qbase) & (col < seq - qbase + blk))
        parts = []
        for h in range(B_Q_HEADS):
            g = h // B_REP
            hs = slice(h * HEAD_DIM, (h + 1) * HEAD_DIM)
            gs = slice(g * HEAD_DIM, (g + 1) * HEAD_DIM)
            sc = lax.dot_general(qb[:, hs], kw[:, gs], (((1,), (1,)), ((), ())),
                                 preferred_element_type=F32)
            sc = jnp.where(valid, sc, NEG_BIG)
            sk = sink_ref[h]
            m = jnp.maximum(jnp.max(sc, axis=-1, keepdims=True), sk)
            p = jnp.exp(sc - m)
            denom = jnp.sum(p, axis=-1, keepdims=True) + jnp.exp(sk - m)
            o = jnp.dot(p.astype(BF16), vw[:, gs], preferred_element_type=F32)
            parts.append(o / denom)
        o_ref[0, pl.ds(j0, blk), :] = jnp.concatenate(parts, axis=1).astype(BF16)
        return carry

    lax.fori_loop(0, tq // blk, block_body, 0)


def _mixer_b(q, k, v, sink, seq):
    b = q.shape[0]
    tq = ATT_TILE
    blk = B_BLOCK
    nblk = tq // blk
    last = seq // blk - 1
    cur = pl.BlockSpec((1, tq, B_KV_WIDTH), lambda bb, t: (bb, t, 0))
    prev = pl.BlockSpec((1, blk, B_KV_WIDTH), lambda bb, t: (bb, jnp.maximum(t * nblk - 1, 0), 0))
    nxt = pl.BlockSpec((1, blk, B_KV_WIDTH), lambda bb, t: (bb, jnp.minimum((t + 1) * nblk, last), 0))
    return pl.pallas_call(
        functools.partial(_mixer_b_kernel, seq=seq),
        grid=(b, seq // tq),
        in_specs=[pl.BlockSpec(memory_space=pltpu.SMEM),
                  pl.BlockSpec((1, tq, B_Q_WIDTH), lambda bb, t: (bb, t, 0)),
                  cur, prev, nxt, cur, prev, nxt],
        out_specs=pl.BlockSpec((1, tq, B_Q_WIDTH), lambda bb, t: (bb, t, 0)),
        out_shape=jax.ShapeDtypeStruct((b, seq, B_Q_WIDTH), BF16),
        scratch_shapes=[pltpu.VMEM((tq + 2 * blk, B_KV_WIDTH), BF16),
                        pltpu.VMEM((tq + 2 * blk, B_KV_WIDTH), BF16)],
        compiler_params=pltpu.CompilerParams(
            dimension_semantics=("arbitrary", "arbitrary"), vmem_limit_bytes=VMEM_LIMIT),
        name="mixer_b",
    )(sink, q, k, k, k, v, v, v)


def _post_attn_kernel(x_ref, oa_ref, ob_ref, gmix_ref, wg_ref, wa_ref, wb_ref, wo_ref, gmoe_ref,
                      wr_ref, br_ref, x1_ref, h2_ref, gate_ref):
    x = x_ref[...]
    h = _rms(x, gmix_ref[...]).astype(BF16)
    gates = jax.nn.sigmoid(jnp.dot(h, wg_ref[...], preferred_element_type=F32))
    ya = jnp.dot(oa_ref[...], wa_ref[...], preferred_element_type=F32)
    yb = jnp.dot(ob_ref[...], wb_ref[...], preferred_element_type=F32)
    mix = gates[:, :D_MODEL] * ya + gates[:, D_MODEL:] * yb
    x1 = x + jnp.dot(mix.astype(BF16), wo_ref[...], preferred_element_type=F32)
    x1_ref[...] = x1
    h2 = _rms(x1, gmoe_ref[...])
    h2_ref[...] = h2.astype(BF16)

    logits = jnp.dot(h2, wr_ref[...], preferred_element_type=F32,
                     precision=lax.Precision.HIGHEST) + br_ref[...]
    tm = logits.shape[0]
    lane = lax.broadcasted_iota(jnp.int32, (tm, LANES), 1)
    big = jnp.int32(LANES)
    is_group = (lane >= N_EXPERTS) & (lane < N_EXPERTS + N_GROUPS)
    gl = jnp.where(is_group, logits, NEG_BIG)
    gmax = jnp.max(gl, axis=-1, keepdims=True)
    g_idx = jnp.min(jnp.where(gl == gmax, lane, big), axis=-1, keepdims=True) - N_EXPERTS
    g_w = 1.0 / jnp.sum(jnp.exp(gl - gmax), axis=-1, keepdims=True)
    in_group = (lane < N_EXPERTS) & (lax.shift_right_logical(lane, 3) == g_idx)
    sel = jnp.where(in_group, logits, NEG_BIG)
    v1 = jnp.max(sel, axis=-1, keepdims=True)
    i1 = jnp.min(jnp.where(sel == v1, lane, big), axis=-1, keepdims=True)
    sel2 = jnp.where(lane == i1, NEG_BIG, sel)
    v2 = jnp.max(sel2, axis=-1, keepdims=True)
    i2 = jnp.min(jnp.where(sel2 == v2, lane, big), axis=-1, keepdims=True)
    e2 = jnp.exp(v2 - v1)
    w1 = g_w / (1.0 + e2)
    w2 = g_w * e2 / (1.0 + e2)
    gate_ref[...] = jnp.where(lane == i1, w1, 0.0) + jnp.where(lane == i2, w2, 0.0)


def _post_attn(x2d, oa, ob, norm_mix, w_gate, w_a, w_b, w_out, norm_moe, w_router, b_router):
    t = x2d.shape[0]
    tm = POST_TILE
    const = lambda i: (0, 0)
    row = lambda w: pl.BlockSpec((tm, w), lambda i: (i, 0))
    full = lambda a: pl.BlockSpec(a.shape, const, pipeline_mode=pl.Buffered(1))
    return pl.pallas_call(
        _post_attn_kernel,
        grid=(t // tm,),
        in_specs=[row(D_MODEL), row(A_WIDTH), row(B_Q_WIDTH), full(norm_mix), full(w_gate), full(w_a),
                  full(w_b), full(w_out), full(norm_moe), full(w_router), full(b_router)],
        out_specs=[row(D_MODEL), row(D_MODEL), row(LANES)],
        out_shape=[jax.ShapeDtypeStruct((t, D_MODEL), F32), jax.ShapeDtypeStruct((t, D_MODEL), BF16),
                   jax.ShapeDtypeStruct((t, LANES), F32)],
        compiler_params=pltpu.CompilerParams(
            dimension_semantics=("arbitrary",), vmem_limit_bytes=VMEM_LIMIT),
        name="post_attn",
    )(x2d, oa, ob, norm_mix, w_gate, w_a, w_b, w_out, norm_moe, w_router, b_router)


def _moe_final_kernel(x1_ref, h2_ref, gate_ref, w1_ref, w3_ref, w2_ref, gf_ref, o_ref, acc):
    e = pl.program_id(1)

    @pl.when(e == 0)
    def _():
        acc[...] = jnp.zeros(acc.shape, F32)

    h = h2_ref[...]
    a = jnp.dot(h, w1_ref[0], preferred_element_type=F32)
    b = jnp.dot(h, w3_ref[0], preferred_element_type=F32)
    hid = (a * jax.nn.sigmoid(a)) * b
    y = jnp.dot(hid.astype(BF16), w2_ref[0], preferred_element_type=F32)
    lane = lax.broadcasted_iota(jnp.int32, gate_ref.shape, 1)
    ge = jnp.sum(jnp.where(lane == e, gate_ref[...], 0.0), axis=-1, keepdims=True)
    acc[...] += ge * y

    @pl.when(e == N_EXPERTS - 1)
    def _():
        o_ref[...] = _rms(x1_ref[...] + acc[...], gf_ref[...])


def _moe_final(x1, h2, gate, w1, w3, w2, norm_final):
    t = x1.shape[0]
    tm = MOE_TILE
    row = lambda w: pl.BlockSpec((tm, w), lambda i, e: (i, 0))
    return pl.pallas_call(
        _moe_final_kernel,
        grid=(t // tm, N_EXPERTS),
        in_specs=[row(D_MODEL), row(D_MODEL), row(LANES),
                  pl.BlockSpec((1, D_MODEL, D_FF), lambda i, e: (e, 0, 0)),
                  pl.BlockSpec((1, D_MODEL, D_FF), lambda i, e: (e, 0, 0)),
                  pl.BlockSpec((1, D_FF, D_MODEL), lambda i, e: (e, 0, 0)),
                  pl.BlockSpec((1, D_MODEL), lambda i, e: (0, 0))],
        out_specs=row(D_MODEL),
        out_shape=jax.ShapeDtypeStruct((t, D_MODEL), F32),
        scratch_shapes=[pltpu.VMEM((tm, D_MODEL), F32)],
        compiler_params=pltpu.CompilerParams(
            dimension_semantics=("arbitrary", "arbitrary"), vmem_limit_bytes=VMEM_LIMIT),
        name="moe_final",
    )(x1, h2, gate, w1, w3, w2, norm_final)


def _prep_weights(norm_mix, w_in, attn_sink, w_branch_a, w_branch_b, w_gate, w_out, norm_moe,
                  w_router_group, b_router_group, w_router_expert, b_router_expert, w1, w3, w2, norm_final):
    wi = w_in[0]
    a_w = 3 * A_WIDTH

    def group_cols(gi):
        return jnp.concatenate([wi[:, role * a_w + gi * A_WIDTH: role * a_w + (gi + 1) * A_WIDTH]
                                for role in range(3)], axis=1)

    w_nat = jnp.concatenate([group_cols(0), wi[:, 3 * a_w:]], axis=1).astype(BF16)
    w_g1 = group_cols(1).astype(BF16)
    w_g2 = group_cols(2).astype(BF16)
    pad = LANES - N_EXPERTS - N_GROUPS
    w_router = jnp.concatenate([w_router_expert[0], w_router_group[0],
                                jnp.zeros((D_MODEL, pad), F32)], axis=1)
    b_router = jnp.concatenate([b_router_expert[0].reshape(-1), b_router_group[0],
                                jnp.zeros((pad,), F32)]).reshape(1, LANES)
    return dict(
        norm_mix=norm_mix[0].reshape(1, D_MODEL), w_nat=w_nat, w_g1=w_g1, w_g2=w_g2, sink=attn_sink[0],
        w_a=w_branch_a[0].astype(BF16), w_b=w_branch_b[0].astype(BF16), w_gate=w_gate[0].astype(BF16),
        w_out=w_out[0].astype(BF16), norm_moe=norm_moe[0].reshape(1, D_MODEL),
        w_router=w_router, b_router=b_router,
        w1=w1[0].astype(BF16), w3=w3[0].astype(BF16), w2=w2[0].astype(BF16),
        norm_final=norm_final.reshape(1, D_MODEL))


def _trunk(x, p):
    b, s, _ = x.shape
    outs = _in_proj(x, p["norm_mix"], p["w_nat"], p["w_g1"], p["w_g2"])
    oa = _mixer_a(outs[:9], s)
    ob = _mixer_b(outs[9], outs[10], outs[11], p["sink"], s)
    t = b * s
    x1, h2, gate = _post_attn(x.reshape(t, D_MODEL), oa.reshape(t, A_WIDTH), ob.reshape(t, B_Q_WIDTH),
                              p["norm_mix"], p["w_gate"], p["w_a"], p["w_b"], p["w_out"], p["norm_moe"],
                              p["w_router"], p["b_router"])
    y = _moe_final(x1, h2, gate, p["w1"], p["w3"], p["w2"], p["norm_final"])
    return y.reshape(b, s, D_MODEL)


def kernel(x_prompt, x_sample, norm_mix, w_in, attn_sink, w_branch_a, w_branch_b, w_gate, w_out, norm_moe,
           w_router_group, b_router_group, w_router_expert, b_router_expert, w1, w3, w2, norm_final):
    p = _prep_weights(norm_mix, w_in, attn_sink, w_branch_a, w_branch_b, w_gate, w_out, norm_moe,
                      w_router_group, b_router_group, w_router_expert, b_router_expert, w1, w3, w2, norm_final)
    return (_trunk(x_prompt, p), _trunk(x_sample, p))
```

```python
import functools

import jax
import jax.numpy as jnp
from jax import lax
from jax.experimental import pallas as pl
from jax.experimental.pallas import tpu as pltpu

D_MODEL = 1024
HEAD_DIM = 64
ROT_DIM = HEAD_DIM // 4
ROT_HALF = ROT_DIM // 2
ROPE_THETA = 500000.0
DILATIONS = (1, 4, 16)
A_HALF = 64
A_BLOCK = 64
A_QBLOCK = 128
A_HEADS = 8
A_WIDTH = A_HEADS * HEAD_DIM
B_Q_HEADS = 8
B_KV_HEADS = 2
B_REP = B_Q_HEADS // B_KV_HEADS
B_WINDOW = 128
B_BLOCK = 128
B_Q_WIDTH = B_Q_HEADS * HEAD_DIM
B_KV_WIDTH = B_KV_HEADS * HEAD_DIM
N_GROUPS = 4
EXPERTS_PER_GROUP = 8
N_EXPERTS = N_GROUPS * EXPERTS_PER_GROUP
D_FF = 256
RMS_EPS = 1e-6
Q_SCALE = HEAD_DIM ** -0.5

LANES = 128
NEG_BIG = -1e30
VMEM_LIMIT = 56 * 1024 * 1024

IN_TILE = 256
ATT_TILE = 1024
POST_TILE = 512
MOE_TILE = 1024

F32 = jnp.float32
BF16 = jnp.bfloat16


def _rms(x, g):
    ms = jnp.mean(x * x, axis=-1, keepdims=True)
    return x * lax.rsqrt(ms + RMS_EPS) * g


def _rope(a, cos, sin, low_half):
    outs = []
    for k in range(a.shape[1] // LANES):
        ak = a[:, k * LANES:(k + 1) * LANES]
        up = pltpu.roll(ak, LANES - ROT_HALF, axis=1)
        dn = pltpu.roll(ak, ROT_HALF, axis=1)
        outs.append(ak * cos + jnp.where(low_half, up, dn) * sin)
    return outs[0] if len(outs) == 1 else jnp.concatenate(outs, axis=1)


def _in_proj_kernel(x_ref, g_ref, cos_ref, sin_ref, wn_ref, w1_ref, w2_ref,
                    q0, k0, v0, q1, k1, v1, q2, k2, v2, bq, bk, bv,
                    xs, h1_s, h2_s, c1_s, s1_s, c2_s, s2_s):
    tm = x_ref.shape[1]
    n_chunks = D_MODEL // LANES
    for c in range(n_chunks):
        xs[c] = x_ref[0, :, c * LANES:(c + 1) * LANES]
    g = g_ref[...]
    low_half = (lax.broadcasted_iota(jnp.int32, (tm, LANES), 1) % HEAD_DIM) < ROT_HALF

    def mm(h, w_ref, c0, n):
        return jnp.dot(h, w_ref[:, c0:c0 + n], preferred_element_type=F32)

    h0 = _rms(x_ref[0], g).astype(BF16)
    cos0, sin0 = cos_ref[...], sin_ref[...]
    q0[0, 0] = (_rope(mm(h0, wn_ref, 0, A_WIDTH), cos0, sin0, low_half) * Q_SCALE).astype(BF16)
    k0[0, 0] = _rope(mm(h0, wn_ref, A_WIDTH, A_WIDTH), cos0, sin0, low_half).astype(BF16)
    v0[0, 0] = mm(h0, wn_ref, 2 * A_WIDTH, A_WIDTH).astype(BF16)
    c = 3 * A_WIDTH
    bq[0] = (_rope(mm(h0, wn_ref, c, B_Q_WIDTH), cos0, sin0, low_half) * Q_SCALE).astype(BF16)
    bk[0] = _rope(mm(h0, wn_ref, c + B_Q_WIDTH, B_KV_WIDTH), cos0, sin0, low_half).astype(BF16)
    bv[0] = mm(h0, wn_ref, c + B_Q_WIDTH + B_KV_WIDTH, B_KV_WIDTH).astype(BF16)

    for d, hs, cs, ss, w_ref, qo, ko, vo in ((DILATIONS[1], h1_s, c1_s, s1_s, w1_ref, q1, k1, v1),
                                             (DILATIONS[2], h2_s, c2_s, s2_s, w2_ref, q2, k2, v2)):
        n = tm // d
        for r in range(d):
            rows = pl.ds(r, n, stride=d)
            xr = jnp.concatenate([xs[c, rows, :] for c in range(n_chunks)], axis=1)
            hs[r * n:(r + 1) * n, :] = _rms(xr, g).astype(BF16)
            cs[r * n:(r + 1) * n, :] = cos_ref[rows, :]
            ss[r * n:(r + 1) * n, :] = sin_ref[rows, :]
        h = hs[...]
        cos, sin = cs[...], ss[...]
        qv = (_rope(mm(h, w_ref, 0, A_WIDTH), cos, sin, low_half) * Q_SCALE).astype(BF16)
        kv = _rope(mm(h, w_ref, A_WIDTH, A_WIDTH), cos, sin, low_half).astype(BF16)
        vv = mm(h, w_ref, 2 * A_WIDTH, A_WIDTH).astype(BF16)
        for r in range(d):
            qo[0, r] = qv[r * n:(r + 1) * n]
            ko[0, r] = kv[r * n:(r + 1) * n]
            vo[0, r] = vv[r * n:(r + 1) * n]


def _rope_tables(s):
    inv = jnp.power(jnp.float32(ROPE_THETA), -jnp.arange(ROT_HALF, dtype=F32) / ROT_HALF)
    ang = jnp.arange(s).astype(F32)[:, None] * inv[None, :]
    cos, sin = jnp.cos(ang), jnp.sin(ang)
    pad_one = jnp.ones((s, HEAD_DIM - ROT_DIM), F32)
    pad_zero = jnp.zeros((s, HEAD_DIM - ROT_DIM), F32)
    cos_h = jnp.concatenate([cos, cos, pad_one], axis=1)
    sin_h = jnp.concatenate([-sin, sin, pad_zero], axis=1)
    return jnp.concatenate([cos_h, cos_h], axis=1), jnp.concatenate([sin_h, sin_h], axis=1)


def _in_proj(x, norm_g, w_nat, w_g1, w_g2):
    b, s, _ = x.shape
    tm = IN_TILE
    cos_t, sin_t = _rope_tables(s)
    const = lambda t, bb: (0, 0)
    one_buf = pl.Buffered(1)
    in_specs = [
        pl.BlockSpec((1, tm, D_MODEL), lambda t, bb: (bb, t, 0)),
        pl.BlockSpec((1, D_MODEL), const),
        pl.BlockSpec((tm, LANES), lambda t, bb: (t, 0)),
        pl.BlockSpec((tm, LANES), lambda t, bb: (t, 0)),
        pl.BlockSpec(w_nat.shape, const, pipeline_mode=one_buf),
        pl.BlockSpec(w_g1.shape, const, pipeline_mode=one_buf),
        pl.BlockSpec(w_g2.shape, const, pipeline_mode=one_buf),
    ]
    out_shape, out_specs = [], []
    for d in DILATIONS:
        for _ in range(3):
            out_shape.append(jax.ShapeDtypeStruct((b, d, s // d, A_WIDTH), BF16))
            out_specs.append(pl.BlockSpec((1, d, tm // d, A_WIDTH), lambda t, bb: (bb, 0, t, 0)))
    for w in (B_Q_WIDTH, B_KV_WIDTH, B_KV_WIDTH):
        out_shape.append(jax.ShapeDtypeStruct((b, s, w), BF16))
        out_specs.append(pl.BlockSpec((1, tm, w), lambda t, bb: (bb, t, 0)))
    scratch = [pltpu.VMEM((D_MODEL // LANES, tm, LANES), F32),
               pltpu.VMEM((tm, D_MODEL), BF16), pltpu.VMEM((tm, D_MODEL), BF16)] + \
              [pltpu.VMEM((tm, LANES), F32) for _ in range(4)]
    return pl.pallas_call(
        _in_proj_kernel,
        grid=(s // tm, b),
        in_specs=in_specs,
        out_specs=out_specs,
        out_shape=out_shape,
        scratch_shapes=scratch,
        compiler_params=pltpu.CompilerParams(
            dimension_semantics=("arbitrary", "arbitrary"), vmem_limit_bytes=VMEM_LIMIT),
        name="in_proj",
    )(x, norm_g.reshape(1, D_MODEL), cos_t, sin_t, w_nat, w_g1, w_g2)


def _attend_heads(qb, kw, vw, valid, first_half):
    second_half = jnp.logical_not(first_half)
    scores = []
    for h in range(A_HEADS):
        cs = slice((h // 2) * LANES, (h // 2 + 1) * LANES)
        qp = qb[:, cs]
        qm = jnp.where(first_half if h % 2 == 0 else second_half, qp, jnp.zeros_like(qp))
        scores.append(lax.dot_general(qm, kw[:, cs], (((1,), (1,)), ((), ())), preferred_element_type=F32))
    sc = jnp.where(valid[None], jnp.stack(scores), NEG_BIG)
    m = jnp.max(sc, axis=-1, keepdims=True)
    p = jnp.exp(sc - m)
    l = jnp.sum(p, axis=-1, keepdims=True)
    pb = p.astype(BF16)
    out = []
    for hp in range(A_HEADS // 2):
        vp = vw[:, hp * LANES:(hp + 1) * LANES]
        o0 = jnp.dot(pb[2 * hp], vp, preferred_element_type=F32)
        o1 = jnp.dot(pb[2 * hp + 1], vp, preferred_element_type=F32)
        out.append((jnp.where(first_half, o0, o1), jnp.where(first_half, m[2 * hp], m[2 * hp + 1]),
                    jnp.where(first_half, l[2 * hp], l[2 * hp + 1])))
    return out


def _mixer_a_kernel(*refs, seq):
    ins, (o_ref, kext, vext, acc_o, acc_m, acc_l) = refs[:21], refs[21:]
    t = pl.program_id(1)
    tq = ATT_TILE
    halo = A_HALF

    for gi, d in enumerate(DILATIONS):
        q, kc, kp, kn, vc, vp, vn = ins[7 * gi:7 * gi + 7]
        n = tq // d
        blk = min(n, A_QBLOCK)
        nblk = n // blk
        win = -(-(blk + 2 * halo) // LANES) * LANES
        ld = seq // d
        col = lax.broadcasted_iota(jnp.int32, (blk, win), 1)
        diff = col - lax.broadcasted_iota(jnp.int32, (blk, win), 0)
        band = (diff >= 0) & (diff <= 2 * halo)
        first_half = lax.broadcasted_iota(jnp.int32, (blk, LANES), 1) < HEAD_DIM

        def residue_body(r, carry, q=q, kc=kc, kp=kp, kn=kn, vc=vc, vp=vp, vn=vn, d=d, n=n, nblk=nblk,
                         ld=ld, blk=blk, win=win, col=col, band=band, first_half=first_half, first=(gi == 0)):
            kext[0:halo, :] = kp[0, r]
            kext[halo:halo + n, :] = kc[0, r]
            kext[halo + n:2 * halo + n, :] = kn[0, r]
            vext[0:halo, :] = vp[0, r]
            vext[halo:halo + n, :] = vc[0, r]
            vext[halo + n:2 * halo + n, :] = vn[0, r]

            def block_body(j, carry2):
                j0 = pl.multiple_of(j * blk, blk)
                qb = q[0, r, pl.ds(j0, blk), :]
                kw = kext[pl.ds(j0, win), :]
                vw = vext[pl.ds(j0, win), :]
                qbase = t * n + j * blk
                valid = band & (col >= halo - qbase) & (col < ld - qbase + halo)
                parts = _attend_heads(qb, kw, vw, valid, first_half)
                if d == 1:
                    rows = pl.ds(j0, blk)
                else:
                    rows = pl.ds(j * (blk * d) + r, blk, stride=d)
                for hp, (o_b, m_b, l_b) in enumerate(parts):
                    if first:
                        acc_o[hp, rows, :] = o_b
                        acc_l[hp, rows, :] = l_b
                        acc_m[hp, rows, :] = m_b
                    else:
                        m_old = acc_m[hp, rows, :]
                        m_new = jnp.maximum(m_old, m_b)
                        a_old = jnp.exp(m_old - m_new)
                        a_new = jnp.exp(m_b - m_new)
                        acc_o[hp, rows, :] = acc_o[hp, rows, :] * a_old + o_b * a_new
                        acc_l[hp, rows, :] = acc_l[hp, rows, :] * a_old + l_b * a_new
                        acc_m[hp, rows, :] = m_new
                return carry2

            lax.fori_loop(0, nblk, block_body, 0)
            return carry

        lax.fori_loop(0, d, residue_body, 0)

    o_ref[0] = jnp.concatenate([acc_o[hp] / acc_l[hp] for hp in range(A_HEADS // 2)], axis=1).astype(BF16)


def _mixer_a(qkv, seq):
    b = qkv[0].shape[0]
    tq = ATT_TILE
    blk = A_BLOCK
    args, in_specs = [], []
    for gi, d in enumerate(DILATIONS):
        q, k, v = qkv[3 * gi:3 * gi + 3]
        n = tq // d
        nblk = n // blk
        last = seq // d // blk - 1
        cur = pl.BlockSpec((1, d, n, A_WIDTH), lambda bb, t: (bb, 0, t, 0))
        prev = pl.BlockSpec((1, d, blk, A_WIDTH),
                            lambda bb, t, nblk=nblk: (bb, 0, jnp.maximum(t * nblk - 1, 0), 0))
        nxt = pl.BlockSpec((1, d, blk, A_WIDTH),
                           lambda bb, t, nblk=nblk, last=last: (bb, 0, jnp.minimum((t + 1) * nblk, last), 0))
        args += [q, k, k, k, v, v, v]
        in_specs += [cur, cur, prev, nxt, cur, prev, nxt]
    return pl.pallas_call(
        functools.partial(_mixer_a_kernel, seq=seq),
        grid=(b, seq // tq),
        in_specs=in_specs,
        out_specs=pl.BlockSpec((1, tq, A_WIDTH), lambda bb, t: (bb, t, 0)),
        out_shape=jax.ShapeDtypeStruct((b, seq, A_WIDTH), BF16),
        scratch_shapes=[pltpu.VMEM((tq + 2 * blk, A_WIDTH), BF16), pltpu.VMEM((tq + 2 * blk, A_WIDTH), BF16),
                        pltpu.VMEM((A_WIDTH // LANES, tq, LANES), F32),
                        pltpu.VMEM((A_WIDTH // LANES, tq, LANES), F32),
                        pltpu.VMEM((A_WIDTH // LANES, tq, LANES), F32)],
        compiler_params=pltpu.CompilerParams(
            dimension_semantics=("arbitrary", "arbitrary"), vmem_limit_bytes=VMEM_LIMIT),
        name="mixer_a",
    )(*args)


def _mixer_b_kernel(sink_ref, q, kc, kp, kn, vc, vp, vn, o_ref, kext, vext, *, seq):
    t = pl.program_id(1)
    tq = ATT_TILE
    blk = B_BLOCK
    win = 3 * blk
    kext[0:blk, :] = kp[0]
    kext[blk:blk + tq, :] = kc[0]
    kext[blk + tq:2 * blk + tq, :] = kn[0]
    vext[0:blk, :] = vp[0]
    vext[blk:blk + tq, :] = vc[0]
    vext[blk + tq:2 * blk + tq, :] = vn[0]
    col = lax.broadcasted_iota(jnp.int32, (blk, win), 1)
    row = lax.broadcasted_iota(jnp.int32, (blk, win), 0)
    diff = col - row
    first_half = lax.broadcasted_iota(jnp.int32, (blk, LANES), 1) < HEAD_DIM
    second_half = jnp.logical_not(first_half)

    def block_body(j, carry):
        j0 = pl.multiple_of(j * blk, blk)
        qb = q[0, pl.ds(j0, blk), :]
        kw = kext[pl.ds(j0, win), :]
        vw = vext[pl.ds(j0, win), :]
        qbase = t * tq + j * blk
        valid = ((diff >= 0) & (diff <= 2 * B_WINDOW)
                 & (col >= blk - qbase) & (col < seq - qbase + blk))
        scores, sinks = [], []
        for c in range(B_Q_HEADS // 2):
            qp = qb[:, c * LANES:(c + 1) * LANES]
            for g, keep in enumerate((first_half, second_half)):
                qm = jnp.where(keep, qp, jnp.zeros_like(qp))
                scores.append(lax.dot_general(qm, kw, (((1,), (1,)), ((), ())), preferred_element_type=F32))
                sinks.append(jnp.full((1, LANES), sink_ref[g * B_REP + c], F32))
        sc = jnp.where(valid[None], jnp.stack(scores), NEG_BIG)
        sk = jnp.stack(sinks)
        part = sc[..., 0:LANES]
        for k in range(1, win // LANES):
            part = jnp.maximum(part, sc[..., k * LANES:(k + 1) * LANES])
        m = jnp.max(jnp.maximum(part, sk), axis=-1, keepdims=True)
        p = jnp.exp(sc - m)
        denom = jnp.sum(p, axis=-1, keepdims=True) + jnp.exp(sk - m)
        pb = p.astype(BF16)
        parts = []
        for c in range(B_Q_HEADS // 2):
            o0 = jnp.dot(pb[2 * c], vw, preferred_element_type=F32) / denom[2 * c]
            o1 = jnp.dot(pb[2 * c + 1], vw, preferred_element_type=F32) / denom[2 * c + 1]
            parts.append(jnp.where(first_half, o0, o1))
        o_ref[0, pl.ds(j0, blk), :] = jnp.concatenate(parts, axis=1).astype(BF16)
        return carry

    lax.fori_loop(0, tq // blk, block_body, 0)


def _mixer_b(q, k, v, sink, seq):
    b = q.shape[0]
    tq = ATT_TILE
    blk = B_BLOCK
    nblk = tq // blk
    last = seq // blk - 1
    cur = pl.BlockSpec((1, tq, B_KV_WIDTH), lambda bb, t: (bb, t, 0))
    prev = pl.BlockSpec((1, blk, B_KV_WIDTH), lambda bb, t: (bb, jnp.maximum(t * nblk - 1, 0), 0))
    nxt = pl.BlockSpec((1, blk, B_KV_WIDTH), lambda bb, t: (bb, jnp.minimum((t + 1) * nblk, last), 0))
    return pl.pallas_call(
        functools.partial(_mixer_b_kernel, seq=seq),
        grid=(b, seq // tq),
        in_specs=[pl.BlockSpec(memory_space=pltpu.SMEM),
                  pl.BlockSpec((1, tq, B_Q_WIDTH), lambda bb, t: (bb, t, 0)),
                  cur, prev, nxt, cur, prev, nxt],
        out_specs=pl.BlockSpec((1, tq, B_Q_WIDTH), lambda bb, t: (bb, t, 0)),
        out_shape=jax.ShapeDtypeStruct((b, seq, B_Q_WIDTH), BF16),
        scratch_shapes=[pltpu.VMEM((tq + 2 * blk, B_KV_WIDTH), BF16),
                        pltpu.VMEM((tq + 2 * blk, B_KV_WIDTH), BF16)],
        compiler_params=pltpu.CompilerParams(
            dimension_semantics=("arbitrary", "arbitrary"), vmem_limit_bytes=VMEM_LIMIT),
        name="mixer_b",
    )(sink, q, k, k, k, v, v, v)


def _post_attn_kernel(x_ref, oa_ref, ob_ref, gmix_ref, wg_ref, wa_ref, wb_ref, wo_ref, gmoe_ref,
                      wr_ref, br_ref, x1_ref, h2_ref, gate_ref):
    x = x_ref[...]
    h = _rms(x, gmix_ref[...]).astype(BF16)
    gates = jax.nn.sigmoid(jnp.dot(h, wg_ref[...], preferred_element_type=F32))
    ya = jnp.dot(oa_ref[...], wa_ref[...], preferred_element_type=F32)
    yb = jnp.dot(ob_ref[...], wb_ref[...], preferred_element_type=F32)
    mix = gates[:, :D_MODEL] * ya + gates[:, D_MODEL:] * yb
    x1 = x + jnp.dot(mix.astype(BF16), wo_ref[...], preferred_element_type=F32)
    x1_ref[...] = x1
    h2 = _rms(x1, gmoe_ref[...])
    h2_ref[...] = h2.astype(BF16)

    logits = jnp.dot(h2, wr_ref[...], preferred_element_type=F32,
                     precision=lax.Precision.HIGHEST) + br_ref[...]
    tm = logits.shape[0]
    lane = lax.broadcasted_iota(jnp.int32, (tm, LANES), 1)
    big = jnp.int32(LANES)
    is_group = (lane >= N_EXPERTS) & (lane < N_EXPERTS + N_GROUPS)
    gl = jnp.where(is_group, logits, NEG_BIG)
    gmax = jnp.max(gl, axis=-1, keepdims=True)
    g_idx = jnp.min(jnp.where(gl == gmax, lane, big), axis=-1, keepdims=True) - N_EXPERTS
    g_w = 1.0 / jnp.sum(jnp.exp(gl - gmax), axis=-1, keepdims=True)
    in_group = (lane < N_EXPERTS) & (lax.shift_right_logical(lane, 3) == g_idx)
    sel = jnp.where(in_group, logits, NEG_BIG)
    v1 = jnp.max(sel, axis=-1, keepdims=True)
    i1 = jnp.min(jnp.where(sel == v1, lane, big), axis=-1, keepdims=True)
    sel2 = jnp.where(lane == i1, NEG_BIG, sel)
    v2 = jnp.max(sel2, axis=-1, keepdims=True)
    i2 = jnp.min(jnp.where(sel2 == v2, lane, big), axis=-1, keepdims=True)
    e2 = jnp.exp(v2 - v1)
    w1 = g_w / (1.0 + e2)
    w2 = g_w * e2 / (1.0 + e2)
    gate_ref[...] = jnp.where(lane == i1, w1, 0.0) + jnp.where(lane == i2, w2, 0.0)


def _post_attn(x2d, oa, ob, norm_mix, w_gate, w_a, w_b, w_out, norm_moe, w_router, b_router):
    t = x2d.shape[0]
    tm = POST_TILE
    const = lambda i: (0, 0)
    row = lambda w: pl.BlockSpec((tm, w), lambda i: (i, 0))
    full = lambda a: pl.BlockSpec(a.shape, const, pipeline_mode=pl.Buffered(1))
    return pl.pallas_call(
        _post_attn_kernel,
        grid=(t // tm,),
        in_specs=[row(D_MODEL), row(A_WIDTH), row(B_Q_WIDTH), full(norm_mix), full(w_gate), full(w_a),
                  full(w_b), full(w_out), full(norm_moe), full(w_router), full(b_router)],
        out_specs=[row(D_MODEL), row(D_MODEL), row(LANES)],
        out_shape=[jax.ShapeDtypeStruct((t, D_MODEL), F32), jax.ShapeDtypeStruct((t, D_MODEL), BF16),
                   jax.ShapeDtypeStruct((t, LANES), F32)],
        compiler_params=pltpu.CompilerParams(
            dimension_semantics=("arbitrary",), vmem_limit_bytes=VMEM_LIMIT),
        name="post_attn",
    )(x2d, oa, ob, norm_mix, w_gate, w_a, w_b, w_out, norm_moe, w_router, b_router)


def _moe_final_kernel(x1_ref, h2_ref, gate_ref, w1_ref, w3_ref, w2_ref, gf_ref, o_ref, acc):
    e = pl.program_id(1)

    @pl.when(e == 0)
    def _():
        acc[...] = jnp.zeros(acc.shape, F32)

    h = h2_ref[...]
    a = jnp.dot(h, w1_ref[0], preferred_element_type=F32)
    b = jnp.dot(h, w3_ref[0], preferred_element_type=F32)
    hid = (a * jax.nn.sigmoid(a)) * b
    y = jnp.dot(hid.astype(BF16), w2_ref[0], preferred_element_type=F32)
    lane = lax.broadcasted_iota(jnp.int32, gate_ref.shape, 1)
    ge = jnp.sum(jnp.where(lane == e, gate_ref[...], 0.0), axis=-1, keepdims=True)
    acc[...] += ge * y

    @pl.when(e == N_EXPERTS - 1)
    def _():
        o_ref[...] = _rms(x1_ref[...] + acc[...], gf_ref[...])


def _moe_final(x1, h2, gate, w1, w3, w2, norm_final):
    t = x1.shape[0]
    tm = MOE_TILE
    row = lambda w: pl.BlockSpec((tm, w), lambda i, e: (i, 0))
    return pl.pallas_call(
        _moe_final_kernel,
        grid=(t // tm, N_EXPERTS),
        in_specs=[row(D_MODEL), row(D_MODEL), row(LANES),
                  pl.BlockSpec((1, D_MODEL, D_FF), lambda i, e: (e, 0, 0)),
                  pl.BlockSpec((1, D_MODEL, D_FF), lambda i, e: (e, 0, 0)),
                  pl.BlockSpec((1, D_FF, D_MODEL), lambda i, e: (e, 0, 0)),
                  pl.BlockSpec((1, D_MODEL), lambda i, e: (0, 0))],
        out_specs=row(D_MODEL),
        out_shape=jax.ShapeDtypeStruct((t, D_MODEL), F32),
        scratch_shapes=[pltpu.VMEM((tm, D_MODEL), F32)],
        compiler_params=pltpu.CompilerParams(
            dimension_semantics=("arbitrary", "arbitrary"), vmem_limit_bytes=VMEM_LIMIT),
        name="moe_final",
    )(x1, h2, gate, w1, w3, w2, norm_final)


def _prep_weights(norm_mix, w_in, attn_sink, w_branch_a, w_branch_b, w_gate, w_out, norm_moe,
                  w_router_group, b_router_group, w_router_expert, b_router_expert, w1, w3, w2, norm_final):
    wi = w_in[0]
    a_w = 3 * A_WIDTH

    def group_cols(gi):
        return jnp.concatenate([wi[:, role * a_w + gi * A_WIDTH: role * a_w + (gi + 1) * A_WIDTH]
                                for role in range(3)], axis=1)

    b_heads = [g * B_REP + c for c in range(B_REP) for g in range(B_KV_HEADS)]
    bq_cols = jnp.concatenate([wi[:, 3 * a_w + h * HEAD_DIM: 3 * a_w + (h + 1) * HEAD_DIM] for h in b_heads],
                              axis=1)
    w_b_rows = jnp.concatenate([w_branch_b[0][h * HEAD_DIM:(h + 1) * HEAD_DIM] for h in b_heads], axis=0)
    w_nat = jnp.concatenate([group_cols(0), bq_cols, wi[:, 3 * a_w + B_Q_WIDTH:]], axis=1).astype(BF16)
    w_g1 = group_cols(1).astype(BF16)
    w_g2 = group_cols(2).astype(BF16)
    pad = LANES - N_EXPERTS - N_GROUPS
    w_router = jnp.concatenate([w_router_expert[0], w_router_group[0],
                                jnp.zeros((D_MODEL, pad), F32)], axis=1)
    b_router = jnp.concatenate([b_router_expert[0].reshape(-1), b_router_group[0],
                                jnp.zeros((pad,), F32)]).reshape(1, LANES)
    return dict(
        norm_mix=norm_mix[0].reshape(1, D_MODEL), w_nat=w_nat, w_g1=w_g1, w_g2=w_g2, sink=attn_sink[0],
        w_a=w_branch_a[0].astype(BF16), w_b=w_b_rows.astype(BF16), w_gate=w_gate[0].astype(BF16),
        w_out=w_out[0].astype(BF16), norm_moe=norm_moe[0].reshape(1, D_MODEL),
        w_router=w_router, b_router=b_router,
        w1=w1[0].astype(BF16), w3=w3[0].astype(BF16), w2=w2[0].astype(BF16),
        norm_final=norm_final.reshape(1, D_MODEL))


def _trunk(x, p):
    b, s, _ = x.shape
    outs = _in_proj(x, p["norm_mix"], p["w_nat"], p["w_g1"], p["w_g2"])
    oa = _mixer_a(outs[:9], s)
    ob = _mixer_b(outs[9], outs[10], outs[11], p["sink"], s)
    t = b * s
    x1, h2, gate = _post_attn(x.reshape(t, D_MODEL), oa.reshape(t, A_WIDTH), ob.reshape(t, B_Q_WIDTH),
                              p["norm_mix"], p["w_gate"], p["w_a"], p["w_b"], p["w_out"], p["norm_moe"],
                              p["w_router"], p["b_router"])
    y = _moe_final(x1, h2, gate, p["w1"], p["w3"], p["w2"], p["norm_final"])
    return y.reshape(b, s, D_MODEL)


def kernel(x_prompt, x_sample, norm_mix, w_in, attn_sink, w_branch_a, w_branch_b, w_gate, w_out, norm_moe,
           w_router_group, b_router_group, w_router_expert, b_router_expert, w1, w3, w2, norm_final):
    p = _prep_weights(norm_mix, w_in, attn_sink, w_branch_a, w_branch_b, w_gate, w_out, norm_moe,
                      w_router_group, b_router_group, w_router_expert, b_router_expert, w1, w3, w2, norm_final)
    return (_trunk(x_prompt, p), _trunk(x_sample, p))
```

```python
import functools

import jax
import jax.numpy as jnp
from jax import lax
from jax.experimental import pallas as pl
from jax.experimental.pallas import tpu as pltpu

D_MODEL = 1024
HEAD_DIM = 64
ROT_DIM = HEAD_DIM // 4
ROT_HALF = ROT_DIM // 2
ROPE_THETA = 500000.0
DILATIONS = (1, 4, 16)
A_HALF = 64
A_BLOCK = 64
A_QBLOCK = 128
A_HEADS = 8
A_WIDTH = A_HEADS * HEAD_DIM
B_Q_HEADS = 8
B_KV_HEADS = 2
B_REP = B_Q_HEADS // B_KV_HEADS
B_WINDOW = 128
B_BLOCK = 128
B_Q_WIDTH = B_Q_HEADS * HEAD_DIM
B_KV_WIDTH = B_KV_HEADS * HEAD_DIM
N_GROUPS = 4
EXPERTS_PER_GROUP = 8
N_EXPERTS = N_GROUPS * EXPERTS_PER_GROUP
D_FF = 256
RMS_EPS = 1e-6
Q_SCALE = HEAD_DIM ** -0.5

LANES = 128
NEG_BIG = -1e30
VMEM_LIMIT = 56 * 1024 * 1024

IN_TILE = 256
ATT_TILE = 1024
POST_TILE = 512
MOE_TILE = 1024
SEG_ALIGN = 16
MOE_CHUNK = 128
MOE_SLOTS = 2 * MOE_TILE + N_EXPERTS * SEG_ALIGN

F32 = jnp.float32
BF16 = jnp.bfloat16


def _rms(x, g):
    ms = jnp.mean(x * x, axis=-1, keepdims=True)
    return x * lax.rsqrt(ms + RMS_EPS) * g


def _rope(a, cos, sin, low_half):
    outs = []
    for k in range(a.shape[1] // LANES):
        ak = a[:, k * LANES:(k + 1) * LANES]
        up = pltpu.roll(ak, LANES - ROT_HALF, axis=1)
        dn = pltpu.roll(ak, ROT_HALF, axis=1)
        outs.append(ak * cos + jnp.where(low_half, up, dn) * sin)
    return outs[0] if len(outs) == 1 else jnp.concatenate(outs, axis=1)


def _in_proj_kernel(x_ref, g_ref, cos_ref, sin_ref, wn_ref, w1_ref, w2_ref,
                    q0, k0, v0, q1, k1, v1, q2, k2, v2, bq, bk, bv,
                    xs, h1_s, h2_s, c1_s, s1_s, c2_s, s2_s):
    tm = x_ref.shape[1]
    n_chunks = D_MODEL // LANES
    for c in range(n_chunks):
        xs[c] = x_ref[0, :, c * LANES:(c + 1) * LANES]
    g = g_ref[...]
    low_half = (lax.broadcasted_iota(jnp.int32, (tm, LANES), 1) % HEAD_DIM) < ROT_HALF

    def mm(h, w_ref, c0, n):
        return jnp.dot(h, w_ref[:, c0:c0 + n], preferred_element_type=F32)

    h0 = _rms(x_ref[0], g).astype(BF16)
    cos0, sin0 = cos_ref[...], sin_ref[...]
    q0[0, 0] = (_rope(mm(h0, wn_ref, 0, A_WIDTH), cos0, sin0, low_half) * Q_SCALE).astype(BF16)
    k0[0, 0] = _rope(mm(h0, wn_ref, A_WIDTH, A_WIDTH), cos0, sin0, low_half).astype(BF16)
    v0[0, 0] = mm(h0, wn_ref, 2 * A_WIDTH, A_WIDTH).astype(BF16)
    c = 3 * A_WIDTH
    bq[0] = (_rope(mm(h0, wn_ref, c, B_Q_WIDTH), cos0, sin0, low_half) * Q_SCALE).astype(BF16)
    bk[0] = _rope(mm(h0, wn_ref, c + B_Q_WIDTH, B_KV_WIDTH), cos0, sin0, low_half).astype(BF16)
    bv[0] = mm(h0, wn_ref, c + B_Q_WIDTH + B_KV_WIDTH, B_KV_WIDTH).astype(BF16)

    for d, hs, cs, ss, w_ref, qo, ko, vo in ((DILATIONS[1], h1_s, c1_s, s1_s, w1_ref, q1, k1, v1),
                                             (DILATIONS[2], h2_s, c2_s, s2_s, w2_ref, q2, k2, v2)):
        n = tm // d
        for r in range(d):
            rows = pl.ds(r, n, stride=d)
            xr = jnp.concatenate([xs[c, rows, :] for c in range(n_chunks)], axis=1)
            hs[r * n:(r + 1) * n, :] = _rms(xr, g).astype(BF16)
            cs[r * n:(r + 1) * n, :] = cos_ref[rows, :]
            ss[r * n:(r + 1) * n, :] = sin_ref[rows, :]
        h = hs[...]
        cos, sin = cs[...], ss[...]
        qv = (_rope(mm(h, w_ref, 0, A_WIDTH), cos, sin, low_half) * Q_SCALE).astype(BF16)
        kv = _rope(mm(h, w_ref, A_WIDTH, A_WIDTH), cos, sin, low_half).astype(BF16)
        vv = mm(h, w_ref, 2 * A_WIDTH, A_WIDTH).astype(BF16)
        for r in range(d):
            qo[0, r] = qv[r * n:(r + 1) * n]
            ko[0, r] = kv[r * n:(r + 1) * n]
            vo[0, r] = vv[r * n:(r + 1) * n]


def _rope_tables(s):
    inv = jnp.power(jnp.float32(ROPE_THETA), -jnp.arange(ROT_HALF, dtype=F32) / ROT_HALF)
    ang = jnp.arange(s).astype(F32)[:, None] * inv[None, :]
    cos, sin = jnp.cos(ang), jnp.sin(ang)
    pad_one = jnp.ones((s, HEAD_DIM - ROT_DIM), F32)
    pad_zero = jnp.zeros((s, HEAD_DIM - ROT_DIM), F32)
    cos_h = jnp.concatenate([cos, cos, pad_one], axis=1)
    sin_h = jnp.concatenate([-sin, sin, pad_zero], axis=1)
    return jnp.concatenate([cos_h, cos_h], axis=1), jnp.concatenate([sin_h, sin_h], axis=1)


def _in_proj(x, norm_g, w_nat, w_g1, w_g2):
    b, s, _ = x.shape
    tm = IN_TILE
    cos_t, sin_t = _rope_tables(s)
    const = lambda t, bb: (0, 0)
    one_buf = pl.Buffered(1)
    in_specs = [
        pl.BlockSpec((1, tm, D_MODEL), lambda t, bb: (bb, t, 0)),
        pl.BlockSpec((1, D_MODEL), const),
        pl.BlockSpec((tm, LANES), lambda t, bb: (t, 0)),
        pl.BlockSpec((tm, LANES), lambda t, bb: (t, 0)),
        pl.BlockSpec(w_nat.shape, const, pipeline_mode=one_buf),
        pl.BlockSpec(w_g1.shape, const, pipeline_mode=one_buf),
        pl.BlockSpec(w_g2.shape, const, pipeline_mode=one_buf),
    ]
    out_shape, out_specs = [], []
    for d in DILATIONS:
        for _ in range(3):
            out_shape.append(jax.ShapeDtypeStruct((b, d, s // d, A_WIDTH), BF16))
            out_specs.append(pl.BlockSpec((1, d, tm // d, A_WIDTH), lambda t, bb: (bb, 0, t, 0)))
    for w in (B_Q_WIDTH, B_KV_WIDTH, B_KV_WIDTH):
        out_shape.append(jax.ShapeDtypeStruct((b, s, w), BF16))
        out_specs.append(pl.BlockSpec((1, tm, w), lambda t, bb: (bb, t, 0)))
    scratch = [pltpu.VMEM((D_MODEL // LANES, tm, LANES), F32),
               pltpu.VMEM((tm, D_MODEL), BF16), pltpu.VMEM((tm, D_MODEL), BF16)] + \
              [pltpu.VMEM((tm, LANES), F32) for _ in range(4)]
    return pl.pallas_call(
        _in_proj_kernel,
        grid=(s // tm, b),
        in_specs=in_specs,
        out_specs=out_specs,
        out_shape=out_shape,
        scratch_shapes=scratch,
        compiler_params=pltpu.CompilerParams(
            dimension_semantics=("arbitrary", "arbitrary"), vmem_limit_bytes=VMEM_LIMIT),
        name="in_proj",
    )(x, norm_g.reshape(1, D_MODEL), cos_t, sin_t, w_nat, w_g1, w_g2)


def _attend_heads(qb, kw, vw, valid, first_half):
    second_half = jnp.logical_not(first_half)
    scores = []
    for h in range(A_HEADS):
        cs = slice((h // 2) * LANES, (h // 2 + 1) * LANES)
        qp = qb[:, cs]
        qm = jnp.where(first_half if h % 2 == 0 else second_half, qp, jnp.zeros_like(qp))
        scores.append(lax.dot_general(qm, kw[:, cs], (((1,), (1,)), ((), ())), preferred_element_type=F32))
    sc = jnp.where(valid[None], jnp.stack(scores), NEG_BIG)
    m = jnp.max(sc, axis=-1, keepdims=True)
    p = jnp.exp(sc - m)
    l = jnp.sum(p, axis=-1, keepdims=True)
    pb = p.astype(BF16)
    out = []
    for hp in range(A_HEADS // 2):
        vp = vw[:, hp * LANES:(hp + 1) * LANES]
        o0 = jnp.dot(pb[2 * hp], vp, preferred_element_type=F32)
        o1 = jnp.dot(pb[2 * hp + 1], vp, preferred_element_type=F32)
        out.append((jnp.where(first_half, o0, o1), jnp.where(first_half, m[2 * hp], m[2 * hp + 1]),
                    jnp.where(first_half, l[2 * hp], l[2 * hp + 1])))
    return out


def _mixer_a_kernel(*refs, seq):
    ins, (o_ref, kext, vext, acc_o, acc_m, acc_l) = refs[:21], refs[21:]
    t = pl.program_id(1)
    tq = ATT_TILE
    halo = A_HALF

    for gi, d in enumerate(DILATIONS):
        q, kc, kp, kn, vc, vp, vn = ins[7 * gi:7 * gi + 7]
        n = tq // d
        blk = min(n, A_QBLOCK)
        nblk = n // blk
        win = -(-(blk + 2 * halo) // LANES) * LANES
        ld = seq // d
        col = lax.broadcasted_iota(jnp.int32, (blk, win), 1)
        diff = col - lax.broadcasted_iota(jnp.int32, (blk, win), 0)
        band = (diff >= 0) & (diff <= 2 * halo)
        first_half = lax.broadcasted_iota(jnp.int32, (blk, LANES), 1) < HEAD_DIM

        def residue_body(r, carry, q=q, kc=kc, kp=kp, kn=kn, vc=vc, vp=vp, vn=vn, d=d, n=n, nblk=nblk,
                         ld=ld, blk=blk, win=win, col=col, band=band, first_half=first_half, first=(gi == 0)):
            kext[0:halo, :] = kp[0, r]
            kext[halo:halo + n, :] = kc[0, r]
            kext[halo + n:2 * halo + n, :] = kn[0, r]
            vext[0:halo, :] = vp[0, r]
            vext[halo:halo + n, :] = vc[0, r]
            vext[halo + n:2 * halo + n, :] = vn[0, r]

            def block_body(j, carry2):
                j0 = pl.multiple_of(j * blk, blk)
                qb = q[0, r, pl.ds(j0, blk), :]
                kw = kext[pl.ds(j0, win), :]
                vw = vext[pl.ds(j0, win), :]
                qbase = t * n + j * blk
                valid = band & (col >= halo - qbase) & (col < ld - qbase + halo)
                parts = _attend_heads(qb, kw, vw, valid, first_half)
                if d == 1:
                    rows = pl.ds(j0, blk)
                else:
                    rows = pl.ds(j * (blk * d) + r, blk, stride=d)
                for hp, (o_b, m_b, l_b) in enumerate(parts):
                    if first:
                        acc_o[hp, rows, :] = o_b
                        acc_l[hp, rows, :] = l_b
                        acc_m[hp, rows, :] = m_b
                    else:
                        m_old = acc_m[hp, rows, :]
                        m_new = jnp.maximum(m_old, m_b)
                        a_old = jnp.exp(m_old - m_new)
                        a_new = jnp.exp(m_b - m_new)
                        acc_o[hp, rows, :] = acc_o[hp, rows, :] * a_old + o_b * a_new
                        acc_l[hp, rows, :] = acc_l[hp, rows, :] * a_old + l_b * a_new
                        acc_m[hp, rows, :] = m_new
                return carry2

            lax.fori_loop(0, nblk, block_body, 0)
            return carry

        lax.fori_loop(0, d, residue_body, 0)

    o_ref[0] = jnp.concatenate([acc_o[hp] / acc_l[hp] for hp in range(A_HEADS // 2)], axis=1).astype(BF16)


def _mixer_a(qkv, seq):
    b = qkv[0].shape[0]
    tq = ATT_TILE
    blk = A_BLOCK
    args, in_specs = [], []
    for gi, d in enumerate(DILATIONS):
        q, k, v = qkv[3 * gi:3 * gi + 3]
        n = tq // d
        nblk = n // blk
        last = seq // d // blk - 1
        cur = pl.BlockSpec((1, d, n, A_WIDTH), lambda bb, t: (bb, 0, t, 0))
        prev = pl.BlockSpec((1, d, blk, A_WIDTH),
                            lambda bb, t, nblk=nblk: (bb, 0, jnp.maximum(t * nblk - 1, 0), 0))
        nxt = pl.BlockSpec((1, d, blk, A_WIDTH),
                           lambda bb, t, nblk=nblk, last=last: (bb, 0, jnp.minimum((t + 1) * nblk, last), 0))
        args += [q, k, k, k, v, v, v]
        in_specs += [cur, cur, prev, nxt, cur, prev, nxt]
    return pl.pallas_call(
        functools.partial(_mixer_a_kernel, seq=seq),
        grid=(b, seq // tq),
        in_specs=in_specs,
        out_specs=pl.BlockSpec((1, tq, A_WIDTH), lambda bb, t: (bb, t, 0)),
        out_shape=jax.ShapeDtypeStruct((b, seq, A_WIDTH), BF16),
        scratch_shapes=[pltpu.VMEM((tq + 2 * blk, A_WIDTH), BF16), pltpu.VMEM((tq + 2 * blk, A_WIDTH), BF16),
                        pltpu.VMEM((A_WIDTH // LANES, tq, LANES), F32),
                        pltpu.VMEM((A_WIDTH // LANES, tq, LANES), F32),
                        pltpu.VMEM((A_WIDTH // LANES, tq, LANES), F32)],
        compiler_params=pltpu.CompilerParams(
            dimension_semantics=("arbitrary", "arbitrary"), vmem_limit_bytes=VMEM_LIMIT),
        name="mixer_a",
    )(*args)


def _mixer_b_kernel(sink_ref, q, kc, kp, kn, vc, vp, vn, o_ref, kext, vext, *, seq):
    t = pl.program_id(1)
    tq = ATT_TILE
    blk = B_BLOCK
    win = 3 * blk
    kext[0:blk, :] = kp[0]
    kext[blk:blk + tq, :] = kc[0]
    kext[blk + tq:2 * blk + tq, :] = kn[0]
    vext[0:blk, :] = vp[0]
    vext[blk:blk + tq, :] = vc[0]
    vext[blk + tq:2 * blk + tq, :] = vn[0]
    col = lax.broadcasted_iota(jnp.int32, (blk, win), 1)
    row = lax.broadcasted_iota(jnp.int32, (blk, win), 0)
    diff = col - row
    first_half = lax.broadcasted_iota(jnp.int32, (blk, LANES), 1) < HEAD_DIM
    second_half = jnp.logical_not(first_half)

    def block_body(j, carry):
        j0 = pl.multiple_of(j * blk, blk)
        qb = q[0, pl.ds(j0, blk), :]
        kw = kext[pl.ds(j0, win), :]
        vw = vext[pl.ds(j0, win), :]
        qbase = t * tq + j * blk
        valid = ((diff >= 0) & (diff <= 2 * B_WINDOW)
                 & (col >= blk - qbase) & (col < seq - qbase + blk))
        scores, sinks = [], []
        for c in range(B_Q_HEADS // 2):
            qp = qb[:, c * LANES:(c + 1) * LANES]
            for g, keep in enumerate((first_half, second_half)):
                qm = jnp.where(keep, qp, jnp.zeros_like(qp))
                scores.append(lax.dot_general(qm, kw, (((1,), (1,)), ((), ())), preferred_element_type=F32))
                sinks.append(jnp.full((1, LANES), sink_ref[g * B_REP + c], F32))
        sc = jnp.where(valid[None], jnp.stack(scores), NEG_BIG)
        sk = jnp.stack(sinks)
        part = sc[..., 0:LANES]
        for k in range(1, win // LANES):
            part = jnp.maximum(part, sc[..., k * LANES:(k + 1) * LANES])
        m = jnp.max(jnp.maximum(part, sk), axis=-1, keepdims=True)
        p = jnp.exp(sc - m)
        denom = jnp.sum(p, axis=-1, keepdims=True) + jnp.exp(sk - m)
        pb = p.astype(BF16)
        parts = []
        for c in range(B_Q_HEADS // 2):
            o0 = jnp.dot(pb[2 * c], vw, preferred_element_type=F32) / denom[2 * c]
            o1 = jnp.dot(pb[2 * c + 1], vw, preferred_element_type=F32) / denom[2 * c + 1]
            parts.append(jnp.where(first_half, o0, o1))
        o_ref[0, pl.ds(j0, blk), :] = jnp.concatenate(parts, axis=1).astype(BF16)
        return carry

    lax.fori_loop(0, tq // blk, block_body, 0)


def _mixer_b(q, k, v, sink, seq):
    b = q.shape[0]
    tq = ATT_TILE
    blk = B_BLOCK
    nblk = tq // blk
    last = seq // blk - 1
    cur = pl.BlockSpec((1, tq, B_KV_WIDTH), lambda bb, t: (bb, t, 0))
    prev = pl.BlockSpec((1, blk, B_KV_WIDTH), lambda bb, t: (bb, jnp.maximum(t * nblk - 1, 0), 0))
    nxt = pl.BlockSpec((1, blk, B_KV_WIDTH), lambda bb, t: (bb, jnp.minimum((t + 1) * nblk, last), 0))
    return pl.pallas_call(
        functools.partial(_mixer_b_kernel, seq=seq),
        grid=(b, seq // tq),
        in_specs=[pl.BlockSpec(memory_space=pltpu.SMEM),
                  pl.BlockSpec((1, tq, B_Q_WIDTH), lambda bb, t: (bb, t, 0)),
                  cur, prev, nxt, cur, prev, nxt],
        out_specs=pl.BlockSpec((1, tq, B_Q_WIDTH), lambda bb, t: (bb, t, 0)),
        out_shape=jax.ShapeDtypeStruct((b, seq, B_Q_WIDTH), BF16),
        scratch_shapes=[pltpu.VMEM((tq + 2 * blk, B_KV_WIDTH), BF16),
                        pltpu.VMEM((tq + 2 * blk, B_KV_WIDTH), BF16)],
        compiler_params=pltpu.CompilerParams(
            dimension_semantics=("arbitrary", "arbitrary"), vmem_limit_bytes=VMEM_LIMIT),
        name="mixer_b",
    )(sink, q, k, k, k, v, v, v)


def _post_attn_kernel(x_ref, oa_ref, ob_ref, gmix_ref, wg_ref, wa_ref, wb_ref, wo_ref, gmoe_ref,
                      wr_ref, br_ref, x1_ref, h2_ref, gate_ref):
    x = x_ref[...]
    h = _rms(x, gmix_ref[...]).astype(BF16)
    gates = jax.nn.sigmoid(jnp.dot(h, wg_ref[...], preferred_element_type=F32))
    ya = jnp.dot(oa_ref[...], wa_ref[...], preferred_element_type=F32)
    yb = jnp.dot(ob_ref[...], wb_ref[...], preferred_element_type=F32)
    mix = gates[:, :D_MODEL] * ya + gates[:, D_MODEL:] * yb
    x1 = x + jnp.dot(mix.astype(BF16), wo_ref[...], preferred_element_type=F32)
    x1_ref[...] = x1
    h2 = _rms(x1, gmoe_ref[...])
    h2_ref[...] = h2.astype(BF16)

    logits = jnp.dot(h2, wr_ref[...], preferred_element_type=F32,
                     precision=lax.Precision.HIGHEST) + br_ref[...]
    tm = logits.shape[0]
    lane = lax.broadcasted_iota(jnp.int32, (tm, LANES), 1)
    big = jnp.int32(LANES)
    is_group = (lane >= N_EXPERTS) & (lane < N_EXPERTS + N_GROUPS)
    gl = jnp.where(is_group, logits, NEG_BIG)
    gmax = jnp.max(gl, axis=-1, keepdims=True)
    g_idx = jnp.min(jnp.where(gl == gmax, lane, big), axis=-1, keepdims=True) - N_EXPERTS
    g_w = 1.0 / jnp.sum(jnp.exp(gl - gmax), axis=-1, keepdims=True)
    in_group = (lane < N_EXPERTS) & (lax.shift_right_logical(lane, 3) == g_idx)
    sel = jnp.where(in_group, logits, NEG_BIG)
    v1 = jnp.max(sel, axis=-1, keepdims=True)
    i1 = jnp.min(jnp.where(sel == v1, lane, big), axis=-1, keepdims=True)
    sel2 = jnp.where(lane == i1, NEG_BIG, sel)
    v2 = jnp.max(sel2, axis=-1, keepdims=True)
    i2 = jnp.min(jnp.where(sel2 == v2, lane, big), axis=-1, keepdims=True)
    e2 = jnp.exp(v2 - v1)
    w1 = g_w / (1.0 + e2)
    w2 = g_w * e2 / (1.0 + e2)
    gate_ref[...] = jnp.where(lane == i1, w1, 0.0) + jnp.where(lane == i2, w2, 0.0)


def _post_attn(x2d, oa, ob, norm_mix, w_gate, w_a, w_b, w_out, norm_moe, w_router, b_router):
    t = x2d.shape[0]
    tm = POST_TILE
    const = lambda i: (0, 0)
    row = lambda w: pl.BlockSpec((tm, w), lambda i: (i, 0))
    full = lambda a: pl.BlockSpec(a.shape, const, pipeline_mode=pl.Buffered(1))
    return pl.pallas_call(
        _post_attn_kernel,
        grid=(t // tm,),
        in_specs=[row(D_MODEL), row(A_WIDTH), row(B_Q_WIDTH), full(norm_mix), full(w_gate), full(w_a),
                  full(w_b), full(w_out), full(norm_moe), full(w_router), full(b_router)],
        out_specs=[row(D_MODEL), row(D_MODEL), row(LANES)],
        out_shape=[jax.ShapeDtypeStruct((t, D_MODEL), F32), jax.ShapeDtypeStruct((t, D_MODEL), BF16),
                   jax.ShapeDtypeStruct((t, LANES), F32)],
        compiler_params=pltpu.CompilerParams(
            dimension_semantics=("arbitrary",), vmem_limit_bytes=VMEM_LIMIT),
        name="post_attn",
    )(x2d, oa, ob, norm_mix, w_gate, w_a, w_b, w_out, norm_moe, w_router, b_router)


def _moe_final_kernel(x1_ref, h2_ref, gate_ref, tri_ref, w1_ref, w3_ref, w2_ref, gf_ref, o_ref,
                      xs, ys, seg_s, slo_s, shi_s, wlo_s, whi_s):
    e = pl.program_id(1)
    tm = x1_ref.shape[0]
    slot_cols = 512
    n_slot_chunks = MOE_SLOTS // slot_cols

    @pl.when(e == 0)
    def _sort():
        g = gate_ref[...]
        hit = g > 0.0
        onehot = jnp.where(hit, 1.0, 0.0)
        count = jnp.sum(onehot, axis=0, keepdims=True)
        padded = jnp.ceil(count * (1.0 / SEG_ALIGN)) * SEG_ALIGN
        r_i = lax.broadcasted_iota(jnp.int32, (LANES, LANES), 0)
        c_i = lax.broadcasted_iota(jnp.int32, (LANES, LANES), 1)
        upper = jnp.where(r_i < c_i, 1.0, 0.0)
        seg = jnp.dot(jnp.broadcast_to(padded, (8, LANES)), upper, preferred_element_type=F32,
                      precision=lax.Precision.HIGHEST)[0:1]
        seg_s[0:1, :] = seg
        seg_s[1:2, :] = padded
        rank = jnp.dot(tri_ref[...], onehot.astype(BF16), preferred_element_type=F32)
        slot = seg + rank
        s_lo = jnp.min(jnp.where(hit, slot, 1e9), axis=-1, keepdims=True)
        s_hi = jnp.max(jnp.where(hit, slot, -1.0), axis=-1, keepdims=True)
        wlo_s[...] = jnp.sum(jnp.where(hit & (slot == s_lo), g, 0.0), axis=-1, keepdims=True)
        whi_s[...] = jnp.sum(jnp.where(hit & (slot == s_hi) & (s_hi > s_lo), g, 0.0), axis=-1, keepdims=True)
        s_lo_i = s_lo.astype(jnp.int32)
        s_hi_i = s_hi.astype(jnp.int32)
        slo_s[...] = s_lo_i
        shi_s[...] = s_hi_i
        h = h2_ref[...]
        for c in range(n_slot_chunks):
            col = lax.broadcasted_iota(jnp.int32, (tm, slot_cols), 1) + c * slot_cols
            perm_t = jnp.where((col == s_lo_i) | (col == s_hi_i), 1.0, 0.0).astype(BF16)
            rows = lax.dot_general(perm_t, h, (((0,), (0,)), ((), ())), preferred_element_type=F32)
            xs[c * slot_cols:(c + 1) * slot_cols, :] = rows.astype(BF16)
        xs[MOE_SLOTS:, :] = jnp.zeros((MOE_CHUNK, D_MODEL), BF16)
        ys[...] = jnp.zeros(ys.shape, BF16)

    lane = lax.broadcasted_iota(jnp.int32, (1, LANES), 1)
    start = jnp.sum(jnp.where(lane == e, seg_s[0:1, :], 0.0)).astype(jnp.int32)
    length = jnp.sum(jnp.where(lane == e, seg_s[1:2, :], 0.0)).astype(jnp.int32)

    def chunk_body(k, carry):
        r0 = pl.multiple_of(start + k * MOE_CHUNK, SEG_ALIGN)
        x = xs[pl.ds(r0, MOE_CHUNK), :]
        a = jnp.dot(x, w1_ref[0], preferred_element_type=F32)
        b = jnp.dot(x, w3_ref[0], preferred_element_type=F32)
        hid = (a * jax.nn.sigmoid(a)) * b
        ys[pl.ds(r0, MOE_CHUNK), :] = jnp.dot(hid.astype(BF16), w2_ref[0],
                                              preferred_element_type=F32).astype(BF16)
        return carry

    lax.fori_loop(0, (length + MOE_CHUNK - 1) // MOE_CHUNK, chunk_body, 0)

    @pl.when(e == N_EXPERTS - 1)
    def _combine():
        s_lo_i, s_hi_i = slo_s[...], shi_s[...]
        w_lo, w_hi = wlo_s[...], whi_s[...]
        acc = x1_ref[...]
        for c in range(n_slot_chunks):
            col = lax.broadcasted_iota(jnp.int32, (tm, slot_cols), 1) + c * slot_cols
            wperm = (jnp.where(col == s_lo_i, w_lo, 0.0) + jnp.where(col == s_hi_i, w_hi, 0.0)).astype(BF16)
            acc = acc + jnp.dot(wperm, ys[c * slot_cols:(c + 1) * slot_cols, :], preferred_element_type=F32)
        o_ref[...] = _rms(acc, gf_ref[...])


def _moe_final(x1, h2, gate, w1, w3, w2, norm_final):
    t = x1.shape[0]
    tm = MOE_TILE
    row = lambda w: pl.BlockSpec((tm, w), lambda i, e: (i, 0))
    tri = jnp.tril(jnp.ones((tm, tm), BF16), -1)
    return pl.pallas_call(
        _moe_final_kernel,
        grid=(t // tm, N_EXPERTS),
        in_specs=[row(D_MODEL), row(D_MODEL), row(LANES),
                  pl.BlockSpec((tm, tm), lambda i, e: (0, 0), pipeline_mode=pl.Buffered(1)),
                  pl.BlockSpec((1, D_MODEL, D_FF), lambda i, e: (e, 0, 0)),
                  pl.BlockSpec((1, D_MODEL, D_FF), lambda i, e: (e, 0, 0)),
                  pl.BlockSpec((1, D_FF, D_MODEL), lambda i, e: (e, 0, 0)),
                  pl.BlockSpec((1, D_MODEL), lambda i, e: (0, 0))],
        out_specs=row(D_MODEL),
        out_shape=jax.ShapeDtypeStruct((t, D_MODEL), F32),
        scratch_shapes=[pltpu.VMEM((MOE_SLOTS + MOE_CHUNK, D_MODEL), BF16),
                        pltpu.VMEM((MOE_SLOTS + MOE_CHUNK, D_MODEL), BF16),
                        pltpu.VMEM((8, LANES), F32),
                        pltpu.VMEM((tm, 1), jnp.int32), pltpu.VMEM((tm, 1), jnp.int32),
                        pltpu.VMEM((tm, 1), F32), pltpu.VMEM((tm, 1), F32)],
        compiler_params=pltpu.CompilerParams(
            dimension_semantics=("arbitrary", "arbitrary"), vmem_limit_bytes=VMEM_LIMIT),
        name="moe_final",
    )(x1, h2, gate, tri, w1, w3, w2, norm_final)


def _prep_weights(norm_mix, w_in, attn_sink, w_branch_a, w_branch_b, w_gate, w_out, norm_moe,
                  w_router_group, b_router_group, w_router_expert, b_router_expert, w1, w3, w2, norm_final):
    wi = w_in[0]
    a_w = 3 * A_WIDTH

    def group_cols(gi):
        return jnp.concatenate([wi[:, role * a_w + gi * A_WIDTH: role * a_w + (gi + 1) * A_WIDTH]
                                for role in range(3)], axis=1)

    b_heads = [g * B_REP + c for c in range(B_REP) for g in range(B_KV_HEADS)]
    bq_cols = jnp.concatenate([wi[:, 3 * a_w + h * HEAD_DIM: 3 * a_w + (h + 1) * HEAD_DIM] for h in b_heads],
                              axis=1)
    w_b_rows = jnp.concatenate([w_branch_b[0][h * HEAD_DIM:(h + 1) * HEAD_DIM] for h in b_heads], axis=0)
    w_nat = jnp.concatenate([group_cols(0), bq_cols, wi[:, 3 * a_w + B_Q_WIDTH:]], axis=1).astype(BF16)
    w_g1 = group_cols(1).astype(BF16)
    w_g2 = group_cols(2).astype(BF16)
    pad = LANES - N_EXPERTS - N_GROUPS
    w_router = jnp.concatenate([w_router_expert[0], w_router_group[0],
                                jnp.zeros((D_MODEL, pad), F32)], axis=1)
    b_router = jnp.concatenate([b_router_expert[0].reshape(-1), b_router_group[0],
                                jnp.zeros((pad,), F32)]).reshape(1, LANES)
    return dict(
        norm_mix=norm_mix[0].reshape(1, D_MODEL), w_nat=w_nat, w_g1=w_g1, w_g2=w_g2, sink=attn_sink[0],
        w_a=w_branch_a[0].astype(BF16), w_b=w_b_rows.astype(BF16), w_gate=w_gate[0].astype(BF16),
        w_out=w_out[0].astype(BF16), norm_moe=norm_moe[0].reshape(1, D_MODEL),
        w_router=w_router, b_router=b_router,
        w1=w1[0].astype(BF16), w3=w3[0].astype(BF16), w2=w2[0].astype(BF16),
        norm_final=norm_final.reshape(1, D_MODEL))


def _trunk(x, p):
    b, s, _ = x.shape
    outs = _in_proj(x, p["norm_mix"], p["w_nat"], p["w_g1"], p["w_g2"])
    oa = _mixer_a(outs[:9], s)
    ob = _mixer_b(outs[9], outs[10], outs[11], p["sink"], s)
    t = b * s
    x1, h2, gate = _post_attn(x.reshape(t, D_MODEL), oa.reshape(t, A_WIDTH), ob.reshape(t, B_Q_WIDTH),
                              p["norm_mix"], p["w_gate"], p["w_a"], p["w_b"], p["w_out"], p["norm_moe"],
                              p["w_router"], p["b_router"])
    y = _moe_final(x1, h2, gate, p["w1"], p["w3"], p["w2"], p["norm_final"])
    return y.reshape(b, s, D_MODEL)


def kernel(x_prompt, x_sample, norm_mix, w_in, attn_sink, w_branch_a, w_branch_b, w_gate, w_out, norm_moe,
           w_router_group, b_router_group, w_router_expert, b_router_expert, w1, w3, w2, norm_final):
    p = _prep_weights(norm_mix, w_in, attn_sink, w_branch_a, w_branch_b, w_gate, w_out, norm_moe,
                      w_router_group, b_router_group, w_router_expert, b_router_expert, w1, w3, w2, norm_final)
    return (_trunk(x_prompt, p), _trunk(x_sample, p))
```

```python
import functools

import jax
import jax.numpy as jnp
from jax import lax
from jax.experimental import pallas as pl
from jax.experimental.pallas import tpu as pltpu

D_MODEL = 1024
HEAD_DIM = 64
ROT_DIM = HEAD_DIM // 4
ROT_HALF = ROT_DIM // 2
ROPE_THETA = 500000.0
DILATIONS = (1, 4, 16)
A_HALF = 64
A_BLOCK = 64
A_QBLOCK = 128
A_HEADS = 8
A_WIDTH = A_HEADS * HEAD_DIM
B_Q_HEADS = 8
B_KV_HEADS = 2
B_REP = B_Q_HEADS // B_KV_HEADS
B_WINDOW = 128
B_BLOCK = 128
B_Q_WIDTH = B_Q_HEADS * HEAD_DIM
B_KV_WIDTH = B_KV_HEADS * HEAD_DIM
N_GROUPS = 4
EXPERTS_PER_GROUP = 8
N_EXPERTS = N_GROUPS * EXPERTS_PER_GROUP
D_FF = 256
RMS_EPS = 1e-6
Q_SCALE = HEAD_DIM ** -0.5

LANES = 128
NEG_BIG = -1e30
VMEM_LIMIT = 56 * 1024 * 1024

IN_TILE = 256
ATT_TILE = 1024
POST_TILE = 512
MOE_TILE = 1024
SEG_ALIGN = 16
MOE_CHUNK = 128
MOE_SUB = 256
MOE_SUB_SLOTS = 2 * MOE_SUB + N_EXPERTS * SEG_ALIGN
MOE_ESTEP = 4

F32 = jnp.float32
BF16 = jnp.bfloat16


def _rms(x, g):
    ms = jnp.mean(x * x, axis=-1, keepdims=True)
    return x * lax.rsqrt(ms + RMS_EPS) * g


def _rope(a, cos, sin, low_half):
    outs = []
    for k in range(a.shape[1] // LANES):
        ak = a[:, k * LANES:(k + 1) * LANES]
        up = pltpu.roll(ak, LANES - ROT_HALF, axis=1)
        dn = pltpu.roll(ak, ROT_HALF, axis=1)
        outs.append(ak * cos + jnp.where(low_half, up, dn) * sin)
    return outs[0] if len(outs) == 1 else jnp.concatenate(outs, axis=1)


def _in_proj_kernel(x_ref, g_ref, cos_ref, sin_ref, wn_ref, w1_ref, w2_ref,
                    q0, k0, v0, q1, k1, v1, q2, k2, v2, bq, bk, bv,
                    xs, h1_s, h2_s, c1_s, s1_s, c2_s, s2_s):
    tm = x_ref.shape[1]
    n_chunks = D_MODEL // LANES
    for c in range(n_chunks):
        xs[c] = x_ref[0, :, c * LANES:(c + 1) * LANES]
    g = g_ref[...]
    low_half = (lax.broadcasted_iota(jnp.int32, (tm, LANES), 1) % HEAD_DIM) < ROT_HALF

    def mm(h, w_ref, c0, n):
        return jnp.dot(h, w_ref[:, c0:c0 + n], preferred_element_type=F32)

    h0 = _rms(x_ref[0], g).astype(BF16)
    cos0, sin0 = cos_ref[...], sin_ref[...]
    q0[0, 0] = (_rope(mm(h0, wn_ref, 0, A_WIDTH), cos0, sin0, low_half) * Q_SCALE).astype(BF16)
    k0[0, 0] = _rope(mm(h0, wn_ref, A_WIDTH, A_WIDTH), cos0, sin0, low_half).astype(BF16)
    v0[0, 0] = mm(h0, wn_ref, 2 * A_WIDTH, A_WIDTH).astype(BF16)
    c = 3 * A_WIDTH
    bq[0] = (_rope(mm(h0, wn_ref, c, B_Q_WIDTH), cos0, sin0, low_half) * Q_SCALE).astype(BF16)
    bk[0] = _rope(mm(h0, wn_ref, c + B_Q_WIDTH, B_KV_WIDTH), cos0, sin0, low_half).astype(BF16)
    bv[0] = mm(h0, wn_ref, c + B_Q_WIDTH + B_KV_WIDTH, B_KV_WIDTH).astype(BF16)

    for d, hs, cs, ss, w_ref, qo, ko, vo in ((DILATIONS[1], h1_s, c1_s, s1_s, w1_ref, q1, k1, v1),
                                             (DILATIONS[2], h2_s, c2_s, s2_s, w2_ref, q2, k2, v2)):
        n = tm // d
        for r in range(d):
            rows = pl.ds(r, n, stride=d)
            xr = jnp.concatenate([xs[c, rows, :] for c in range(n_chunks)], axis=1)
            hs[r * n:(r + 1) * n, :] = _rms(xr, g).astype(BF16)
            cs[r * n:(r + 1) * n, :] = cos_ref[rows, :]
            ss[r * n:(r + 1) * n, :] = sin_ref[rows, :]
        h = hs[...]
        cos, sin = cs[...], ss[...]
        qv = (_rope(mm(h, w_ref, 0, A_WIDTH), cos, sin, low_half) * Q_SCALE).astype(BF16)
        kv = _rope(mm(h, w_ref, A_WIDTH, A_WIDTH), cos, sin, low_half).astype(BF16)
        vv = mm(h, w_ref, 2 * A_WIDTH, A_WIDTH).astype(BF16)
        for r in range(d):
            qo[0, r] = qv[r * n:(r + 1) * n]
            ko[0, r] = kv[r * n:(r + 1) * n]
            vo[0, r] = vv[r * n:(r + 1) * n]


def _rope_tables(s):
    inv = jnp.power(jnp.float32(ROPE_THETA), -jnp.arange(ROT_HALF, dtype=F32) / ROT_HALF)
    ang = jnp.arange(s).astype(F32)[:, None] * inv[None, :]
    cos, sin = jnp.cos(ang), jnp.sin(ang)
    pad_one = jnp.ones((s, HEAD_DIM - ROT_DIM), F32)
    pad_zero = jnp.zeros((s, HEAD_DIM - ROT_DIM), F32)
    cos_h = jnp.concatenate([cos, cos, pad_one], axis=1)
    sin_h = jnp.concatenate([-sin, sin, pad_zero], axis=1)
    return jnp.concatenate([cos_h, cos_h], axis=1), jnp.concatenate([sin_h, sin_h], axis=1)


def _in_proj(x, norm_g, w_nat, w_g1, w_g2):
    b, s, _ = x.shape
    tm = IN_TILE
    cos_t, sin_t = _rope_tables(s)
    const = lambda t, bb: (0, 0)
    one_buf = pl.Buffered(1)
    in_specs = [
        pl.BlockSpec((1, tm, D_MODEL), lambda t, bb: (bb, t, 0)),
        pl.BlockSpec((1, D_MODEL), const),
        pl.BlockSpec((tm, LANES), lambda t, bb: (t, 0)),
        pl.BlockSpec((tm, LANES), lambda t, bb: (t, 0)),
        pl.BlockSpec(w_nat.shape, const, pipeline_mode=one_buf),
        pl.BlockSpec(w_g1.shape, const, pipeline_mode=one_buf),
        pl.BlockSpec(w_g2.shape, const, pipeline_mode=one_buf),
    ]
    out_shape, out_specs = [], []
    for d in DILATIONS:
        for _ in range(3):
            out_shape.append(jax.ShapeDtypeStruct((b, d, s // d, A_WIDTH), BF16))
            out_specs.append(pl.BlockSpec((1, d, tm // d, A_WIDTH), lambda t, bb: (bb, 0, t, 0)))
    for w in (B_Q_WIDTH, B_KV_WIDTH, B_KV_WIDTH):
        out_shape.append(jax.ShapeDtypeStruct((b, s, w), BF16))
        out_specs.append(pl.BlockSpec((1, tm, w), lambda t, bb: (bb, t, 0)))
    scratch = [pltpu.VMEM((D_MODEL // LANES, tm, LANES), F32),
               pltpu.VMEM((tm, D_MODEL), BF16), pltpu.VMEM((tm, D_MODEL), BF16)] + \
              [pltpu.VMEM((tm, LANES), F32) for _ in range(4)]
    return pl.pallas_call(
        _in_proj_kernel,
        grid=(s // tm, b),
        in_specs=in_specs,
        out_specs=out_specs,
        out_shape=out_shape,
        scratch_shapes=scratch,
        compiler_params=pltpu.CompilerParams(
            dimension_semantics=("arbitrary", "arbitrary"), vmem_limit_bytes=VMEM_LIMIT),
        name="in_proj",
    )(x, norm_g.reshape(1, D_MODEL), cos_t, sin_t, w_nat, w_g1, w_g2)


def _attend_heads(qb, kw, vw, valid, first_half):
    second_half = jnp.logical_not(first_half)
    scores = []
    for h in range(A_HEADS):
        cs = slice((h // 2) * LANES, (h // 2 + 1) * LANES)
        qp = qb[:, cs]
        qm = jnp.where(first_half if h % 2 == 0 else second_half, qp, jnp.zeros_like(qp))
        scores.append(lax.dot_general(qm, kw[:, cs], (((1,), (1,)), ((), ())), preferred_element_type=F32))
    sc = jnp.where(valid[None], jnp.stack(scores), NEG_BIG)
    m = jnp.max(sc, axis=-1, keepdims=True)
    p = jnp.exp(sc - m)
    l = jnp.sum(p, axis=-1, keepdims=True)
    pb = p.astype(BF16)
    out = []
    for hp in range(A_HEADS // 2):
        vp = vw[:, hp * LANES:(hp + 1) * LANES]
        o0 = jnp.dot(pb[2 * hp], vp, preferred_element_type=F32)
        o1 = jnp.dot(pb[2 * hp + 1], vp, preferred_element_type=F32)
        out.append((jnp.where(first_half, o0, o1), jnp.where(first_half, m[2 * hp], m[2 * hp + 1]),
                    jnp.where(first_half, l[2 * hp], l[2 * hp + 1])))
    return out


def _mixer_a_kernel(*refs, seq):
    ins, (o_ref, kext, vext, acc_o, acc_m, acc_l) = refs[:21], refs[21:]
    t = pl.program_id(1)
    tq = ATT_TILE
    halo = A_HALF

    for gi, d in enumerate(DILATIONS):
        q, kc, kp, kn, vc, vp, vn = ins[7 * gi:7 * gi + 7]
        n = tq // d
        blk = min(n, A_QBLOCK)
        nblk = n // blk
        win = -(-(blk + 2 * halo) // LANES) * LANES
        ld = seq // d
        col = lax.broadcasted_iota(jnp.int32, (blk, win), 1)
        diff = col - lax.broadcasted_iota(jnp.int32, (blk, win), 0)
        band = (diff >= 0) & (diff <= 2 * halo)
        first_half = lax.broadcasted_iota(jnp.int32, (blk, LANES), 1) < HEAD_DIM

        def residue_body(r, carry, q=q, kc=kc, kp=kp, kn=kn, vc=vc, vp=vp, vn=vn, d=d, n=n, nblk=nblk,
                         ld=ld, blk=blk, win=win, col=col, band=band, first_half=first_half, first=(gi == 0)):
            kext[0:halo, :] = kp[0, r]
            kext[halo:halo + n, :] = kc[0, r]
            kext[halo + n:2 * halo + n, :] = kn[0, r]
            vext[0:halo, :] = vp[0, r]
            vext[halo:halo + n, :] = vc[0, r]
            vext[halo + n:2 * halo + n, :] = vn[0, r]

            def block_body(j, carry2):
                j0 = pl.multiple_of(j * blk, blk)
                qb = q[0, r, pl.ds(j0, blk), :]
                kw = kext[pl.ds(j0, win), :]
                vw = vext[pl.ds(j0, win), :]
                qbase = t * n + j * blk
                valid = band & (col >= halo - qbase) & (col < ld - qbase + halo)
                parts = _attend_heads(qb, kw, vw, valid, first_half)
                if d == 1:
                    rows = pl.ds(j0, blk)
                else:
                    rows = pl.ds(j * (blk * d) + r, blk, stride=d)
                for hp, (o_b, m_b, l_b) in enumerate(parts):
                    if first:
                        acc_o[hp, rows, :] = o_b
                        acc_l[hp, rows, :] = l_b
                        acc_m[hp, rows, :] = m_b
                    else:
                        m_old = acc_m[hp, rows, :]
                        m_new = jnp.maximum(m_old, m_b)
                        a_old = jnp.exp(m_old - m_new)
                        a_new = jnp.exp(m_b - m_new)
                        acc_o[hp, rows, :] = acc_o[hp, rows, :] * a_old + o_b * a_new
                        acc_l[hp, rows, :] = acc_l[hp, rows, :] * a_old + l_b * a_new
                        acc_m[hp, rows, :] = m_new
                return carry2

            lax.fori_loop(0, nblk, block_body, 0)
            return carry

        lax.fori_loop(0, d, residue_body, 0)

    o_ref[0] = jnp.concatenate([acc_o[hp] / acc_l[hp] for hp in range(A_HEADS // 2)], axis=1).astype(BF16)


def _mixer_a(qkv, seq):
    b = qkv[0].shape[0]
    tq = ATT_TILE
    blk = A_BLOCK
    args, in_specs = [], []
    for gi, d in enumerate(DILATIONS):
        q, k, v = qkv[3 * gi:3 * gi + 3]
        n = tq // d
        nblk = n // blk
        last = seq // d // blk - 1
        cur = pl.BlockSpec((1, d, n, A_WIDTH), lambda bb, t: (bb, 0, t, 0))
        prev = pl.BlockSpec((1, d, blk, A_WIDTH),
                            lambda bb, t, nblk=nblk: (bb, 0, jnp.maximum(t * nblk - 1, 0), 0))
        nxt = pl.BlockSpec((1, d, blk, A_WIDTH),
                           lambda bb, t, nblk=nblk, last=last: (bb, 0, jnp.minimum((t + 1) * nblk, last), 0))
        args += [q, k, k, k, v, v, v]
        in_specs += [cur, cur, prev, nxt, cur, prev, nxt]
    return pl.pallas_call(
        functools.partial(_mixer_a_kernel, seq=seq),
        grid=(b, seq // tq),
        in_specs=in_specs,
        out_specs=pl.BlockSpec((1, tq, A_WIDTH), lambda bb, t: (bb, t, 0)),
        out_shape=jax.ShapeDtypeStruct((b, seq, A_WIDTH), BF16),
        scratch_shapes=[pltpu.VMEM((tq + 2 * blk, A_WIDTH), BF16), pltpu.VMEM((tq + 2 * blk, A_WIDTH), BF16),
                        pltpu.VMEM((A_WIDTH // LANES, tq, LANES), F32),
                        pltpu.VMEM((A_WIDTH // LANES, tq, LANES), F32),
                        pltpu.VMEM((A_WIDTH // LANES, tq, LANES), F32)],
        compiler_params=pltpu.CompilerParams(
            dimension_semantics=("arbitrary", "arbitrary"), vmem_limit_bytes=VMEM_LIMIT),
        name="mixer_a",
    )(*args)


def _mixer_b_kernel(sink_ref, q, kc, kp, kn, vc, vp, vn, o_ref, kext, vext, *, seq):
    t = pl.program_id(1)
    tq = ATT_TILE
    blk = B_BLOCK
    win = 3 * blk
    kext[0:blk, :] = kp[0]
    kext[blk:blk + tq, :] = kc[0]
    kext[blk + tq:2 * blk + tq, :] = kn[0]
    vext[0:blk, :] = vp[0]
    vext[blk:blk + tq, :] = vc[0]
    vext[blk + tq:2 * blk + tq, :] = vn[0]
    col = lax.broadcasted_iota(jnp.int32, (blk, win), 1)
    row = lax.broadcasted_iota(jnp.int32, (blk, win), 0)
    diff = col - row
    first_half = lax.broadcasted_iota(jnp.int32, (blk, LANES), 1) < HEAD_DIM
    second_half = jnp.logical_not(first_half)

    def block_body(j, carry):
        j0 = pl.multiple_of(j * blk, blk)
        qb = q[0, pl.ds(j0, blk), :]
        kw = kext[pl.ds(j0, win), :]
        vw = vext[pl.ds(j0, win), :]
        qbase = t * tq + j * blk
        valid = ((diff >= 0) & (diff <= 2 * B_WINDOW)
                 & (col >= blk - qbase) & (col < seq - qbase + blk))
        scores, sinks = [], []
        for c in range(B_Q_HEADS // 2):
            qp = qb[:, c * LANES:(c + 1) * LANES]
            for g, keep in enumerate((first_half, second_half)):
                qm = jnp.where(keep, qp, jnp.zeros_like(qp))
                scores.append(lax.dot_general(qm, kw, (((1,), (1,)), ((), ())), preferred_element_type=F32))
                sinks.append(jnp.full((1, LANES), sink_ref[g * B_REP + c], F32))
        sc = jnp.where(valid[None], jnp.stack(scores), NEG_BIG)
        sk = jnp.stack(sinks)
        part = sc[..., 0:LANES]
        for k in range(1, win // LANES):
            part = jnp.maximum(part, sc[..., k * LANES:(k + 1) * LANES])
        m = jnp.max(jnp.maximum(part, sk), axis=-1, keepdims=True)
        p = jnp.exp(sc - m)
        denom = jnp.sum(p, axis=-1, keepdims=True) + jnp.exp(sk - m)
        pb = p.astype(BF16)
        parts = []
        for c in range(B_Q_HEADS // 2):
            o0 = jnp.dot(pb[2 * c], vw, preferred_element_type=F32) / denom[2 * c]
            o1 = jnp.dot(pb[2 * c + 1], vw, preferred_element_type=F32) / denom[2 * c + 1]
            parts.append(jnp.where(first_half, o0, o1))
        o_ref[0, pl.ds(j0, blk), :] = jnp.concatenate(parts, axis=1).astype(BF16)
        return carry

    lax.fori_loop(0, tq // blk, block_body, 0)


def _mixer_b(q, k, v, sink, seq):
    b = q.shape[0]
    tq = ATT_TILE
    blk = B_BLOCK
    nblk = tq // blk
    last = seq // blk - 1
    cur = pl.BlockSpec((1, tq, B_KV_WIDTH), lambda bb, t: (bb, t, 0))
    prev = pl.BlockSpec((1, blk, B_KV_WIDTH), lambda bb, t: (bb, jnp.maximum(t * nblk - 1, 0), 0))
    nxt = pl.BlockSpec((1, blk, B_KV_WIDTH), lambda bb, t: (bb, jnp.minimum((t + 1) * nblk, last), 0))
    return pl.pallas_call(
        functools.partial(_mixer_b_kernel, seq=seq),
        grid=(b, seq // tq),
        in_specs=[pl.BlockSpec(memory_space=pltpu.SMEM),
                  pl.BlockSpec((1, tq, B_Q_WIDTH), lambda bb, t: (bb, t, 0)),
                  cur, prev, nxt, cur, prev, nxt],
        out_specs=pl.BlockSpec((1, tq, B_Q_WIDTH), lambda bb, t: (bb, t, 0)),
        out_shape=jax.ShapeDtypeStruct((b, seq, B_Q_WIDTH), BF16),
        scratch_shapes=[pltpu.VMEM((tq + 2 * blk, B_KV_WIDTH), BF16),
                        pltpu.VMEM((tq + 2 * blk, B_KV_WIDTH), BF16)],
        compiler_params=pltpu.CompilerParams(
            dimension_semantics=("arbitrary", "arbitrary"), vmem_limit_bytes=VMEM_LIMIT),
        name="mixer_b",
    )(sink, q, k, k, k, v, v, v)


def _post_attn_kernel(x_ref, oa_ref, ob_ref, gmix_ref, wg_ref, wa_ref, wb_ref, wo_ref, gmoe_ref,
                      wr_ref, br_ref, x1_ref, h2_ref, gate_ref):
    x = x_ref[...]
    h = _rms(x, gmix_ref[...]).astype(BF16)
    gates = jax.nn.sigmoid(jnp.dot(h, wg_ref[...], preferred_element_type=F32))
    ya = jnp.dot(oa_ref[...], wa_ref[...], preferred_element_type=F32)
    yb = jnp.dot(ob_ref[...], wb_ref[...], preferred_element_type=F32)
    mix = gates[:, :D_MODEL] * ya + gates[:, D_MODEL:] * yb
    x1 = x + jnp.dot(mix.astype(BF16), wo_ref[...], preferred_element_type=F32)
    x1_ref[...] = x1
    h2 = _rms(x1, gmoe_ref[...])
    h2_hi = h2.astype(BF16)
    h2_ref[...] = h2_hi

    h2_lo = (h2 - h2_hi.astype(F32)).astype(BF16)
    logits = jnp.dot(jnp.concatenate([h2_hi, h2_lo, h2_hi], axis=1), wr_ref[...],
                     preferred_element_type=F32) + br_ref[...]
    tm = logits.shape[0]
    lane = lax.broadcasted_iota(jnp.int32, (tm, LANES), 1)
    big = jnp.int32(LANES)
    is_group = (lane >= N_EXPERTS) & (lane < N_EXPERTS + N_GROUPS)
    gl = jnp.where(is_group, logits, NEG_BIG)
    gmax = jnp.max(gl, axis=-1, keepdims=True)
    g_idx = jnp.min(jnp.where(gl == gmax, lane, big), axis=-1, keepdims=True) - N_EXPERTS
    g_w = 1.0 / jnp.sum(jnp.exp(gl - gmax), axis=-1, keepdims=True)
    in_group = (lane < N_EXPERTS) & (lax.shift_right_logical(lane, 3) == g_idx)
    sel = jnp.where(in_group, logits, NEG_BIG)
    v1 = jnp.max(sel, axis=-1, keepdims=True)
    i1 = jnp.min(jnp.where(sel == v1, lane, big), axis=-1, keepdims=True)
    sel2 = jnp.where(lane == i1, NEG_BIG, sel)
    v2 = jnp.max(sel2, axis=-1, keepdims=True)
    i2 = jnp.min(jnp.where(sel2 == v2, lane, big), axis=-1, keepdims=True)
    e2 = jnp.exp(v2 - v1)
    w1 = g_w / (1.0 + e2)
    w2 = g_w * e2 / (1.0 + e2)
    gate_ref[...] = jnp.where(lane == i1, w1, 0.0) + jnp.where(lane == i2, w2, 0.0)


def _post_attn(x2d, oa, ob, norm_mix, w_gate, w_a, w_b, w_out, norm_moe, w_router, b_router):
    t = x2d.shape[0]
    tm = POST_TILE
    const = lambda i: (0, 0)
    row = lambda w: pl.BlockSpec((tm, w), lambda i: (i, 0))
    full = lambda a: pl.BlockSpec(a.shape, const, pipeline_mode=pl.Buffered(1))
    return pl.pallas_call(
        _post_attn_kernel,
        grid=(t // tm,),
        in_specs=[row(D_MODEL), row(A_WIDTH), row(B_Q_WIDTH), full(norm_mix), full(w_gate), full(w_a),
                  full(w_b), full(w_out), full(norm_moe), full(w_router), full(b_router)],
        out_specs=[row(D_MODEL), row(D_MODEL), row(LANES)],
        out_shape=[jax.ShapeDtypeStruct((t, D_MODEL), F32), jax.ShapeDtypeStruct((t, D_MODEL), BF16),
                   jax.ShapeDtypeStruct((t, LANES), F32)],
        compiler_params=pltpu.CompilerParams(
            dimension_semantics=("arbitrary",), vmem_limit_bytes=VMEM_LIMIT),
        name="post_attn",
    )(x2d, oa, ob, norm_mix, w_gate, w_a, w_b, w_out, norm_moe, w_router, b_router)


def _moe_final_kernel(x1_ref, h2_ref, gate_ref, tri_ref, w1_ref, w3_ref, w2_ref, gf_ref, o_ref,
                      xs, stage, seg_s, slo_s, shi_s, wlo_s, whi_s):
    step = pl.program_id(1)
    first_step = (pl.program_id(0) == 0) & (step == 0)
    n_sub = x1_ref.shape[0] // MOE_SUB

    @pl.when(first_step)
    def _init():
        xs[n_sub * MOE_SUB_SLOTS:, :] = jnp.zeros((MOE_CHUNK, D_MODEL), BF16)
        stage[...] = jnp.zeros(stage.shape, BF16)

    @pl.when(step == 0)
    def _sort():
        r_i = lax.broadcasted_iota(jnp.int32, (LANES, LANES), 0)
        c_i = lax.broadcasted_iota(jnp.int32, (LANES, LANES), 1)
        upper = jnp.where(r_i < c_i, 1.0, 0.0)
        col = lax.broadcasted_iota(jnp.int32, (MOE_SUB, MOE_SUB_SLOTS), 1)
        for s in range(n_sub):
            rows = slice(s * MOE_SUB, (s + 1) * MOE_SUB)
            g = gate_ref[rows, :]
            hit = g > 0.0
            onehot = jnp.where(hit, 1.0, 0.0)
            count = jnp.sum(onehot, axis=0, keepdims=True)
            padded = jnp.ceil(count * (1.0 / SEG_ALIGN)) * SEG_ALIGN
            seg = jnp.dot(jnp.broadcast_to(padded, (8, LANES)), upper, preferred_element_type=F32,
                          precision=lax.Precision.HIGHEST)[0:1]
            seg_i = seg.astype(jnp.int32)
            padded_i = padded.astype(jnp.int32)
            for ex in range(N_EXPERTS):
                seg_s[2 * s, ex] = seg_i[0, ex] + s * MOE_SUB_SLOTS
                seg_s[2 * s + 1, ex] = padded_i[0, ex]
            rank = jnp.dot(tri_ref[...], onehot.astype(BF16), preferred_element_type=F32)
            slot = seg + rank
            s_lo = jnp.min(jnp.where(hit, slot, 1e9), axis=-1, keepdims=True)
            s_hi = jnp.max(jnp.where(hit, slot, -1.0), axis=-1, keepdims=True)
            wlo_s[rows, :] = jnp.sum(jnp.where(hit & (slot == s_lo), g, 0.0), axis=-1, keepdims=True)
            whi_s[rows, :] = jnp.sum(jnp.where(hit & (slot == s_hi) & (s_hi > s_lo), g, 0.0),
                                     axis=-1, keepdims=True)
            s_lo_i = s_lo.astype(jnp.int32)
            s_hi_i = s_hi.astype(jnp.int32)
            slo_s[rows, :] = s_lo_i
            shi_s[rows, :] = s_hi_i
            perm_t = jnp.where((col == s_lo_i) | (col == s_hi_i), 1.0, 0.0).astype(BF16)
            sorted_rows = lax.dot_general(perm_t, h2_ref[rows, :], (((0,), (0,)), ((), ())),
                                          preferred_element_type=F32)
            xs[s * MOE_SUB_SLOTS:(s + 1) * MOE_SUB_SLOTS, :] = sorted_rows.astype(BF16)

    for j in range(MOE_ESTEP):
        e = step * MOE_ESTEP + j
        starts = [seg_s[2 * s, e] for s in range(n_sub)]
        lens = [seg_s[2 * s + 1, e] for s in range(n_sub)]

        dst = jnp.int32(0)
        dsts = []
        for s in range(n_sub):
            dsts.append(dst)

            def gather_body(k, carry, src0=starts[s], dst0=dst):
                src = pl.multiple_of(src0 + k * MOE_CHUNK, SEG_ALIGN)
                to = pl.multiple_of(dst0 + k * MOE_CHUNK, SEG_ALIGN)
                stage[pl.ds(to, MOE_CHUNK), :] = xs[pl.ds(src, MOE_CHUNK), :]
                return carry

            lax.fori_loop(0, (lens[s] + MOE_CHUNK - 1) // MOE_CHUNK, gather_body, 0)
            dst = dst + lens[s]
        total = dst

        def chunk_body(k, carry, j=j):
            r0 = pl.multiple_of(k * MOE_CHUNK, MOE_CHUNK)
            x = stage[pl.ds(r0, MOE_CHUNK), :]
            a = jnp.dot(x, w1_ref[j], preferred_element_type=F32)
            b = jnp.dot(x, w3_ref[j], preferred_element_type=F32)
            hid = (a * jax.nn.sigmoid(a)) * b
            stage[pl.ds(r0, MOE_CHUNK), :] = jnp.dot(hid.astype(BF16), w2_ref[j],
                                                     preferred_element_type=F32).astype(BF16)
            return carry

        lax.fori_loop(0, (total + MOE_CHUNK - 1) // MOE_CHUNK, chunk_body, 0)

        for s in range(n_sub):
            def scatter_body(k, carry, src0=dsts[s], dst0=starts[s]):
                src = pl.multiple_of(src0 + k * SEG_ALIGN, SEG_ALIGN)
                to = pl.multiple_of(dst0 + k * SEG_ALIGN, SEG_ALIGN)
                xs[pl.ds(to, SEG_ALIGN), :] = stage[pl.ds(src, SEG_ALIGN), :]
                return carry

            lax.fori_loop(0, lens[s] // SEG_ALIGN, scatter_body, 0)

    @pl.when(step == N_EXPERTS // MOE_ESTEP - 1)
    def _combine():
        col = lax.broadcasted_iota(jnp.int32, (MOE_SUB, MOE_SUB_SLOTS), 1)
        for s in range(n_sub):
            rows = slice(s * MOE_SUB, (s + 1) * MOE_SUB)
            wperm = (jnp.where(col == slo_s[rows, :], wlo_s[rows, :], 0.0)
                     + jnp.where(col == shi_s[rows, :], whi_s[rows, :], 0.0)).astype(BF16)
            moe = jnp.dot(wperm, xs[s * MOE_SUB_SLOTS:(s + 1) * MOE_SUB_SLOTS, :], preferred_element_type=F32)
            o_ref[rows, :] = _rms(x1_ref[rows, :] + moe, gf_ref[...])


def _moe_final(x1, h2, gate, w1, w3, w2, norm_final):
    t = x1.shape[0]
    tm = MOE_TILE
    n_sub = tm // MOE_SUB
    row = lambda w: pl.BlockSpec((tm, w), lambda i, e: (i, 0))
    tri = jnp.tril(jnp.ones((MOE_SUB, MOE_SUB), BF16), -1)
    return pl.pallas_call(
        _moe_final_kernel,
        grid=(t // tm, N_EXPERTS // MOE_ESTEP),
        in_specs=[row(D_MODEL), row(D_MODEL), row(LANES),
                  pl.BlockSpec((MOE_SUB, MOE_SUB), lambda i, e: (0, 0), pipeline_mode=pl.Buffered(1)),
                  pl.BlockSpec((MOE_ESTEP, D_MODEL, D_FF), lambda i, e: (e, 0, 0)),
                  pl.BlockSpec((MOE_ESTEP, D_MODEL, D_FF), lambda i, e: (e, 0, 0)),
                  pl.BlockSpec((MOE_ESTEP, D_FF, D_MODEL), lambda i, e: (e, 0, 0)),
                  pl.BlockSpec((1, D_MODEL), lambda i, e: (0, 0))],
        out_specs=row(D_MODEL),
        out_shape=jax.ShapeDtypeStruct((t, D_MODEL), F32),
        scratch_shapes=[pltpu.VMEM((n_sub * MOE_SUB_SLOTS + MOE_CHUNK, D_MODEL), BF16),
                        pltpu.VMEM((2 * tm + 2 * MOE_CHUNK, D_MODEL), BF16),
                        pltpu.SMEM((2 * n_sub, N_EXPERTS), jnp.int32),
                        pltpu.VMEM((tm, 1), jnp.int32), pltpu.VMEM((tm, 1), jnp.int32),
                        pltpu.VMEM((tm, 1), F32), pltpu.VMEM((tm, 1), F32)],
        compiler_params=pltpu.CompilerParams(
            dimension_semantics=("arbitrary", "arbitrary"), vmem_limit_bytes=VMEM_LIMIT),
        name="moe_final",
    )(x1, h2, gate, tri, w1, w3, w2, norm_final)


def _prep_weights(norm_mix, w_in, attn_sink, w_branch_a, w_branch_b, w_gate, w_out, norm_moe,
                  w_router_group, b_router_group, w_router_expert, b_router_expert, w1, w3, w2, norm_final):
    wi = w_in[0]
    a_w = 3 * A_WIDTH

    def group_cols(gi):
        return jnp.concatenate([wi[:, role * a_w + gi * A_WIDTH: role * a_w + (gi + 1) * A_WIDTH]
                                for role in range(3)], axis=1)

    b_heads = [g * B_REP + c for c in range(B_REP) for g in range(B_KV_HEADS)]
    bq_cols = jnp.concatenate([wi[:, 3 * a_w + h * HEAD_DIM: 3 * a_w + (h + 1) * HEAD_DIM] for h in b_heads],
                              axis=1)
    w_b_rows = jnp.concatenate([w_branch_b[0][h * HEAD_DIM:(h + 1) * HEAD_DIM] for h in b_heads], axis=0)
    w_nat = jnp.concatenate([group_cols(0), bq_cols, wi[:, 3 * a_w + B_Q_WIDTH:]], axis=1).astype(BF16)
    w_g1 = group_cols(1).astype(BF16)
    w_g2 = group_cols(2).astype(BF16)
    pad = LANES - N_EXPERTS - N_GROUPS
    w_router = jnp.concatenate([w_router_expert[0], w_router_group[0],
                                jnp.zeros((D_MODEL, pad), F32)], axis=1)
    wr_hi = w_router.astype(BF16)
    wr_lo = (w_router - wr_hi.astype(F32)).astype(BF16)
    w_router = jnp.concatenate([wr_hi, wr_hi, wr_lo], axis=0)
    b_router = jnp.concatenate([b_router_expert[0].reshape(-1), b_router_group[0],
                                jnp.zeros((pad,), F32)]).reshape(1, LANES)
    return dict(
        norm_mix=norm_mix[0].reshape(1, D_MODEL), w_nat=w_nat, w_g1=w_g1, w_g2=w_g2, sink=attn_sink[0],
        w_a=w_branch_a[0].astype(BF16), w_b=w_b_rows.astype(BF16), w_gate=w_gate[0].astype(BF16),
        w_out=w_out[0].astype(BF16), norm_moe=norm_moe[0].reshape(1, D_MODEL),
        w_router=w_router, b_router=b_router,
        w1=w1[0].astype(BF16), w3=w3[0].astype(BF16), w2=w2[0].astype(BF16),
        norm_final=norm_final.reshape(1, D_MODEL))


def _trunk(x, p):
    b, s, _ = x.shape
    outs = _in_proj(x, p["norm_mix"], p["w_nat"], p["w_g1"], p["w_g2"])
    oa = _mixer_a(outs[:9], s)
    ob = _mixer_b(outs[9], outs[10], outs[11], p["sink"], s)
    t = b * s
    x1, h2, gate = _post_attn(x.reshape(t, D_MODEL), oa.reshape(t, A_WIDTH), ob.reshape(t, B_Q_WIDTH),
                              p["norm_mix"], p["w_gate"], p["w_a"], p["w_b"], p["w_out"], p["norm_moe"],
                              p["w_router"], p["b_router"])
    y = _moe_final(x1, h2, gate, p["w1"], p["w3"], p["w2"], p["norm_final"])
    return y.reshape(b, s, D_MODEL)


def kernel(x_prompt, x_sample, norm_mix, w_in, attn_sink, w_branch_a, w_branch_b, w_gate, w_out, norm_moe,
           w_router_group, b_router_group, w_router_expert, b_router_expert, w1, w3, w2, norm_final):
    p = _prep_weights(norm_mix, w_in, attn_sink, w_branch_a, w_branch_b, w_gate, w_out, norm_moe,
                      w_router_group, b_router_group, w_router_expert, b_router_expert, w1, w3, w2, norm_final)
    return (_trunk(x_prompt, p), _trunk(x_sample, p))
```

```python
import functools

import jax
import jax.numpy as jnp
from jax import lax
from jax.experimental import pallas as pl
from jax.experimental.pallas import tpu as pltpu

D_MODEL = 1024
HEAD_DIM = 64
ROT_DIM = HEAD_DIM // 4
ROT_HALF = ROT_DIM // 2
ROPE_THETA = 500000.0
DILATIONS = (1, 4, 16)
A_HALF = 64
A_BLOCK = 64
A_QBLOCK = 128
A_HEADS = 8
A_WIDTH = A_HEADS * HEAD_DIM
B_Q_HEADS = 8
B_KV_HEADS = 2
B_REP = B_Q_HEADS // B_KV_HEADS
B_WINDOW = 128
B_BLOCK = 128
B_Q_WIDTH = B_Q_HEADS * HEAD_DIM
B_KV_WIDTH = B_KV_HEADS * HEAD_DIM
N_GROUPS = 4
EXPERTS_PER_GROUP = 8
N_EXPERTS = N_GROUPS * EXPERTS_PER_GROUP
D_FF = 256
RMS_EPS = 1e-6
Q_SCALE = HEAD_DIM ** -0.5

LANES = 128
NEG_BIG = -1e30
VMEM_LIMIT = 56 * 1024 * 1024

IN_TILE = 256
ATT_TILE = 1024
A_TILE = 1024
A_STEP_WIDTH = 512
POST_TILE = 512
MOE_TILE = 1024
SEG_ALIGN = 16
MOE_CHUNK = 128
MOE_SUB = 256
MOE_SUB_SLOTS = 2 * MOE_SUB + N_EXPERTS * SEG_ALIGN
MOE_ESTEP = 4

F32 = jnp.float32
BF16 = jnp.bfloat16


def _rms(x, g):
    ms = jnp.mean(x * x, axis=-1, keepdims=True)
    return x * lax.rsqrt(ms + RMS_EPS) * g


def _rope(a, cos, sin, low_half):
    outs = []
    for k in range(a.shape[1] // LANES):
        ak = a[:, k * LANES:(k + 1) * LANES]
        up = pltpu.roll(ak, LANES - ROT_HALF, axis=1)
        dn = pltpu.roll(ak, ROT_HALF, axis=1)
        outs.append(ak * cos + jnp.where(low_half, up, dn) * sin)
    return outs[0] if len(outs) == 1 else jnp.concatenate(outs, axis=1)


def _in_proj_kernel(x_ref, g_ref, cos_ref, sin_ref, wn_ref, w1_ref, w2_ref,
                    q0, k0, v0, q1, k1, v1, q2, k2, v2, bq, bk, bv,
                    xs, h1_s, h2_s, c1_s, s1_s, c2_s, s2_s):
    tm = x_ref.shape[1]
    n_chunks = D_MODEL // LANES
    for c in range(n_chunks):
        xs[c] = x_ref[0, :, c * LANES:(c + 1) * LANES]
    g = g_ref[...]
    low_half = (lax.broadcasted_iota(jnp.int32, (tm, LANES), 1) % HEAD_DIM) < ROT_HALF

    def mm(h, w_ref, c0, n):
        return jnp.dot(h, w_ref[:, c0:c0 + n], preferred_element_type=F32)

    h0 = _rms(x_ref[0], g).astype(BF16)
    cos0, sin0 = cos_ref[...], sin_ref[...]
    q0[0, 0] = (_rope(mm(h0, wn_ref, 0, A_WIDTH), cos0, sin0, low_half) * Q_SCALE).astype(BF16)
    k0[0, 0] = _rope(mm(h0, wn_ref, A_WIDTH, A_WIDTH), cos0, sin0, low_half).astype(BF16)
    v0[0, 0] = mm(h0, wn_ref, 2 * A_WIDTH, A_WIDTH).astype(BF16)
    c = 3 * A_WIDTH
    bq[0] = (_rope(mm(h0, wn_ref, c, B_Q_WIDTH), cos0, sin0, low_half) * Q_SCALE).astype(BF16)
    bk[0] = _rope(mm(h0, wn_ref, c + B_Q_WIDTH, B_KV_WIDTH), cos0, sin0, low_half).astype(BF16)
    bv[0] = mm(h0, wn_ref, c + B_Q_WIDTH + B_KV_WIDTH, B_KV_WIDTH).astype(BF16)

    for d, hs, cs, ss, w_ref, qo, ko, vo in ((DILATIONS[1], h1_s, c1_s, s1_s, w1_ref, q1, k1, v1),
                                             (DILATIONS[2], h2_s, c2_s, s2_s, w2_ref, q2, k2, v2)):
        n = tm // d
        for r in range(d):
            rows = pl.ds(r, n, stride=d)
            xr = jnp.concatenate([xs[c, rows, :] for c in range(n_chunks)], axis=1)
            hs[r * n:(r + 1) * n, :] = _rms(xr, g).astype(BF16)
            cs[r * n:(r + 1) * n, :] = cos_ref[rows, :]
            ss[r * n:(r + 1) * n, :] = sin_ref[rows, :]
        h = hs[...]
        cos, sin = cs[...], ss[...]
        qv = (_rope(mm(h, w_ref, 0, A_WIDTH), cos, sin, low_half) * Q_SCALE).astype(BF16)
        kv = _rope(mm(h, w_ref, A_WIDTH, A_WIDTH), cos, sin, low_half).astype(BF16)
        vv = mm(h, w_ref, 2 * A_WIDTH, A_WIDTH).astype(BF16)
        for r in range(d):
            qo[0, r] = qv[r * n:(r + 1) * n]
            ko[0, r] = kv[r * n:(r + 1) * n]
            vo[0, r] = vv[r * n:(r + 1) * n]


def _rope_tables(s):
    inv = jnp.power(jnp.float32(ROPE_THETA), -jnp.arange(ROT_HALF, dtype=F32) / ROT_HALF)
    ang = jnp.arange(s).astype(F32)[:, None] * inv[None, :]
    cos, sin = jnp.cos(ang), jnp.sin(ang)
    pad_one = jnp.ones((s, HEAD_DIM - ROT_DIM), F32)
    pad_zero = jnp.zeros((s, HEAD_DIM - ROT_DIM), F32)
    cos_h = jnp.concatenate([cos, cos, pad_one], axis=1)
    sin_h = jnp.concatenate([-sin, sin, pad_zero], axis=1)
    return jnp.concatenate([cos_h, cos_h], axis=1), jnp.concatenate([sin_h, sin_h], axis=1)


def _in_proj(x, norm_g, w_nat, w_g1, w_g2):
    b, s, _ = x.shape
    tm = IN_TILE
    cos_t, sin_t = _rope_tables(s)
    const = lambda t, bb: (0, 0)
    one_buf = pl.Buffered(1)
    in_specs = [
        pl.BlockSpec((1, tm, D_MODEL), lambda t, bb: (bb, t, 0)),
        pl.BlockSpec((1, D_MODEL), const),
        pl.BlockSpec((tm, LANES), lambda t, bb: (t, 0)),
        pl.BlockSpec((tm, LANES), lambda t, bb: (t, 0)),
        pl.BlockSpec(w_nat.shape, const, pipeline_mode=one_buf),
        pl.BlockSpec(w_g1.shape, const, pipeline_mode=one_buf),
        pl.BlockSpec(w_g2.shape, const, pipeline_mode=one_buf),
    ]
    out_shape, out_specs = [], []
    for d in DILATIONS:
        for _ in range(3):
            out_shape.append(jax.ShapeDtypeStruct((b, d, s // d, A_WIDTH), BF16))
            out_specs.append(pl.BlockSpec((1, d, tm // d, A_WIDTH), lambda t, bb: (bb, 0, t, 0)))
    for w in (B_Q_WIDTH, B_KV_WIDTH, B_KV_WIDTH):
        out_shape.append(jax.ShapeDtypeStruct((b, s, w), BF16))
        out_specs.append(pl.BlockSpec((1, tm, w), lambda t, bb: (bb, t, 0)))
    scratch = [pltpu.VMEM((D_MODEL // LANES, tm, LANES), F32),
               pltpu.VMEM((tm, D_MODEL), BF16), pltpu.VMEM((tm, D_MODEL), BF16)] + \
              [pltpu.VMEM((tm, LANES), F32) for _ in range(4)]
    return pl.pallas_call(
        _in_proj_kernel,
        grid=(s // tm, b),
        in_specs=in_specs,
        out_specs=out_specs,
        out_shape=out_shape,
        scratch_shapes=scratch,
        compiler_params=pltpu.CompilerParams(
            dimension_semantics=("arbitrary", "arbitrary"), vmem_limit_bytes=VMEM_LIMIT),
        name="in_proj",
    )(x, norm_g.reshape(1, D_MODEL), cos_t, sin_t, w_nat, w_g1, w_g2)


def _attend_heads(blocks, valid, first_half):
    second_half = jnp.logical_not(first_half)
    n_heads = blocks[0][0].shape[1] // HEAD_DIM
    scores = []
    for qb, kw, _ in blocks:
        for h in range(n_heads):
            cs = slice((h // 2) * LANES, (h // 2 + 1) * LANES)
            qp = qb[:, cs]
            qm = jnp.where(first_half if h % 2 == 0 else second_half, qp, jnp.zeros_like(qp))
            scores.append(lax.dot_general(qm, kw[:, cs], (((1,), (1,)), ((), ())),
                                          preferred_element_type=F32))
    sc = jnp.where(valid[None], jnp.stack(scores), NEG_BIG)
    m = jnp.max(sc, axis=-1, keepdims=True)
    p = jnp.exp(sc - m)
    l = jnp.sum(p, axis=-1, keepdims=True)
    pb = p.astype(BF16)
    results = []
    for bi, (_, _, vw) in enumerate(blocks):
        out = []
        for hp in range(n_heads // 2):
            i0 = bi * n_heads + 2 * hp
            vp = vw[:, hp * LANES:(hp + 1) * LANES]
            o0 = jnp.dot(pb[i0], vp, preferred_element_type=F32)
            o1 = jnp.dot(pb[i0 + 1], vp, preferred_element_type=F32)
            out.append((jnp.where(first_half, o0, o1), jnp.where(first_half, m[i0], m[i0 + 1]),
                        jnp.where(first_half, l[i0], l[i0 + 1])))
        results.append(out)
    return results


def _mixer_a_kernel(*refs, seq):
    ins, (o_ref, kext, vext, acc_o, acc_m, acc_l) = refs[:21], refs[21:]
    t = pl.program_id(1)
    tq = A_TILE
    halo = A_HALF

    first_group = len(DILATIONS) - 1
    for gi in reversed(range(len(DILATIONS))):
        d = DILATIONS[gi]
        q, kc, kp, kn, vc, vp, vn = ins[7 * gi:7 * gi + 7]
        n = tq // d
        blk = min(n, A_QBLOCK)
        nblk = n // blk
        win = -(-(blk + 2 * halo) // LANES) * LANES
        tail = win - (blk + 2 * halo) if n == blk else 0
        ld = seq // d
        col = lax.broadcasted_iota(jnp.int32, (blk, win), 1)
        diff = col - lax.broadcasted_iota(jnp.int32, (blk, win), 0)
        band = (diff >= 0) & (diff <= 2 * halo)
        first_half = lax.broadcasted_iota(jnp.int32, (blk, LANES), 1) < HEAD_DIM

        pair = A_QBLOCK // blk

        def residue_body(rp, carry, q=q, kc=kc, kp=kp, kn=kn, vc=vc, vp=vp, vn=vn, d=d, n=n, nblk=nblk,
                         ld=ld, blk=blk, win=win, col=col, band=band, first_half=first_half,
                         first=(gi == first_group), pair=pair, tail=tail):
            for u in range(pair):
                r = rp * pair + u
                base = u * win
                kext[base:base + halo, :] = kp[0, r]
                kext[base + halo:base + halo + n, :] = kc[0, r]
                kext[base + halo + n:base + 2 * halo + n, :] = kn[0, r]
                vext[base:base + halo, :] = vp[0, r]
                vext[base + halo:base + halo + n, :] = vc[0, r]
                vext[base + halo + n:base + 2 * halo + n, :] = vn[0, r]
                if tail:
                    kext[base + 2 * halo + n:base + win, :] = jnp.zeros((tail, kext.shape[1]), BF16)
                    vext[base + 2 * halo + n:base + win, :] = jnp.zeros((tail, vext.shape[1]), BF16)

            def block_body(j, carry2):
                j0 = pl.multiple_of(j * blk, blk)
                qbase = t * n + j * blk
                valid = band & (col >= halo - qbase) & (col < ld - qbase + halo)
                blocks = [(q[0, rp * pair + u, pl.ds(j0, blk), :],
                           kext[pl.ds(u * win + j0, win), :],
                           vext[pl.ds(u * win + j0, win), :]) for u in range(pair)]
                results = _attend_heads(blocks, valid, first_half)
                for u, parts in enumerate(results):
                    if d == 1:
                        rows = pl.ds(j0, blk)
                    else:
                        rows = pl.ds(j * (blk * d) + rp * pair + u, blk, stride=d)
                    for hp, (o_b, m_b, l_b) in enumerate(parts):
                        if first:
                            acc_o[hp, rows, :] = o_b
                            acc_l[hp, rows, :] = l_b
                            acc_m[hp, rows, :] = m_b
                        else:
                            m_old = acc_m[hp, rows, :]
                            m_new = jnp.maximum(m_old, m_b)
                            a_old = jnp.exp(m_old - m_new)
                            a_new = jnp.exp(m_b - m_new)
                            acc_o[hp, rows, :] = acc_o[hp, rows, :] * a_old + o_b * a_new
                            acc_l[hp, rows, :] = acc_l[hp, rows, :] * a_old + l_b * a_new
                            acc_m[hp, rows, :] = m_new
                return carry2

            lax.fori_loop(0, nblk, block_body, 0)
            return carry

        lax.fori_loop(0, d // pair, residue_body, 0)

    o_ref[0] = jnp.concatenate([acc_o[hp] / acc_l[hp] for hp in range(A_STEP_WIDTH // LANES)],
                               axis=1).astype(BF16)


def _mixer_a(qkv, seq):
    b = qkv[0].shape[0]
    tq = A_TILE
    blk = A_BLOCK
    aw = A_STEP_WIDTH
    args, in_specs = [], []
    for gi, d in enumerate(DILATIONS):
        q, k, v = qkv[3 * gi:3 * gi + 3]
        n = tq // d
        nblk = n // blk
        last = seq // d // blk - 1
        cur = pl.BlockSpec((1, d, n, aw), lambda bb, t, hh: (bb, 0, t, hh))
        prev = pl.BlockSpec((1, d, blk, aw),
                            lambda bb, t, hh, nblk=nblk: (bb, 0, jnp.maximum(t * nblk - 1, 0), hh))
        nxt = pl.BlockSpec((1, d, blk, aw),
                           lambda bb, t, hh, nblk=nblk, last=last: (bb, 0, jnp.minimum((t + 1) * nblk, last), hh))
        args += [q, k, k, k, v, v, v]
        in_specs += [cur, cur, prev, nxt, cur, prev, nxt]
    return pl.pallas_call(
        functools.partial(_mixer_a_kernel, seq=seq),
        grid=(b, seq // tq, A_WIDTH // aw),
        in_specs=in_specs,
        out_specs=pl.BlockSpec((1, tq, aw), lambda bb, t, hh: (bb, t, hh)),
        out_shape=jax.ShapeDtypeStruct((b, seq, A_WIDTH), BF16),
        scratch_shapes=[pltpu.VMEM((tq + 2 * blk, aw), BF16), pltpu.VMEM((tq + 2 * blk, aw), BF16),
                        pltpu.VMEM((aw // LANES, tq, LANES), F32),
                        pltpu.VMEM((aw // LANES, tq, LANES), F32),
                        pltpu.VMEM((aw // LANES, tq, LANES), F32)],
        compiler_params=pltpu.CompilerParams(
            dimension_semantics=("arbitrary", "arbitrary", "arbitrary"), vmem_limit_bytes=VMEM_LIMIT),
        name="mixer_a",
    )(*args)


def _mixer_b_kernel(sink_ref, q, kc, kp, kn, vc, vp, vn, o_ref, kext, vext, *, seq):
    t = pl.program_id(1)
    tq = ATT_TILE
    blk = B_BLOCK
    win = 3 * blk
    kext[0:blk, :] = kp[0]
    kext[blk:blk + tq, :] = kc[0]
    kext[blk + tq:2 * blk + tq, :] = kn[0]
    vext[0:blk, :] = vp[0]
    vext[blk:blk + tq, :] = vc[0]
    vext[blk + tq:2 * blk + tq, :] = vn[0]
    col = lax.broadcasted_iota(jnp.int32, (blk, win), 1)
    row = lax.broadcasted_iota(jnp.int32, (blk, win), 0)
    diff = col - row
    first_half = lax.broadcasted_iota(jnp.int32, (blk, LANES), 1) < HEAD_DIM
    second_half = jnp.logical_not(first_half)

    def block_body(j, carry):
        j0 = pl.multiple_of(j * blk, blk)
        qb = q[0, pl.ds(j0, blk), :]
        kw = kext[pl.ds(j0, win), :]
        vw = vext[pl.ds(j0, win), :]
        qbase = t * tq + j * blk
        valid = ((diff >= 0) & (diff <= 2 * B_WINDOW)
                 & (col >= blk - qbase) & (col < seq - qbase + blk))
        scores, sinks = [], []
        for c in range(B_Q_HEADS // 2):
            qp = qb[:, c * LANES:(c + 1) * LANES]
            for g, keep in enumerate((first_half, second_half)):
                qm = jnp.where(keep, qp, jnp.zeros_like(qp))
                scores.append(lax.dot_general(qm, kw, (((1,), (1,)), ((), ())), preferred_element_type=F32))
                sinks.append(jnp.full((1, LANES), sink_ref[g * B_REP + c], F32))
        sc = jnp.where(valid[None], jnp.stack(scores), NEG_BIG)
        sk = jnp.stack(sinks)
        part = sc[..., 0:LANES]
        for k in range(1, win // LANES):
            part = jnp.maximum(part, sc[..., k * LANES:(k + 1) * LANES])
        m = jnp.max(jnp.maximum(part, sk), axis=-1, keepdims=True)
        p = jnp.exp(sc - m)
        denom = jnp.sum(p, axis=-1, keepdims=True) + jnp.exp(sk - m)
        pb = p.astype(BF16)
        parts = []
        for c in range(B_Q_HEADS // 2):
            o0 = jnp.dot(pb[2 * c], vw, preferred_element_type=F32) / denom[2 * c]
            o1 = jnp.dot(pb[2 * c + 1], vw, preferred_element_type=F32) / denom[2 * c + 1]
            parts.append(jnp.where(first_half, o0, o1))
        o_ref[0, pl.ds(j0, blk), :] = jnp.concatenate(parts, axis=1).astype(BF16)
        return carry

    lax.fori_loop(0, tq // blk, block_body, 0)


def _mixer_b(q, k, v, sink, seq):
    b = q.shape[0]
    tq = ATT_TILE
    blk = B_BLOCK
    nblk = tq // blk
    last = seq // blk - 1
    cur = pl.BlockSpec((1, tq, B_KV_WIDTH), lambda bb, t: (bb, t, 0))
    prev = pl.BlockSpec((1, blk, B_KV_WIDTH), lambda bb, t: (bb, jnp.maximum(t * nblk - 1, 0), 0))
    nxt = pl.BlockSpec((1, blk, B_KV_WIDTH), lambda bb, t: (bb, jnp.minimum((t + 1) * nblk, last), 0))
    return pl.pallas_call(
        functools.partial(_mixer_b_kernel, seq=seq),
        grid=(b, seq // tq),
        in_specs=[pl.BlockSpec(memory_space=pltpu.SMEM),
                  pl.BlockSpec((1, tq, B_Q_WIDTH), lambda bb, t: (bb, t, 0)),
                  cur, prev, nxt, cur, prev, nxt],
        out_specs=pl.BlockSpec((1, tq, B_Q_WIDTH), lambda bb, t: (bb, t, 0)),
        out_shape=jax.ShapeDtypeStruct((b, seq, B_Q_WIDTH), BF16),
        scratch_shapes=[pltpu.VMEM((tq + 2 * blk, B_KV_WIDTH), BF16),
                        pltpu.VMEM((tq + 2 * blk, B_KV_WIDTH), BF16)],
        compiler_params=pltpu.CompilerParams(
            dimension_semantics=("arbitrary", "arbitrary"), vmem_limit_bytes=VMEM_LIMIT),
        name="mixer_b",
    )(sink, q, k, k, k, v, v, v)


def _post_attn_kernel(x_ref, oa_ref, ob_ref, gmix_ref, wg_ref, wa_ref, wb_ref, wo_ref, gmoe_ref,
                      wr_ref, br_ref, x1_ref, h2_ref, gate_ref):
    x = x_ref[...]
    h = _rms(x, gmix_ref[...]).astype(BF16)
    gates = jax.nn.sigmoid(jnp.dot(h, wg_ref[...], preferred_element_type=F32))
    ya = jnp.dot(oa_ref[...], wa_ref[...], preferred_element_type=F32)
    yb = jnp.dot(ob_ref[...], wb_ref[...], preferred_element_type=F32)
    mix = gates[:, :D_MODEL] * ya + gates[:, D_MODEL:] * yb
    x1 = x + jnp.dot(mix.astype(BF16), wo_ref[...], preferred_element_type=F32)
    x1_ref[...] = x1
    h2 = _rms(x1, gmoe_ref[...])
    h2_hi = h2.astype(BF16)
    h2_ref[...] = h2_hi

    h2_lo = (h2 - h2_hi.astype(F32)).astype(BF16)
    wr = wr_ref[...]
    wr_hi = wr.astype(BF16)
    wr_lo = (wr - wr_hi.astype(F32)).astype(BF16)
    logits = jnp.dot(jnp.concatenate([h2_hi, h2_lo, h2_hi], axis=1),
                     jnp.concatenate([wr_hi, wr_hi, wr_lo], axis=0),
                     preferred_element_type=F32) + br_ref[...]
    tm = logits.shape[0]
    lane = lax.broadcasted_iota(jnp.int32, (tm, LANES), 1)
    big = jnp.int32(LANES)
    is_group = (lane >= N_EXPERTS) & (lane < N_EXPERTS + N_GROUPS)
    gl = jnp.where(is_group, logits, NEG_BIG)
    gmax = jnp.max(gl, axis=-1, keepdims=True)
    g_idx = jnp.min(jnp.where(gl == gmax, lane, big), axis=-1, keepdims=True) - N_EXPERTS
    g_w = 1.0 / jnp.sum(jnp.exp(gl - gmax), axis=-1, keepdims=True)
    in_group = (lane < N_EXPERTS) & (lax.shift_right_logical(lane, 3) == g_idx)
    sel = jnp.where(in_group, logits, NEG_BIG)
    v1 = jnp.max(sel, axis=-1, keepdims=True)
    i1 = jnp.min(jnp.where(sel == v1, lane, big), axis=-1, keepdims=True)
    sel2 = jnp.where(lane == i1, NEG_BIG, sel)
    v2 = jnp.max(sel2, axis=-1, keepdims=True)
    i2 = jnp.min(jnp.where(sel2 == v2, lane, big), axis=-1, keepdims=True)
    e2 = jnp.exp(v2 - v1)
    w1 = g_w / (1.0 + e2)
    w2 = g_w * e2 / (1.0 + e2)
    gate_ref[...] = jnp.where(lane == i1, w1, 0.0) + jnp.where(lane == i2, w2, 0.0)


def _post_attn(x2d, oa, ob, norm_mix, w_gate, w_a, w_b, w_out, norm_moe, w_router, b_router):
    t = x2d.shape[0]
    tm = POST_TILE
    const = lambda i: (0, 0)
    row = lambda w: pl.BlockSpec((tm, w), lambda i: (i, 0))
    full = lambda a: pl.BlockSpec(a.shape, const, pipeline_mode=pl.Buffered(1))
    return pl.pallas_call(
        _post_attn_kernel,
        grid=(t // tm,),
        in_specs=[row(D_MODEL), row(A_WIDTH), row(B_Q_WIDTH), full(norm_mix), full(w_gate), full(w_a),
                  full(w_b), full(w_out), full(norm_moe), full(w_router), full(b_router)],
        out_specs=[row(D_MODEL), row(D_MODEL), row(LANES)],
        out_shape=[jax.ShapeDtypeStruct((t, D_MODEL), F32), jax.ShapeDtypeStruct((t, D_MODEL), BF16),
                   jax.ShapeDtypeStruct((t, LANES), F32)],
        compiler_params=pltpu.CompilerParams(
            dimension_semantics=("arbitrary",), vmem_limit_bytes=VMEM_LIMIT),
        name="post_attn",
    )(x2d, oa, ob, norm_mix, w_gate, w_a, w_b, w_out, norm_moe, w_router, b_router)


def _moe_final_kernel(x1_ref, h2_ref, gate_ref, tri_ref, w1_ref, w3_ref, w2_ref, gf_ref, o_ref,
                      xs, stage, seg_s, slo_s, shi_s, wlo_s, whi_s):
    step = pl.program_id(1)
    first_step = (pl.program_id(0) == 0) & (step == 0)
    n_sub = x1_ref.shape[0] // MOE_SUB

    @pl.when(first_step)
    def _init():
        xs[n_sub * MOE_SUB_SLOTS:, :] = jnp.zeros((MOE_CHUNK, D_MODEL), BF16)
        stage[...] = jnp.zeros(stage.shape, BF16)

    @pl.when(step == 0)
    def _sort():
        r_i = lax.broadcasted_iota(jnp.int32, (LANES, LANES), 0)
        c_i = lax.broadcasted_iota(jnp.int32, (LANES, LANES), 1)
        upper = jnp.where(r_i < c_i, 1.0, 0.0)
        col = lax.broadcasted_iota(jnp.int32, (MOE_SUB, MOE_SUB_SLOTS), 1)
        for s in range(n_sub):
            rows = slice(s * MOE_SUB, (s + 1) * MOE_SUB)
            g = gate_ref[rows, :]
            hit = g > 0.0
            onehot = jnp.where(hit, 1.0, 0.0)
            count = jnp.sum(onehot, axis=0, keepdims=True)
            padded = jnp.ceil(count * (1.0 / SEG_ALIGN)) * SEG_ALIGN
            seg = jnp.dot(jnp.broadcast_to(padded, (8, LANES)), upper, preferred_element_type=F32,
                          precision=lax.Precision.HIGHEST)[0:1]
            seg_i = seg.astype(jnp.int32)
            padded_i = padded.astype(jnp.int32)
            for ex in range(N_EXPERTS):
                seg_s[2 * s, ex] = seg_i[0, ex] + s * MOE_SUB_SLOTS
                seg_s[2 * s + 1, ex] = padded_i[0, ex]
            rank = jnp.dot(tri_ref[...], onehot.astype(BF16), preferred_element_type=F32)
            slot = seg + rank
            s_lo = jnp.min(jnp.where(hit, slot, 1e9), axis=-1, keepdims=True)
            s_hi = jnp.max(jnp.where(hit, slot, -1.0), axis=-1, keepdims=True)
            wlo_s[rows, :] = jnp.sum(jnp.where(hit & (slot == s_lo), g, 0.0), axis=-1, keepdims=True)
            whi_s[rows, :] = jnp.sum(jnp.where(hit & (slot == s_hi) & (s_hi > s_lo), g, 0.0),
                                     axis=-1, keepdims=True)
            s_lo_i = s_lo.astype(jnp.int32)
            s_hi_i = s_hi.astype(jnp.int32)
            slo_s[rows, :] = s_lo_i
            shi_s[rows, :] = s_hi_i
            perm_t = jnp.where((col == s_lo_i) | (col == s_hi_i), 1.0, 0.0).astype(BF16)
            sorted_rows = lax.dot_general(perm_t, h2_ref[rows, :], (((0,), (0,)), ((), ())),
                                          preferred_element_type=F32)
            xs[s * MOE_SUB_SLOTS:(s + 1) * MOE_SUB_SLOTS, :] = sorted_rows.astype(BF16)

    for j in range(MOE_ESTEP):
        e = step * MOE_ESTEP + j
        starts = [seg_s[2 * s, e] for s in range(n_sub)]
        lens = [seg_s[2 * s + 1, e] for s in range(n_sub)]

        dst = jnp.int32(0)
        dsts = []
        for s in range(n_sub):
            dsts.append(dst)

            def gather_body(k, carry, src0=starts[s], dst0=dst):
                src = pl.multiple_of(src0 + k * MOE_CHUNK, SEG_ALIGN)
                to = pl.multiple_of(dst0 + k * MOE_CHUNK, SEG_ALIGN)
                stage[pl.ds(to, MOE_CHUNK), :] = xs[pl.ds(src, MOE_CHUNK), :]
                return carry

            lax.fori_loop(0, (lens[s] + MOE_CHUNK - 1) // MOE_CHUNK, gather_body, 0)
            dst = dst + lens[s]
        total = dst

        def chunk_body(k, carry, j=j):
            r0 = pl.multiple_of(k * MOE_CHUNK, MOE_CHUNK)
            x = stage[pl.ds(r0, MOE_CHUNK), :]
            a = jnp.dot(x, w1_ref[j], preferred_element_type=F32)
            b = jnp.dot(x, w3_ref[j], preferred_element_type=F32)
            hid = (a * jax.nn.sigmoid(a)) * b
            stage[pl.ds(r0, MOE_CHUNK), :] = jnp.dot(hid.astype(BF16), w2_ref[j],
                                                     preferred_element_type=F32).astype(BF16)
            return carry

        lax.fori_loop(0, (total + MOE_CHUNK - 1) // MOE_CHUNK, chunk_body, 0)

        for s in range(n_sub):
            def scatter_body(k, carry, src0=dsts[s], dst0=starts[s]):
                src = pl.multiple_of(src0 + k * SEG_ALIGN, SEG_ALIGN)
                to = pl.multiple_of(dst0 + k * SEG_ALIGN, SEG_ALIGN)
                xs[pl.ds(to, SEG_ALIGN), :] = stage[pl.ds(src, SEG_ALIGN), :]
                return carry

            lax.fori_loop(0, lens[s] // SEG_ALIGN, scatter_body, 0)

    @pl.when(step == N_EXPERTS // MOE_ESTEP - 1)
    def _combine():
        col = lax.broadcasted_iota(jnp.int32, (MOE_SUB, MOE_SUB_SLOTS), 1)
        for s in range(n_sub):
            rows = slice(s * MOE_SUB, (s + 1) * MOE_SUB)
            wperm = (jnp.where(col == slo_s[rows, :], wlo_s[rows, :], 0.0)
                     + jnp.where(col == shi_s[rows, :], whi_s[rows, :], 0.0)).astype(BF16)
            moe = jnp.dot(wperm, xs[s * MOE_SUB_SLOTS:(s + 1) * MOE_SUB_SLOTS, :], preferred_element_type=F32)
            o_ref[rows, :] = _rms(x1_ref[rows, :] + moe, gf_ref[...])


def _moe_final(x1, h2, gate, w1, w3, w2, norm_final):
    t = x1.shape[0]
    tm = MOE_TILE
    n_sub = tm // MOE_SUB
    row = lambda w: pl.BlockSpec((tm, w), lambda i, e: (i, 0))
    tri = jnp.tril(jnp.ones((MOE_SUB, MOE_SUB), BF16), -1)
    return pl.pallas_call(
        _moe_final_kernel,
        grid=(t // tm, N_EXPERTS // MOE_ESTEP),
        in_specs=[row(D_MODEL), row(D_MODEL), row(LANES),
                  pl.BlockSpec((MOE_SUB, MOE_SUB), lambda i, e: (0, 0), pipeline_mode=pl.Buffered(1)),
                  pl.BlockSpec((MOE_ESTEP, D_MODEL, D_FF), lambda i, e: (e, 0, 0)),
                  pl.BlockSpec((MOE_ESTEP, D_MODEL, D_FF), lambda i, e: (e, 0, 0)),
                  pl.BlockSpec((MOE_ESTEP, D_FF, D_MODEL), lambda i, e: (e, 0, 0)),
                  pl.BlockSpec((1, D_MODEL), lambda i, e: (0, 0))],
        out_specs=row(D_MODEL),
        out_shape=jax.ShapeDtypeStruct((t, D_MODEL), F32),
        scratch_shapes=[pltpu.VMEM((n_sub * MOE_SUB_SLOTS + MOE_CHUNK, D_MODEL), BF16),
                        pltpu.VMEM((2 * tm + 2 * MOE_CHUNK, D_MODEL), BF16),
                        pltpu.SMEM((2 * n_sub, N_EXPERTS), jnp.int32),
                        pltpu.VMEM((tm, 1), jnp.int32), pltpu.VMEM((tm, 1), jnp.int32),
                        pltpu.VMEM((tm, 1), F32), pltpu.VMEM((tm, 1), F32)],
        compiler_params=pltpu.CompilerParams(
            dimension_semantics=("arbitrary", "arbitrary"), vmem_limit_bytes=VMEM_LIMIT),
        name="moe_final",
    )(x1, h2, gate, tri, w1, w3, w2, norm_final)


def _prep_weights(norm_mix, w_in, attn_sink, w_branch_a, w_branch_b, w_gate, w_out, norm_moe,
                  w_router_group, b_router_group, w_router_expert, b_router_expert, w1, w3, w2, norm_final):
    wi = w_in[0]
    a_w = 3 * A_WIDTH

    def group_cols(gi):
        return jnp.concatenate([wi[:, role * a_w + gi * A_WIDTH: role * a_w + (gi + 1) * A_WIDTH]
                                for role in range(3)], axis=1)

    b_heads = [g * B_REP + c for c in range(B_REP) for g in range(B_KV_HEADS)]
    bq_cols = jnp.concatenate([wi[:, 3 * a_w + h * HEAD_DIM: 3 * a_w + (h + 1) * HEAD_DIM] for h in b_heads],
                              axis=1)
    w_b_rows = jnp.concatenate([w_branch_b[0][h * HEAD_DIM:(h + 1) * HEAD_DIM] for h in b_heads], axis=0)
    w_nat = jnp.concatenate([group_cols(0), bq_cols, wi[:, 3 * a_w + B_Q_WIDTH:]], axis=1).astype(BF16)
    w_g1 = group_cols(1).astype(BF16)
    w_g2 = group_cols(2).astype(BF16)
    pad = LANES - N_EXPERTS - N_GROUPS
    w_router = jnp.concatenate([w_router_expert[0], w_router_group[0],
                                jnp.zeros((D_MODEL, pad), F32)], axis=1)
    b_router = jnp.concatenate([b_router_expert[0].reshape(-1), b_router_group[0],
                                jnp.zeros((pad,), F32)]).reshape(1, LANES)
    return dict(
        norm_mix=norm_mix[0].reshape(1, D_MODEL), w_nat=w_nat, w_g1=w_g1, w_g2=w_g2, sink=attn_sink[0],
        w_a=w_branch_a[0].astype(BF16), w_b=w_b_rows.astype(BF16), w_gate=w_gate[0].astype(BF16),
        w_out=w_out[0].astype(BF16), norm_moe=norm_moe[0].reshape(1, D_MODEL),
        w_router=w_router, b_router=b_router,
        w1=w1[0].astype(BF16), w3=w3[0].astype(BF16), w2=w2[0].astype(BF16),
        norm_final=norm_final.reshape(1, D_MODEL))


def _trunk(x, p):
    b, s, _ = x.shape
    outs = _in_proj(x, p["norm_mix"], p["w_nat"], p["w_g1"], p["w_g2"])
    oa = _mixer_a(outs[:9], s)
    ob = _mixer_b(outs[9], outs[10], outs[11], p["sink"], s)
    t = b * s
    x1, h2, gate = _post_attn(x.reshape(t, D_MODEL), oa.reshape(t, A_WIDTH), ob.reshape(t, B_Q_WIDTH),
                              p["norm_mix"], p["w_gate"], p["w_a"], p["w_b"], p["w_out"], p["norm_moe"],
                              p["w_router"], p["b_router"])
    y = _moe_final(x1, h2, gate, p["w1"], p["w3"], p["w2"], p["norm_final"])
    return y.reshape(b, s, D_MODEL)


def kernel(x_prompt, x_sample, norm_mix, w_in, attn_sink, w_branch_a, w_branch_b, w_gate, w_out, norm_moe,
           w_router_group, b_router_group, w_router_expert, b_router_expert, w1, w3, w2, norm_final):
    p = _prep_weights(norm_mix, w_in, attn_sink, w_branch_a, w_branch_b, w_gate, w_out, norm_moe,
                      w_router_group, b_router_group, w_router_expert, b_router_expert, w1, w3, w2, norm_final)
    return (_trunk(x_prompt, p), _trunk(x_sample, p))
```

```python
import functools

import jax
import jax.numpy as jnp
from jax import lax
from jax.experimental import pallas as pl
from jax.experimental.pallas import tpu as pltpu

D_MODEL = 1024
HEAD_DIM = 64
ROT_DIM = HEAD_DIM // 4
ROT_HALF = ROT_DIM // 2
ROPE_THETA = 500000.0
DILATIONS = (1, 4, 16)
A_HALF = 64
A_BLOCK = 64
A_QBLOCK = 128
A_HEADS = 8
A_WIDTH = A_HEADS * HEAD_DIM
B_Q_HEADS = 8
B_KV_HEADS = 2
B_REP = B_Q_HEADS // B_KV_HEADS
B_WINDOW = 128
B_BLOCK = 128
B_Q_WIDTH = B_Q_HEADS * HEAD_DIM
B_KV_WIDTH = B_KV_HEADS * HEAD_DIM
N_GROUPS = 4
EXPERTS_PER_GROUP = 8
N_EXPERTS = N_GROUPS * EXPERTS_PER_GROUP
D_FF = 256
RMS_EPS = 1e-6
Q_SCALE = HEAD_DIM ** -0.5

LANES = 128
NEG_BIG = -1e30
VMEM_LIMIT = 56 * 1024 * 1024

IN_TILE = 512
ATT_TILE = 1024
A_TILE = 1024
A_STEP_WIDTH = 512
POST_TILE = 1024
POST_SLAB = 256
MOE_TILE = 1024
SEG_ALIGN = 16
MOE_CHUNK = 128
MOE_SLOTS = 2 * MOE_TILE + N_EXPERTS * SEG_ALIGN
MOE_ESTEP = 4

F32 = jnp.float32
BF16 = jnp.bfloat16


def _rms(x, g):
    ms = jnp.mean(x * x, axis=-1, keepdims=True)
    return x * lax.rsqrt(ms + RMS_EPS) * g


def _rope(a, cos, sin, low_half):
    outs = []
    for k in range(a.shape[1] // LANES):
        ak = a[:, k * LANES:(k + 1) * LANES]
        up = pltpu.roll(ak, LANES - ROT_HALF, axis=1)
        dn = pltpu.roll(ak, ROT_HALF, axis=1)
        outs.append(ak * cos + jnp.where(low_half, up, dn) * sin)
    return outs[0] if len(outs) == 1 else jnp.concatenate(outs, axis=1)


def _in_proj_kernel(x_ref, g_ref, cos_ref, sin_ref, wn_ref, w1_ref, w2_ref,
                    q0, k0, v0, q1, k1, v1, q2, k2, v2, bq, bk, bv,
                    xs, h1_s, h2_s, c1_s, s1_s, c2_s, s2_s):
    tm = x_ref.shape[1]
    n_chunks = D_MODEL // LANES
    for c in range(n_chunks):
        xs[c] = x_ref[0, :, c * LANES:(c + 1) * LANES]
    g = g_ref[...]
    low_half = (lax.broadcasted_iota(jnp.int32, (tm, LANES), 1) % HEAD_DIM) < ROT_HALF

    def mm(h, w_ref, c0, n):
        return jnp.dot(h, w_ref[:, c0:c0 + n], preferred_element_type=F32)

    h0 = _rms(x_ref[0], g).astype(BF16)
    cos0, sin0 = cos_ref[...], sin_ref[...]
    q0[0, 0] = (_rope(mm(h0, wn_ref, 0, A_WIDTH), cos0, sin0, low_half) * Q_SCALE).astype(BF16)
    k0[0, 0] = _rope(mm(h0, wn_ref, A_WIDTH, A_WIDTH), cos0, sin0, low_half).astype(BF16)
    v0[0, 0] = mm(h0, wn_ref, 2 * A_WIDTH, A_WIDTH).astype(BF16)
    c = 3 * A_WIDTH
    bq[0] = (_rope(mm(h0, wn_ref, c, B_Q_WIDTH), cos0, sin0, low_half) * Q_SCALE).astype(BF16)
    bk[0] = _rope(mm(h0, wn_ref, c + B_Q_WIDTH, B_KV_WIDTH), cos0, sin0, low_half).astype(BF16)
    bv[0] = mm(h0, wn_ref, c + B_Q_WIDTH + B_KV_WIDTH, B_KV_WIDTH).astype(BF16)

    for d, hs, cs, ss, w_ref, qo, ko, vo in ((DILATIONS[1], h1_s, c1_s, s1_s, w1_ref, q1, k1, v1),
                                             (DILATIONS[2], h2_s, c2_s, s2_s, w2_ref, q2, k2, v2)):
        n = tm // d
        for r in range(d):
            rows = pl.ds(r, n, stride=d)
            xr = jnp.concatenate([xs[c, rows, :] for c in range(n_chunks)], axis=1)
            hs[r * n:(r + 1) * n, :] = _rms(xr, g).astype(BF16)
            cs[r * n:(r + 1) * n, :] = cos_ref[rows, :]
            ss[r * n:(r + 1) * n, :] = sin_ref[rows, :]
        h = hs[...]
        cos, sin = cs[...], ss[...]
        qv = (_rope(mm(h, w_ref, 0, A_WIDTH), cos, sin, low_half) * Q_SCALE).astype(BF16)
        kv = _rope(mm(h, w_ref, A_WIDTH, A_WIDTH), cos, sin, low_half).astype(BF16)
        vv = mm(h, w_ref, 2 * A_WIDTH, A_WIDTH).astype(BF16)
        for r in range(d):
            qo[0, r] = qv[r * n:(r + 1) * n]
            ko[0, r] = kv[r * n:(r + 1) * n]
            vo[0, r] = vv[r * n:(r + 1) * n]


def _rope_tables(s):
    inv = jnp.power(jnp.float32(ROPE_THETA), -jnp.arange(ROT_HALF, dtype=F32) / ROT_HALF)
    ang = jnp.arange(s).astype(F32)[:, None] * inv[None, :]
    cos, sin = jnp.cos(ang), jnp.sin(ang)
    pad_one = jnp.ones((s, HEAD_DIM - ROT_DIM), F32)
    pad_zero = jnp.zeros((s, HEAD_DIM - ROT_DIM), F32)
    cos_h = jnp.concatenate([cos, cos, pad_one], axis=1)
    sin_h = jnp.concatenate([-sin, sin, pad_zero], axis=1)
    return jnp.concatenate([cos_h, cos_h], axis=1), jnp.concatenate([sin_h, sin_h], axis=1)


def _in_proj(x, norm_g, w_nat, w_g1, w_g2):
    b, s, _ = x.shape
    tm = IN_TILE
    cos_t, sin_t = _rope_tables(s)
    const = lambda t, bb: (0, 0)
    one_buf = pl.Buffered(1)
    in_specs = [
        pl.BlockSpec((1, tm, D_MODEL), lambda t, bb: (bb, t, 0)),
        pl.BlockSpec((1, D_MODEL), const),
        pl.BlockSpec((tm, LANES), lambda t, bb: (t, 0)),
        pl.BlockSpec((tm, LANES), lambda t, bb: (t, 0)),
        pl.BlockSpec(w_nat.shape, const, pipeline_mode=one_buf),
        pl.BlockSpec(w_g1.shape, const, pipeline_mode=one_buf),
        pl.BlockSpec(w_g2.shape, const, pipeline_mode=one_buf),
    ]
    out_shape, out_specs = [], []
    for d in DILATIONS:
        for _ in range(3):
            out_shape.append(jax.ShapeDtypeStruct((b, d, s // d, A_WIDTH), BF16))
            out_specs.append(pl.BlockSpec((1, d, tm // d, A_WIDTH), lambda t, bb: (bb, 0, t, 0)))
    for w in (B_Q_WIDTH, B_KV_WIDTH, B_KV_WIDTH):
        out_shape.append(jax.ShapeDtypeStruct((b, s, w), BF16))
        out_specs.append(pl.BlockSpec((1, tm, w), lambda t, bb: (bb, t, 0)))
    scratch = [pltpu.VMEM((D_MODEL // LANES, tm, LANES), F32),
               pltpu.VMEM((tm, D_MODEL), BF16), pltpu.VMEM((tm, D_MODEL), BF16)] + \
              [pltpu.VMEM((tm, LANES), F32) for _ in range(4)]
    return pl.pallas_call(
        _in_proj_kernel,
        grid=(s // tm, b),
        in_specs=in_specs,
        out_specs=out_specs,
        out_shape=out_shape,
        scratch_shapes=scratch,
        compiler_params=pltpu.CompilerParams(
            dimension_semantics=("arbitrary", "arbitrary"), vmem_limit_bytes=VMEM_LIMIT),
        name="in_proj",
    )(x, norm_g.reshape(1, D_MODEL), cos_t, sin_t, w_nat, w_g1, w_g2)


def _attend_heads(blocks, valid, first_half):
    second_half = jnp.logical_not(first_half)
    n_heads = blocks[0][0].shape[1] // HEAD_DIM
    scores = []
    for qb, kw, _ in blocks:
        for h in range(n_heads):
            cs = slice((h // 2) * LANES, (h // 2 + 1) * LANES)
            qp = qb[:, cs]
            qm = jnp.where(first_half if h % 2 == 0 else second_half, qp, jnp.zeros_like(qp))
            scores.append(lax.dot_general(qm, kw[:, cs], (((1,), (1,)), ((), ())),
                                          preferred_element_type=F32))
    sc = jnp.where(valid[None], jnp.stack(scores), NEG_BIG)
    m = jnp.max(sc, axis=-1, keepdims=True)
    p = jnp.exp(sc - m)
    l = jnp.sum(p, axis=-1, keepdims=True)
    pb = p.astype(BF16)
    results = []
    for bi, (_, _, vw) in enumerate(blocks):
        out = []
        for hp in range(n_heads // 2):
            i0 = bi * n_heads + 2 * hp
            vp = vw[:, hp * LANES:(hp + 1) * LANES]
            o0 = jnp.dot(pb[i0], vp, preferred_element_type=F32)
            o1 = jnp.dot(pb[i0 + 1], vp, preferred_element_type=F32)
            out.append((jnp.where(first_half, o0, o1), jnp.where(first_half, m[i0], m[i0 + 1]),
                        jnp.where(first_half, l[i0], l[i0 + 1])))
        results.append(out)
    return results


def _mixer_a_kernel(*refs, seq):
    ins, (o_ref, kext, vext, acc_o, acc_m, acc_l) = refs[:21], refs[21:]
    t = pl.program_id(1)
    tq = A_TILE
    halo = A_HALF

    first_group = len(DILATIONS) - 1
    for gi in reversed(range(len(DILATIONS))):
        d = DILATIONS[gi]
        q, kc, kp, kn, vc, vp, vn = ins[7 * gi:7 * gi + 7]
        n = tq // d
        blk = min(n, A_QBLOCK)
        nblk = n // blk
        win = -(-(blk + 2 * halo) // LANES) * LANES
        tail = win - (blk + 2 * halo) if n == blk else 0
        ld = seq // d
        col = lax.broadcasted_iota(jnp.int32, (blk, win), 1)
        diff = col - lax.broadcasted_iota(jnp.int32, (blk, win), 0)
        band = (diff >= 0) & (diff <= 2 * halo)
        first_half = lax.broadcasted_iota(jnp.int32, (blk, LANES), 1) < HEAD_DIM

        pair = A_QBLOCK // blk

        def residue_body(rp, carry, q=q, kc=kc, kp=kp, kn=kn, vc=vc, vp=vp, vn=vn, d=d, n=n, nblk=nblk,
                         ld=ld, blk=blk, win=win, col=col, band=band, first_half=first_half,
                         first=(gi == first_group), pair=pair, tail=tail):
            for u in range(pair):
                r = rp * pair + u
                base = u * win
                kext[base:base + halo, :] = kp[0, r]
                kext[base + halo:base + halo + n, :] = kc[0, r]
                kext[base + halo + n:base + 2 * halo + n, :] = kn[0, r]
                vext[base:base + halo, :] = vp[0, r]
                vext[base + halo:base + halo + n, :] = vc[0, r]
                vext[base + halo + n:base + 2 * halo + n, :] = vn[0, r]
                if tail:
                    kext[base + 2 * halo + n:base + win, :] = jnp.zeros((tail, kext.shape[1]), BF16)
                    vext[base + 2 * halo + n:base + win, :] = jnp.zeros((tail, vext.shape[1]), BF16)

            def block_body(j, carry2):
                j0 = pl.multiple_of(j * blk, blk)
                qbase = t * n + j * blk
                valid = band & (col >= halo - qbase) & (col < ld - qbase + halo)
                blocks = [(q[0, rp * pair + u, pl.ds(j0, blk), :],
                           kext[pl.ds(u * win + j0, win), :],
                           vext[pl.ds(u * win + j0, win), :]) for u in range(pair)]
                results = _attend_heads(blocks, valid, first_half)
                for u, parts in enumerate(results):
                    if d == 1:
                        rows = pl.ds(j0, blk)
                    else:
                        rows = pl.ds(j * (blk * d) + rp * pair + u, blk, stride=d)
                    for hp, (o_b, m_b, l_b) in enumerate(parts):
                        if first:
                            acc_o[hp, rows, :] = o_b
                            acc_l[hp, rows, :] = l_b
                            acc_m[hp, rows, :] = m_b
                        else:
                            m_old = acc_m[hp, rows, :]
                            m_new = jnp.maximum(m_old, m_b)
                            a_old = jnp.exp(m_old - m_new)
                            a_new = jnp.exp(m_b - m_new)
                            acc_o[hp, rows, :] = acc_o[hp, rows, :] * a_old + o_b * a_new
                            acc_l[hp, rows, :] = acc_l[hp, rows, :] * a_old + l_b * a_new
                            acc_m[hp, rows, :] = m_new
                return carry2

            lax.fori_loop(0, nblk, block_body, 0)
            return carry

        lax.fori_loop(0, d // pair, residue_body, 0)

    o_ref[0] = jnp.concatenate([acc_o[hp] / acc_l[hp] for hp in range(A_STEP_WIDTH // LANES)],
                               axis=1).astype(BF16)


def _mixer_a(qkv, seq):
    b = qkv[0].shape[0]
    tq = A_TILE
    blk = A_BLOCK
    aw = A_STEP_WIDTH
    args, in_specs = [], []
    for gi, d in enumerate(DILATIONS):
        q, k, v = qkv[3 * gi:3 * gi + 3]
        n = tq // d
        nblk = n // blk
        last = seq // d // blk - 1
        cur = pl.BlockSpec((1, d, n, aw), lambda bb, t, hh: (bb, 0, t, hh))
        prev = pl.BlockSpec((1, d, blk, aw),
                            lambda bb, t, hh, nblk=nblk: (bb, 0, jnp.maximum(t * nblk - 1, 0), hh))
        nxt = pl.BlockSpec((1, d, blk, aw),
                           lambda bb, t, hh, nblk=nblk, last=last: (bb, 0, jnp.minimum((t + 1) * nblk, last), hh))
        args += [q, k, k, k, v, v, v]
        in_specs += [cur, cur, prev, nxt, cur, prev, nxt]
    return pl.pallas_call(
        functools.partial(_mixer_a_kernel, seq=seq),
        grid=(b, seq // tq, A_WIDTH // aw),
        in_specs=in_specs,
        out_specs=pl.BlockSpec((1, tq, aw), lambda bb, t, hh: (bb, t, hh)),
        out_shape=jax.ShapeDtypeStruct((b, seq, A_WIDTH), BF16),
        scratch_shapes=[pltpu.VMEM((tq + 2 * blk, aw), BF16), pltpu.VMEM((tq + 2 * blk, aw), BF16),
                        pltpu.VMEM((aw // LANES, tq, LANES), F32),
                        pltpu.VMEM((aw // LANES, tq, LANES), F32),
                        pltpu.VMEM((aw // LANES, tq, LANES), F32)],
        compiler_params=pltpu.CompilerParams(
            dimension_semantics=("arbitrary", "arbitrary", "arbitrary"), vmem_limit_bytes=VMEM_LIMIT),
        name="mixer_a",
    )(*args)


def _mixer_b_kernel(sink_ref, q, kc, kp, kn, vc, vp, vn, o_ref, kext, vext, *, seq):
    t = pl.program_id(1)
    tq = ATT_TILE
    blk = B_BLOCK
    win = 3 * blk
    kext[0:blk, :] = kp[0]
    kext[blk:blk + tq, :] = kc[0]
    kext[blk + tq:2 * blk + tq, :] = kn[0]
    vext[0:blk, :] = vp[0]
    vext[blk:blk + tq, :] = vc[0]
    vext[blk + tq:2 * blk + tq, :] = vn[0]
    col = lax.broadcasted_iota(jnp.int32, (blk, win), 1)
    row = lax.broadcasted_iota(jnp.int32, (blk, win), 0)
    diff = col - row
    first_half = lax.broadcasted_iota(jnp.int32, (blk, LANES), 1) < HEAD_DIM
    second_half = jnp.logical_not(first_half)

    def block_body(j, carry):
        j0 = pl.multiple_of(j * blk, blk)
        qb = q[0, pl.ds(j0, blk), :]
        kw = kext[pl.ds(j0, win), :]
        vw = vext[pl.ds(j0, win), :]
        qbase = t * tq + j * blk
        valid = ((diff >= 0) & (diff <= 2 * B_WINDOW)
                 & (col >= blk - qbase) & (col < seq - qbase + blk))
        scores, sinks = [], []
        for c in range(B_Q_HEADS // 2):
            qp = qb[:, c * LANES:(c + 1) * LANES]
            for g, keep in enumerate((first_half, second_half)):
                qm = jnp.where(keep, qp, jnp.zeros_like(qp))
                scores.append(lax.dot_general(qm, kw, (((1,), (1,)), ((), ())), preferred_element_type=F32))
                sinks.append(jnp.full((1, LANES), sink_ref[g * B_REP + c], F32))
        sc = jnp.where(valid[None], jnp.stack(scores), NEG_BIG)
        sk = jnp.stack(sinks)
        part = sc[..., 0:LANES]
        for k in range(1, win // LANES):
            part = jnp.maximum(part, sc[..., k * LANES:(k + 1) * LANES])
        m = jnp.max(jnp.maximum(part, sk), axis=-1, keepdims=True)
        p = jnp.exp(sc - m)
        denom = jnp.sum(p, axis=-1, keepdims=True) + jnp.exp(sk - m)
        pb = p.astype(BF16)
        parts = []
        for c in range(B_Q_HEADS // 2):
            o0 = jnp.dot(pb[2 * c], vw, preferred_element_type=F32) / denom[2 * c]
            o1 = jnp.dot(pb[2 * c + 1], vw, preferred_element_type=F32) / denom[2 * c + 1]
            parts.append(jnp.where(first_half, o0, o1))
        o_ref[0, pl.ds(j0, blk), :] = jnp.concatenate(parts, axis=1).astype(BF16)
        return carry

    lax.fori_loop(0, tq // blk, block_body, 0)


def _mixer_b(q, k, v, sink, seq):
    b = q.shape[0]
    tq = ATT_TILE
    blk = B_BLOCK
    nblk = tq // blk
    last = seq // blk - 1
    cur = pl.BlockSpec((1, tq, B_KV_WIDTH), lambda bb, t: (bb, t, 0))
    prev = pl.BlockSpec((1, blk, B_KV_WIDTH), lambda bb, t: (bb, jnp.maximum(t * nblk - 1, 0), 0))
    nxt = pl.BlockSpec((1, blk, B_KV_WIDTH), lambda bb, t: (bb, jnp.minimum((t + 1) * nblk, last), 0))
    return pl.pallas_call(
        functools.partial(_mixer_b_kernel, seq=seq),
        grid=(b, seq // tq),
        in_specs=[pl.BlockSpec(memory_space=pltpu.SMEM),
                  pl.BlockSpec((1, tq, B_Q_WIDTH), lambda bb, t: (bb, t, 0)),
                  cur, prev, nxt, cur, prev, nxt],
        out_specs=pl.BlockSpec((1, tq, B_Q_WIDTH), lambda bb, t: (bb, t, 0)),
        out_shape=jax.ShapeDtypeStruct((b, seq, B_Q_WIDTH), BF16),
        scratch_shapes=[pltpu.VMEM((tq + 2 * blk, B_KV_WIDTH), BF16),
                        pltpu.VMEM((tq + 2 * blk, B_KV_WIDTH), BF16)],
        compiler_params=pltpu.CompilerParams(
            dimension_semantics=("arbitrary", "arbitrary"), vmem_limit_bytes=VMEM_LIMIT),
        name="mixer_b",
    )(sink, q, k, k, k, v, v, v)


def _post_attn_kernel(x_ref, oa_ref, ob_ref, gmix_ref, wg_ref, wa_ref, wb_ref, wo_ref, gmoe_ref,
                      wr_ref, br_ref, x1_ref, h2_ref, gate_ref):
    wr = wr_ref[...]
    wr_hi = wr.astype(BF16)
    wr_lo = (wr - wr_hi.astype(F32)).astype(BF16)
    wr3 = jnp.concatenate([wr_hi, wr_hi, wr_lo], axis=0)
    for s in range(x_ref.shape[0] // POST_SLAB):
        rows = slice(s * POST_SLAB, (s + 1) * POST_SLAB)
        _post_attn_rows(x_ref, oa_ref, ob_ref, gmix_ref, wg_ref, wa_ref, wb_ref, wo_ref, gmoe_ref, wr3,
                        br_ref, x1_ref, h2_ref, gate_ref, rows)


def _post_attn_rows(x_ref, oa_ref, ob_ref, gmix_ref, wg_ref, wa_ref, wb_ref, wo_ref, gmoe_ref, wr3,
                    br_ref, x1_ref, h2_ref, gate_ref, rows):
    x = x_ref[rows, :]
    h = _rms(x, gmix_ref[...]).astype(BF16)
    gates = jax.nn.sigmoid(jnp.dot(h, wg_ref[...], preferred_element_type=F32))
    ya = jnp.dot(oa_ref[rows, :], wa_ref[...], preferred_element_type=F32)
    yb = jnp.dot(ob_ref[rows, :], wb_ref[...], preferred_element_type=F32)
    mix = gates[:, :D_MODEL] * ya + gates[:, D_MODEL:] * yb
    x1 = x + jnp.dot(mix.astype(BF16), wo_ref[...], preferred_element_type=F32)
    x1_ref[rows, :] = x1
    h2 = _rms(x1, gmoe_ref[...])
    h2_hi = h2.astype(BF16)
    h2_ref[rows, :] = h2_hi

    h2_lo = (h2 - h2_hi.astype(F32)).astype(BF16)
    logits = jnp.dot(jnp.concatenate([h2_hi, h2_lo, h2_hi], axis=1), wr3,
                     preferred_element_type=F32) + br_ref[...]
    tm = logits.shape[0]
    lane = lax.broadcasted_iota(jnp.int32, (tm, LANES), 1)
    big = jnp.int32(LANES)
    is_group = (lane >= N_EXPERTS) & (lane < N_EXPERTS + N_GROUPS)
    gl = jnp.where(is_group, logits, NEG_BIG)
    gmax = jnp.max(gl, axis=-1, keepdims=True)
    g_idx = jnp.min(jnp.where(gl == gmax, lane, big), axis=-1, keepdims=True) - N_EXPERTS
    g_w = 1.0 / jnp.sum(jnp.exp(gl - gmax), axis=-1, keepdims=True)
    in_group = (lane < N_EXPERTS) & (lax.shift_right_logical(lane, 3) == g_idx)
    sel = jnp.where(in_group, logits, NEG_BIG)
    v1 = jnp.max(sel, axis=-1, keepdims=True)
    i1 = jnp.min(jnp.where(sel == v1, lane, big), axis=-1, keepdims=True)
    sel2 = jnp.where(lane == i1, NEG_BIG, sel)
    v2 = jnp.max(sel2, axis=-1, keepdims=True)
    i2 = jnp.min(jnp.where(sel2 == v2, lane, big), axis=-1, keepdims=True)
    e2 = jnp.exp(v2 - v1)
    w1 = g_w / (1.0 + e2)
    w2 = g_w * e2 / (1.0 + e2)
    gate_ref[rows, :] = jnp.where(lane == i1, w1, 0.0) + jnp.where(lane == i2, w2, 0.0)


def _post_attn(x2d, oa, ob, norm_mix, w_gate, w_a, w_b, w_out, norm_moe, w_router, b_router):
    t = x2d.shape[0]
    tm = POST_TILE
    const = lambda i: (0, 0)
    row = lambda w: pl.BlockSpec((tm, w), lambda i: (i, 0))
    full = lambda a: pl.BlockSpec(a.shape, const, pipeline_mode=pl.Buffered(1))
    return pl.pallas_call(
        _post_attn_kernel,
        grid=(t // tm,),
        in_specs=[row(D_MODEL), row(A_WIDTH), row(B_Q_WIDTH), full(norm_mix), full(w_gate), full(w_a),
                  full(w_b), full(w_out), full(norm_moe), full(w_router), full(b_router)],
        out_specs=[row(D_MODEL), row(D_MODEL), row(LANES)],
        out_shape=[jax.ShapeDtypeStruct((t, D_MODEL), F32), jax.ShapeDtypeStruct((t, D_MODEL), BF16),
                   jax.ShapeDtypeStruct((t, LANES), F32)],
        compiler_params=pltpu.CompilerParams(
            dimension_semantics=("arbitrary",), vmem_limit_bytes=VMEM_LIMIT),
        name="post_attn",
    )(x2d, oa, ob, norm_mix, w_gate, w_a, w_b, w_out, norm_moe, w_router, b_router)


def _moe_final_kernel(x1_ref, h2_ref, gate_ref, tri_ref, w1_ref, w3_ref, w2_ref, gf_ref, o_ref,
                      xs, ys, seg_s, slo_s, shi_s, wlo_s, whi_s):
    step = pl.program_id(1)
    tm = x1_ref.shape[0]
    slot_cols = 512
    n_slot_chunks = MOE_SLOTS // slot_cols

    @pl.when(step == 0)
    def _sort():
        g = gate_ref[...]
        hit = g > 0.0
        onehot = jnp.where(hit, 1.0, 0.0)
        count = jnp.sum(onehot, axis=0, keepdims=True)
        padded = jnp.ceil(count * (1.0 / SEG_ALIGN)) * SEG_ALIGN
        r_i = lax.broadcasted_iota(jnp.int32, (LANES, LANES), 0)
        c_i = lax.broadcasted_iota(jnp.int32, (LANES, LANES), 1)
        upper = jnp.where(r_i < c_i, 1.0, 0.0)
        seg = jnp.dot(jnp.broadcast_to(padded, (8, LANES)), upper, preferred_element_type=F32,
                      precision=lax.Precision.HIGHEST)[0:1]
        seg_i = seg.astype(jnp.int32)
        padded_i = padded.astype(jnp.int32)
        for ex in range(N_EXPERTS):
            seg_s[0, ex] = seg_i[0, ex]
            seg_s[1, ex] = padded_i[0, ex]
        rank = jnp.dot(tri_ref[...], onehot.astype(BF16), preferred_element_type=F32)
        slot = seg + rank
        s_lo = jnp.min(jnp.where(hit, slot, 1e9), axis=-1, keepdims=True)
        s_hi = jnp.max(jnp.where(hit, slot, -1.0), axis=-1, keepdims=True)
        wlo_s[...] = jnp.sum(jnp.where(hit & (slot == s_lo), g, 0.0), axis=-1, keepdims=True)
        whi_s[...] = jnp.sum(jnp.where(hit & (slot == s_hi) & (s_hi > s_lo), g, 0.0), axis=-1, keepdims=True)
        s_lo_i = s_lo.astype(jnp.int32)
        s_hi_i = s_hi.astype(jnp.int32)
        slo_s[...] = s_lo_i
        shi_s[...] = s_hi_i
        h = h2_ref[...]
        for c in range(n_slot_chunks):
            col = lax.broadcasted_iota(jnp.int32, (tm, slot_cols), 1) + c * slot_cols
            perm_t = jnp.where((col == s_lo_i) | (col == s_hi_i), 1.0, 0.0).astype(BF16)
            rows = lax.dot_general(perm_t, h, (((0,), (0,)), ((), ())), preferred_element_type=F32)
            xs[c * slot_cols:(c + 1) * slot_cols, :] = rows.astype(BF16)
        xs[MOE_SLOTS:, :] = jnp.zeros((MOE_CHUNK, D_MODEL), BF16)
        ys[...] = jnp.zeros(ys.shape, BF16)

    def ffn(x, j):
        a = jnp.dot(x, w1_ref[j], preferred_element_type=F32)
        b = jnp.dot(x, w3_ref[j], preferred_element_type=F32)
        hid = (a * jax.nn.sigmoid(a)) * b
        return jnp.dot(hid.astype(BF16), w2_ref[j], preferred_element_type=F32).astype(BF16)

    starts = [seg_s[0, step * MOE_ESTEP + j] for j in range(MOE_ESTEP)]
    lens = [seg_s[1, step * MOE_ESTEP + j] for j in range(MOE_ESTEP)]

    for j in range(MOE_ESTEP):
        def extra_body(k, carry, j=j):
            r0 = pl.multiple_of(starts[j] + k * MOE_CHUNK, SEG_ALIGN)
            ys[pl.ds(r0, MOE_CHUNK), :] = ffn(xs[pl.ds(r0, MOE_CHUNK), :], j)
            return carry

        lax.fori_loop(1, (lens[j] + MOE_CHUNK - 1) // MOE_CHUNK, extra_body, 0)

    firsts = [ffn(xs[pl.ds(pl.multiple_of(starts[j], SEG_ALIGN), MOE_CHUNK), :], j) for j in range(MOE_ESTEP)]
    for j in range(MOE_ESTEP):
        ys[pl.ds(pl.multiple_of(starts[j], SEG_ALIGN), MOE_CHUNK), :] = firsts[j]

    @pl.when(step == N_EXPERTS // MOE_ESTEP - 1)
    def _combine():
        s_lo_i, s_hi_i = slo_s[...], shi_s[...]
        w_lo, w_hi = wlo_s[...], whi_s[...]
        acc = x1_ref[...]
        for c in range(n_slot_chunks):
            col = lax.broadcasted_iota(jnp.int32, (tm, slot_cols), 1) + c * slot_cols
            wperm = (jnp.where(col == s_lo_i, w_lo, 0.0) + jnp.where(col == s_hi_i, w_hi, 0.0)).astype(BF16)
            acc = acc + jnp.dot(wperm, ys[c * slot_cols:(c + 1) * slot_cols, :], preferred_element_type=F32)
        o_ref[...] = _rms(acc, gf_ref[...])


def _moe_final(x1, h2, gate, w1, w3, w2, norm_final):
    t = x1.shape[0]
    tm = MOE_TILE
    row = lambda w: pl.BlockSpec((tm, w), lambda i, e: (i, 0))
    tri = jnp.tril(jnp.ones((tm, tm), BF16), -1)
    return pl.pallas_call(
        _moe_final_kernel,
        grid=(t // tm, N_EXPERTS // MOE_ESTEP),
        in_specs=[pl.BlockSpec((tm, D_MODEL), lambda i, e: (i, 0), pipeline_mode=pl.Buffered(1)),
                  row(D_MODEL), row(LANES),
                  pl.BlockSpec((tm, tm), lambda i, e: (0, 0), pipeline_mode=pl.Buffered(1)),
                  pl.BlockSpec((MOE_ESTEP, D_MODEL, D_FF), lambda i, e: (e, 0, 0)),
                  pl.BlockSpec((MOE_ESTEP, D_MODEL, D_FF), lambda i, e: (e, 0, 0)),
                  pl.BlockSpec((MOE_ESTEP, D_FF, D_MODEL), lambda i, e: (e, 0, 0)),
                  pl.BlockSpec((1, D_MODEL), lambda i, e: (0, 0))],
        out_specs=row(D_MODEL),
        out_shape=jax.ShapeDtypeStruct((t, D_MODEL), F32),
        scratch_shapes=[pltpu.VMEM((MOE_SLOTS + MOE_CHUNK, D_MODEL), BF16),
                        pltpu.VMEM((MOE_SLOTS + MOE_CHUNK, D_MODEL), BF16),
                        pltpu.SMEM((2, N_EXPERTS), jnp.int32),
                        pltpu.VMEM((tm, 1), jnp.int32), pltpu.VMEM((tm, 1), jnp.int32),
                        pltpu.VMEM((tm, 1), F32), pltpu.VMEM((tm, 1), F32)],
        compiler_params=pltpu.CompilerParams(
            dimension_semantics=("arbitrary", "arbitrary"), vmem_limit_bytes=VMEM_LIMIT),
        name="moe_final",
    )(x1, h2, gate, tri, w1, w3, w2, norm_final)


def _prep_weights(norm_mix, w_in, attn_sink, w_branch_a, w_branch_b, w_gate, w_out, norm_moe,
                  w_router_group, b_router_group, w_router_expert, b_router_expert, w1, w3, w2, norm_final):
    wi = w_in[0]
    a_w = 3 * A_WIDTH

    def group_cols(gi):
        return jnp.concatenate([wi[:, role * a_w + gi * A_WIDTH: role * a_w + (gi + 1) * A_WIDTH]
                                for role in range(3)], axis=1)

    b_heads = [g * B_REP + c for c in range(B_REP) for g in range(B_KV_HEADS)]
    bq_cols = jnp.concatenate([wi[:, 3 * a_w + h * HEAD_DIM: 3 * a_w + (h + 1) * HEAD_DIM] for h in b_heads],
                              axis=1)
    w_b_rows = jnp.concatenate([w_branch_b[0][h * HEAD_DIM:(h + 1) * HEAD_DIM] for h in b_heads], axis=0)
    w_nat = jnp.concatenate([group_cols(0), bq_cols, wi[:, 3 * a_w + B_Q_WIDTH:]], axis=1).astype(BF16)
    w_g1 = group_cols(1).astype(BF16)
    w_g2 = group_cols(2).astype(BF16)
    pad = LANES - N_EXPERTS - N_GROUPS
    w_router = jnp.concatenate([w_router_expert[0], w_router_group[0],
                                jnp.zeros((D_MODEL, pad), F32)], axis=1)
    b_router = jnp.concatenate([b_router_expert[0].reshape(-1), b_router_group[0],
                                jnp.zeros((pad,), F32)]).reshape(1, LANES)
    return dict(
        norm_mix=norm_mix[0].reshape(1, D_MODEL), w_nat=w_nat, w_g1=w_g1, w_g2=w_g2, sink=attn_sink[0],
        w_a=w_branch_a[0].astype(BF16), w_b=w_b_rows.astype(BF16), w_gate=w_gate[0].astype(BF16),
        w_out=w_out[0].astype(BF16), norm_moe=norm_moe[0].reshape(1, D_MODEL),
        w_router=w_router, b_router=b_router,
        w1=w1[0].astype(BF16), w3=w3[0].astype(BF16), w2=w2[0].astype(BF16),
        norm_final=norm_final.reshape(1, D_MODEL))


def _trunk(x, p):
    b, s, _ = x.shape
    outs = _in_proj(x, p["norm_mix"], p["w_nat"], p["w_g1"], p["w_g2"])
    oa = _mixer_a(outs[:9], s)
    ob = _mixer_b(outs[9], outs[10], outs[11], p["sink"], s)
    t = b * s
    x1, h2, gate = _post_attn(x.reshape(t, D_MODEL), oa.reshape(t, A_WIDTH), ob.reshape(t, B_Q_WIDTH),
                              p["norm_mix"], p["w_gate"], p["w_a"], p["w_b"], p["w_out"], p["norm_moe"],
                              p["w_router"], p["b_router"])
    y = _moe_final(x1, h2, gate, p["w1"], p["w3"], p["w2"], p["norm_final"])
    return y.reshape(b, s, D_MODEL)


def kernel(x_prompt, x_sample, norm_mix, w_in, attn_sink, w_branch_a, w_branch_b, w_gate, w_out, norm_moe,
           w_router_group, b_router_group, w_router_expert, b_router_expert, w1, w3, w2, norm_final):
    p = _prep_weights(norm_mix, w_in, attn_sink, w_branch_a, w_branch_b, w_gate, w_out, norm_moe,
                      w_router_group, b_router_group, w_router_expert, b_router_expert, w1, w3, w2, norm_final)
    return (_trunk(x_prompt, p), _trunk(x_sample, p))
```

```python
import functools

import jax
import jax.numpy as jnp
from jax import lax
from jax.experimental import pallas as pl
from jax.experimental.pallas import tpu as pltpu

D_MODEL = 1024
HEAD_DIM = 64
ROT_DIM = HEAD_DIM // 4
ROT_HALF = ROT_DIM // 2
ROPE_THETA = 500000.0
DILATIONS = (1, 4, 16)
A_HALF = 64
A_BLOCK = 64
A_QBLOCK = 128
A_HEADS = 8
A_WIDTH = A_HEADS * HEAD_DIM
B_Q_HEADS = 8
B_KV_HEADS = 2
B_REP = B_Q_HEADS // B_KV_HEADS
B_WINDOW = 128
B_BLOCK = 128
B_Q_WIDTH = B_Q_HEADS * HEAD_DIM
B_KV_WIDTH = B_KV_HEADS * HEAD_DIM
N_GROUPS = 4
EXPERTS_PER_GROUP = 8
N_EXPERTS = N_GROUPS * EXPERTS_PER_GROUP
D_FF = 256
RMS_EPS = 1e-6
Q_SCALE = HEAD_DIM ** -0.5

LANES = 128
NEG_BIG = -1e30
VMEM_LIMIT = 56 * 1024 * 1024

IN_TILE = 512
ATT_TILE = 1024
A_TILE = 1024
A_STEP_WIDTH = 512
POST_TILE = 1024
POST_SLAB = 256
MOE_TILE = 1024
SEG_ALIGN = 16
MOE_CHUNK = 128
MOE_SLOTS = 2 * MOE_TILE + N_EXPERTS * SEG_ALIGN
MOE_ESTEP = 4
MOE_SORT_COLS = 512

F32 = jnp.float32
BF16 = jnp.bfloat16


def _rms(x, g):
    ms = jnp.mean(x * x, axis=-1, keepdims=True)
    return x * lax.rsqrt(ms + RMS_EPS) * g


def _rope(a, cos, sin, low_half):
    outs = []
    for k in range(a.shape[1] // LANES):
        ak = a[:, k * LANES:(k + 1) * LANES]
        up = pltpu.roll(ak, LANES - ROT_HALF, axis=1)
        dn = pltpu.roll(ak, ROT_HALF, axis=1)
        outs.append(ak * cos + jnp.where(low_half, up, dn) * sin)
    return outs[0] if len(outs) == 1 else jnp.concatenate(outs, axis=1)


def _in_proj_kernel(x_ref, g_ref, cos_ref, sin_ref, wn_ref, w1_ref, w2_ref,
                    q0, k0, v0, q1, k1, v1, q2, k2, v2, bq, bk, bv,
                    xs, h1_s, h2_s, c1_s, s1_s, c2_s, s2_s):
    tm = x_ref.shape[1]
    n_chunks = D_MODEL // LANES
    for c in range(n_chunks):
        xs[c] = x_ref[0, :, c * LANES:(c + 1) * LANES]
    g = g_ref[...]
    low_half = (lax.broadcasted_iota(jnp.int32, (tm, LANES), 1) % HEAD_DIM) < ROT_HALF

    def mm(h, w_ref, c0, n):
        return jnp.dot(h, w_ref[:, c0:c0 + n], preferred_element_type=F32)

    h0 = _rms(x_ref[0], g).astype(BF16)
    cos0, sin0 = cos_ref[...], sin_ref[...]
    q0[0, 0] = (_rope(mm(h0, wn_ref, 0, A_WIDTH), cos0, sin0, low_half) * Q_SCALE).astype(BF16)
    k0[0, 0] = _rope(mm(h0, wn_ref, A_WIDTH, A_WIDTH), cos0, sin0, low_half).astype(BF16)
    v0[0, 0] = mm(h0, wn_ref, 2 * A_WIDTH, A_WIDTH).astype(BF16)
    c = 3 * A_WIDTH
    bq[0] = (_rope(mm(h0, wn_ref, c, B_Q_WIDTH), cos0, sin0, low_half) * Q_SCALE).astype(BF16)
    bk[0] = _rope(mm(h0, wn_ref, c + B_Q_WIDTH, B_KV_WIDTH), cos0, sin0, low_half).astype(BF16)
    bv[0] = mm(h0, wn_ref, c + B_Q_WIDTH + B_KV_WIDTH, B_KV_WIDTH).astype(BF16)

    for d, hs, cs, ss, w_ref, qo, ko, vo in ((DILATIONS[1], h1_s, c1_s, s1_s, w1_ref, q1, k1, v1),
                                             (DILATIONS[2], h2_s, c2_s, s2_s, w2_ref, q2, k2, v2)):
        n = tm // d
        for r in range(d):
            rows = pl.ds(r, n, stride=d)
            xr = jnp.concatenate([xs[c, rows, :] for c in range(n_chunks)], axis=1)
            hs[r * n:(r + 1) * n, :] = _rms(xr, g).astype(BF16)
            cs[r * n:(r + 1) * n, :] = cos_ref[rows, :]
            ss[r * n:(r + 1) * n, :] = sin_ref[rows, :]
        h = hs[...]
        cos, sin = cs[...], ss[...]
        qv = (_rope(mm(h, w_ref, 0, A_WIDTH), cos, sin, low_half) * Q_SCALE).astype(BF16)
        kv = _rope(mm(h, w_ref, A_WIDTH, A_WIDTH), cos, sin, low_half).astype(BF16)
        vv = mm(h, w_ref, 2 * A_WIDTH, A_WIDTH).astype(BF16)
        for r in range(d):
            qo[0, r] = qv[r * n:(r + 1) * n]
            ko[0, r] = kv[r * n:(r + 1) * n]
            vo[0, r] = vv[r * n:(r + 1) * n]


def _rope_tables(s):
    inv = jnp.power(jnp.float32(ROPE_THETA), -jnp.arange(ROT_HALF, dtype=F32) / ROT_HALF)
    ang = jnp.arange(s).astype(F32)[:, None] * inv[None, :]
    cos, sin = jnp.cos(ang), jnp.sin(ang)
    pad_one = jnp.ones((s, HEAD_DIM - ROT_DIM), F32)
    pad_zero = jnp.zeros((s, HEAD_DIM - ROT_DIM), F32)
    cos_h = jnp.concatenate([cos, cos, pad_one], axis=1)
    sin_h = jnp.concatenate([-sin, sin, pad_zero], axis=1)
    return jnp.concatenate([cos_h, cos_h], axis=1), jnp.concatenate([sin_h, sin_h], axis=1)


def _in_proj(x, norm_g, w_nat, w_g1, w_g2):
    b, s, _ = x.shape
    tm = IN_TILE
    cos_t, sin_t = _rope_tables(s)
    const = lambda t, bb: (0, 0)
    one_buf = pl.Buffered(1)
    in_specs = [
        pl.BlockSpec((1, tm, D_MODEL), lambda t, bb: (bb, t, 0)),
        pl.BlockSpec((1, D_MODEL), const),
        pl.BlockSpec((tm, LANES), lambda t, bb: (t, 0)),
        pl.BlockSpec((tm, LANES), lambda t, bb: (t, 0)),
        pl.BlockSpec(w_nat.shape, const, pipeline_mode=one_buf),
        pl.BlockSpec(w_g1.shape, const, pipeline_mode=one_buf),
        pl.BlockSpec(w_g2.shape, const, pipeline_mode=one_buf),
    ]
    out_shape, out_specs = [], []
    for d in DILATIONS:
        for _ in range(3):
            out_shape.append(jax.ShapeDtypeStruct((b, d, s // d, A_WIDTH), BF16))
            out_specs.append(pl.BlockSpec((1, d, tm // d, A_WIDTH), lambda t, bb: (bb, 0, t, 0)))
    for w in (B_Q_WIDTH, B_KV_WIDTH, B_KV_WIDTH):
        out_shape.append(jax.ShapeDtypeStruct((b, s, w), BF16))
        out_specs.append(pl.BlockSpec((1, tm, w), lambda t, bb: (bb, t, 0)))
    scratch = [pltpu.VMEM((D_MODEL // LANES, tm, LANES), F32),
               pltpu.VMEM((tm, D_MODEL), BF16), pltpu.VMEM((tm, D_MODEL), BF16)] + \
              [pltpu.VMEM((tm, LANES), F32) for _ in range(4)]
    return pl.pallas_call(
        _in_proj_kernel,
        grid=(s // tm, b),
        in_specs=in_specs,
        out_specs=out_specs,
        out_shape=out_shape,
        scratch_shapes=scratch,
        compiler_params=pltpu.CompilerParams(
            dimension_semantics=("arbitrary", "arbitrary"), vmem_limit_bytes=VMEM_LIMIT),
        name="in_proj",
    )(x, norm_g.reshape(1, D_MODEL), cos_t, sin_t, w_nat, w_g1, w_g2)


def _attend_heads(blocks, valid, first_half):
    second_half = jnp.logical_not(first_half)
    n_heads = blocks[0][0].shape[1] // HEAD_DIM
    scores = []
    for qb, kw, _ in blocks:
        for h in range(n_heads):
            cs = slice((h // 2) * LANES, (h // 2 + 1) * LANES)
            qp = qb[:, cs]
            qm = jnp.where(first_half if h % 2 == 0 else second_half, qp, jnp.zeros_like(qp))
            scores.append(lax.dot_general(qm, kw[:, cs], (((1,), (1,)), ((), ())),
                                          preferred_element_type=F32))
    sc = jnp.where(valid[None], jnp.stack(scores), NEG_BIG)
    m = jnp.max(sc, axis=-1, keepdims=True)
    p = jnp.exp(sc - m)
    l = jnp.sum(p, axis=-1, keepdims=True)
    pb = p.astype(BF16)
    results = []
    for bi, (_, _, vw) in enumerate(blocks):
        out = []
        for hp in range(n_heads // 2):
            i0 = bi * n_heads + 2 * hp
            vp = vw[:, hp * LANES:(hp + 1) * LANES]
            o0 = jnp.dot(pb[i0], vp, preferred_element_type=F32)
            o1 = jnp.dot(pb[i0 + 1], vp, preferred_element_type=F32)
            out.append((jnp.where(first_half, o0, o1), jnp.where(first_half, m[i0], m[i0 + 1]),
                        jnp.where(first_half, l[i0], l[i0 + 1])))
        results.append(out)
    return results


def _mixer_a_kernel(*refs, seq):
    ins, (o_ref, kext, vext, acc_o, acc_m, acc_l) = refs[:21], refs[21:]
    t = pl.program_id(1)
    tq = A_TILE
    halo = A_HALF

    first_group = len(DILATIONS) - 1
    for gi in reversed(range(len(DILATIONS))):
        d = DILATIONS[gi]
        q, kc, kp, kn, vc, vp, vn = ins[7 * gi:7 * gi + 7]
        n = tq // d
        blk = min(n, A_QBLOCK)
        nblk = n // blk
        win = -(-(blk + 2 * halo) // LANES) * LANES
        tail = win - (blk + 2 * halo) if n == blk else 0
        ld = seq // d
        col = lax.broadcasted_iota(jnp.int32, (blk, win), 1)
        diff = col - lax.broadcasted_iota(jnp.int32, (blk, win), 0)
        band = (diff >= 0) & (diff <= 2 * halo)
        first_half = lax.broadcasted_iota(jnp.int32, (blk, LANES), 1) < HEAD_DIM

        pair = A_QBLOCK // blk

        def residue_body(rp, carry, q=q, kc=kc, kp=kp, kn=kn, vc=vc, vp=vp, vn=vn, d=d, n=n, nblk=nblk,
                         ld=ld, blk=blk, win=win, col=col, band=band, first_half=first_half,
                         first=(gi == first_group), pair=pair, tail=tail):
            for u in range(pair):
                r = rp * pair + u
                base = u * win
                kext[base:base + halo, :] = kp[0, r]
                kext[base + halo:base + halo + n, :] = kc[0, r]
                kext[base + halo + n:base + 2 * halo + n, :] = kn[0, r]
                vext[base:base + halo, :] = vp[0, r]
                vext[base + halo:base + halo + n, :] = vc[0, r]
                vext[base + halo + n:base + 2 * halo + n, :] = vn[0, r]
                if tail:
                    kext[base + 2 * halo + n:base + win, :] = jnp.zeros((tail, kext.shape[1]), BF16)
                    vext[base + 2 * halo + n:base + win, :] = jnp.zeros((tail, vext.shape[1]), BF16)

            def block_body(j, carry2):
                j0 = pl.multiple_of(j * blk, blk)
                qbase = t * n + j * blk
                valid = band & (col >= halo - qbase) & (col < ld - qbase + halo)
                blocks = [(q[0, rp * pair + u, pl.ds(j0, blk), :],
                           kext[pl.ds(u * win + j0, win), :],
                           vext[pl.ds(u * win + j0, win), :]) for u in range(pair)]
                results = _attend_heads(blocks, valid, first_half)
                for u, parts in enumerate(results):
                    if d == 1:
                        rows = pl.ds(j0, blk)
                    else:
                        rows = pl.ds(j * (blk * d) + rp * pair + u, blk, stride=d)
                    for hp, (o_b, m_b, l_b) in enumerate(parts):
                        if first:
                            acc_o[hp, rows, :] = o_b
                            acc_l[hp, rows, :] = l_b
                            acc_m[hp, rows, :] = m_b
                        else:
                            m_old = acc_m[hp, rows, :]
                            m_new = jnp.maximum(m_old, m_b)
                            a_old = jnp.exp(m_old - m_new)
                            a_new = jnp.exp(m_b - m_new)
                            acc_o[hp, rows, :] = acc_o[hp, rows, :] * a_old + o_b * a_new
                            acc_l[hp, rows, :] = acc_l[hp, rows, :] * a_old + l_b * a_new
                            acc_m[hp, rows, :] = m_new
                return carry2

            lax.fori_loop(0, nblk, block_body, 0)
            return carry

        lax.fori_loop(0, d // pair, residue_body, 0)

    o_ref[0] = jnp.concatenate([acc_o[hp] / acc_l[hp] for hp in range(A_STEP_WIDTH // LANES)],
                               axis=1).astype(BF16)


def _mixer_a(qkv, seq):
    b = qkv[0].shape[0]
    tq = A_TILE
    blk = A_BLOCK
    aw = A_STEP_WIDTH
    args, in_specs = [], []
    for gi, d in enumerate(DILATIONS):
        q, k, v = qkv[3 * gi:3 * gi + 3]
        n = tq // d
        nblk = n // blk
        last = seq // d // blk - 1
        cur = pl.BlockSpec((1, d, n, aw), lambda bb, t, hh: (bb, 0, t, hh))
        prev = pl.BlockSpec((1, d, blk, aw),
                            lambda bb, t, hh, nblk=nblk: (bb, 0, jnp.maximum(t * nblk - 1, 0), hh))
        nxt = pl.BlockSpec((1, d, blk, aw),
                           lambda bb, t, hh, nblk=nblk, last=last: (bb, 0, jnp.minimum((t + 1) * nblk, last), hh))
        args += [q, k, k, k, v, v, v]
        in_specs += [cur, cur, prev, nxt, cur, prev, nxt]
    return pl.pallas_call(
        functools.partial(_mixer_a_kernel, seq=seq),
        grid=(b, seq // tq, A_WIDTH // aw),
        in_specs=in_specs,
        out_specs=pl.BlockSpec((1, tq, aw), lambda bb, t, hh: (bb, t, hh)),
        out_shape=jax.ShapeDtypeStruct((b, seq, A_WIDTH), BF16),
        scratch_shapes=[pltpu.VMEM((tq + 2 * blk, aw), BF16), pltpu.VMEM((tq + 2 * blk, aw), BF16),
                        pltpu.VMEM((aw // LANES, tq, LANES), F32),
                        pltpu.VMEM((aw // LANES, tq, LANES), F32),
                        pltpu.VMEM((aw // LANES, tq, LANES), F32)],
        compiler_params=pltpu.CompilerParams(
            dimension_semantics=("arbitrary", "arbitrary", "arbitrary"), vmem_limit_bytes=VMEM_LIMIT),
        name="mixer_a",
    )(*args)


def _mixer_b_kernel(sink_ref, q, kc, kp, kn, vc, vp, vn, o_ref, kext, vext, *, seq):
    t = pl.program_id(1)
    tq = ATT_TILE
    blk = B_BLOCK
    win = 3 * blk
    kext[0:blk, :] = kp[0]
    kext[blk:blk + tq, :] = kc[0]
    kext[blk + tq:2 * blk + tq, :] = kn[0]
    vext[0:blk, :] = vp[0]
    vext[blk:blk + tq, :] = vc[0]
    vext[blk + tq:2 * blk + tq, :] = vn[0]
    col = lax.broadcasted_iota(jnp.int32, (blk, win), 1)
    row = lax.broadcasted_iota(jnp.int32, (blk, win), 0)
    diff = col - row
    first_half = lax.broadcasted_iota(jnp.int32, (blk, LANES), 1) < HEAD_DIM
    second_half = jnp.logical_not(first_half)

    def block_body(j, carry):
        j0 = pl.multiple_of(j * blk, blk)
        qb = q[0, pl.ds(j0, blk), :]
        kw = kext[pl.ds(j0, win), :]
        vw = vext[pl.ds(j0, win), :]
        qbase = t * tq + j * blk
        valid = ((diff >= 0) & (diff <= 2 * B_WINDOW)
                 & (col >= blk - qbase) & (col < seq - qbase + blk))
        scores, sinks = [], []
        for c in range(B_Q_HEADS // 2):
            qp = qb[:, c * LANES:(c + 1) * LANES]
            for g, keep in enumerate((first_half, second_half)):
                qm = jnp.where(keep, qp, jnp.zeros_like(qp))
                scores.append(lax.dot_general(qm, kw, (((1,), (1,)), ((), ())), preferred_element_type=F32))
                sinks.append(jnp.full((1, LANES), sink_ref[g * B_REP + c], F32))
        sc = jnp.where(valid[None], jnp.stack(scores), NEG_BIG)
        sk = jnp.stack(sinks)
        part = sc[..., 0:LANES]
        for k in range(1, win // LANES):
            part = jnp.maximum(part, sc[..., k * LANES:(k + 1) * LANES])
        m = jnp.max(jnp.maximum(part, sk), axis=-1, keepdims=True)
        p = jnp.exp(sc - m)
        denom = jnp.sum(p, axis=-1, keepdims=True) + jnp.exp(sk - m)
        pb = p.astype(BF16)
        parts = []
        for c in range(B_Q_HEADS // 2):
            o0 = jnp.dot(pb[2 * c], vw, preferred_element_type=F32) / denom[2 * c]
            o1 = jnp.dot(pb[2 * c + 1], vw, preferred_element_type=F32) / denom[2 * c + 1]
            parts.append(jnp.where(first_half, o0, o1))
        o_ref[0, pl.ds(j0, blk), :] = jnp.concatenate(parts, axis=1).astype(BF16)
        return carry

    lax.fori_loop(0, tq // blk, block_body, 0)


def _mixer_b(q, k, v, sink, seq):
    b = q.shape[0]
    tq = ATT_TILE
    blk = B_BLOCK
    nblk = tq // blk
    last = seq // blk - 1
    cur = pl.BlockSpec((1, tq, B_KV_WIDTH), lambda bb, t: (bb, t, 0))
    prev = pl.BlockSpec((1, blk, B_KV_WIDTH), lambda bb, t: (bb, jnp.maximum(t * nblk - 1, 0), 0))
    nxt = pl.BlockSpec((1, blk, B_KV_WIDTH), lambda bb, t: (bb, jnp.minimum((t + 1) * nblk, last), 0))
    return pl.pallas_call(
        functools.partial(_mixer_b_kernel, seq=seq),
        grid=(b, seq // tq),
        in_specs=[pl.BlockSpec(memory_space=pltpu.SMEM),
                  pl.BlockSpec((1, tq, B_Q_WIDTH), lambda bb, t: (bb, t, 0)),
                  cur, prev, nxt, cur, prev, nxt],
        out_specs=pl.BlockSpec((1, tq, B_Q_WIDTH), lambda bb, t: (bb, t, 0)),
        out_shape=jax.ShapeDtypeStruct((b, seq, B_Q_WIDTH), BF16),
        scratch_shapes=[pltpu.VMEM((tq + 2 * blk, B_KV_WIDTH), BF16),
                        pltpu.VMEM((tq + 2 * blk, B_KV_WIDTH), BF16)],
        compiler_params=pltpu.CompilerParams(
            dimension_semantics=("arbitrary", "arbitrary"), vmem_limit_bytes=VMEM_LIMIT),
        name="mixer_b",
    )(sink, q, k, k, k, v, v, v)


def _post_attn_kernel(x_ref, oa_ref, ob_ref, gmix_ref, wg_ref, wa_ref, wb_ref, wo_ref, gmoe_ref,
                      wr_ref, br_ref, x1_ref, h2_ref, gate_ref):
    wr = wr_ref[...]
    wr_hi = wr.astype(BF16)
    wr_lo = (wr - wr_hi.astype(F32)).astype(BF16)
    wr3 = jnp.concatenate([wr_hi, wr_hi, wr_lo], axis=0)
    for s in range(x_ref.shape[0] // POST_SLAB):
        rows = slice(s * POST_SLAB, (s + 1) * POST_SLAB)
        _post_attn_rows(x_ref, oa_ref, ob_ref, gmix_ref, wg_ref, wa_ref, wb_ref, wo_ref, gmoe_ref, wr3,
                        br_ref, x1_ref, h2_ref, gate_ref, rows)


def _post_attn_rows(x_ref, oa_ref, ob_ref, gmix_ref, wg_ref, wa_ref, wb_ref, wo_ref, gmoe_ref, wr3,
                    br_ref, x1_ref, h2_ref, gate_ref, rows):
    x = x_ref[rows, :]
    h = _rms(x, gmix_ref[...]).astype(BF16)
    gates = jax.nn.sigmoid(jnp.dot(h, wg_ref[...], preferred_element_type=F32))
    ya = jnp.dot(oa_ref[rows, :], wa_ref[...], preferred_element_type=F32)
    yb = jnp.dot(ob_ref[rows, :], wb_ref[...], preferred_element_type=F32)
    mix = gates[:, :D_MODEL] * ya + gates[:, D_MODEL:] * yb
    x1 = x + jnp.dot(mix.astype(BF16), wo_ref[...], preferred_element_type=F32)
    x1_ref[rows, :] = x1
    h2 = _rms(x1, gmoe_ref[...])
    h2_hi = h2.astype(BF16)
    h2_ref[rows, :] = h2_hi

    h2_lo = (h2 - h2_hi.astype(F32)).astype(BF16)
    logits = jnp.dot(jnp.concatenate([h2_hi, h2_lo, h2_hi], axis=1), wr3,
                     preferred_element_type=F32) + br_ref[...]
    tm = logits.shape[0]
    lane = lax.broadcasted_iota(jnp.int32, (tm, LANES), 1)
    big = jnp.int32(LANES)
    is_group = (lane >= N_EXPERTS) & (lane < N_EXPERTS + N_GROUPS)
    gl = jnp.where(is_group, logits, NEG_BIG)
    gmax = jnp.max(gl, axis=-1, keepdims=True)
    g_idx = jnp.min(jnp.where(gl == gmax, lane, big), axis=-1, keepdims=True) - N_EXPERTS
    g_w = 1.0 / jnp.sum(jnp.exp(gl - gmax), axis=-1, keepdims=True)
    in_group = (lane < N_EXPERTS) & (lax.shift_right_logical(lane, 3) == g_idx)
    sel = jnp.where(in_group, logits, NEG_BIG)
    v1 = jnp.max(sel, axis=-1, keepdims=True)
    i1 = jnp.min(jnp.where(sel == v1, lane, big), axis=-1, keepdims=True)
    sel2 = jnp.where(lane == i1, NEG_BIG, sel)
    v2 = jnp.max(sel2, axis=-1, keepdims=True)
    i2 = jnp.min(jnp.where(sel2 == v2, lane, big), axis=-1, keepdims=True)
    e2 = jnp.exp(v2 - v1)
    w1 = g_w / (1.0 + e2)
    w2 = g_w * e2 / (1.0 + e2)
    gate_ref[rows, :] = jnp.where(lane == i1, w1, 0.0) + jnp.where(lane == i2, w2, 0.0)


def _post_attn(x2d, oa, ob, norm_mix, w_gate, w_a, w_b, w_out, norm_moe, w_router, b_router):
    t = x2d.shape[0]
    tm = POST_TILE
    const = lambda i: (0, 0)
    row = lambda w: pl.BlockSpec((tm, w), lambda i: (i, 0))
    full = lambda a: pl.BlockSpec(a.shape, const, pipeline_mode=pl.Buffered(1))
    return pl.pallas_call(
        _post_attn_kernel,
        grid=(t // tm,),
        in_specs=[row(D_MODEL), row(A_WIDTH), row(B_Q_WIDTH), full(norm_mix), full(w_gate), full(w_a),
                  full(w_b), full(w_out), full(norm_moe), full(w_router), full(b_router)],
        out_specs=[row(D_MODEL), row(D_MODEL), row(LANES)],
        out_shape=[jax.ShapeDtypeStruct((t, D_MODEL), F32), jax.ShapeDtypeStruct((t, D_MODEL), BF16),
                   jax.ShapeDtypeStruct((t, LANES), F32)],
        compiler_params=pltpu.CompilerParams(
            dimension_semantics=("arbitrary",), vmem_limit_bytes=VMEM_LIMIT),
        name="post_attn",
    )(x2d, oa, ob, norm_mix, w_gate, w_a, w_b, w_out, norm_moe, w_router, b_router)


def _moe_final_kernel(x1_ref, h2_ref, gate_ref, tri_ref, w1_ref, w3_ref, w2_ref, gf_ref, o_ref,
                      xs, ys, seg_s, prog_s, slo_s, shi_s, wlo_s, whi_s):
    step = pl.program_id(1)
    tm = x1_ref.shape[0]

    @pl.when(step == 0)
    def _plan():
        g = gate_ref[...]
        hit = g > 0.0
        onehot = jnp.where(hit, 1.0, 0.0)
        count = jnp.sum(onehot, axis=0, keepdims=True)
        padded = jnp.ceil(count * (1.0 / SEG_ALIGN)) * SEG_ALIGN
        r_i = lax.broadcasted_iota(jnp.int32, (LANES, LANES), 0)
        c_i = lax.broadcasted_iota(jnp.int32, (LANES, LANES), 1)
        upper = jnp.where(r_i < c_i, 1.0, 0.0)
        seg = jnp.dot(jnp.broadcast_to(padded, (8, LANES)), upper, preferred_element_type=F32,
                      precision=lax.Precision.HIGHEST)[0:1]
        seg_i = seg.astype(jnp.int32)
        padded_i = padded.astype(jnp.int32)
        for ex in range(N_EXPERTS):
            seg_s[0, ex] = seg_i[0, ex]
            seg_s[1, ex] = padded_i[0, ex]
        rank = jnp.dot(tri_ref[...], onehot.astype(BF16), preferred_element_type=F32)
        slot = seg + rank
        s_lo = jnp.min(jnp.where(hit, slot, 1e9), axis=-1, keepdims=True)
        s_hi = jnp.max(jnp.where(hit, slot, -1.0), axis=-1, keepdims=True)
        wlo_s[...] = jnp.sum(jnp.where(hit & (slot == s_lo), g, 0.0), axis=-1, keepdims=True)
        whi_s[...] = jnp.sum(jnp.where(hit & (slot == s_hi) & (s_hi > s_lo), g, 0.0), axis=-1, keepdims=True)
        s_lo_i = s_lo.astype(jnp.int32)
        s_hi_i = s_hi.astype(jnp.int32)
        slo_s[...] = s_lo_i
        shi_s[...] = s_hi_i
        prog_s[0] = 0
        prog_s[1] = 0
        xs[MOE_SLOTS:, :] = jnp.zeros((MOE_CHUNK, D_MODEL), BF16)
        ys[...] = jnp.zeros(ys.shape, BF16)
        o_ref[...] = x1_ref[...]

    first = step * MOE_ESTEP
    last = first + MOE_ESTEP - 1
    starts = [seg_s[0, first + j] for j in range(MOE_ESTEP)]
    lens = [seg_s[1, first + j] for j in range(MOE_ESTEP)]

    def sort_chunk(c, carry):
        c0 = pl.multiple_of(c * MOE_SORT_COLS, MOE_SORT_COLS)
        col = lax.broadcasted_iota(jnp.int32, (tm, MOE_SORT_COLS), 1) + c0
        perm_t = jnp.where((col == slo_s[...]) | (col == shi_s[...]), 1.0, 0.0).astype(BF16)
        rows = lax.dot_general(perm_t, h2_ref[...], (((0,), (0,)), ((), ())), preferred_element_type=F32)
        xs[pl.ds(c0, MOE_SORT_COLS), :] = rows.astype(BF16)
        return carry

    read_end = starts[-1] + lens[-1] + MOE_CHUNK
    sorted_to = jnp.minimum((read_end + MOE_SORT_COLS - 1) // MOE_SORT_COLS, MOE_SLOTS // MOE_SORT_COLS)
    lax.fori_loop(prog_s[0], sorted_to, sort_chunk, 0)
    prog_s[0] = jnp.maximum(prog_s[0], sorted_to)

    def ffn(x, j):
        a = jnp.dot(x, w1_ref[j], preferred_element_type=F32)
        b = jnp.dot(x, w3_ref[j], preferred_element_type=F32)
        hid = (a * jax.nn.sigmoid(a)) * b
        return jnp.dot(hid.astype(BF16), w2_ref[j], preferred_element_type=F32).astype(BF16)

    for j in range(MOE_ESTEP):
        def extra_body(k, carry, j=j):
            r0 = pl.multiple_of(starts[j] + k * MOE_CHUNK, SEG_ALIGN)
            ys[pl.ds(r0, MOE_CHUNK), :] = ffn(xs[pl.ds(r0, MOE_CHUNK), :], j)
            return carry

        lax.fori_loop(1, (lens[j] + MOE_CHUNK - 1) // MOE_CHUNK, extra_body, 0)

    firsts = [ffn(xs[pl.ds(pl.multiple_of(starts[j], SEG_ALIGN), MOE_CHUNK), :], j) for j in range(MOE_ESTEP)]
    for j in range(MOE_ESTEP):
        ys[pl.ds(pl.multiple_of(starts[j], SEG_ALIGN), MOE_CHUNK), :] = firsts[j]

    def combine_chunk(c, carry):
        c0 = pl.multiple_of(c * MOE_SORT_COLS, MOE_SORT_COLS)
        col = lax.broadcasted_iota(jnp.int32, (tm, MOE_SORT_COLS), 1) + c0
        wperm = (jnp.where(col == slo_s[...], wlo_s[...], 0.0)
                 + jnp.where(col == shi_s[...], whi_s[...], 0.0)).astype(BF16)
        o_ref[...] += jnp.dot(wperm, ys[pl.ds(c0, MOE_SORT_COLS), :], preferred_element_type=F32)
        return carry

    is_last = step == N_EXPERTS // MOE_ESTEP - 1
    used = starts[-1] + lens[-1]
    next_start = seg_s[0, jnp.minimum(last + 1, N_EXPERTS - 1)]
    final_to = jnp.where(is_last, (used + MOE_SORT_COLS - 1) // MOE_SORT_COLS, next_start // MOE_SORT_COLS)
    lax.fori_loop(prog_s[1], final_to, combine_chunk, 0)
    prog_s[1] = jnp.maximum(prog_s[1], final_to)

    @pl.when(is_last)
    def _finish():
        o_ref[...] = _rms(o_ref[...], gf_ref[...])


def _moe_final(x1, h2, gate, w1, w3, w2, norm_final):
    t = x1.shape[0]
    tm = MOE_TILE
    row = lambda w: pl.BlockSpec((tm, w), lambda i, e: (i, 0))
    tri = jnp.tril(jnp.ones((tm, tm), BF16), -1)
    return pl.pallas_call(
        _moe_final_kernel,
        grid=(t // tm, N_EXPERTS // MOE_ESTEP),
        in_specs=[pl.BlockSpec((tm, D_MODEL), lambda i, e: (i, 0), pipeline_mode=pl.Buffered(1)),
                  row(D_MODEL), row(LANES),
                  pl.BlockSpec((tm, tm), lambda i, e: (0, 0), pipeline_mode=pl.Buffered(1)),
                  pl.BlockSpec((MOE_ESTEP, D_MODEL, D_FF), lambda i, e: (e, 0, 0)),
                  pl.BlockSpec((MOE_ESTEP, D_MODEL, D_FF), lambda i, e: (e, 0, 0)),
                  pl.BlockSpec((MOE_ESTEP, D_FF, D_MODEL), lambda i, e: (e, 0, 0)),
                  pl.BlockSpec((1, D_MODEL), lambda i, e: (0, 0))],
        out_specs=row(D_MODEL),
        out_shape=jax.ShapeDtypeStruct((t, D_MODEL), F32),
        scratch_shapes=[pltpu.VMEM((MOE_SLOTS + MOE_CHUNK, D_MODEL), BF16),
                        pltpu.VMEM((MOE_SLOTS + MOE_CHUNK, D_MODEL), BF16),
                        pltpu.SMEM((2, N_EXPERTS), jnp.int32), pltpu.SMEM((2,), jnp.int32),
                        pltpu.VMEM((tm, 1), jnp.int32), pltpu.VMEM((tm, 1), jnp.int32),
                        pltpu.VMEM((tm, 1), F32), pltpu.VMEM((tm, 1), F32)],
        compiler_params=pltpu.CompilerParams(
            dimension_semantics=("arbitrary", "arbitrary"), vmem_limit_bytes=VMEM_LIMIT),
        name="moe_final",
    )(x1, h2, gate, tri, w1, w3, w2, norm_final)


def _prep_weights(norm_mix, w_in, attn_sink, w_branch_a, w_branch_b, w_gate, w_out, norm_moe,
                  w_router_group, b_router_group, w_router_expert, b_router_expert, w1, w3, w2, norm_final):
    wi = w_in[0]
    a_w = 3 * A_WIDTH

    def group_cols(gi):
        return jnp.concatenate([wi[:, role * a_w + gi * A_WIDTH: role * a_w + (gi + 1) * A_WIDTH]
                                for role in range(3)], axis=1)

    b_heads = [g * B_REP + c for c in range(B_REP) for g in range(B_KV_HEADS)]
    bq_cols = jnp.concatenate([wi[:, 3 * a_w + h * HEAD_DIM: 3 * a_w + (h + 1) * HEAD_DIM] for h in b_heads],
                              axis=1)
    w_b_rows = jnp.concatenate([w_branch_b[0][h * HEAD_DIM:(h + 1) * HEAD_DIM] for h in b_heads], axis=0)
    w_nat = jnp.concatenate([group_cols(0), bq_cols, wi[:, 3 * a_w + B_Q_WIDTH:]], axis=1).astype(BF16)
    w_g1 = group_cols(1).astype(BF16)
    w_g2 = group_cols(2).astype(BF16)
    pad = LANES - N_EXPERTS - N_GROUPS
    w_router = jnp.concatenate([w_router_expert[0], w_router_group[0],
                                jnp.zeros((D_MODEL, pad), F32)], axis=1)
    b_router = jnp.concatenate([b_router_expert[0].reshape(-1), b_router_group[0],
                                jnp.zeros((pad,), F32)]).reshape(1, LANES)
    return dict(
        norm_mix=norm_mix[0].reshape(1, D_MODEL), w_nat=w_nat, w_g1=w_g1, w_g2=w_g2, sink=attn_sink[0],
        w_a=w_branch_a[0].astype(BF16), w_b=w_b_rows.astype(BF16), w_gate=w_gate[0].astype(BF16),
        w_out=w_out[0].astype(BF16), norm_moe=norm_moe[0].reshape(1, D_MODEL),
        w_router=w_router, b_router=b_router,
        w1=w1[0].astype(BF16), w3=w3[0].astype(BF16), w2=w2[0].astype(BF16),
        norm_final=norm_final.reshape(1, D_MODEL))


def _trunk(x, p):
    b, s, _ = x.shape
    outs = _in_proj(x, p["norm_mix"], p["w_nat"], p["w_g1"], p["w_g2"])
    oa = _mixer_a(outs[:9], s)
    ob = _mixer_b(outs[9], outs[10], outs[11], p["sink"], s)
    t = b * s
    x1, h2, gate = _post_attn(x.reshape(t, D_MODEL), oa.reshape(t, A_WIDTH), ob.reshape(t, B_Q_WIDTH),
                              p["norm_mix"], p["w_gate"], p["w_a"], p["w_b"], p["w_out"], p["norm_moe"],
                              p["w_router"], p["b_router"])
    y = _moe_final(x1, h2, gate, p["w1"], p["w3"], p["w2"], p["norm_final"])
    return y.reshape(b, s, D_MODEL)


def kernel(x_prompt, x_sample, norm_mix, w_in, attn_sink, w_branch_a, w_branch_b, w_gate, w_out, norm_moe,
           w_router_group, b_router_group, w_router_expert, b_router_expert, w1, w3, w2, norm_final):
    p = _prep_weights(norm_mix, w_in, attn_sink, w_branch_a, w_branch_b, w_gate, w_out, norm_moe,
                      w_router_group, b_router_group, w_router_expert, b_router_expert, w1, w3, w2, norm_final)
    return (_trunk(x_prompt, p), _trunk(x_sample, p))
```

```python
import functools

import jax
import jax.numpy as jnp
from jax import lax
from jax.experimental import pallas as pl
from jax.experimental.pallas import tpu as pltpu

D_MODEL = 1024
HEAD_DIM = 64
ROT_DIM = HEAD_DIM // 4
ROT_HALF = ROT_DIM // 2
ROPE_THETA = 500000.0
DILATIONS = (1, 4, 16)
A_HALF = 64
A_BLOCK = 64
A_QBLOCK = 128
A_HEADS = 8
A_WIDTH = A_HEADS * HEAD_DIM
B_Q_HEADS = 8
B_KV_HEADS = 2
B_REP = B_Q_HEADS // B_KV_HEADS
B_WINDOW = 128
B_BLOCK = 128
B_Q_WIDTH = B_Q_HEADS * HEAD_DIM
B_KV_WIDTH = B_KV_HEADS * HEAD_DIM
N_GROUPS = 4
EXPERTS_PER_GROUP = 8
N_EXPERTS = N_GROUPS * EXPERTS_PER_GROUP
D_FF = 256
RMS_EPS = 1e-6
Q_SCALE = HEAD_DIM ** -0.5

LANES = 128
NEG_BIG = -1e30
VMEM_LIMIT = 60000 * 1024

IN_TILE = 512
ATT_TILE = 1024
A_TILE = 1024
A_STEP_WIDTH = 512
POST_TILE = 1024
POST_SLAB = 256
MOE_TILE = 1024
SEG_ALIGN = 16
MOE_CHUNK = 128
MOE_SLOTS = 2 * MOE_TILE + N_EXPERTS * SEG_ALIGN
MOE_ESTEP = 4
MOE_SORT_COLS = 512

F32 = jnp.float32
BF16 = jnp.bfloat16


def _rms(x, g):
    ms = jnp.mean(x * x, axis=-1, keepdims=True)
    return x * lax.rsqrt(ms + RMS_EPS) * g


def _rope(a, cos, sin, low_half):
    outs = []
    for k in range(a.shape[1] // LANES):
        ak = a[:, k * LANES:(k + 1) * LANES]
        up = pltpu.roll(ak, LANES - ROT_HALF, axis=1)
        dn = pltpu.roll(ak, ROT_HALF, axis=1)
        outs.append(ak * cos + jnp.where(low_half, up, dn) * sin)
    return outs[0] if len(outs) == 1 else jnp.concatenate(outs, axis=1)


def _in_proj_kernel(x_ref, g_ref, cos_ref, sin_ref, wn_ref, w1_ref, w2_ref,
                    q0, k0, v0, q1, k1, v1, q2, k2, v2, bq, bk, bv,
                    xs, h1_s, h2_s, c1_s, s1_s, c2_s, s2_s):
    tm = x_ref.shape[1]
    n_chunks = D_MODEL // LANES
    for c in range(n_chunks):
        xs[c] = x_ref[0, :, c * LANES:(c + 1) * LANES]
    g = g_ref[...]
    low_half = (lax.broadcasted_iota(jnp.int32, (tm, LANES), 1) % HEAD_DIM) < ROT_HALF

    def mm(h, w_ref, c0, n):
        return jnp.dot(h, w_ref[:, c0:c0 + n], preferred_element_type=F32)

    h0 = _rms(x_ref[0], g).astype(BF16)
    cos0, sin0 = cos_ref[...], sin_ref[...]
    q0[0, 0] = (_rope(mm(h0, wn_ref, 0, A_WIDTH), cos0, sin0, low_half) * Q_SCALE).astype(BF16)
    k0[0, 0] = _rope(mm(h0, wn_ref, A_WIDTH, A_WIDTH), cos0, sin0, low_half).astype(BF16)
    v0[0, 0] = mm(h0, wn_ref, 2 * A_WIDTH, A_WIDTH).astype(BF16)
    c = 3 * A_WIDTH
    bq[0] = (_rope(mm(h0, wn_ref, c, B_Q_WIDTH), cos0, sin0, low_half) * Q_SCALE).astype(BF16)
    bk[0] = _rope(mm(h0, wn_ref, c + B_Q_WIDTH, B_KV_WIDTH), cos0, sin0, low_half).astype(BF16)
    bv[0] = mm(h0, wn_ref, c + B_Q_WIDTH + B_KV_WIDTH, B_KV_WIDTH).astype(BF16)

    for d, hs, cs, ss, w_ref, qo, ko, vo in ((DILATIONS[1], h1_s, c1_s, s1_s, w1_ref, q1, k1, v1),
                                             (DILATIONS[2], h2_s, c2_s, s2_s, w2_ref, q2, k2, v2)):
        n = tm // d
        for r in range(d):
            rows = pl.ds(r, n, stride=d)
            xr = jnp.concatenate([xs[c, rows, :] for c in range(n_chunks)], axis=1)
            hs[r * n:(r + 1) * n, :] = _rms(xr, g).astype(BF16)
            cs[r * n:(r + 1) * n, :] = cos_ref[rows, :]
            ss[r * n:(r + 1) * n, :] = sin_ref[rows, :]
        h = hs[...]
        cos, sin = cs[...], ss[...]
        qv = (_rope(mm(h, w_ref, 0, A_WIDTH), cos, sin, low_half) * Q_SCALE).astype(BF16)
        kv = _rope(mm(h, w_ref, A_WIDTH, A_WIDTH), cos, sin, low_half).astype(BF16)
        vv = mm(h, w_ref, 2 * A_WIDTH, A_WIDTH).astype(BF16)
        for r in range(d):
            qo[0, r] = qv[r * n:(r + 1) * n]
            ko[0, r] = kv[r * n:(r + 1) * n]
            vo[0, r] = vv[r * n:(r + 1) * n]


def _rope_tables(s):
    inv = jnp.power(jnp.float32(ROPE_THETA), -jnp.arange(ROT_HALF, dtype=F32) / ROT_HALF)
    ang = jnp.arange(s).astype(F32)[:, None] * inv[None, :]
    cos, sin = jnp.cos(ang), jnp.sin(ang)
    pad_one = jnp.ones((s, HEAD_DIM - ROT_DIM), F32)
    pad_zero = jnp.zeros((s, HEAD_DIM - ROT_DIM), F32)
    cos_h = jnp.concatenate([cos, cos, pad_one], axis=1)
    sin_h = jnp.concatenate([-sin, sin, pad_zero], axis=1)
    return jnp.concatenate([cos_h, cos_h], axis=1), jnp.concatenate([sin_h, sin_h], axis=1)


def _in_proj(x, norm_g, w_nat, w_g1, w_g2):
    b, s, _ = x.shape
    tm = IN_TILE
    cos_t, sin_t = _rope_tables(s)
    const = lambda t, bb: (0, 0)
    one_buf = pl.Buffered(1)
    in_specs = [
        pl.BlockSpec((1, tm, D_MODEL), lambda t, bb: (bb, t, 0)),
        pl.BlockSpec((1, D_MODEL), const),
        pl.BlockSpec((tm, LANES), lambda t, bb: (t, 0)),
        pl.BlockSpec((tm, LANES), lambda t, bb: (t, 0)),
        pl.BlockSpec(w_nat.shape, const, pipeline_mode=one_buf),
        pl.BlockSpec(w_g1.shape, const, pipeline_mode=one_buf),
        pl.BlockSpec(w_g2.shape, const, pipeline_mode=one_buf),
    ]
    out_shape, out_specs = [], []
    for d in DILATIONS:
        for _ in range(3):
            out_shape.append(jax.ShapeDtypeStruct((b, d, s // d, A_WIDTH), BF16))
            out_specs.append(pl.BlockSpec((1, d, tm // d, A_WIDTH), lambda t, bb: (bb, 0, t, 0)))
    for w in (B_Q_WIDTH, B_KV_WIDTH, B_KV_WIDTH):
        out_shape.append(jax.ShapeDtypeStruct((b, s, w), BF16))
        out_specs.append(pl.BlockSpec((1, tm, w), lambda t, bb: (bb, t, 0)))
    scratch = [pltpu.VMEM((D_MODEL // LANES, tm, LANES), F32),
               pltpu.VMEM((tm, D_MODEL), BF16), pltpu.VMEM((tm, D_MODEL), BF16)] + \
              [pltpu.VMEM((tm, LANES), F32) for _ in range(4)]
    return pl.pallas_call(
        _in_proj_kernel,
        grid=(s // tm, b),
        in_specs=in_specs,
        out_specs=out_specs,
        out_shape=out_shape,
        scratch_shapes=scratch,
        compiler_params=pltpu.CompilerParams(
            dimension_semantics=("arbitrary", "arbitrary"), vmem_limit_bytes=VMEM_LIMIT),
        name="in_proj",
    )(x, norm_g.reshape(1, D_MODEL), cos_t, sin_t, w_nat, w_g1, w_g2)


def _attend_heads(blocks, valid, first_half):
    second_half = jnp.logical_not(first_half)
    n_heads = blocks[0][0].shape[1] // HEAD_DIM
    scores = []
    for qb, kw, _ in blocks:
        for h in range(n_heads):
            cs = slice((h // 2) * LANES, (h // 2 + 1) * LANES)
            qp = qb[:, cs]
            qm = jnp.where(first_half if h % 2 == 0 else second_half, qp, jnp.zeros_like(qp))
            scores.append(lax.dot_general(qm, kw[:, cs], (((1,), (1,)), ((), ())),
                                          preferred_element_type=F32))
    sc = jnp.where(valid[None], jnp.stack(scores), NEG_BIG)
    m = jnp.max(sc, axis=-1, keepdims=True)
    p = jnp.exp(sc - m)
    l = jnp.sum(p, axis=-1, keepdims=True)
    pb = p.astype(BF16)
    results = []
    for bi, (_, _, vw) in enumerate(blocks):
        out = []
        for hp in range(n_heads // 2):
            i0 = bi * n_heads + 2 * hp
            vp = vw[:, hp * LANES:(hp + 1) * LANES]
            o0 = jnp.dot(pb[i0], vp, preferred_element_type=F32)
            o1 = jnp.dot(pb[i0 + 1], vp, preferred_element_type=F32)
            out.append((jnp.where(first_half, o0, o1), jnp.where(first_half, m[i0], m[i0 + 1]),
                        jnp.where(first_half, l[i0], l[i0 + 1])))
        results.append(out)
    return results


def _mixer_a_kernel(*refs, seq):
    ins, (o_ref, kext, vext, acc_o, acc_m, acc_l) = refs[:21], refs[21:]
    t = pl.program_id(1)
    tq = A_TILE
    halo = A_HALF

    first_group = len(DILATIONS) - 1
    for gi in reversed(range(len(DILATIONS))):
        d = DILATIONS[gi]
        q, kc, kp, kn, vc, vp, vn = ins[7 * gi:7 * gi + 7]
        n = tq // d
        blk = min(n, A_QBLOCK)
        nblk = n // blk
        win = -(-(blk + 2 * halo) // LANES) * LANES
        tail = win - (blk + 2 * halo) if n == blk else 0
        ld = seq // d
        col = lax.broadcasted_iota(jnp.int32, (blk, win), 1)
        diff = col - lax.broadcasted_iota(jnp.int32, (blk, win), 0)
        band = (diff >= 0) & (diff <= 2 * halo)
        first_half = lax.broadcasted_iota(jnp.int32, (blk, LANES), 1) < HEAD_DIM

        pair = A_QBLOCK // blk

        def residue_body(rp, carry, q=q, kc=kc, kp=kp, kn=kn, vc=vc, vp=vp, vn=vn, d=d, n=n, nblk=nblk,
                         ld=ld, blk=blk, win=win, col=col, band=band, first_half=first_half,
                         first=(gi == first_group), pair=pair, tail=tail):
            for u in range(pair):
                r = rp * pair + u
                base = u * win
                kext[base:base + halo, :] = kp[0, r]
                kext[base + halo:base + halo + n, :] = kc[0, r]
                kext[base + halo + n:base + 2 * halo + n, :] = kn[0, r]
                vext[base:base + halo, :] = vp[0, r]
                vext[base + halo:base + halo + n, :] = vc[0, r]
                vext[base + halo + n:base + 2 * halo + n, :] = vn[0, r]
                if tail:
                    kext[base + 2 * halo + n:base + win, :] = jnp.zeros((tail, kext.shape[1]), BF16)
                    vext[base + 2 * halo + n:base + win, :] = jnp.zeros((tail, vext.shape[1]), BF16)

            def block_body(j, carry2):
                j0 = pl.multiple_of(j * blk, blk)
                qbase = t * n + j * blk
                valid = band & (col >= halo - qbase) & (col < ld - qbase + halo)
                blocks = [(q[0, rp * pair + u, pl.ds(j0, blk), :],
                           kext[pl.ds(u * win + j0, win), :],
                           vext[pl.ds(u * win + j0, win), :]) for u in range(pair)]
                results = _attend_heads(blocks, valid, first_half)
                for u, parts in enumerate(results):
                    if d == 1:
                        rows = pl.ds(j0, blk)
                    else:
                        rows = pl.ds(j * (blk * d) + rp * pair + u, blk, stride=d)
                    for hp, (o_b, m_b, l_b) in enumerate(parts):
                        if first:
                            acc_o[hp, rows, :] = o_b
                            acc_l[hp, rows, :] = l_b
                            acc_m[hp, rows, :] = m_b
                        else:
                            m_old = acc_m[hp, rows, :]
                            m_new = jnp.maximum(m_old, m_b)
                            a_old = jnp.exp(m_old - m_new)
                            a_new = jnp.exp(m_b - m_new)
                            acc_o[hp, rows, :] = acc_o[hp, rows, :] * a_old + o_b * a_new
                            acc_l[hp, rows, :] = acc_l[hp, rows, :] * a_old + l_b * a_new
                            acc_m[hp, rows, :] = m_new
                return carry2

            lax.fori_loop(0, nblk, block_body, 0)
            return carry

        lax.fori_loop(0, d // pair, residue_body, 0)

    o_ref[0] = jnp.concatenate([acc_o[hp] / acc_l[hp] for hp in range(A_STEP_WIDTH // LANES)],
                               axis=1).astype(BF16)


def _mixer_a(qkv, seq):
    b = qkv[0].shape[0]
    tq = A_TILE
    blk = A_BLOCK
    aw = A_STEP_WIDTH
    args, in_specs = [], []
    for gi, d in enumerate(DILATIONS):
        q, k, v = qkv[3 * gi:3 * gi + 3]
        n = tq // d
        nblk = n // blk
        last = seq // d // blk - 1
        cur = pl.BlockSpec((1, d, n, aw), lambda bb, t, hh: (bb, 0, t, hh))
        prev = pl.BlockSpec((1, d, blk, aw),
                            lambda bb, t, hh, nblk=nblk: (bb, 0, jnp.maximum(t * nblk - 1, 0), hh))
        nxt = pl.BlockSpec((1, d, blk, aw),
                           lambda bb, t, hh, nblk=nblk, last=last: (bb, 0, jnp.minimum((t + 1) * nblk, last), hh))
        args += [q, k, k, k, v, v, v]
        in_specs += [cur, cur, prev, nxt, cur, prev, nxt]
    return pl.pallas_call(
        functools.partial(_mixer_a_kernel, seq=seq),
        grid=(b, seq // tq, A_WIDTH // aw),
        in_specs=in_specs,
        out_specs=pl.BlockSpec((1, tq, aw), lambda bb, t, hh: (bb, t, hh)),
        out_shape=jax.ShapeDtypeStruct((b, seq, A_WIDTH), BF16),
        scratch_shapes=[pltpu.VMEM((tq + 2 * blk, aw), BF16), pltpu.VMEM((tq + 2 * blk, aw), BF16),
                        pltpu.VMEM((aw // LANES, tq, LANES), F32),
                        pltpu.VMEM((aw // LANES, tq, LANES), F32),
                        pltpu.VMEM((aw // LANES, tq, LANES), F32)],
        compiler_params=pltpu.CompilerParams(
            dimension_semantics=("arbitrary", "arbitrary", "arbitrary"), vmem_limit_bytes=VMEM_LIMIT),
        name="mixer_a",
    )(*args)


def _mixer_b_kernel(sink_ref, q, kc, kp, kn, vc, vp, vn, o_ref, kext, vext, *, seq):
    t = pl.program_id(1)
    tq = ATT_TILE
    blk = B_BLOCK
    win = 3 * blk
    kext[0:blk, :] = kp[0]
    kext[blk:blk + tq, :] = kc[0]
    kext[blk + tq:2 * blk + tq, :] = kn[0]
    vext[0:blk, :] = vp[0]
    vext[blk:blk + tq, :] = vc[0]
    vext[blk + tq:2 * blk + tq, :] = vn[0]
    col = lax.broadcasted_iota(jnp.int32, (blk, win), 1)
    row = lax.broadcasted_iota(jnp.int32, (blk, win), 0)
    diff = col - row
    first_half = lax.broadcasted_iota(jnp.int32, (blk, LANES), 1) < HEAD_DIM
    second_half = jnp.logical_not(first_half)

    def block_body(j, carry):
        j0 = pl.multiple_of(j * blk, blk)
        qb = q[0, pl.ds(j0, blk), :]
        kw = kext[pl.ds(j0, win), :]
        vw = vext[pl.ds(j0, win), :]
        qbase = t * tq + j * blk
        valid = ((diff >= 0) & (diff <= 2 * B_WINDOW)
                 & (col >= blk - qbase) & (col < seq - qbase + blk))
        scores, sinks = [], []
        for c in range(B_Q_HEADS // 2):
            qp = qb[:, c * LANES:(c + 1) * LANES]
            for g, keep in enumerate((first_half, second_half)):
                qm = jnp.where(keep, qp, jnp.zeros_like(qp))
                scores.append(lax.dot_general(qm, kw, (((1,), (1,)), ((), ())), preferred_element_type=F32))
                sinks.append(jnp.full((1, LANES), sink_ref[g * B_REP + c], F32))
        sc = jnp.where(valid[None], jnp.stack(scores), NEG_BIG)
        sk = jnp.stack(sinks)
        part = sc[..., 0:LANES]
        for k in range(1, win // LANES):
            part = jnp.maximum(part, sc[..., k * LANES:(k + 1) * LANES])
        m = jnp.max(jnp.maximum(part, sk), axis=-1, keepdims=True)
        p = jnp.exp(sc - m)
        denom = jnp.sum(p, axis=-1, keepdims=True) + jnp.exp(sk - m)
        pb = p.astype(BF16)
        parts = []
        for c in range(B_Q_HEADS // 2):
            o0 = jnp.dot(pb[2 * c], vw, preferred_element_type=F32) / denom[2 * c]
            o1 = jnp.dot(pb[2 * c + 1], vw, preferred_element_type=F32) / denom[2 * c + 1]
            parts.append(jnp.where(first_half, o0, o1))
        o_ref[0, pl.ds(j0, blk), :] = jnp.concatenate(parts, axis=1).astype(BF16)
        return carry

    lax.fori_loop(0, tq // blk, block_body, 0)


def _mixer_b(q, k, v, sink, seq):
    b = q.shape[0]
    tq = ATT_TILE
    blk = B_BLOCK
    nblk = tq // blk
    last = seq // blk - 1
    cur = pl.BlockSpec((1, tq, B_KV_WIDTH), lambda bb, t: (bb, t, 0))
    prev = pl.BlockSpec((1, blk, B_KV_WIDTH), lambda bb, t: (bb, jnp.maximum(t * nblk - 1, 0), 0))
    nxt = pl.BlockSpec((1, blk, B_KV_WIDTH), lambda bb, t: (bb, jnp.minimum((t + 1) * nblk, last), 0))
    return pl.pallas_call(
        functools.partial(_mixer_b_kernel, seq=seq),
        grid=(b, seq // tq),
        in_specs=[pl.BlockSpec(memory_space=pltpu.SMEM),
                  pl.BlockSpec((1, tq, B_Q_WIDTH), lambda bb, t: (bb, t, 0)),
                  cur, prev, nxt, cur, prev, nxt],
        out_specs=pl.BlockSpec((1, tq, B_Q_WIDTH), lambda bb, t: (bb, t, 0)),
        out_shape=jax.ShapeDtypeStruct((b, seq, B_Q_WIDTH), BF16),
        scratch_shapes=[pltpu.VMEM((tq + 2 * blk, B_KV_WIDTH), BF16),
                        pltpu.VMEM((tq + 2 * blk, B_KV_WIDTH), BF16)],
        compiler_params=pltpu.CompilerParams(
            dimension_semantics=("arbitrary", "arbitrary"), vmem_limit_bytes=VMEM_LIMIT),
        name="mixer_b",
    )(sink, q, k, k, k, v, v, v)


def _post_attn_kernel(x_ref, oa_ref, ob_ref, gmix_ref, wg_ref, wa_ref, wb_ref, wo_ref, gmoe_ref,
                      wr_ref, br_ref, x1_ref, h2_ref, gate_ref):
    wr = wr_ref[...]
    wr_hi = wr.astype(BF16)
    wr_lo = (wr - wr_hi.astype(F32)).astype(BF16)
    wr3 = jnp.concatenate([wr_hi, wr_hi, wr_lo], axis=0)
    for s in range(x_ref.shape[0] // POST_SLAB):
        rows = slice(s * POST_SLAB, (s + 1) * POST_SLAB)
        _post_attn_rows(x_ref, oa_ref, ob_ref, gmix_ref, wg_ref, wa_ref, wb_ref, wo_ref, gmoe_ref, wr3,
                        br_ref, x1_ref, h2_ref, gate_ref, rows)


def _post_attn_rows(x_ref, oa_ref, ob_ref, gmix_ref, wg_ref, wa_ref, wb_ref, wo_ref, gmoe_ref, wr3,
                    br_ref, x1_ref, h2_ref, gate_ref, rows):
    x = x_ref[rows, :]
    h = _rms(x, gmix_ref[...]).astype(BF16)
    gates = jax.nn.sigmoid(jnp.dot(h, wg_ref[...], preferred_element_type=F32))
    ya = jnp.dot(oa_ref[rows, :], wa_ref[...], preferred_element_type=F32)
    yb = jnp.dot(ob_ref[rows, :], wb_ref[...], preferred_element_type=F32)
    mix = gates[:, :D_MODEL] * ya + gates[:, D_MODEL:] * yb
    x1 = x + jnp.dot(mix.astype(BF16), wo_ref[...], preferred_element_type=F32)
    x1_ref[rows, :] = x1
    h2 = _rms(x1, gmoe_ref[...])
    h2_hi = h2.astype(BF16)
    h2_ref[rows, :] = h2_hi

    h2_lo = (h2 - h2_hi.astype(F32)).astype(BF16)
    logits = jnp.dot(jnp.concatenate([h2_hi, h2_lo, h2_hi], axis=1), wr3,
                     preferred_element_type=F32) + br_ref[...]
    tm = logits.shape[0]
    lane = lax.broadcasted_iota(jnp.int32, (tm, LANES), 1)
    big = jnp.int32(LANES)
    is_group = (lane >= N_EXPERTS) & (lane < N_EXPERTS + N_GROUPS)
    gl = jnp.where(is_group, logits, NEG_BIG)
    gmax = jnp.max(gl, axis=-1, keepdims=True)
    g_idx = jnp.min(jnp.where(gl == gmax, lane, big), axis=-1, keepdims=True) - N_EXPERTS
    g_w = 1.0 / jnp.sum(jnp.exp(gl - gmax), axis=-1, keepdims=True)
    in_group = (lane < N_EXPERTS) & (lax.shift_right_logical(lane, 3) == g_idx)
    sel = jnp.where(in_group, logits, NEG_BIG)
    v1 = jnp.max(sel, axis=-1, keepdims=True)
    i1 = jnp.min(jnp.where(sel == v1, lane, big), axis=-1, keepdims=True)
    sel2 = jnp.where(lane == i1, NEG_BIG, sel)
    v2 = jnp.max(sel2, axis=-1, keepdims=True)
    i2 = jnp.min(jnp.where(sel2 == v2, lane, big), axis=-1, keepdims=True)
    e2 = jnp.exp(v2 - v1)
    w1 = g_w / (1.0 + e2)
    w2 = g_w * e2 / (1.0 + e2)
    gate_ref[rows, :] = jnp.where(lane == i1, w1, 0.0) + jnp.where(lane == i2, w2, 0.0)


def _post_attn(x2d, oa, ob, norm_mix, w_gate, w_a, w_b, w_out, norm_moe, w_router, b_router):
    t = x2d.shape[0]
    tm = POST_TILE
    const = lambda i: (0, 0)
    row = lambda w: pl.BlockSpec((tm, w), lambda i: (i, 0))
    full = lambda a: pl.BlockSpec(a.shape, const, pipeline_mode=pl.Buffered(1))
    return pl.pallas_call(
        _post_attn_kernel,
        grid=(t // tm,),
        in_specs=[row(D_MODEL), row(A_WIDTH), row(B_Q_WIDTH), full(norm_mix), full(w_gate), full(w_a),
                  full(w_b), full(w_out), full(norm_moe), full(w_router), full(b_router)],
        out_specs=[row(D_MODEL), row(D_MODEL), row(LANES)],
        out_shape=[jax.ShapeDtypeStruct((t, D_MODEL), F32), jax.ShapeDtypeStruct((t, D_MODEL), BF16),
                   jax.ShapeDtypeStruct((t, LANES), F32)],
        compiler_params=pltpu.CompilerParams(
            dimension_semantics=("arbitrary",), vmem_limit_bytes=VMEM_LIMIT),
        name="post_attn",
    )(x2d, oa, ob, norm_mix, w_gate, w_a, w_b, w_out, norm_moe, w_router, b_router)


def _moe_final_kernel(x1_ref, h2_ref, gate_ref, tri_ref, w1_ref, w3_ref, w2_ref, gf_ref, o_ref,
                      xs, ys, seg_s, prog_s, slo_s, shi_s, wlo_s, whi_s):
    step = pl.program_id(1)
    tm = x1_ref.shape[0]

    @pl.when(step == 0)
    def _plan():
        g = gate_ref[...]
        hit = g > 0.0
        onehot = jnp.where(hit, 1.0, 0.0)
        count = jnp.sum(onehot, axis=0, keepdims=True)
        padded = jnp.ceil(count * (1.0 / SEG_ALIGN)) * SEG_ALIGN
        r_i = lax.broadcasted_iota(jnp.int32, (LANES, LANES), 0)
        c_i = lax.broadcasted_iota(jnp.int32, (LANES, LANES), 1)
        upper = jnp.where(r_i < c_i, 1.0, 0.0)
        seg = jnp.dot(jnp.broadcast_to(padded, (8, LANES)), upper, preferred_element_type=F32,
                      precision=lax.Precision.HIGHEST)[0:1]
        seg_i = seg.astype(jnp.int32)
        padded_i = padded.astype(jnp.int32)
        for ex in range(N_EXPERTS):
            seg_s[0, ex] = seg_i[0, ex]
            seg_s[1, ex] = padded_i[0, ex]
        rank = jnp.dot(tri_ref[...], onehot.astype(BF16), preferred_element_type=F32)
        slot = seg + rank
        s_lo = jnp.min(jnp.where(hit, slot, 1e9), axis=-1, keepdims=True)
        s_hi = jnp.max(jnp.where(hit, slot, -1.0), axis=-1, keepdims=True)
        wlo_s[...] = jnp.sum(jnp.where(hit & (slot == s_lo), g, 0.0), axis=-1, keepdims=True)
        whi_s[...] = jnp.sum(jnp.where(hit & (slot == s_hi) & (s_hi > s_lo), g, 0.0), axis=-1, keepdims=True)
        s_lo_i = s_lo.astype(jnp.int32)
        s_hi_i = s_hi.astype(jnp.int32)
        slo_s[...] = s_lo_i
        shi_s[...] = s_hi_i
        prog_s[0] = 0
        prog_s[1] = 0
        xs[MOE_SLOTS:, :] = jnp.zeros((MOE_CHUNK, D_MODEL), BF16)
        ys[...] = jnp.zeros(ys.shape, BF16)
        o_ref[...] = x1_ref[...]

    first = step * MOE_ESTEP
    last = first + MOE_ESTEP - 1
    starts = [seg_s[0, first + j] for j in range(MOE_ESTEP)]
    lens = [seg_s[1, first + j] for j in range(MOE_ESTEP)]

    def sort_chunk(c, carry):
        c0 = pl.multiple_of(c * MOE_SORT_COLS, MOE_SORT_COLS)
        col = lax.broadcasted_iota(jnp.int32, (tm, MOE_SORT_COLS), 1) + c0
        perm_t = jnp.where((col == slo_s[...]) | (col == shi_s[...]), 1.0, 0.0).astype(BF16)
        rows = lax.dot_general(perm_t, h2_ref[...], (((0,), (0,)), ((), ())), preferred_element_type=F32)
        xs[pl.ds(c0, MOE_SORT_COLS), :] = rows.astype(BF16)
        return carry

    read_end = starts[-1] + lens[-1] + MOE_CHUNK
    sorted_to = jnp.minimum((read_end + MOE_SORT_COLS - 1) // MOE_SORT_COLS, MOE_SLOTS // MOE_SORT_COLS)
    lax.fori_loop(prog_s[0], sorted_to, sort_chunk, 0)
    prog_s[0] = jnp.maximum(prog_s[0], sorted_to)

    def ffn(x, j):
        a = jnp.dot(x, w1_ref[j], preferred_element_type=F32)
        b = jnp.dot(x, w3_ref[j], preferred_element_type=F32)
        hid = (a * jax.nn.sigmoid(a)) * b
        return jnp.dot(hid.astype(BF16), w2_ref[j], preferred_element_type=F32).astype(BF16)

    for j in range(MOE_ESTEP):
        def extra_body(k, carry, j=j):
            r0 = pl.multiple_of(starts[j] + k * MOE_CHUNK, SEG_ALIGN)
            ys[pl.ds(r0, MOE_CHUNK), :] = ffn(xs[pl.ds(r0, MOE_CHUNK), :], j)
            return carry

        lax.fori_loop(1, (lens[j] + MOE_CHUNK - 1) // MOE_CHUNK, extra_body, 0)

    firsts = [ffn(xs[pl.ds(pl.multiple_of(starts[j], SEG_ALIGN), MOE_CHUNK), :], j) for j in range(MOE_ESTEP)]
    for j in range(MOE_ESTEP):
        ys[pl.ds(pl.multiple_of(starts[j], SEG_ALIGN), MOE_CHUNK), :] = firsts[j]

    def combine_chunk(c, carry):
        c0 = pl.multiple_of(c * MOE_SORT_COLS, MOE_SORT_COLS)
        col = lax.broadcasted_iota(jnp.int32, (tm, MOE_SORT_COLS), 1) + c0
        wperm = (jnp.where(col == slo_s[...], wlo_s[...], 0.0)
                 + jnp.where(col == shi_s[...], whi_s[...], 0.0)).astype(BF16)
        o_ref[...] += jnp.dot(wperm, ys[pl.ds(c0, MOE_SORT_COLS), :], preferred_element_type=F32)
        return carry

    is_last = step == N_EXPERTS // MOE_ESTEP - 1
    used = starts[-1] + lens[-1]
    next_start = seg_s[0, jnp.minimum(last + 1, N_EXPERTS - 1)]
    final_to = jnp.where(is_last, (used + MOE_SORT_COLS - 1) // MOE_SORT_COLS, next_start // MOE_SORT_COLS)
    lax.fori_loop(prog_s[1], final_to, combine_chunk, 0)
    prog_s[1] = jnp.maximum(prog_s[1], final_to)

    @pl.when(is_last)
    def _finish():
        o_ref[...] = _rms(o_ref[...], gf_ref[...])


def _moe_final(x1, h2, gate, w1, w3, w2, norm_final):
    t = x1.shape[0]
    tm = MOE_TILE
    row = lambda w: pl.BlockSpec((tm, w), lambda i, e: (i, 0))
    tri = jnp.tril(jnp.ones((tm, tm), BF16), -1)
    return pl.pallas_call(
        _moe_final_kernel,
        grid=(t // tm, N_EXPERTS // MOE_ESTEP),
        in_specs=[row(D_MODEL), row(D_MODEL), row(LANES),
                  pl.BlockSpec((tm, tm), lambda i, e: (0, 0), pipeline_mode=pl.Buffered(1)),
                  pl.BlockSpec((MOE_ESTEP, D_MODEL, D_FF), lambda i, e: (e, 0, 0)),
                  pl.BlockSpec((MOE_ESTEP, D_MODEL, D_FF), lambda i, e: (e, 0, 0)),
                  pl.BlockSpec((MOE_ESTEP, D_FF, D_MODEL), lambda i, e: (e, 0, 0)),
                  pl.BlockSpec((1, D_MODEL), lambda i, e: (0, 0))],
        out_specs=row(D_MODEL),
        out_shape=jax.ShapeDtypeStruct((t, D_MODEL), F32),
        scratch_shapes=[pltpu.VMEM((MOE_SLOTS + MOE_CHUNK, D_MODEL), BF16),
                        pltpu.VMEM((MOE_SLOTS + MOE_CHUNK, D_MODEL), BF16),
                        pltpu.SMEM((2, N_EXPERTS), jnp.int32), pltpu.SMEM((2,), jnp.int32),
                        pltpu.VMEM((tm, 1), jnp.int32), pltpu.VMEM((tm, 1), jnp.int32),
                        pltpu.VMEM((tm, 1), F32), pltpu.VMEM((tm, 1), F32)],
        compiler_params=pltpu.CompilerParams(
            dimension_semantics=("arbitrary", "arbitrary"), vmem_limit_bytes=VMEM_LIMIT),
        name="moe_final",
    )(x1, h2, gate, tri, w1, w3, w2, norm_final)


def _prep_weights(norm_mix, w_in, attn_sink, w_branch_a, w_branch_b, w_gate, w_out, norm_moe,
                  w_router_group, b_router_group, w_router_expert, b_router_expert, w1, w3, w2, norm_final):
    wi = w_in[0]
    a_w = 3 * A_WIDTH

    def group_cols(gi):
        return jnp.concatenate([wi[:, role * a_w + gi * A_WIDTH: role * a_w + (gi + 1) * A_WIDTH]
                                for role in range(3)], axis=1)

    b_heads = [g * B_REP + c for c in range(B_REP) for g in range(B_KV_HEADS)]
    bq_cols = jnp.concatenate([wi[:, 3 * a_w + h * HEAD_DIM: 3 * a_w + (h + 1) * HEAD_DIM] for h in b_heads],
                              axis=1)
    w_b_rows = jnp.concatenate([w_branch_b[0][h * HEAD_DIM:(h + 1) * HEAD_DIM] for h in b_heads], axis=0)
    w_nat = jnp.concatenate([group_cols(0), bq_cols, wi[:, 3 * a_w + B_Q_WIDTH:]], axis=1).astype(BF16)
    w_g1 = group_cols(1).astype(BF16)
    w_g2 = group_cols(2).astype(BF16)
    pad = LANES - N_EXPERTS - N_GROUPS
    w_router = jnp.concatenate([w_router_expert[0], w_router_group[0],
                                jnp.zeros((D_MODEL, pad), F32)], axis=1)
    b_router = jnp.concatenate([b_router_expert[0].reshape(-1), b_router_group[0],
                                jnp.zeros((pad,), F32)]).reshape(1, LANES)
    return dict(
        norm_mix=norm_mix[0].reshape(1, D_MODEL), w_nat=w_nat, w_g1=w_g1, w_g2=w_g2, sink=attn_sink[0],
        w_a=w_branch_a[0].astype(BF16), w_b=w_b_rows.astype(BF16), w_gate=w_gate[0].astype(BF16),
        w_out=w_out[0].astype(BF16), norm_moe=norm_moe[0].reshape(1, D_MODEL),
        w_router=w_router, b_router=b_router,
        w1=w1[0].astype(BF16), w3=w3[0].astype(BF16), w2=w2[0].astype(BF16),
        norm_final=norm_final.reshape(1, D_MODEL))


def _trunk(x, p):
    b, s, _ = x.shape
    outs = _in_proj(x, p["norm_mix"], p["w_nat"], p["w_g1"], p["w_g2"])
    oa = _mixer_a(outs[:9], s)
    ob = _mixer_b(outs[9], outs[10], outs[11], p["sink"], s)
    t = b * s
    x1, h2, gate = _post_attn(x.reshape(t, D_MODEL), oa.reshape(t, A_WIDTH), ob.reshape(t, B_Q_WIDTH),
                              p["norm_mix"], p["w_gate"], p["w_a"], p["w_b"], p["w_out"], p["norm_moe"],
                              p["w_router"], p["b_router"])
    y = _moe_final(x1, h2, gate, p["w1"], p["w3"], p["w2"], p["norm_final"])
    return y.reshape(b, s, D_MODEL)


def kernel(x_prompt, x_sample, norm_mix, w_in, attn_sink, w_branch_a, w_branch_b, w_gate, w_out, norm_moe,
           w_router_group, b_router_group, w_router_expert, b_router_expert, w1, w3, w2, norm_final):
    p = _prep_weights(norm_mix, w_in, attn_sink, w_branch_a, w_branch_b, w_gate, w_out, norm_moe,
                      w_router_group, b_router_group, w_router_expert, b_router_expert, w1, w3, w2, norm_final)
    return (_trunk(x_prompt, p), _trunk(x_sample, p))
```

```python
import functools

import jax
import jax.numpy as jnp
from jax import lax
from jax.experimental import pallas as pl
from jax.experimental.pallas import tpu as pltpu

D_MODEL = 1024
HEAD_DIM = 64
ROT_DIM = HEAD_DIM // 4
ROT_HALF = ROT_DIM // 2
ROPE_THETA = 500000.0
DILATIONS = (1, 4, 16)
A_HALF = 64
A_BLOCK = 64
A_QBLOCK = 128
A_HEADS = 8
A_WIDTH = A_HEADS * HEAD_DIM
B_Q_HEADS = 8
B_KV_HEADS = 2
B_REP = B_Q_HEADS // B_KV_HEADS
B_WINDOW = 128
B_BLOCK = 128
B_Q_WIDTH = B_Q_HEADS * HEAD_DIM
B_KV_WIDTH = B_KV_HEADS * HEAD_DIM
N_GROUPS = 4
EXPERTS_PER_GROUP = 8
N_EXPERTS = N_GROUPS * EXPERTS_PER_GROUP
D_FF = 256
RMS_EPS = 1e-6
Q_SCALE = HEAD_DIM ** -0.5

LANES = 128
NEG_BIG = -1e30
VMEM_LIMIT = 56 * 1024 * 1024

IN_TILE = 512
ATT_TILE = 1024
A_TILE = 1024
A_STEP_WIDTH = 512
POST_TILE = 1024
POST_SLAB = 256
MOE_TILE = 1024
SEG_ALIGN = 16
MOE_CHUNK = 128
MOE_SUB = 256
MOE_SUB_SLOTS = 2 * MOE_SUB + N_EXPERTS * SEG_ALIGN
MOE_ESTEP = 4

F32 = jnp.float32
BF16 = jnp.bfloat16


def _rms(x, g):
    ms = jnp.mean(x * x, axis=-1, keepdims=True)
    return x * lax.rsqrt(ms + RMS_EPS) * g


def _rope(a, cos, sin, low_half):
    outs = []
    for k in range(a.shape[1] // LANES):
        ak = a[:, k * LANES:(k + 1) * LANES]
        up = pltpu.roll(ak, LANES - ROT_HALF, axis=1)
        dn = pltpu.roll(ak, ROT_HALF, axis=1)
        outs.append(ak * cos + jnp.where(low_half, up, dn) * sin)
    return outs[0] if len(outs) == 1 else jnp.concatenate(outs, axis=1)


def _in_proj_kernel(x_ref, g_ref, cos_ref, sin_ref, wn_ref, w1_ref, w2_ref,
                    q0, k0, v0, q1, k1, v1, q2, k2, v2, bq, bk, bv,
                    xs, h1_s, h2_s, c1_s, s1_s, c2_s, s2_s):
    tm = x_ref.shape[1]
    n_chunks = D_MODEL // LANES
    for c in range(n_chunks):
        xs[c] = x_ref[0, :, c * LANES:(c + 1) * LANES]
    g = g_ref[...]
    low_half = (lax.broadcasted_iota(jnp.int32, (tm, LANES), 1) % HEAD_DIM) < ROT_HALF

    def mm(h, w_ref, c0, n):
        return jnp.dot(h, w_ref[:, c0:c0 + n], preferred_element_type=F32)

    h0 = _rms(x_ref[0], g).astype(BF16)
    cos0, sin0 = cos_ref[...], sin_ref[...]
    q0[0, 0] = (_rope(mm(h0, wn_ref, 0, A_WIDTH), cos0, sin0, low_half) * Q_SCALE).astype(BF16)
    k0[0, 0] = _rope(mm(h0, wn_ref, A_WIDTH, A_WIDTH), cos0, sin0, low_half).astype(BF16)
    v0[0, 0] = mm(h0, wn_ref, 2 * A_WIDTH, A_WIDTH).astype(BF16)
    c = 3 * A_WIDTH
    bq[0] = (_rope(mm(h0, wn_ref, c, B_Q_WIDTH), cos0, sin0, low_half) * Q_SCALE).astype(BF16)
    bk[0] = _rope(mm(h0, wn_ref, c + B_Q_WIDTH, B_KV_WIDTH), cos0, sin0, low_half).astype(BF16)
    bv[0] = mm(h0, wn_ref, c + B_Q_WIDTH + B_KV_WIDTH, B_KV_WIDTH).astype(BF16)

    for d, hs, cs, ss, w_ref, qo, ko, vo in ((DILATIONS[1], h1_s, c1_s, s1_s, w1_ref, q1, k1, v1),
                                             (DILATIONS[2], h2_s, c2_s, s2_s, w2_ref, q2, k2, v2)):
        n = tm // d
        for r in range(d):
            rows = pl.ds(r, n, stride=d)
            xr = jnp.concatenate([xs[c, rows, :] for c in range(n_chunks)], axis=1)
            hs[r * n:(r + 1) * n, :] = _rms(xr, g).astype(BF16)
            cs[r * n:(r + 1) * n, :] = cos_ref[rows, :]
            ss[r * n:(r + 1) * n, :] = sin_ref[rows, :]
        h = hs[...]
        cos, sin = cs[...], ss[...]
        qv = (_rope(mm(h, w_ref, 0, A_WIDTH), cos, sin, low_half) * Q_SCALE).astype(BF16)
        kv = _rope(mm(h, w_ref, A_WIDTH, A_WIDTH), cos, sin, low_half).astype(BF16)
        vv = mm(h, w_ref, 2 * A_WIDTH, A_WIDTH).astype(BF16)
        for r in range(d):
            qo[0, r] = qv[r * n:(r + 1) * n]
            ko[0, r] = kv[r * n:(r + 1) * n]
            vo[0, r] = vv[r * n:(r + 1) * n]


def _rope_tables(s):
    inv = jnp.power(jnp.float32(ROPE_THETA), -jnp.arange(ROT_HALF, dtype=F32) / ROT_HALF)
    ang = jnp.arange(s).astype(F32)[:, None] * inv[None, :]
    cos, sin = jnp.cos(ang), jnp.sin(ang)
    pad_one = jnp.ones((s, HEAD_DIM - ROT_DIM), F32)
    pad_zero = jnp.zeros((s, HEAD_DIM - ROT_DIM), F32)
    cos_h = jnp.concatenate([cos, cos, pad_one], axis=1)
    sin_h = jnp.concatenate([-sin, sin, pad_zero], axis=1)
    return jnp.concatenate([cos_h, cos_h], axis=1), jnp.concatenate([sin_h, sin_h], axis=1)


def _in_proj(x, norm_g, w_nat, w_g1, w_g2):
    b, s, _ = x.shape
    tm = IN_TILE
    cos_t, sin_t = _rope_tables(s)
    const = lambda t, bb: (0, 0)
    one_buf = pl.Buffered(1)
    in_specs = [
        pl.BlockSpec((1, tm, D_MODEL), lambda t, bb: (bb, t, 0)),
        pl.BlockSpec((1, D_MODEL), const),
        pl.BlockSpec((tm, LANES), lambda t, bb: (t, 0)),
        pl.BlockSpec((tm, LANES), lambda t, bb: (t, 0)),
        pl.BlockSpec(w_nat.shape, const, pipeline_mode=one_buf),
        pl.BlockSpec(w_g1.shape, const, pipeline_mode=one_buf),
        pl.BlockSpec(w_g2.shape, const, pipeline_mode=one_buf),
    ]
    out_shape, out_specs = [], []
    for d in DILATIONS:
        for _ in range(3):
            out_shape.append(jax.ShapeDtypeStruct((b, d, s // d, A_WIDTH), BF16))
            out_specs.append(pl.BlockSpec((1, d, tm // d, A_WIDTH), lambda t, bb: (bb, 0, t, 0)))
    for w in (B_Q_WIDTH, B_KV_WIDTH, B_KV_WIDTH):
        out_shape.append(jax.ShapeDtypeStruct((b, s, w), BF16))
        out_specs.append(pl.BlockSpec((1, tm, w), lambda t, bb: (bb, t, 0)))
    scratch = [pltpu.VMEM((D_MODEL // LANES, tm, LANES), F32),
               pltpu.VMEM((tm, D_MODEL), BF16), pltpu.VMEM((tm, D_MODEL), BF16)] + \
              [pltpu.VMEM((tm, LANES), F32) for _ in range(4)]
    return pl.pallas_call(
        _in_proj_kernel,
        grid=(s // tm, b),
        in_specs=in_specs,
        out_specs=out_specs,
        out_shape=out_shape,
        scratch_shapes=scratch,
        compiler_params=pltpu.CompilerParams(
            dimension_semantics=("arbitrary", "arbitrary"), vmem_limit_bytes=VMEM_LIMIT),
        name="in_proj",
    )(x, norm_g.reshape(1, D_MODEL), cos_t, sin_t, w_nat, w_g1, w_g2)


def _attend_heads(blocks, valid, first_half):
    second_half = jnp.logical_not(first_half)
    n_heads = blocks[0][0].shape[1] // HEAD_DIM
    scores = []
    for qb, kw, _ in blocks:
        for h in range(n_heads):
            cs = slice((h // 2) * LANES, (h // 2 + 1) * LANES)
            qp = qb[:, cs]
            qm = jnp.where(first_half if h % 2 == 0 else second_half, qp, jnp.zeros_like(qp))
            scores.append(lax.dot_general(qm, kw[:, cs], (((1,), (1,)), ((), ())),
                                          preferred_element_type=F32))
    sc = jnp.where(valid[None], jnp.stack(scores), NEG_BIG)
    m = jnp.max(sc, axis=-1, keepdims=True)
    p = jnp.exp(sc - m)
    l = jnp.sum(p, axis=-1, keepdims=True)
    pb = p.astype(BF16)
    results = []
    for bi, (_, _, vw) in enumerate(blocks):
        out = []
        for hp in range(n_heads // 2):
            i0 = bi * n_heads + 2 * hp
            vp = vw[:, hp * LANES:(hp + 1) * LANES]
            o0 = jnp.dot(pb[i0], vp, preferred_element_type=F32)
            o1 = jnp.dot(pb[i0 + 1], vp, preferred_element_type=F32)
            out.append((jnp.where(first_half, o0, o1), jnp.where(first_half, m[i0], m[i0 + 1]),
                        jnp.where(first_half, l[i0], l[i0 + 1])))
        results.append(out)
    return results


def _mixer_a_kernel(*refs, seq):
    ins, (o_ref, kext, vext, acc_o, acc_m, acc_l) = refs[:21], refs[21:]
    t = pl.program_id(1)
    tq = A_TILE
    halo = A_HALF

    first_group = len(DILATIONS) - 1
    for gi in reversed(range(len(DILATIONS))):
        d = DILATIONS[gi]
        q, kc, kp, kn, vc, vp, vn = ins[7 * gi:7 * gi + 7]
        n = tq // d
        blk = min(n, A_QBLOCK)
        nblk = n // blk
        win = -(-(blk + 2 * halo) // LANES) * LANES
        tail = win - (blk + 2 * halo) if n == blk else 0
        ld = seq // d
        col = lax.broadcasted_iota(jnp.int32, (blk, win), 1)
        diff = col - lax.broadcasted_iota(jnp.int32, (blk, win), 0)
        band = (diff >= 0) & (diff <= 2 * halo)
        first_half = lax.broadcasted_iota(jnp.int32, (blk, LANES), 1) < HEAD_DIM

        pair = A_QBLOCK // blk

        def residue_body(rp, carry, q=q, kc=kc, kp=kp, kn=kn, vc=vc, vp=vp, vn=vn, d=d, n=n, nblk=nblk,
                         ld=ld, blk=blk, win=win, col=col, band=band, first_half=first_half,
                         first=(gi == first_group), pair=pair, tail=tail):
            for u in range(pair):
                r = rp * pair + u
                base = u * win
                kext[base:base + halo, :] = kp[0, r]
                kext[base + halo:base + halo + n, :] = kc[0, r]
                kext[base + halo + n:base + 2 * halo + n, :] = kn[0, r]
                vext[base:base + halo, :] = vp[0, r]
                vext[base + halo:base + halo + n, :] = vc[0, r]
                vext[base + halo + n:base + 2 * halo + n, :] = vn[0, r]
                if tail:
                    kext[base + 2 * halo + n:base + win, :] = jnp.zeros((tail, kext.shape[1]), BF16)
                    vext[base + 2 * halo + n:base + win, :] = jnp.zeros((tail, vext.shape[1]), BF16)

            def block_body(j, carry2):
                j0 = pl.multiple_of(j * blk, blk)
                qbase = t * n + j * blk
                valid = band & (col >= halo - qbase) & (col < ld - qbase + halo)
                blocks = [(q[0, rp * pair + u, pl.ds(j0, blk), :],
                           kext[pl.ds(u * win + j0, win), :],
                           vext[pl.ds(u * win + j0, win), :]) for u in range(pair)]
                results = _attend_heads(blocks, valid, first_half)
                for u, parts in enumerate(results):
                    if d == 1:
                        rows = pl.ds(j0, blk)
                    else:
                        rows = pl.ds(j * (blk * d) + rp * pair + u, blk, stride=d)
                    for hp, (o_b, m_b, l_b) in enumerate(parts):
                        if first:
                            acc_o[hp, rows, :] = o_b
                            acc_l[hp, rows, :] = l_b
                            acc_m[hp, rows, :] = m_b
                        else:
                            m_old = acc_m[hp, rows, :]
                            m_new = jnp.maximum(m_old, m_b)
                            a_old = jnp.exp(m_old - m_new)
                            a_new = jnp.exp(m_b - m_new)
                            acc_o[hp, rows, :] = acc_o[hp, rows, :] * a_old + o_b * a_new
                            acc_l[hp, rows, :] = acc_l[hp, rows, :] * a_old + l_b * a_new
                            acc_m[hp, rows, :] = m_new
                return carry2

            lax.fori_loop(0, nblk, block_body, 0)
            return carry

        lax.fori_loop(0, d // pair, residue_body, 0)

    o_ref[0] = jnp.concatenate([acc_o[hp] / acc_l[hp] for hp in range(A_STEP_WIDTH // LANES)],
                               axis=1).astype(BF16)


def _mixer_a(qkv, seq):
    b = qkv[0].shape[0]
    tq = A_TILE
    blk = A_BLOCK
    aw = A_STEP_WIDTH
    args, in_specs = [], []
    for gi, d in enumerate(DILATIONS):
        q, k, v = qkv[3 * gi:3 * gi + 3]
        n = tq // d
        nblk = n // blk
        last = seq // d // blk - 1
        cur = pl.BlockSpec((1, d, n, aw), lambda bb, t, hh: (bb, 0, t, hh))
        prev = pl.BlockSpec((1, d, blk, aw),
                            lambda bb, t, hh, nblk=nblk: (bb, 0, jnp.maximum(t * nblk - 1, 0), hh))
        nxt = pl.BlockSpec((1, d, blk, aw),
                           lambda bb, t, hh, nblk=nblk, last=last: (bb, 0, jnp.minimum((t + 1) * nblk, last), hh))
        args += [q, k, k, k, v, v, v]
        in_specs += [cur, cur, prev, nxt, cur, prev, nxt]
    return pl.pallas_call(
        functools.partial(_mixer_a_kernel, seq=seq),
        grid=(b, seq // tq, A_WIDTH // aw),
        in_specs=in_specs,
        out_specs=pl.BlockSpec((1, tq, aw), lambda bb, t, hh: (bb, t, hh)),
        out_shape=jax.ShapeDtypeStruct((b, seq, A_WIDTH), BF16),
        scratch_shapes=[pltpu.VMEM((tq + 2 * blk, aw), BF16), pltpu.VMEM((tq + 2 * blk, aw), BF16),
                        pltpu.VMEM((aw // LANES, tq, LANES), F32),
                        pltpu.VMEM((aw // LANES, tq, LANES), F32),
                        pltpu.VMEM((aw // LANES, tq, LANES), F32)],
        compiler_params=pltpu.CompilerParams(
            dimension_semantics=("arbitrary", "arbitrary", "arbitrary"), vmem_limit_bytes=VMEM_LIMIT),
        name="mixer_a",
    )(*args)


def _mixer_b_kernel(sink_ref, q, kc, kp, kn, vc, vp, vn, o_ref, kext, vext, *, seq):
    t = pl.program_id(1)
    tq = ATT_TILE
    blk = B_BLOCK
    win = 3 * blk
    kext[0:blk, :] = kp[0]
    kext[blk:blk + tq, :] = kc[0]
    kext[blk + tq:2 * blk + tq, :] = kn[0]
    vext[0:blk, :] = vp[0]
    vext[blk:blk + tq, :] = vc[0]
    vext[blk + tq:2 * blk + tq, :] = vn[0]
    col = lax.broadcasted_iota(jnp.int32, (blk, win), 1)
    row = lax.broadcasted_iota(jnp.int32, (blk, win), 0)
    diff = col - row
    first_half = lax.broadcasted_iota(jnp.int32, (blk, LANES), 1) < HEAD_DIM
    second_half = jnp.logical_not(first_half)

    def block_body(j, carry):
        j0 = pl.multiple_of(j * blk, blk)
        qb = q[0, pl.ds(j0, blk), :]
        kw = kext[pl.ds(j0, win), :]
        vw = vext[pl.ds(j0, win), :]
        qbase = t * tq + j * blk
        valid = ((diff >= 0) & (diff <= 2 * B_WINDOW)
                 & (col >= blk - qbase) & (col < seq - qbase + blk))
        scores, sinks = [], []
        for c in range(B_Q_HEADS // 2):
            qp = qb[:, c * LANES:(c + 1) * LANES]
            for g, keep in enumerate((first_half, second_half)):
                qm = jnp.where(keep, qp, jnp.zeros_like(qp))
                scores.append(lax.dot_general(qm, kw, (((1,), (1,)), ((), ())), preferred_element_type=F32))
                sinks.append(jnp.full((1, LANES), sink_ref[g * B_REP + c], F32))
        sc = jnp.where(valid[None], jnp.stack(scores), NEG_BIG)
        sk = jnp.stack(sinks)
        part = sc[..., 0:LANES]
        for k in range(1, win // LANES):
            part = jnp.maximum(part, sc[..., k * LANES:(k + 1) * LANES])
        m = jnp.max(jnp.maximum(part, sk), axis=-1, keepdims=True)
        p = jnp.exp(sc - m)
        denom = jnp.sum(p, axis=-1, keepdims=True) + jnp.exp(sk - m)
        pb = p.astype(BF16)
        parts = []
        for c in range(B_Q_HEADS // 2):
            o0 = jnp.dot(pb[2 * c], vw, preferred_element_type=F32) / denom[2 * c]
            o1 = jnp.dot(pb[2 * c + 1], vw, preferred_element_type=F32) / denom[2 * c + 1]
            parts.append(jnp.where(first_half, o0, o1))
        o_ref[0, pl.ds(j0, blk), :] = jnp.concatenate(parts, axis=1).astype(BF16)
        return carry

    lax.fori_loop(0, tq // blk, block_body, 0)


def _mixer_b(q, k, v, sink, seq):
    b = q.shape[0]
    tq = ATT_TILE
    blk = B_BLOCK
    nblk = tq // blk
    last = seq // blk - 1
    cur = pl.BlockSpec((1, tq, B_KV_WIDTH), lambda bb, t: (bb, t, 0))
    prev = pl.BlockSpec((1, blk, B_KV_WIDTH), lambda bb, t: (bb, jnp.maximum(t * nblk - 1, 0), 0))
    nxt = pl.BlockSpec((1, blk, B_KV_WIDTH), lambda bb, t: (bb, jnp.minimum((t + 1) * nblk, last), 0))
    return pl.pallas_call(
        functools.partial(_mixer_b_kernel, seq=seq),
        grid=(b, seq // tq),
        in_specs=[pl.BlockSpec(memory_space=pltpu.SMEM),
                  pl.BlockSpec((1, tq, B_Q_WIDTH), lambda bb, t: (bb, t, 0)),
                  cur, prev, nxt, cur, prev, nxt],
        out_specs=pl.BlockSpec((1, tq, B_Q_WIDTH), lambda bb, t: (bb, t, 0)),
        out_shape=jax.ShapeDtypeStruct((b, seq, B_Q_WIDTH), BF16),
        scratch_shapes=[pltpu.VMEM((tq + 2 * blk, B_KV_WIDTH), BF16),
                        pltpu.VMEM((tq + 2 * blk, B_KV_WIDTH), BF16)],
        compiler_params=pltpu.CompilerParams(
            dimension_semantics=("arbitrary", "arbitrary"), vmem_limit_bytes=VMEM_LIMIT),
        name="mixer_b",
    )(sink, q, k, k, k, v, v, v)


def _post_attn_kernel(x_ref, oa_ref, ob_ref, gmix_ref, wg_ref, wa_ref, wb_ref, wo_ref, gmoe_ref,
                      wr_ref, br_ref, x1_ref, h2_ref, gate_ref):
    wr = wr_ref[...]
    wr_hi = wr.astype(BF16)
    wr_lo = (wr - wr_hi.astype(F32)).astype(BF16)
    wr3 = jnp.concatenate([wr_hi, wr_hi, wr_lo], axis=0)
    for s in range(x_ref.shape[0] // POST_SLAB):
        rows = slice(s * POST_SLAB, (s + 1) * POST_SLAB)
        _post_attn_rows(x_ref, oa_ref, ob_ref, gmix_ref, wg_ref, wa_ref, wb_ref, wo_ref, gmoe_ref, wr3,
                        br_ref, x1_ref, h2_ref, gate_ref, rows)


def _post_attn_rows(x_ref, oa_ref, ob_ref, gmix_ref, wg_ref, wa_ref, wb_ref, wo_ref, gmoe_ref, wr3,
                    br_ref, x1_ref, h2_ref, gate_ref, rows):
    x = x_ref[rows, :]
    h = _rms(x, gmix_ref[...]).astype(BF16)
    gates = jax.nn.sigmoid(jnp.dot(h, wg_ref[...], preferred_element_type=F32))
    ya = jnp.dot(oa_ref[rows, :], wa_ref[...], preferred_element_type=F32)
    yb = jnp.dot(ob_ref[rows, :], wb_ref[...], preferred_element_type=F32)
    mix = gates[:, :D_MODEL] * ya + gates[:, D_MODEL:] * yb
    x1 = x + jnp.dot(mix.astype(BF16), wo_ref[...], preferred_element_type=F32)
    x1_ref[rows, :] = x1
    h2 = _rms(x1, gmoe_ref[...])
    h2_hi = h2.astype(BF16)
    h2_ref[rows, :] = h2_hi

    h2_lo = (h2 - h2_hi.astype(F32)).astype(BF16)
    logits = jnp.dot(jnp.concatenate([h2_hi, h2_lo, h2_hi], axis=1), wr3,
                     preferred_element_type=F32) + br_ref[...]
    tm = logits.shape[0]
    lane = lax.broadcasted_iota(jnp.int32, (tm, LANES), 1)
    big = jnp.int32(LANES)
    is_group = (lane >= N_EXPERTS) & (lane < N_EXPERTS + N_GROUPS)
    gl = jnp.where(is_group, logits, NEG_BIG)
    gmax = jnp.max(gl, axis=-1, keepdims=True)
    g_idx = jnp.min(jnp.where(gl == gmax, lane, big), axis=-1, keepdims=True) - N_EXPERTS
    g_w = 1.0 / jnp.sum(jnp.exp(gl - gmax), axis=-1, keepdims=True)
    in_group = (lane < N_EXPERTS) & (lax.shift_right_logical(lane, 3) == g_idx)
    sel = jnp.where(in_group, logits, NEG_BIG)
    v1 = jnp.max(sel, axis=-1, keepdims=True)
    i1 = jnp.min(jnp.where(sel == v1, lane, big), axis=-1, keepdims=True)
    sel2 = jnp.where(lane == i1, NEG_BIG, sel)
    v2 = jnp.max(sel2, axis=-1, keepdims=True)
    i2 = jnp.min(jnp.where(sel2 == v2, lane, big), axis=-1, keepdims=True)
    e2 = jnp.exp(v2 - v1)
    w1 = g_w / (1.0 + e2)
    w2 = g_w * e2 / (1.0 + e2)
    gate_ref[rows, :] = jnp.where(lane == i1, w1, 0.0) + jnp.where(lane == i2, w2, 0.0)


def _post_attn(x2d, oa, ob, norm_mix, w_gate, w_a, w_b, w_out, norm_moe, w_router, b_router):
    t = x2d.shape[0]
    tm = POST_TILE
    const = lambda i: (0, 0)
    row = lambda w: pl.BlockSpec((tm, w), lambda i: (i, 0))
    full = lambda a: pl.BlockSpec(a.shape, const, pipeline_mode=pl.Buffered(1))
    return pl.pallas_call(
        _post_attn_kernel,
        grid=(t // tm,),
        in_specs=[row(D_MODEL), row(A_WIDTH), row(B_Q_WIDTH), full(norm_mix), full(w_gate), full(w_a),
                  full(w_b), full(w_out), full(norm_moe), full(w_router), full(b_router)],
        out_specs=[row(D_MODEL), row(D_MODEL), row(LANES)],
        out_shape=[jax.ShapeDtypeStruct((t, D_MODEL), F32), jax.ShapeDtypeStruct((t, D_MODEL), BF16),
                   jax.ShapeDtypeStruct((t, LANES), F32)],
        compiler_params=pltpu.CompilerParams(
            dimension_semantics=("arbitrary",), vmem_limit_bytes=VMEM_LIMIT),
        name="post_attn",
    )(x2d, oa, ob, norm_mix, w_gate, w_a, w_b, w_out, norm_moe, w_router, b_router)


def _moe_final_kernel(x1_ref, h2_ref, gate_ref, tri_ref, w1_ref, w3_ref, w2_ref, gf_ref, o_ref,
                      xs, stage, seg_s, slo_s, shi_s, wlo_s, whi_s):
    step = pl.program_id(1)
    first_step = (pl.program_id(0) == 0) & (step == 0)
    n_sub = x1_ref.shape[0] // MOE_SUB

    @pl.when(first_step)
    def _init():
        xs[n_sub * MOE_SUB_SLOTS:, :] = jnp.zeros((MOE_CHUNK, D_MODEL), BF16)
        stage[...] = jnp.zeros(stage.shape, BF16)

    @pl.when(step == 0)
    def _sort():
        r_i = lax.broadcasted_iota(jnp.int32, (LANES, LANES), 0)
        c_i = lax.broadcasted_iota(jnp.int32, (LANES, LANES), 1)
        upper = jnp.where(r_i < c_i, 1.0, 0.0)
        col = lax.broadcasted_iota(jnp.int32, (MOE_SUB, MOE_SUB_SLOTS), 1)
        for s in range(n_sub):
            rows = slice(s * MOE_SUB, (s + 1) * MOE_SUB)
            g = gate_ref[rows, :]
            hit = g > 0.0
            onehot = jnp.where(hit, 1.0, 0.0)
            count = jnp.sum(onehot, axis=0, keepdims=True)
            padded = jnp.ceil(count * (1.0 / SEG_ALIGN)) * SEG_ALIGN
            seg = jnp.dot(jnp.broadcast_to(padded, (8, LANES)), upper, preferred_element_type=F32,
                          precision=lax.Precision.HIGHEST)[0:1]
            seg_i = seg.astype(jnp.int32)
            padded_i = padded.astype(jnp.int32)
            for ex in range(N_EXPERTS):
                seg_s[2 * s, ex] = seg_i[0, ex] + s * MOE_SUB_SLOTS
                seg_s[2 * s + 1, ex] = padded_i[0, ex]
            rank = jnp.dot(tri_ref[...], onehot.astype(BF16), preferred_element_type=F32)
            slot = seg + rank
            s_lo = jnp.min(jnp.where(hit, slot, 1e9), axis=-1, keepdims=True)
            s_hi = jnp.max(jnp.where(hit, slot, -1.0), axis=-1, keepdims=True)
            wlo_s[rows, :] = jnp.sum(jnp.where(hit & (slot == s_lo), g, 0.0), axis=-1, keepdims=True)
            whi_s[rows, :] = jnp.sum(jnp.where(hit & (slot == s_hi) & (s_hi > s_lo), g, 0.0),
                                     axis=-1, keepdims=True)
            s_lo_i = s_lo.astype(jnp.int32)
            s_hi_i = s_hi.astype(jnp.int32)
            slo_s[rows, :] = s_lo_i
            shi_s[rows, :] = s_hi_i
            perm_t = jnp.where((col == s_lo_i) | (col == s_hi_i), 1.0, 0.0).astype(BF16)
            sorted_rows = lax.dot_general(perm_t, h2_ref[rows, :], (((0,), (0,)), ((), ())),
                                          preferred_element_type=F32)
            xs[s * MOE_SUB_SLOTS:(s + 1) * MOE_SUB_SLOTS, :] = sorted_rows.astype(BF16)

    for j in range(MOE_ESTEP):
        e = step * MOE_ESTEP + j
        starts = [seg_s[2 * s, e] for s in range(n_sub)]
        lens = [seg_s[2 * s + 1, e] for s in range(n_sub)]

        dst = jnp.int32(0)
        dsts = []
        for s in range(n_sub):
            dsts.append(dst)

            def gather_body(k, carry, src0=starts[s], dst0=dst):
                src = pl.multiple_of(src0 + k * MOE_CHUNK, SEG_ALIGN)
                to = pl.multiple_of(dst0 + k * MOE_CHUNK, SEG_ALIGN)
                stage[pl.ds(to, MOE_CHUNK), :] = xs[pl.ds(src, MOE_CHUNK), :]
                return carry

            lax.fori_loop(0, (lens[s] + MOE_CHUNK - 1) // MOE_CHUNK, gather_body, 0)
            dst = dst + lens[s]
        total = dst

        def chunk_body(k, carry, j=j):
            r0 = pl.multiple_of(k * MOE_CHUNK, MOE_CHUNK)
            x = stage[pl.ds(r0, MOE_CHUNK), :]
            a = jnp.dot(x, w1_ref[j], preferred_element_type=F32)
            b = jnp.dot(x, w3_ref[j], preferred_element_type=F32)
            hid = (a * jax.nn.sigmoid(a)) * b
            stage[pl.ds(r0, MOE_CHUNK), :] = jnp.dot(hid.astype(BF16), w2_ref[j],
                                                     preferred_element_type=F32).astype(BF16)
            return carry

        lax.fori_loop(0, (total + MOE_CHUNK - 1) // MOE_CHUNK, chunk_body, 0)

        for s in range(n_sub):
            def scatter_body(k, carry, src0=dsts[s], dst0=starts[s]):
                src = pl.multiple_of(src0 + k * SEG_ALIGN, SEG_ALIGN)
                to = pl.multiple_of(dst0 + k * SEG_ALIGN, SEG_ALIGN)
                xs[pl.ds(to, SEG_ALIGN), :] = stage[pl.ds(src, SEG_ALIGN), :]
                return carry

            lax.fori_loop(0, lens[s] // SEG_ALIGN, scatter_body, 0)

    @pl.when(step == N_EXPERTS // MOE_ESTEP - 1)
    def _combine():
        col = lax.broadcasted_iota(jnp.int32, (MOE_SUB, MOE_SUB_SLOTS), 1)
        for s in range(n_sub):
            rows = slice(s * MOE_SUB, (s + 1) * MOE_SUB)
            wperm = (jnp.where(col == slo_s[rows, :], wlo_s[rows, :], 0.0)
                     + jnp.where(col == shi_s[rows, :], whi_s[rows, :], 0.0)).astype(BF16)
            moe = jnp.dot(wperm, xs[s * MOE_SUB_SLOTS:(s + 1) * MOE_SUB_SLOTS, :], preferred_element_type=F32)
            o_ref[rows, :] = _rms(x1_ref[rows, :] + moe, gf_ref[...])


def _moe_final(x1, h2, gate, w1, w3, w2, norm_final):
    t = x1.shape[0]
    tm = MOE_TILE
    n_sub = tm // MOE_SUB
    row = lambda w: pl.BlockSpec((tm, w), lambda i, e: (i, 0))
    tri = jnp.tril(jnp.ones((MOE_SUB, MOE_SUB), BF16), -1)
    return pl.pallas_call(
        _moe_final_kernel,
        grid=(t // tm, N_EXPERTS // MOE_ESTEP),
        in_specs=[row(D_MODEL), row(D_MODEL), row(LANES),
                  pl.BlockSpec((MOE_SUB, MOE_SUB), lambda i, e: (0, 0), pipeline_mode=pl.Buffered(1)),
                  pl.BlockSpec((MOE_ESTEP, D_MODEL, D_FF), lambda i, e: (e, 0, 0)),
                  pl.BlockSpec((MOE_ESTEP, D_MODEL, D_FF), lambda i, e: (e, 0, 0)),
                  pl.BlockSpec((MOE_ESTEP, D_FF, D_MODEL), lambda i, e: (e, 0, 0)),
                  pl.BlockSpec((1, D_MODEL), lambda i, e: (0, 0))],
        out_specs=row(D_MODEL),
        out_shape=jax.ShapeDtypeStruct((t, D_MODEL), F32),
        scratch_shapes=[pltpu.VMEM((n_sub * MOE_SUB_SLOTS + MOE_CHUNK, D_MODEL), BF16),
                        pltpu.VMEM((2 * tm + 2 * MOE_CHUNK, D_MODEL), BF16),
                        pltpu.SMEM((2 * n_sub, N_EXPERTS), jnp.int32),
                        pltpu.VMEM((tm, 1), jnp.int32), pltpu.VMEM((tm, 1), jnp.int32),
                        pltpu.VMEM((tm, 1), F32), pltpu.VMEM((tm, 1), F32)],
        compiler_params=pltpu.CompilerParams(
            dimension_semantics=("arbitrary", "arbitrary"), vmem_limit_bytes=VMEM_LIMIT),
        name="moe_final",
    )(x1, h2, gate, tri, w1, w3, w2, norm_final)


def _prep_weights(norm_mix, w_in, attn_sink, w_branch_a, w_branch_b, w_gate, w_out, norm_moe,
                  w_router_group, b_router_group, w_router_expert, b_router_expert, w1, w3, w2, norm_final):
    wi = w_in[0]
    a_w = 3 * A_WIDTH

    def group_cols(gi):
        return jnp.concatenate([wi[:, role * a_w + gi * A_WIDTH: role * a_w + (gi + 1) * A_WIDTH]
                                for role in range(3)], axis=1)

    b_heads = [g * B_REP + c for c in range(B_REP) for g in range(B_KV_HEADS)]
    bq_cols = jnp.concatenate([wi[:, 3 * a_w + h * HEAD_DIM: 3 * a_w + (h + 1) * HEAD_DIM] for h in b_heads],
                              axis=1)
    w_b_rows = jnp.concatenate([w_branch_b[0][h * HEAD_DIM:(h + 1) * HEAD_DIM] for h in b_heads], axis=0)
    w_nat = jnp.concatenate([group_cols(0), bq_cols, wi[:, 3 * a_w + B_Q_WIDTH:]], axis=1).astype(BF16)
    w_g1 = group_cols(1).astype(BF16)
    w_g2 = group_cols(2).astype(BF16)
    pad = LANES - N_EXPERTS - N_GROUPS
    w_router = jnp.concatenate([w_router_expert[0], w_router_group[0],
                                jnp.zeros((D_MODEL, pad), F32)], axis=1)
    b_router = jnp.concatenate([b_router_expert[0].reshape(-1), b_router_group[0],
                                jnp.zeros((pad,), F32)]).reshape(1, LANES)
    return dict(
        norm_mix=norm_mix[0].reshape(1, D_MODEL), w_nat=w_nat, w_g1=w_g1, w_g2=w_g2, sink=attn_sink[0],
        w_a=w_branch_a[0].astype(BF16), w_b=w_b_rows.astype(BF16), w_gate=w_gate[0].astype(BF16),
        w_out=w_out[0].astype(BF16), norm_moe=norm_moe[0].reshape(1, D_MODEL),
        w_router=w_router, b_router=b_router,
        w1=w1[0].astype(BF16), w3=w3[0].astype(BF16), w2=w2[0].astype(BF16),
        norm_final=norm_final.reshape(1, D_MODEL))


def _trunk(x, p):
    b, s, _ = x.shape
    outs = _in_proj(x, p["norm_mix"], p["w_nat"], p["w_g1"], p["w_g2"])
    oa = _mixer_a(outs[:9], s)
    ob = _mixer_b(outs[9], outs[10], outs[11], p["sink"], s)
    t = b * s
    x1, h2, gate = _post_attn(x.reshape(t, D_MODEL), oa.reshape(t, A_WIDTH), ob.reshape(t, B_Q_WIDTH),
                              p["norm_mix"], p["w_gate"], p["w_a"], p["w_b"], p["w_out"], p["norm_moe"],
                              p["w_router"], p["b_router"])
    y = _moe_final(x1, h2, gate, p["w1"], p["w3"], p["w2"], p["norm_final"])
    return y.reshape(b, s, D_MODEL)


def kernel(x_prompt, x_sample, norm_mix, w_in, attn_sink, w_branch_a, w_branch_b, w_gate, w_out, norm_moe,
           w_router_group, b_router_group, w_router_expert, b_router_expert, w1, w3, w2, norm_final):
    p = _prep_weights(norm_mix, w_in, attn_sink, w_branch_a, w_branch_b, w_gate, w_out, norm_moe,
                      w_router_group, b_router_group, w_router_expert, b_router_expert, w1, w3, w2, norm_final)
    return (_trunk(x_prompt, p), _trunk(x_sample, p))
```

```python
import functools

import jax
import jax.numpy as jnp
from jax import lax
from jax.experimental import pallas as pl
from jax.experimental.pallas import tpu as pltpu

D_MODEL = 1024
HEAD_DIM = 64
ROT_DIM = HEAD_DIM // 4
ROT_HALF = ROT_DIM // 2
ROPE_THETA = 500000.0
DILATIONS = (1, 4, 16)
A_HALF = 64
A_BLOCK = 64
A_QBLOCK = 128
A_HEADS = 8
A_WIDTH = A_HEADS * HEAD_DIM
B_Q_HEADS = 8
B_KV_HEADS = 2
B_REP = B_Q_HEADS // B_KV_HEADS
B_WINDOW = 128
B_BLOCK = 128
B_Q_WIDTH = B_Q_HEADS * HEAD_DIM
B_KV_WIDTH = B_KV_HEADS * HEAD_DIM
N_GROUPS = 4
EXPERTS_PER_GROUP = 8
N_EXPERTS = N_GROUPS * EXPERTS_PER_GROUP
D_FF = 256
RMS_EPS = 1e-6
Q_SCALE = HEAD_DIM ** -0.5

LANES = 128
NEG_BIG = -1e30
VMEM_LIMIT = 58 * 1024 * 1024

IN_TILE = 512
ATT_TILE = 1024
A_TILE = 1024
A_STEP_WIDTH = 512
POST_TILE = 1024
POST_SLAB = 256
MOE_TILE = 1024
SEG_ALIGN = 16
MOE_CHUNK = 128
MOE_COPY = 32
MOE_SUB = 256
MOE_SUB_SLOTS = 2 * MOE_SUB + N_EXPERTS * SEG_ALIGN
MOE_ESTEP = 4

F32 = jnp.float32
BF16 = jnp.bfloat16


def _rms(x, g):
    ms = jnp.mean(x * x, axis=-1, keepdims=True)
    return x * lax.rsqrt(ms + RMS_EPS) * g


def _rope(a, cos, sin, low_half):
    outs = []
    for k in range(a.shape[1] // LANES):
        ak = a[:, k * LANES:(k + 1) * LANES]
        up = pltpu.roll(ak, LANES - ROT_HALF, axis=1)
        dn = pltpu.roll(ak, ROT_HALF, axis=1)
        outs.append(ak * cos + jnp.where(low_half, up, dn) * sin)
    return outs[0] if len(outs) == 1 else jnp.concatenate(outs, axis=1)


def _in_proj_kernel(x_ref, g_ref, cos_ref, sin_ref, wn_ref, w1_ref, w2_ref,
                    q0, k0, v0, q1, k1, v1, q2, k2, v2, bq, bk, bv,
                    xs, h1_s, h2_s, c1_s, s1_s, c2_s, s2_s):
    tm = x_ref.shape[1]
    n_chunks = D_MODEL // LANES
    for c in range(n_chunks):
        xs[c] = x_ref[0, :, c * LANES:(c + 1) * LANES]
    g = g_ref[...]
    low_half = (lax.broadcasted_iota(jnp.int32, (tm, LANES), 1) % HEAD_DIM) < ROT_HALF

    def mm(h, w_ref, c0, n):
        return jnp.dot(h, w_ref[:, c0:c0 + n], preferred_element_type=F32)

    h0 = _rms(x_ref[0], g).astype(BF16)
    cos0, sin0 = cos_ref[...], sin_ref[...]
    q0[0, 0] = (_rope(mm(h0, wn_ref, 0, A_WIDTH), cos0, sin0, low_half) * Q_SCALE).astype(BF16)
    k0[0, 0] = _rope(mm(h0, wn_ref, A_WIDTH, A_WIDTH), cos0, sin0, low_half).astype(BF16)
    v0[0, 0] = mm(h0, wn_ref, 2 * A_WIDTH, A_WIDTH).astype(BF16)
    c = 3 * A_WIDTH
    bq[0] = (_rope(mm(h0, wn_ref, c, B_Q_WIDTH), cos0, sin0, low_half) * Q_SCALE).astype(BF16)
    bk[0] = _rope(mm(h0, wn_ref, c + B_Q_WIDTH, B_KV_WIDTH), cos0, sin0, low_half).astype(BF16)
    bv[0] = mm(h0, wn_ref, c + B_Q_WIDTH + B_KV_WIDTH, B_KV_WIDTH).astype(BF16)

    for d, hs, cs, ss, w_ref, qo, ko, vo in ((DILATIONS[1], h1_s, c1_s, s1_s, w1_ref, q1, k1, v1),
                                             (DILATIONS[2], h2_s, c2_s, s2_s, w2_ref, q2, k2, v2)):
        n = tm // d
        for r in range(d):
            rows = pl.ds(r, n, stride=d)
            xr = jnp.concatenate([xs[c, rows, :] for c in range(n_chunks)], axis=1)
            hs[r * n:(r + 1) * n, :] = _rms(xr, g).astype(BF16)
            cs[r * n:(r + 1) * n, :] = cos_ref[rows, :]
            ss[r * n:(r + 1) * n, :] = sin_ref[rows, :]
        h = hs[...]
        cos, sin = cs[...], ss[...]
        qv = (_rope(mm(h, w_ref, 0, A_WIDTH), cos, sin, low_half) * Q_SCALE).astype(BF16)
        kv = _rope(mm(h, w_ref, A_WIDTH, A_WIDTH), cos, sin, low_half).astype(BF16)
        vv = mm(h, w_ref, 2 * A_WIDTH, A_WIDTH).astype(BF16)
        for r in range(d):
            qo[0, r] = qv[r * n:(r + 1) * n]
            ko[0, r] = kv[r * n:(r + 1) * n]
            vo[0, r] = vv[r * n:(r + 1) * n]


def _rope_tables(s):
    inv = jnp.power(jnp.float32(ROPE_THETA), -jnp.arange(ROT_HALF, dtype=F32) / ROT_HALF)
    ang = jnp.arange(s).astype(F32)[:, None] * inv[None, :]
    cos, sin = jnp.cos(ang), jnp.sin(ang)
    pad_one = jnp.ones((s, HEAD_DIM - ROT_DIM), F32)
    pad_zero = jnp.zeros((s, HEAD_DIM - ROT_DIM), F32)
    cos_h = jnp.concatenate([cos, cos, pad_one], axis=1)
    sin_h = jnp.concatenate([-sin, sin, pad_zero], axis=1)
    return jnp.concatenate([cos_h, cos_h], axis=1), jnp.concatenate([sin_h, sin_h], axis=1)


def _in_proj(x, norm_g, w_nat, w_g1, w_g2):
    b, s, _ = x.shape
    tm = IN_TILE
    cos_t, sin_t = _rope_tables(s)
    const = lambda t, bb: (0, 0)
    one_buf = pl.Buffered(1)
    in_specs = [
        pl.BlockSpec((1, tm, D_MODEL), lambda t, bb: (bb, t, 0)),
        pl.BlockSpec((1, D_MODEL), const),
        pl.BlockSpec((tm, LANES), lambda t, bb: (t, 0)),
        pl.BlockSpec((tm, LANES), lambda t, bb: (t, 0)),
        pl.BlockSpec(w_nat.shape, const, pipeline_mode=one_buf),
        pl.BlockSpec(w_g1.shape, const, pipeline_mode=one_buf),
        pl.BlockSpec(w_g2.shape, const, pipeline_mode=one_buf),
    ]
    out_shape, out_specs = [], []
    for d in DILATIONS:
        for _ in range(3):
            out_shape.append(jax.ShapeDtypeStruct((b, d, s // d, A_WIDTH), BF16))
            out_specs.append(pl.BlockSpec((1, d, tm // d, A_WIDTH), lambda t, bb: (bb, 0, t, 0)))
    for w in (B_Q_WIDTH, B_KV_WIDTH, B_KV_WIDTH):
        out_shape.append(jax.ShapeDtypeStruct((b, s, w), BF16))
        out_specs.append(pl.BlockSpec((1, tm, w), lambda t, bb: (bb, t, 0)))
    scratch = [pltpu.VMEM((D_MODEL // LANES, tm, LANES), F32),
               pltpu.VMEM((tm, D_MODEL), BF16), pltpu.VMEM((tm, D_MODEL), BF16)] + \
              [pltpu.VMEM((tm, LANES), F32) for _ in range(4)]
    return pl.pallas_call(
        _in_proj_kernel,
        grid=(s // tm, b),
        in_specs=in_specs,
        out_specs=out_specs,
        out_shape=out_shape,
        scratch_shapes=scratch,
        compiler_params=pltpu.CompilerParams(
            dimension_semantics=("arbitrary", "arbitrary"), vmem_limit_bytes=VMEM_LIMIT),
        name="in_proj",
    )(x, norm_g.reshape(1, D_MODEL), cos_t, sin_t, w_nat, w_g1, w_g2)


def _attend_heads(blocks, valid, first_half):
    second_half = jnp.logical_not(first_half)
    n_heads = blocks[0][0].shape[1] // HEAD_DIM
    scores = []
    for qb, kw, _ in blocks:
        for h in range(n_heads):
            cs = slice((h // 2) * LANES, (h // 2 + 1) * LANES)
            qp = qb[:, cs]
            qm = jnp.where(first_half if h % 2 == 0 else second_half, qp, jnp.zeros_like(qp))
            scores.append(lax.dot_general(qm, kw[:, cs], (((1,), (1,)), ((), ())),
                                          preferred_element_type=F32))
    sc = jnp.where(valid[None], jnp.stack(scores), NEG_BIG)
    m = jnp.max(sc, axis=-1, keepdims=True)
    p = jnp.exp(sc - m)
    l = jnp.sum(p, axis=-1, keepdims=True)
    pb = p.astype(BF16)
    results = []
    for bi, (_, _, vw) in enumerate(blocks):
        out = []
        for hp in range(n_heads // 2):
            i0 = bi * n_heads + 2 * hp
            vp = vw[:, hp * LANES:(hp + 1) * LANES]
            o0 = jnp.dot(pb[i0], vp, preferred_element_type=F32)
            o1 = jnp.dot(pb[i0 + 1], vp, preferred_element_type=F32)
            out.append((jnp.where(first_half, o0, o1), jnp.where(first_half, m[i0], m[i0 + 1]),
                        jnp.where(first_half, l[i0], l[i0 + 1])))
        results.append(out)
    return results


def _mixer_a_kernel(*refs, seq):
    ins, (o_ref, kext, vext, acc_o, acc_m, acc_l) = refs[:21], refs[21:]
    t = pl.program_id(1)
    tq = A_TILE
    halo = A_HALF

    first_group = len(DILATIONS) - 1
    for gi in reversed(range(len(DILATIONS))):
        d = DILATIONS[gi]
        q, kc, kp, kn, vc, vp, vn = ins[7 * gi:7 * gi + 7]
        n = tq // d
        blk = min(n, A_QBLOCK)
        nblk = n // blk
        win = -(-(blk + 2 * halo) // LANES) * LANES
        tail = win - (blk + 2 * halo) if n == blk else 0
        ld = seq // d
        col = lax.broadcasted_iota(jnp.int32, (blk, win), 1)
        diff = col - lax.broadcasted_iota(jnp.int32, (blk, win), 0)
        band = (diff >= 0) & (diff <= 2 * halo)
        first_half = lax.broadcasted_iota(jnp.int32, (blk, LANES), 1) < HEAD_DIM

        pair = A_QBLOCK // blk

        def residue_body(rp, carry, q=q, kc=kc, kp=kp, kn=kn, vc=vc, vp=vp, vn=vn, d=d, n=n, nblk=nblk,
                         ld=ld, blk=blk, win=win, col=col, band=band, first_half=first_half,
                         first=(gi == first_group), pair=pair, tail=tail):
            for u in range(pair):
                r = rp * pair + u
                base = u * win
                kext[base:base + halo, :] = kp[0, r]
                kext[base + halo:base + halo + n, :] = kc[0, r]
                kext[base + halo + n:base + 2 * halo + n, :] = kn[0, r]
                vext[base:base + halo, :] = vp[0, r]
                vext[base + halo:base + halo + n, :] = vc[0, r]
                vext[base + halo + n:base + 2 * halo + n, :] = vn[0, r]
                if tail:
                    kext[base + 2 * halo + n:base + win, :] = jnp.zeros((tail, kext.shape[1]), BF16)
                    vext[base + 2 * halo + n:base + win, :] = jnp.zeros((tail, vext.shape[1]), BF16)

            def block_body(j, carry2):
                j0 = pl.multiple_of(j * blk, blk)
                qbase = t * n + j * blk
                valid = band & (col >= halo - qbase) & (col < ld - qbase + halo)
                blocks = [(q[0, rp * pair + u, pl.ds(j0, blk), :],
                           kext[pl.ds(u * win + j0, win), :],
                           vext[pl.ds(u * win + j0, win), :]) for u in range(pair)]
                results = _attend_heads(blocks, valid, first_half)
                for u, parts in enumerate(results):
                    if d == 1:
                        rows = pl.ds(j0, blk)
                    else:
                        rows = pl.ds(j * (blk * d) + rp * pair + u, blk, stride=d)
                    for hp, (o_b, m_b, l_b) in enumerate(parts):
                        if first:
                            acc_o[hp, rows, :] = o_b
                            acc_l[hp, rows, :] = l_b
                            acc_m[hp, rows, :] = m_b
                        else:
                            m_old = acc_m[hp, rows, :]
                            m_new = jnp.maximum(m_old, m_b)
                            a_old = jnp.exp(m_old - m_new)
                            a_new = jnp.exp(m_b - m_new)
                            acc_o[hp, rows, :] = acc_o[hp, rows, :] * a_old + o_b * a_new
                            acc_l[hp, rows, :] = acc_l[hp, rows, :] * a_old + l_b * a_new
                            acc_m[hp, rows, :] = m_new
                return carry2

            lax.fori_loop(0, nblk, block_body, 0)
            return carry

        lax.fori_loop(0, d // pair, residue_body, 0)

    o_ref[0] = jnp.concatenate([acc_o[hp] / acc_l[hp] for hp in range(A_STEP_WIDTH // LANES)],
                               axis=1).astype(BF16)


def _mixer_a(qkv, seq):
    b = qkv[0].shape[0]
    tq = A_TILE
    blk = A_BLOCK
    aw = A_STEP_WIDTH
    args, in_specs = [], []
    for gi, d in enumerate(DILATIONS):
        q, k, v = qkv[3 * gi:3 * gi + 3]
        n = tq // d
        nblk = n // blk
        last = seq // d // blk - 1
        cur = pl.BlockSpec((1, d, n, aw), lambda bb, t, hh: (bb, 0, t, hh))
        prev = pl.BlockSpec((1, d, blk, aw),
                            lambda bb, t, hh, nblk=nblk: (bb, 0, jnp.maximum(t * nblk - 1, 0), hh))
        nxt = pl.BlockSpec((1, d, blk, aw),
                           lambda bb, t, hh, nblk=nblk, last=last: (bb, 0, jnp.minimum((t + 1) * nblk, last), hh))
        args += [q, k, k, k, v, v, v]
        in_specs += [cur, cur, prev, nxt, cur, prev, nxt]
    return pl.pallas_call(
        functools.partial(_mixer_a_kernel, seq=seq),
        grid=(b, seq // tq, A_WIDTH // aw),
        in_specs=in_specs,
        out_specs=pl.BlockSpec((1, tq, aw), lambda bb, t, hh: (bb, t, hh)),
        out_shape=jax.ShapeDtypeStruct((b, seq, A_WIDTH), BF16),
        scratch_shapes=[pltpu.VMEM((tq + 2 * blk, aw), BF16), pltpu.VMEM((tq + 2 * blk, aw), BF16),
                        pltpu.VMEM((aw // LANES, tq, LANES), F32),
                        pltpu.VMEM((aw // LANES, tq, LANES), F32),
                        pltpu.VMEM((aw // LANES, tq, LANES), F32)],
        compiler_params=pltpu.CompilerParams(
            dimension_semantics=("arbitrary", "arbitrary", "arbitrary"), vmem_limit_bytes=VMEM_LIMIT),
        name="mixer_a",
    )(*args)


def _mixer_b_kernel(sink_ref, q, kc, kp, kn, vc, vp, vn, o_ref, kext, vext, *, seq):
    t = pl.program_id(1)
    tq = ATT_TILE
    blk = B_BLOCK
    win = 3 * blk
    kext[0:blk, :] = kp[0]
    kext[blk:blk + tq, :] = kc[0]
    kext[blk + tq:2 * blk + tq, :] = kn[0]
    vext[0:blk, :] = vp[0]
    vext[blk:blk + tq, :] = vc[0]
    vext[blk + tq:2 * blk + tq, :] = vn[0]
    col = lax.broadcasted_iota(jnp.int32, (blk, win), 1)
    row = lax.broadcasted_iota(jnp.int32, (blk, win), 0)
    diff = col - row
    first_half = lax.broadcasted_iota(jnp.int32, (blk, LANES), 1) < HEAD_DIM
    second_half = jnp.logical_not(first_half)

    def block_body(j, carry):
        j0 = pl.multiple_of(j * blk, blk)
        qb = q[0, pl.ds(j0, blk), :]
        kw = kext[pl.ds(j0, win), :]
        vw = vext[pl.ds(j0, win), :]
        qbase = t * tq + j * blk
        valid = ((diff >= 0) & (diff <= 2 * B_WINDOW)
                 & (col >= blk - qbase) & (col < seq - qbase + blk))
        scores, sinks = [], []
        for c in range(B_Q_HEADS // 2):
            qp = qb[:, c * LANES:(c + 1) * LANES]
            for g, keep in enumerate((first_half, second_half)):
                qm = jnp.where(keep, qp, jnp.zeros_like(qp))
                scores.append(lax.dot_general(qm, kw, (((1,), (1,)), ((), ())), preferred_element_type=F32))
                sinks.append(jnp.full((1, LANES), sink_ref[g * B_REP + c], F32))
        sc = jnp.where(valid[None], jnp.stack(scores), NEG_BIG)
        sk = jnp.stack(sinks)
        part = sc[..., 0:LANES]
        for k in range(1, win // LANES):
            part = jnp.maximum(part, sc[..., k * LANES:(k + 1) * LANES])
        m = jnp.max(jnp.maximum(part, sk), axis=-1, keepdims=True)
        p = jnp.exp(sc - m)
        denom = jnp.sum(p, axis=-1, keepdims=True) + jnp.exp(sk - m)
        pb = p.astype(BF16)
        parts = []
        for c in range(B_Q_HEADS // 2):
            o0 = jnp.dot(pb[2 * c], vw, preferred_element_type=F32) / denom[2 * c]
            o1 = jnp.dot(pb[2 * c + 1], vw, preferred_element_type=F32) / denom[2 * c + 1]
            parts.append(jnp.where(first_half, o0, o1))
        o_ref[0, pl.ds(j0, blk), :] = jnp.concatenate(parts, axis=1).astype(BF16)
        return carry

    lax.fori_loop(0, tq // blk, block_body, 0)


def _mixer_b(q, k, v, sink, seq):
    b = q.shape[0]
    tq = ATT_TILE
    blk = B_BLOCK
    nblk = tq // blk
    last = seq // blk - 1
    cur = pl.BlockSpec((1, tq, B_KV_WIDTH), lambda bb, t: (bb, t, 0))
    prev = pl.BlockSpec((1, blk, B_KV_WIDTH), lambda bb, t: (bb, jnp.maximum(t * nblk - 1, 0), 0))
    nxt = pl.BlockSpec((1, blk, B_KV_WIDTH), lambda bb, t: (bb, jnp.minimum((t + 1) * nblk, last), 0))
    return pl.pallas_call(
        functools.partial(_mixer_b_kernel, seq=seq),
        grid=(b, seq // tq),
        in_specs=[pl.BlockSpec(memory_space=pltpu.SMEM),
                  pl.BlockSpec((1, tq, B_Q_WIDTH), lambda bb, t: (bb, t, 0)),
                  cur, prev, nxt, cur, prev, nxt],
        out_specs=pl.BlockSpec((1, tq, B_Q_WIDTH), lambda bb, t: (bb, t, 0)),
        out_shape=jax.ShapeDtypeStruct((b, seq, B_Q_WIDTH), BF16),
        scratch_shapes=[pltpu.VMEM((tq + 2 * blk, B_KV_WIDTH), BF16),
                        pltpu.VMEM((tq + 2 * blk, B_KV_WIDTH), BF16)],
        compiler_params=pltpu.CompilerParams(
            dimension_semantics=("arbitrary", "arbitrary"), vmem_limit_bytes=VMEM_LIMIT),
        name="mixer_b",
    )(sink, q, k, k, k, v, v, v)


def _post_attn_kernel(x_ref, oa_ref, ob_ref, gmix_ref, wg_ref, wa_ref, wb_ref, wo_ref, gmoe_ref,
                      wr_ref, br_ref, x1_ref, h2_ref, gate_ref):
    wr = wr_ref[...]
    wr_hi = wr.astype(BF16)
    wr_lo = (wr - wr_hi.astype(F32)).astype(BF16)
    wr3 = jnp.concatenate([wr_hi, wr_hi, wr_lo], axis=0)
    for s in range(x_ref.shape[0] // POST_SLAB):
        rows = slice(s * POST_SLAB, (s + 1) * POST_SLAB)
        _post_attn_rows(x_ref, oa_ref, ob_ref, gmix_ref, wg_ref, wa_ref, wb_ref, wo_ref, gmoe_ref, wr3,
                        br_ref, x1_ref, h2_ref, gate_ref, rows)


def _post_attn_rows(x_ref, oa_ref, ob_ref, gmix_ref, wg_ref, wa_ref, wb_ref, wo_ref, gmoe_ref, wr3,
                    br_ref, x1_ref, h2_ref, gate_ref, rows):
    x = x_ref[rows, :]
    h = _rms(x, gmix_ref[...]).astype(BF16)
    gates = jax.nn.sigmoid(jnp.dot(h, wg_ref[...], preferred_element_type=F32))
    ya = jnp.dot(oa_ref[rows, :], wa_ref[...], preferred_element_type=F32)
    yb = jnp.dot(ob_ref[rows, :], wb_ref[...], preferred_element_type=F32)
    mix = gates[:, :D_MODEL] * ya + gates[:, D_MODEL:] * yb
    x1 = x + jnp.dot(mix.astype(BF16), wo_ref[...], preferred_element_type=F32)
    x1_ref[rows, :] = x1
    h2 = _rms(x1, gmoe_ref[...])
    h2_hi = h2.astype(BF16)
    h2_ref[rows, :] = h2_hi

    h2_lo = (h2 - h2_hi.astype(F32)).astype(BF16)
    logits = jnp.dot(jnp.concatenate([h2_hi, h2_lo, h2_hi], axis=1), wr3,
                     preferred_element_type=F32) + br_ref[...]
    tm = logits.shape[0]
    lane = lax.broadcasted_iota(jnp.int32, (tm, LANES), 1)
    big = jnp.int32(LANES)
    is_group = (lane >= N_EXPERTS) & (lane < N_EXPERTS + N_GROUPS)
    gl = jnp.where(is_group, logits, NEG_BIG)
    gmax = jnp.max(gl, axis=-1, keepdims=True)
    g_idx = jnp.min(jnp.where(gl == gmax, lane, big), axis=-1, keepdims=True) - N_EXPERTS
    g_w = 1.0 / jnp.sum(jnp.exp(gl - gmax), axis=-1, keepdims=True)
    in_group = (lane < N_EXPERTS) & (lax.shift_right_logical(lane, 3) == g_idx)
    sel = jnp.where(in_group, logits, NEG_BIG)
    v1 = jnp.max(sel, axis=-1, keepdims=True)
    i1 = jnp.min(jnp.where(sel == v1, lane, big), axis=-1, keepdims=True)
    sel2 = jnp.where(lane == i1, NEG_BIG, sel)
    v2 = jnp.max(sel2, axis=-1, keepdims=True)
    i2 = jnp.min(jnp.where(sel2 == v2, lane, big), axis=-1, keepdims=True)
    e2 = jnp.exp(v2 - v1)
    w1 = g_w / (1.0 + e2)
    w2 = g_w * e2 / (1.0 + e2)
    gate_ref[rows, :] = jnp.where(lane == i1, w1, 0.0) + jnp.where(lane == i2, w2, 0.0)


def _post_attn(x2d, oa, ob, norm_mix, w_gate, w_a, w_b, w_out, norm_moe, w_router, b_router):
    t = x2d.shape[0]
    tm = POST_TILE
    const = lambda i: (0, 0)
    row = lambda w: pl.BlockSpec((tm, w), lambda i: (i, 0))
    full = lambda a: pl.BlockSpec(a.shape, const, pipeline_mode=pl.Buffered(1))
    return pl.pallas_call(
        _post_attn_kernel,
        grid=(t // tm,),
        in_specs=[row(D_MODEL), row(A_WIDTH), row(B_Q_WIDTH), full(norm_mix), full(w_gate), full(w_a),
                  full(w_b), full(w_out), full(norm_moe), full(w_router), full(b_router)],
        out_specs=[row(D_MODEL), row(D_MODEL), row(LANES)],
        out_shape=[jax.ShapeDtypeStruct((t, D_MODEL), F32), jax.ShapeDtypeStruct((t, D_MODEL), BF16),
                   jax.ShapeDtypeStruct((t, LANES), F32)],
        compiler_params=pltpu.CompilerParams(
            dimension_semantics=("arbitrary",), vmem_limit_bytes=VMEM_LIMIT),
        name="post_attn",
    )(x2d, oa, ob, norm_mix, w_gate, w_a, w_b, w_out, norm_moe, w_router, b_router)


def _moe_final_kernel(x1_ref, h2_ref, gate_ref, tri_ref, w1_ref, w3_ref, w2_ref, gf_ref, o_ref,
                      xs, stage, out_stage, seg_s, slo_s, shi_s, wlo_s, whi_s):
    step = pl.program_id(1)
    first_step = (pl.program_id(0) == 0) & (step == 0)
    n_sub = x1_ref.shape[0] // MOE_SUB

    @pl.when(first_step)
    def _init():
        xs[n_sub * MOE_SUB_SLOTS:, :] = jnp.zeros((MOE_CHUNK, D_MODEL), BF16)
        stage[...] = jnp.zeros(stage.shape, BF16)

    @pl.when(step == 0)
    def _sort():
        r_i = lax.broadcasted_iota(jnp.int32, (LANES, LANES), 0)
        c_i = lax.broadcasted_iota(jnp.int32, (LANES, LANES), 1)
        upper = jnp.where(r_i < c_i, 1.0, 0.0)
        col = lax.broadcasted_iota(jnp.int32, (MOE_SUB, MOE_SUB_SLOTS), 1)
        for s in range(n_sub):
            rows = slice(s * MOE_SUB, (s + 1) * MOE_SUB)
            g = gate_ref[rows, :]
            hit = g > 0.0
            onehot = jnp.where(hit, 1.0, 0.0)
            count = jnp.sum(onehot, axis=0, keepdims=True)
            padded = jnp.ceil(count * (1.0 / SEG_ALIGN)) * SEG_ALIGN
            seg = jnp.dot(jnp.broadcast_to(padded, (8, LANES)), upper, preferred_element_type=F32,
                          precision=lax.Precision.HIGHEST)[0:1]
            seg_i = seg.astype(jnp.int32)
            padded_i = padded.astype(jnp.int32)
            for ex in range(N_EXPERTS):
                seg_s[2 * s, ex] = seg_i[0, ex] + s * MOE_SUB_SLOTS
                seg_s[2 * s + 1, ex] = padded_i[0, ex]
            rank = jnp.dot(tri_ref[...], onehot.astype(BF16), preferred_element_type=F32)
            slot = seg + rank
            s_lo = jnp.min(jnp.where(hit, slot, 1e9), axis=-1, keepdims=True)
            s_hi = jnp.max(jnp.where(hit, slot, -1.0), axis=-1, keepdims=True)
            wlo_s[rows, :] = jnp.sum(jnp.where(hit & (slot == s_lo), g, 0.0), axis=-1, keepdims=True)
            whi_s[rows, :] = jnp.sum(jnp.where(hit & (slot == s_hi) & (s_hi > s_lo), g, 0.0),
                                     axis=-1, keepdims=True)
            s_lo_i = s_lo.astype(jnp.int32)
            s_hi_i = s_hi.astype(jnp.int32)
            slo_s[rows, :] = s_lo_i
            shi_s[rows, :] = s_hi_i
            perm_t = jnp.where((col == s_lo_i) | (col == s_hi_i), 1.0, 0.0).astype(BF16)
            sorted_rows = lax.dot_general(perm_t, h2_ref[rows, :], (((0,), (0,)), ((), ())),
                                          preferred_element_type=F32)
            xs[s * MOE_SUB_SLOTS:(s + 1) * MOE_SUB_SLOTS, :] = sorted_rows.astype(BF16)

    def ffn(x, j):
        a = jnp.dot(x, w1_ref[j], preferred_element_type=F32)
        b = jnp.dot(x, w3_ref[j], preferred_element_type=F32)
        hid = (a * jax.nn.sigmoid(a)) * b
        return jnp.dot(hid.astype(BF16), w2_ref[j], preferred_element_type=F32).astype(BF16)

    bases, totals, seg_starts, seg_lens, seg_dsts = [], [], [], [], []
    base = jnp.int32(0)
    for j in range(MOE_ESTEP):
        e = step * MOE_ESTEP + j
        starts = [seg_s[2 * s, e] for s in range(n_sub)]
        lens = [seg_s[2 * s + 1, e] for s in range(n_sub)]
        dst = base
        dsts = []
        for s in range(n_sub):
            dsts.append(dst)

            def gather_body(k, carry, src0=starts[s], dst0=dst):
                src = pl.multiple_of(src0 + k * MOE_COPY, SEG_ALIGN)
                to = pl.multiple_of(dst0 + k * MOE_COPY, SEG_ALIGN)
                stage[pl.ds(to, MOE_COPY), :] = xs[pl.ds(src, MOE_COPY), :]
                return carry

            lax.fori_loop(0, (lens[s] + MOE_COPY - 1) // MOE_COPY, gather_body, 0)
            dst = dst + lens[s]
        bases.append(base)
        totals.append(dst - base)
        seg_starts.append(starts)
        seg_lens.append(lens)
        seg_dsts.append(dsts)
        base = base + ((dst - base + MOE_CHUNK - 1) // MOE_CHUNK) * MOE_CHUNK

    for j in range(MOE_ESTEP):
        def extra_body(k, carry, j=j):
            r0 = pl.multiple_of(bases[j] + k * MOE_CHUNK, MOE_CHUNK)
            out_stage[pl.ds(r0, MOE_CHUNK), :] = ffn(stage[pl.ds(r0, MOE_CHUNK), :], j)
            return carry

        lax.fori_loop(1, (totals[j] + MOE_CHUNK - 1) // MOE_CHUNK, extra_body, 0)

    firsts = [ffn(stage[pl.ds(pl.multiple_of(bases[j], MOE_CHUNK), MOE_CHUNK), :], j) for j in range(MOE_ESTEP)]
    for j in range(MOE_ESTEP):
        out_stage[pl.ds(pl.multiple_of(bases[j], MOE_CHUNK), MOE_CHUNK), :] = firsts[j]

    for j in range(MOE_ESTEP):
        for s in range(n_sub):
            def scatter_body(k, carry, src0=seg_dsts[j][s], dst0=seg_starts[j][s]):
                src = pl.multiple_of(src0 + k * SEG_ALIGN, SEG_ALIGN)
                to = pl.multiple_of(dst0 + k * SEG_ALIGN, SEG_ALIGN)
                xs[pl.ds(to, SEG_ALIGN), :] = out_stage[pl.ds(src, SEG_ALIGN), :]
                return carry

            lax.fori_loop(0, seg_lens[j][s] // SEG_ALIGN, scatter_body, 0)

    @pl.when(step == N_EXPERTS // MOE_ESTEP - 1)
    def _combine():
        col = lax.broadcasted_iota(jnp.int32, (MOE_SUB, MOE_SUB_SLOTS), 1)
        for s in range(n_sub):
            rows = slice(s * MOE_SUB, (s + 1) * MOE_SUB)
            wperm = (jnp.where(col == slo_s[rows, :], wlo_s[rows, :], 0.0)
                     + jnp.where(col == shi_s[rows, :], whi_s[rows, :], 0.0)).astype(BF16)
            moe = jnp.dot(wperm, xs[s * MOE_SUB_SLOTS:(s + 1) * MOE_SUB_SLOTS, :], preferred_element_type=F32)
            o_ref[rows, :] = _rms(x1_ref[rows, :] + moe, gf_ref[...])


def _moe_final(x1, h2, gate, w1, w3, w2, norm_final):
    t = x1.shape[0]
    tm = MOE_TILE
    n_sub = tm // MOE_SUB
    row = lambda w: pl.BlockSpec((tm, w), lambda i, e: (i, 0))
    tri = jnp.tril(jnp.ones((MOE_SUB, MOE_SUB), BF16), -1)
    return pl.pallas_call(
        _moe_final_kernel,
        grid=(t // tm, N_EXPERTS // MOE_ESTEP),
        in_specs=[row(D_MODEL), row(D_MODEL), row(LANES),
                  pl.BlockSpec((MOE_SUB, MOE_SUB), lambda i, e: (0, 0), pipeline_mode=pl.Buffered(1)),
                  pl.BlockSpec((MOE_ESTEP, D_MODEL, D_FF), lambda i, e: (e, 0, 0)),
                  pl.BlockSpec((MOE_ESTEP, D_MODEL, D_FF), lambda i, e: (e, 0, 0)),
                  pl.BlockSpec((MOE_ESTEP, D_FF, D_MODEL), lambda i, e: (e, 0, 0)),
                  pl.BlockSpec((1, D_MODEL), lambda i, e: (0, 0))],
        out_specs=row(D_MODEL),
        out_shape=jax.ShapeDtypeStruct((t, D_MODEL), F32),
        scratch_shapes=[pltpu.VMEM((n_sub * MOE_SUB_SLOTS + MOE_CHUNK, D_MODEL), BF16),
                        pltpu.VMEM((2 * tm + (MOE_ESTEP + 1) * MOE_CHUNK, D_MODEL), BF16),
                        pltpu.VMEM((2 * tm + (MOE_ESTEP + 1) * MOE_CHUNK, D_MODEL), BF16),
                        pltpu.SMEM((2 * n_sub, N_EXPERTS), jnp.int32),
                        pltpu.VMEM((tm, 1), jnp.int32), pltpu.VMEM((tm, 1), jnp.int32),
                        pltpu.VMEM((tm, 1), F32), pltpu.VMEM((tm, 1), F32)],
        compiler_params=pltpu.CompilerParams(
            dimension_semantics=("arbitrary", "arbitrary"), vmem_limit_bytes=VMEM_LIMIT),
        name="moe_final",
    )(x1, h2, gate, tri, w1, w3, w2, norm_final)


def _prep_weights(norm_mix, w_in, attn_sink, w_branch_a, w_branch_b, w_gate, w_out, norm_moe,
                  w_router_group, b_router_group, w_router_expert, b_router_expert, w1, w3, w2, norm_final):
    wi = w_in[0]
    a_w = 3 * A_WIDTH

    def group_cols(gi):
        return jnp.concatenate([wi[:, role * a_w + gi * A_WIDTH: role * a_w + (gi + 1) * A_WIDTH]
                                for role in range(3)], axis=1)

    b_heads = [g * B_REP + c for c in range(B_REP) for g in range(B_KV_HEADS)]
    bq_cols = jnp.concatenate([wi[:, 3 * a_w + h * HEAD_DIM: 3 * a_w + (h + 1) * HEAD_DIM] for h in b_heads],
                              axis=1)
    w_b_rows = jnp.concatenate([w_branch_b[0][h * HEAD_DIM:(h + 1) * HEAD_DIM] for h in b_heads], axis=0)
    w_nat = jnp.concatenate([group_cols(0), bq_cols, wi[:, 3 * a_w + B_Q_WIDTH:]], axis=1).astype(BF16)
    w_g1 = group_cols(1).astype(BF16)
    w_g2 = group_cols(2).astype(BF16)
    pad = LANES - N_EXPERTS - N_GROUPS
    w_router = jnp.concatenate([w_router_expert[0], w_router_group[0],
                                jnp.zeros((D_MODEL, pad), F32)], axis=1)
    b_router = jnp.concatenate([b_router_expert[0].reshape(-1), b_router_group[0],
                                jnp.zeros((pad,), F32)]).reshape(1, LANES)
    return dict(
        norm_mix=norm_mix[0].reshape(1, D_MODEL), w_nat=w_nat, w_g1=w_g1, w_g2=w_g2, sink=attn_sink[0],
        w_a=w_branch_a[0].astype(BF16), w_b=w_b_rows.astype(BF16), w_gate=w_gate[0].astype(BF16),
        w_out=w_out[0].astype(BF16), norm_moe=norm_moe[0].reshape(1, D_MODEL),
        w_router=w_router, b_router=b_router,
        w1=w1[0].astype(BF16), w3=w3[0].astype(BF16), w2=w2[0].astype(BF16),
        norm_final=norm_final.reshape(1, D_MODEL))


def _trunk(x, p):
    b, s, _ = x.shape
    outs = _in_proj(x, p["norm_mix"], p["w_nat"], p["w_g1"], p["w_g2"])
    oa = _mixer_a(outs[:9], s)
    ob = _mixer_b(outs[9], outs[10], outs[11], p["sink"], s)
    t = b * s
    x1, h2, gate = _post_attn(x.reshape(t, D_MODEL), oa.reshape(t, A_WIDTH), ob.reshape(t, B_Q_WIDTH),
                              p["norm_mix"], p["w_gate"], p["w_a"], p["w_b"], p["w_out"], p["norm_moe"],
                              p["w_router"], p["b_router"])
    y = _moe_final(x1, h2, gate, p["w1"], p["w3"], p["w2"], p["norm_final"])
    return y.reshape(b, s, D_MODEL)


def kernel(x_prompt, x_sample, norm_mix, w_in, attn_sink, w_branch_a, w_branch_b, w_gate, w_out, norm_moe,
           w_router_group, b_router_group, w_router_expert, b_router_expert, w1, w3, w2, norm_final):
    p = _prep_weights(norm_mix, w_in, attn_sink, w_branch_a, w_branch_b, w_gate, w_out, norm_moe,
                      w_router_group, b_router_group, w_router_expert, b_router_expert, w1, w3, w2, norm_final)
    return (_trunk(x_prompt, p), _trunk(x_sample, p))
```

```python
import functools

import jax
import jax.numpy as jnp
from jax import lax
from jax.experimental import pallas as pl
from jax.experimental.pallas import tpu as pltpu

D_MODEL = 1024
HEAD_DIM = 64
ROT_DIM = HEAD_DIM // 4
ROT_HALF = ROT_DIM // 2
ROPE_THETA = 500000.0
DILATIONS = (1, 4, 16)
A_HALF = 64
A_BLOCK = 64
A_QBLOCK = 128
A_HEADS = 8
A_WIDTH = A_HEADS * HEAD_DIM
B_Q_HEADS = 8
B_KV_HEADS = 2
B_REP = B_Q_HEADS // B_KV_HEADS
B_WINDOW = 128
B_BLOCK = 128
B_Q_WIDTH = B_Q_HEADS * HEAD_DIM
B_KV_WIDTH = B_KV_HEADS * HEAD_DIM
N_GROUPS = 4
EXPERTS_PER_GROUP = 8
N_EXPERTS = N_GROUPS * EXPERTS_PER_GROUP
D_FF = 256
RMS_EPS = 1e-6
Q_SCALE = HEAD_DIM ** -0.5

LANES = 128
NEG_BIG = -1e30
VMEM_LIMIT = 58 * 1024 * 1024

IN_TILE = 512
ATT_TILE = 1024
A_TILE = 1024
A_STEP_WIDTH = 512
POST_TILE = 1024
POST_SLAB = 256
MOE_TILE = 1024
SEG_ALIGN = 16
MOE_CHUNK = 128
MOE_COPY = 32
MOE_SUB = 256
MOE_SUB_SLOTS = 2 * MOE_SUB + N_EXPERTS * SEG_ALIGN
MOE_ESTEP = 4

F32 = jnp.float32
BF16 = jnp.bfloat16


def _rms(x, g):
    ms = jnp.mean(x * x, axis=-1, keepdims=True)
    return x * lax.rsqrt(ms + RMS_EPS) * g


def _rope(a, cos, sin, low_half):
    outs = []
    for k in range(a.shape[1] // LANES):
        ak = a[:, k * LANES:(k + 1) * LANES]
        up = pltpu.roll(ak, LANES - ROT_HALF, axis=1)
        dn = pltpu.roll(ak, ROT_HALF, axis=1)
        outs.append(ak * cos + jnp.where(low_half, up, dn) * sin)
    return outs[0] if len(outs) == 1 else jnp.concatenate(outs, axis=1)


def _in_proj_kernel(x_ref, g_ref, cos_ref, sin_ref, wn_ref, w1_ref, w2_ref,
                    q0, k0, v0, q1, k1, v1, q2, k2, v2, bq, bk, bv,
                    xs, h1_s, h2_s, c1_s, s1_s, c2_s, s2_s):
    tm = x_ref.shape[1]
    n_chunks = D_MODEL // LANES
    for c in range(n_chunks):
        xs[c] = x_ref[0, :, c * LANES:(c + 1) * LANES]
    g = g_ref[...]
    low_half = (lax.broadcasted_iota(jnp.int32, (tm, LANES), 1) % HEAD_DIM) < ROT_HALF

    def mm(h, w_ref, c0, n):
        return jnp.dot(h, w_ref[:, c0:c0 + n], preferred_element_type=F32)

    h0 = _rms(x_ref[0], g).astype(BF16)
    cos0, sin0 = cos_ref[...], sin_ref[...]
    q0[0, 0] = (_rope(mm(h0, wn_ref, 0, A_WIDTH), cos0, sin0, low_half) * Q_SCALE).astype(BF16)
    k0[0, 0] = _rope(mm(h0, wn_ref, A_WIDTH, A_WIDTH), cos0, sin0, low_half).astype(BF16)
    v0[0, 0] = mm(h0, wn_ref, 2 * A_WIDTH, A_WIDTH).astype(BF16)
    c = 3 * A_WIDTH
    bq[0] = (_rope(mm(h0, wn_ref, c, B_Q_WIDTH), cos0, sin0, low_half) * Q_SCALE).astype(BF16)
    bk[0] = _rope(mm(h0, wn_ref, c + B_Q_WIDTH, B_KV_WIDTH), cos0, sin0, low_half).astype(BF16)
    bv[0] = mm(h0, wn_ref, c + B_Q_WIDTH + B_KV_WIDTH, B_KV_WIDTH).astype(BF16)

    for d, hs, cs, ss, w_ref, qo, ko, vo in ((DILATIONS[1], h1_s, c1_s, s1_s, w1_ref, q1, k1, v1),
                                             (DILATIONS[2], h2_s, c2_s, s2_s, w2_ref, q2, k2, v2)):
        n = tm // d
        for r in range(d):
            rows = pl.ds(r, n, stride=d)
            xr = jnp.concatenate([xs[c, rows, :] for c in range(n_chunks)], axis=1)
            hs[r * n:(r + 1) * n, :] = _rms(xr, g).astype(BF16)
            cs[r * n:(r + 1) * n, :] = cos_ref[rows, :]
            ss[r * n:(r + 1) * n, :] = sin_ref[rows, :]
        h = hs[...]
        cos, sin = cs[...], ss[...]
        qv = (_rope(mm(h, w_ref, 0, A_WIDTH), cos, sin, low_half) * Q_SCALE).astype(BF16)
        kv = _rope(mm(h, w_ref, A_WIDTH, A_WIDTH), cos, sin, low_half).astype(BF16)
        vv = mm(h, w_ref, 2 * A_WIDTH, A_WIDTH).astype(BF16)
        for r in range(d):
            qo[0, r] = qv[r * n:(r + 1) * n]
            ko[0, r] = kv[r * n:(r + 1) * n]
            vo[0, r] = vv[r * n:(r + 1) * n]


def _rope_tables(s):
    inv = jnp.power(jnp.float32(ROPE_THETA), -jnp.arange(ROT_HALF, dtype=F32) / ROT_HALF)
    ang = jnp.arange(s).astype(F32)[:, None] * inv[None, :]
    cos, sin = jnp.cos(ang), jnp.sin(ang)
    pad_one = jnp.ones((s, HEAD_DIM - ROT_DIM), F32)
    pad_zero = jnp.zeros((s, HEAD_DIM - ROT_DIM), F32)
    cos_h = jnp.concatenate([cos, cos, pad_one], axis=1)
    sin_h = jnp.concatenate([-sin, sin, pad_zero], axis=1)
    return jnp.concatenate([cos_h, cos_h], axis=1), jnp.concatenate([sin_h, sin_h], axis=1)


def _in_proj(x, norm_g, w_nat, w_g1, w_g2, cos_t, sin_t):
    b, s, _ = x.shape
    tm = IN_TILE
    const = lambda t, bb: (0, 0)
    one_buf = pl.Buffered(1)
    in_specs = [
        pl.BlockSpec((1, tm, D_MODEL), lambda t, bb: (bb, t, 0)),
        pl.BlockSpec((1, D_MODEL), const),
        pl.BlockSpec((tm, LANES), lambda t, bb: (t, 0)),
        pl.BlockSpec((tm, LANES), lambda t, bb: (t, 0)),
        pl.BlockSpec(w_nat.shape, const, pipeline_mode=one_buf),
        pl.BlockSpec(w_g1.shape, const, pipeline_mode=one_buf),
        pl.BlockSpec(w_g2.shape, const, pipeline_mode=one_buf),
    ]
    out_shape, out_specs = [], []
    for d in DILATIONS:
        for _ in range(3):
            out_shape.append(jax.ShapeDtypeStruct((b, d, s // d, A_WIDTH), BF16))
            out_specs.append(pl.BlockSpec((1, d, tm // d, A_WIDTH), lambda t, bb: (bb, 0, t, 0)))
    for w in (B_Q_WIDTH, B_KV_WIDTH, B_KV_WIDTH):
        out_shape.append(jax.ShapeDtypeStruct((b, s, w), BF16))
        out_specs.append(pl.BlockSpec((1, tm, w), lambda t, bb: (bb, t, 0)))
    scratch = [pltpu.VMEM((D_MODEL // LANES, tm, LANES), F32),
               pltpu.VMEM((tm, D_MODEL), BF16), pltpu.VMEM((tm, D_MODEL), BF16)] + \
              [pltpu.VMEM((tm, LANES), F32) for _ in range(4)]
    return pl.pallas_call(
        _in_proj_kernel,
        grid=(s // tm, b),
        in_specs=in_specs,
        out_specs=out_specs,
        out_shape=out_shape,
        scratch_shapes=scratch,
        compiler_params=pltpu.CompilerParams(
            dimension_semantics=("arbitrary", "arbitrary"), vmem_limit_bytes=VMEM_LIMIT),
        name="in_proj",
    )(x, norm_g.reshape(1, D_MODEL), cos_t, sin_t, w_nat, w_g1, w_g2)


def _attend_heads(blocks, valid, first_half):
    second_half = jnp.logical_not(first_half)
    n_heads = blocks[0][0].shape[1] // HEAD_DIM
    scores = []
    for qb, kw, _ in blocks:
        for h in range(n_heads):
            cs = slice((h // 2) * LANES, (h // 2 + 1) * LANES)
            qp = qb[:, cs]
            qm = jnp.where(first_half if h % 2 == 0 else second_half, qp, jnp.zeros_like(qp))
            scores.append(lax.dot_general(qm, kw[:, cs], (((1,), (1,)), ((), ())),
                                          preferred_element_type=F32))
    sc = jnp.where(valid[None], jnp.stack(scores), NEG_BIG)
    m = jnp.max(sc, axis=-1, keepdims=True)
    p = jnp.exp(sc - m)
    l = jnp.sum(p, axis=-1, keepdims=True)
    pb = p.astype(BF16)
    results = []
    for bi, (_, _, vw) in enumerate(blocks):
        out = []
        for hp in range(n_heads // 2):
            i0 = bi * n_heads + 2 * hp
            vp = vw[:, hp * LANES:(hp + 1) * LANES]
            o0 = jnp.dot(pb[i0], vp, preferred_element_type=F32)
            o1 = jnp.dot(pb[i0 + 1], vp, preferred_element_type=F32)
            out.append((jnp.where(first_half, o0, o1), jnp.where(first_half, m[i0], m[i0 + 1]),
                        jnp.where(first_half, l[i0], l[i0 + 1])))
        results.append(out)
    return results


def _mixer_a_kernel(*refs, seq):
    ins, (o_ref, kext, vext, acc_o, acc_m, acc_l) = refs[:21], refs[21:]
    t = pl.program_id(1)
    tq = A_TILE
    halo = A_HALF

    first_group = len(DILATIONS) - 1
    for gi in reversed(range(len(DILATIONS))):
        d = DILATIONS[gi]
        q, kc, kp, kn, vc, vp, vn = ins[7 * gi:7 * gi + 7]
        n = tq // d
        blk = min(n, A_QBLOCK)
        nblk = n // blk
        win = -(-(blk + 2 * halo) // LANES) * LANES
        tail = win - (blk + 2 * halo) if n == blk else 0
        ld = seq // d
        col = lax.broadcasted_iota(jnp.int32, (blk, win), 1)
        diff = col - lax.broadcasted_iota(jnp.int32, (blk, win), 0)
        band = (diff >= 0) & (diff <= 2 * halo)
        first_half = lax.broadcasted_iota(jnp.int32, (blk, LANES), 1) < HEAD_DIM

        pair = A_QBLOCK // blk

        def residue_body(rp, carry, q=q, kc=kc, kp=kp, kn=kn, vc=vc, vp=vp, vn=vn, d=d, n=n, nblk=nblk,
                         ld=ld, blk=blk, win=win, col=col, band=band, first_half=first_half,
                         first=(gi == first_group), pair=pair, tail=tail):
            for u in range(pair):
                r = rp * pair + u
                base = u * win
                kext[base:base + halo, :] = kp[0, r]
                kext[base + halo:base + halo + n, :] = kc[0, r]
                kext[base + halo + n:base + 2 * halo + n, :] = kn[0, r]
                vext[base:base + halo, :] = vp[0, r]
                vext[base + halo:base + halo + n, :] = vc[0, r]
                vext[base + halo + n:base + 2 * halo + n, :] = vn[0, r]
                if tail:
                    kext[base + 2 * halo + n:base + win, :] = jnp.zeros((tail, kext.shape[1]), BF16)
                    vext[base + 2 * halo + n:base + win, :] = jnp.zeros((tail, vext.shape[1]), BF16)

            def block_body(j, carry2):
                j0 = pl.multiple_of(j * blk, blk)
                qbase = t * n + j * blk
                valid = band & (col >= halo - qbase) & (col < ld - qbase + halo)
                blocks = [(q[0, rp * pair + u, pl.ds(j0, blk), :],
                           kext[pl.ds(u * win + j0, win), :],
                           vext[pl.ds(u * win + j0, win), :]) for u in range(pair)]
                results = _attend_heads(blocks, valid, first_half)
                for u, parts in enumerate(results):
                    if d == 1:
                        rows = pl.ds(j0, blk)
                    else:
                        rows = pl.ds(j * (blk * d) + rp * pair + u, blk, stride=d)
                    for hp, (o_b, m_b, l_b) in enumerate(parts):
                        if first:
                            acc_o[hp, rows, :] = o_b
                            acc_l[hp, rows, :] = l_b
                            acc_m[hp, rows, :] = m_b
                        else:
                            m_old = acc_m[hp, rows, :]
                            m_new = jnp.maximum(m_old, m_b)
                            a_old = jnp.exp(m_old - m_new)
                            a_new = jnp.exp(m_b - m_new)
                            acc_o[hp, rows, :] = acc_o[hp, rows, :] * a_old + o_b * a_new
                            acc_l[hp, rows, :] = acc_l[hp, rows, :] * a_old + l_b * a_new
                            acc_m[hp, rows, :] = m_new
                return carry2

            lax.fori_loop(0, nblk, block_body, 0)
            return carry

        lax.fori_loop(0, d // pair, residue_body, 0)

    o_ref[0] = jnp.concatenate([acc_o[hp] / acc_l[hp] for hp in range(A_STEP_WIDTH // LANES)],
                               axis=1).astype(BF16)


def _mixer_a(qkv, seq):
    b = qkv[0].shape[0]
    tq = A_TILE
    blk = A_BLOCK
    aw = A_STEP_WIDTH
    args, in_specs = [], []
    for gi, d in enumerate(DILATIONS):
        q, k, v = qkv[3 * gi:3 * gi + 3]
        n = tq // d
        nblk = n // blk
        last = seq // d // blk - 1
        cur = pl.BlockSpec((1, d, n, aw), lambda bb, t, hh: (bb, 0, t, hh))
        prev = pl.BlockSpec((1, d, blk, aw),
                            lambda bb, t, hh, nblk=nblk: (bb, 0, jnp.maximum(t * nblk - 1, 0), hh))
        nxt = pl.BlockSpec((1, d, blk, aw),
                           lambda bb, t, hh, nblk=nblk, last=last: (bb, 0, jnp.minimum((t + 1) * nblk, last), hh))
        args += [q, k, k, k, v, v, v]
        in_specs += [cur, cur, prev, nxt, cur, prev, nxt]
    return pl.pallas_call(
        functools.partial(_mixer_a_kernel, seq=seq),
        grid=(b, seq // tq, A_WIDTH // aw),
        in_specs=in_specs,
        out_specs=pl.BlockSpec((1, tq, aw), lambda bb, t, hh: (bb, t, hh)),
        out_shape=jax.ShapeDtypeStruct((b, seq, A_WIDTH), BF16),
        scratch_shapes=[pltpu.VMEM((tq + 2 * blk, aw), BF16), pltpu.VMEM((tq + 2 * blk, aw), BF16),
                        pltpu.VMEM((aw // LANES, tq, LANES), F32),
                        pltpu.VMEM((aw // LANES, tq, LANES), F32),
                        pltpu.VMEM((aw // LANES, tq, LANES), F32)],
        compiler_params=pltpu.CompilerParams(
            dimension_semantics=("arbitrary", "arbitrary", "arbitrary"), vmem_limit_bytes=VMEM_LIMIT),
        name="mixer_a",
    )(*args)


def _mixer_b_kernel(sink_ref, q, kc, kp, kn, vc, vp, vn, o_ref, kext, vext, *, seq):
    t = pl.program_id(1)
    tq = ATT_TILE
    blk = B_BLOCK
    win = 3 * blk
    kext[0:blk, :] = kp[0]
    kext[blk:blk + tq, :] = kc[0]
    kext[blk + tq:2 * blk + tq, :] = kn[0]
    vext[0:blk, :] = vp[0]
    vext[blk:blk + tq, :] = vc[0]
    vext[blk + tq:2 * blk + tq, :] = vn[0]
    col = lax.broadcasted_iota(jnp.int32, (blk, win), 1)
    row = lax.broadcasted_iota(jnp.int32, (blk, win), 0)
    diff = col - row
    first_half = lax.broadcasted_iota(jnp.int32, (blk, LANES), 1) < HEAD_DIM
    second_half = jnp.logical_not(first_half)

    def block_body(j, carry):
        j0 = pl.multiple_of(j * blk, blk)
        qb = q[0, pl.ds(j0, blk), :]
        kw = kext[pl.ds(j0, win), :]
        vw = vext[pl.ds(j0, win), :]
        qbase = t * tq + j * blk
        valid = ((diff >= 0) & (diff <= 2 * B_WINDOW)
                 & (col >= blk - qbase) & (col < seq - qbase + blk))
        scores, sinks = [], []
        for c in range(B_Q_HEADS // 2):
            qp = qb[:, c * LANES:(c + 1) * LANES]
            for g, keep in enumerate((first_half, second_half)):
                qm = jnp.where(keep, qp, jnp.zeros_like(qp))
                scores.append(lax.dot_general(qm, kw, (((1,), (1,)), ((), ())), preferred_element_type=F32))
                sinks.append(jnp.full((1, LANES), sink_ref[g * B_REP + c], F32))
        sc = jnp.where(valid[None], jnp.stack(scores), NEG_BIG)
        sk = jnp.stack(sinks)
        part = sc[..., 0:LANES]
        for k in range(1, win // LANES):
            part = jnp.maximum(part, sc[..., k * LANES:(k + 1) * LANES])
        m = jnp.max(jnp.maximum(part, sk), axis=-1, keepdims=True)
        p = jnp.exp(sc - m)
        denom = jnp.sum(p, axis=-1, keepdims=True) + jnp.exp(sk - m)
        pb = p.astype(BF16)
        parts = []
        for c in range(B_Q_HEADS // 2):
            o0 = jnp.dot(pb[2 * c], vw, preferred_element_type=F32) / denom[2 * c]
            o1 = jnp.dot(pb[2 * c + 1], vw, preferred_element_type=F32) / denom[2 * c + 1]
            parts.append(jnp.where(first_half, o0, o1))
        o_ref[0, pl.ds(j0, blk), :] = jnp.concatenate(parts, axis=1).astype(BF16)
        return carry

    lax.fori_loop(0, tq // blk, block_body, 0)


def _mixer_b(q, k, v, sink, seq):
    b = q.shape[0]
    tq = ATT_TILE
    blk = B_BLOCK
    nblk = tq // blk
    last = seq // blk - 1
    cur = pl.BlockSpec((1, tq, B_KV_WIDTH), lambda bb, t: (bb, t, 0))
    prev = pl.BlockSpec((1, blk, B_KV_WIDTH), lambda bb, t: (bb, jnp.maximum(t * nblk - 1, 0), 0))
    nxt = pl.BlockSpec((1, blk, B_KV_WIDTH), lambda bb, t: (bb, jnp.minimum((t + 1) * nblk, last), 0))
    return pl.pallas_call(
        functools.partial(_mixer_b_kernel, seq=seq),
        grid=(b, seq // tq),
        in_specs=[pl.BlockSpec(memory_space=pltpu.SMEM),
                  pl.BlockSpec((1, tq, B_Q_WIDTH), lambda bb, t: (bb, t, 0)),
                  cur, prev, nxt, cur, prev, nxt],
        out_specs=pl.BlockSpec((1, tq, B_Q_WIDTH), lambda bb, t: (bb, t, 0)),
        out_shape=jax.ShapeDtypeStruct((b, seq, B_Q_WIDTH), BF16),
        scratch_shapes=[pltpu.VMEM((tq + 2 * blk, B_KV_WIDTH), BF16),
                        pltpu.VMEM((tq + 2 * blk, B_KV_WIDTH), BF16)],
        compiler_params=pltpu.CompilerParams(
            dimension_semantics=("arbitrary", "arbitrary"), vmem_limit_bytes=VMEM_LIMIT),
        name="mixer_b",
    )(sink, q, k, k, k, v, v, v)


def _post_attn_kernel(x_ref, oa_ref, ob_ref, gmix_ref, wg_ref, wa_ref, wb_ref, wo_ref, gmoe_ref,
                      wr_ref, br_ref, x1_ref, h2_ref, gate_ref):
    wr = wr_ref[...]
    wr_hi = wr.astype(BF16)
    wr_lo = (wr - wr_hi.astype(F32)).astype(BF16)
    wr3 = jnp.concatenate([wr_hi, wr_hi, wr_lo], axis=0)
    for s in range(x_ref.shape[0] // POST_SLAB):
        rows = slice(s * POST_SLAB, (s + 1) * POST_SLAB)
        _post_attn_rows(x_ref, oa_ref, ob_ref, gmix_ref, wg_ref, wa_ref, wb_ref, wo_ref, gmoe_ref, wr3,
                        br_ref, x1_ref, h2_ref, gate_ref, rows)


def _post_attn_rows(x_ref, oa_ref, ob_ref, gmix_ref, wg_ref, wa_ref, wb_ref, wo_ref, gmoe_ref, wr3,
                    br_ref, x1_ref, h2_ref, gate_ref, rows):
    x = x_ref[rows, :]
    h = _rms(x, gmix_ref[...]).astype(BF16)
    gates = jax.nn.sigmoid(jnp.dot(h, wg_ref[...], preferred_element_type=F32))
    ya = jnp.dot(oa_ref[rows, :], wa_ref[...], preferred_element_type=F32)
    yb = jnp.dot(ob_ref[rows, :], wb_ref[...], preferred_element_type=F32)
    mix = gates[:, :D_MODEL] * ya + gates[:, D_MODEL:] * yb
    x1 = x + jnp.dot(mix.astype(BF16), wo_ref[...], preferred_element_type=F32)
    x1_ref[rows, :] = x1
    h2 = _rms(x1, gmoe_ref[...])
    h2_hi = h2.astype(BF16)
    h2_ref[rows, :] = h2_hi

    h2_lo = (h2 - h2_hi.astype(F32)).astype(BF16)
    logits = jnp.dot(jnp.concatenate([h2_hi, h2_lo, h2_hi], axis=1), wr3,
                     preferred_element_type=F32) + br_ref[...]
    tm = logits.shape[0]
    lane = lax.broadcasted_iota(jnp.int32, (tm, LANES), 1)
    big = jnp.int32(LANES)
    is_group = (lane >= N_EXPERTS) & (lane < N_EXPERTS + N_GROUPS)
    gl = jnp.where(is_group, logits, NEG_BIG)
    gmax = jnp.max(gl, axis=-1, keepdims=True)
    g_idx = jnp.min(jnp.where(gl == gmax, lane, big), axis=-1, keepdims=True) - N_EXPERTS
    g_w = 1.0 / jnp.sum(jnp.exp(gl - gmax), axis=-1, keepdims=True)
    in_group = (lane < N_EXPERTS) & (lax.shift_right_logical(lane, 3) == g_idx)
    sel = jnp.where(in_group, logits, NEG_BIG)
    v1 = jnp.max(sel, axis=-1, keepdims=True)
    i1 = jnp.min(jnp.where(sel == v1, lane, big), axis=-1, keepdims=True)
    sel2 = jnp.where(lane == i1, NEG_BIG, sel)
    v2 = jnp.max(sel2, axis=-1, keepdims=True)
    i2 = jnp.min(jnp.where(sel2 == v2, lane, big), axis=-1, keepdims=True)
    e2 = jnp.exp(v2 - v1)
    w1 = g_w / (1.0 + e2)
    w2 = g_w * e2 / (1.0 + e2)
    gate_ref[rows, :] = jnp.where(lane == i1, w1, 0.0) + jnp.where(lane == i2, w2, 0.0)


def _post_attn(x2d, oa, ob, norm_mix, w_gate, w_a, w_b, w_out, norm_moe, w_router, b_router):
    t = x2d.shape[0]
    tm = POST_TILE
    const = lambda i: (0, 0)
    row = lambda w: pl.BlockSpec((tm, w), lambda i: (i, 0))
    full = lambda a: pl.BlockSpec(a.shape, const, pipeline_mode=pl.Buffered(1))
    return pl.pallas_call(
        _post_attn_kernel,
        grid=(t // tm,),
        in_specs=[row(D_MODEL), row(A_WIDTH), row(B_Q_WIDTH), full(norm_mix), full(w_gate), full(w_a),
                  full(w_b), full(w_out), full(norm_moe), full(w_router), full(b_router)],
        out_specs=[row(D_MODEL), row(D_MODEL), row(LANES)],
        out_shape=[jax.ShapeDtypeStruct((t, D_MODEL), F32), jax.ShapeDtypeStruct((t, D_MODEL), BF16),
                   jax.ShapeDtypeStruct((t, LANES), F32)],
        compiler_params=pltpu.CompilerParams(
            dimension_semantics=("arbitrary",), vmem_limit_bytes=VMEM_LIMIT),
        name="post_attn",
    )(x2d, oa, ob, norm_mix, w_gate, w_a, w_b, w_out, norm_moe, w_router, b_router)


def _moe_final_kernel(x1_ref, h2_ref, gate_ref, tri_ref, w1_ref, w3_ref, w2_ref, gf_ref, o_ref,
                      xs, stage, out_stage, seg_s, slo_s, shi_s, wlo_s, whi_s):
    step = pl.program_id(1)
    first_step = (pl.program_id(0) == 0) & (step == 0)
    n_sub = x1_ref.shape[0] // MOE_SUB

    @pl.when(first_step)
    def _init():
        xs[n_sub * MOE_SUB_SLOTS:, :] = jnp.zeros((MOE_CHUNK, D_MODEL), BF16)
        stage[...] = jnp.zeros(stage.shape, BF16)

    @pl.when(step == 0)
    def _sort():
        r_i = lax.broadcasted_iota(jnp.int32, (LANES, LANES), 0)
        c_i = lax.broadcasted_iota(jnp.int32, (LANES, LANES), 1)
        upper = jnp.where(r_i < c_i, 1.0, 0.0)
        col = lax.broadcasted_iota(jnp.int32, (MOE_SUB, MOE_SUB_SLOTS), 1)
        for s in range(n_sub):
            rows = slice(s * MOE_SUB, (s + 1) * MOE_SUB)
            g = gate_ref[rows, :]
            hit = g > 0.0
            onehot = jnp.where(hit, 1.0, 0.0)
            count = jnp.sum(onehot, axis=0, keepdims=True)
            padded = jnp.ceil(count * (1.0 / SEG_ALIGN)) * SEG_ALIGN
            seg = jnp.dot(jnp.broadcast_to(padded, (8, LANES)), upper, preferred_element_type=F32,
                          precision=lax.Precision.HIGHEST)[0:1]
            seg_i = seg.astype(jnp.int32)
            padded_i = padded.astype(jnp.int32)
            for ex in range(N_EXPERTS):
                seg_s[2 * s, ex] = seg_i[0, ex] + s * MOE_SUB_SLOTS
                seg_s[2 * s + 1, ex] = padded_i[0, ex]
            rank = jnp.dot(tri_ref[...], onehot.astype(BF16), preferred_element_type=F32)
            slot = seg + rank
            s_lo = jnp.min(jnp.where(hit, slot, 1e9), axis=-1, keepdims=True)
            s_hi = jnp.max(jnp.where(hit, slot, -1.0), axis=-1, keepdims=True)
            wlo_s[rows, :] = jnp.sum(jnp.where(hit & (slot == s_lo), g, 0.0), axis=-1, keepdims=True)
            whi_s[rows, :] = jnp.sum(jnp.where(hit & (slot == s_hi) & (s_hi > s_lo), g, 0.0),
                                     axis=-1, keepdims=True)
            s_lo_i = s_lo.astype(jnp.int32)
            s_hi_i = s_hi.astype(jnp.int32)
            slo_s[rows, :] = s_lo_i
            shi_s[rows, :] = s_hi_i
            perm_t = jnp.where((col == s_lo_i) | (col == s_hi_i), 1.0, 0.0).astype(BF16)
            sorted_rows = lax.dot_general(perm_t, h2_ref[rows, :], (((0,), (0,)), ((), ())),
                                          preferred_element_type=F32)
            xs[s * MOE_SUB_SLOTS:(s + 1) * MOE_SUB_SLOTS, :] = sorted_rows.astype(BF16)

    def ffn(x, j):
        a = jnp.dot(x, w1_ref[j], preferred_element_type=F32)
        b = jnp.dot(x, w3_ref[j], preferred_element_type=F32)
        hid = (a * jax.nn.sigmoid(a)) * b
        return jnp.dot(hid.astype(BF16), w2_ref[j], preferred_element_type=F32).astype(BF16)

    bases, totals, seg_starts, seg_lens, seg_dsts = [], [], [], [], []
    base = jnp.int32(0)
    for j in range(MOE_ESTEP):
        e = step * MOE_ESTEP + j
        starts = [seg_s[2 * s, e] for s in range(n_sub)]
        lens = [seg_s[2 * s + 1, e] for s in range(n_sub)]
        dst = base
        dsts = []
        for s in range(n_sub):
            dsts.append(dst)
            dst = dst + lens[s]
        bases.append(base)
        totals.append(dst - base)
        seg_starts.append(starts)
        seg_lens.append(lens)
        seg_dsts.append(dsts)
        base = base + ((dst - base + MOE_CHUNK - 1) // MOE_CHUNK) * MOE_CHUNK

    for j in range(MOE_ESTEP):
        for s in range(n_sub):
            def gather_body(k, carry, src0=seg_starts[j][s], dst0=seg_dsts[j][s]):
                src = pl.multiple_of(src0 + k * MOE_COPY, SEG_ALIGN)
                to = pl.multiple_of(dst0 + k * MOE_COPY, SEG_ALIGN)
                stage[pl.ds(to, MOE_COPY), :] = xs[pl.ds(src, MOE_COPY), :]
                return carry

            lax.fori_loop(1, (seg_lens[j][s] + MOE_COPY - 1) // MOE_COPY, gather_body, 0)
    for j in range(MOE_ESTEP):
        for s in range(n_sub):
            stage[pl.ds(pl.multiple_of(seg_dsts[j][s], SEG_ALIGN), MOE_COPY), :] = \
                xs[pl.ds(pl.multiple_of(seg_starts[j][s], SEG_ALIGN), MOE_COPY), :]

    for j in range(MOE_ESTEP):
        def extra_body(k, carry, j=j):
            r0 = pl.multiple_of(bases[j] + k * MOE_CHUNK, MOE_CHUNK)
            out_stage[pl.ds(r0, MOE_CHUNK), :] = ffn(stage[pl.ds(r0, MOE_CHUNK), :], j)
            return carry

        lax.fori_loop(1, (totals[j] + MOE_CHUNK - 1) // MOE_CHUNK, extra_body, 0)

    firsts = [ffn(stage[pl.ds(pl.multiple_of(bases[j], MOE_CHUNK), MOE_CHUNK), :], j) for j in range(MOE_ESTEP)]
    for j in range(MOE_ESTEP):
        out_stage[pl.ds(pl.multiple_of(bases[j], MOE_CHUNK), MOE_CHUNK), :] = firsts[j]

    for j in range(MOE_ESTEP):
        for s in range(n_sub):
            def scatter_body(k, carry, src0=seg_dsts[j][s], dst0=seg_starts[j][s]):
                src = pl.multiple_of(src0 + k * SEG_ALIGN, SEG_ALIGN)
                to = pl.multiple_of(dst0 + k * SEG_ALIGN, SEG_ALIGN)
                xs[pl.ds(to, SEG_ALIGN), :] = out_stage[pl.ds(src, SEG_ALIGN), :]
                return carry

            lax.fori_loop(0, seg_lens[j][s] // SEG_ALIGN, scatter_body, 0)

    @pl.when(step == N_EXPERTS // MOE_ESTEP - 1)
    def _combine():
        col = lax.broadcasted_iota(jnp.int32, (MOE_SUB, MOE_SUB_SLOTS), 1)
        for s in range(n_sub):
            rows = slice(s * MOE_SUB, (s + 1) * MOE_SUB)
            wperm = (jnp.where(col == slo_s[rows, :], wlo_s[rows, :], 0.0)
                     + jnp.where(col == shi_s[rows, :], whi_s[rows, :], 0.0)).astype(BF16)
            moe = jnp.dot(wperm, xs[s * MOE_SUB_SLOTS:(s + 1) * MOE_SUB_SLOTS, :], preferred_element_type=F32)
            o_ref[rows, :] = _rms(x1_ref[rows, :] + moe, gf_ref[...])


def _moe_final(x1, h2, gate, w1, w3, w2, norm_final):
    t = x1.shape[0]
    tm = MOE_TILE
    n_sub = tm // MOE_SUB
    row = lambda w: pl.BlockSpec((tm, w), lambda i, e: (i, 0))
    tri = jnp.tril(jnp.ones((MOE_SUB, MOE_SUB), BF16), -1)
    return pl.pallas_call(
        _moe_final_kernel,
        grid=(t // tm, N_EXPERTS // MOE_ESTEP),
        in_specs=[row(D_MODEL), row(D_MODEL), row(LANES),
                  pl.BlockSpec((MOE_SUB, MOE_SUB), lambda i, e: (0, 0), pipeline_mode=pl.Buffered(1)),
                  pl.BlockSpec((MOE_ESTEP, D_MODEL, D_FF), lambda i, e: (e, 0, 0)),
                  pl.BlockSpec((MOE_ESTEP, D_MODEL, D_FF), lambda i, e: (e, 0, 0)),
                  pl.BlockSpec((MOE_ESTEP, D_FF, D_MODEL), lambda i, e: (e, 0, 0)),
                  pl.BlockSpec((1, D_MODEL), lambda i, e: (0, 0))],
        out_specs=row(D_MODEL),
        out_shape=jax.ShapeDtypeStruct((t, D_MODEL), F32),
        scratch_shapes=[pltpu.VMEM((n_sub * MOE_SUB_SLOTS + MOE_CHUNK, D_MODEL), BF16),
                        pltpu.VMEM((2 * tm + (MOE_ESTEP + 1) * MOE_CHUNK, D_MODEL), BF16),
                        pltpu.VMEM((2 * tm + (MOE_ESTEP + 1) * MOE_CHUNK, D_MODEL), BF16),
                        pltpu.SMEM((2 * n_sub, N_EXPERTS), jnp.int32),
                        pltpu.VMEM((tm, 1), jnp.int32), pltpu.VMEM((tm, 1), jnp.int32),
                        pltpu.VMEM((tm, 1), F32), pltpu.VMEM((tm, 1), F32)],
        compiler_params=pltpu.CompilerParams(
            dimension_semantics=("arbitrary", "arbitrary"), vmem_limit_bytes=VMEM_LIMIT),
        name="moe_final",
    )(x1, h2, gate, tri, w1, w3, w2, norm_final)


def _prep_weights(norm_mix, w_in, attn_sink, w_branch_a, w_branch_b, w_gate, w_out, norm_moe,
                  w_router_group, b_router_group, w_router_expert, b_router_expert, w1, w3, w2, norm_final):
    wi = w_in[0]
    a_w = 3 * A_WIDTH

    def group_cols(gi):
        return jnp.concatenate([wi[:, role * a_w + gi * A_WIDTH: role * a_w + (gi + 1) * A_WIDTH]
                                for role in range(3)], axis=1)

    b_heads = [g * B_REP + c for c in range(B_REP) for g in range(B_KV_HEADS)]
    bq_cols = jnp.concatenate([wi[:, 3 * a_w + h * HEAD_DIM: 3 * a_w + (h + 1) * HEAD_DIM] for h in b_heads],
                              axis=1)
    w_b_rows = jnp.concatenate([w_branch_b[0][h * HEAD_DIM:(h + 1) * HEAD_DIM] for h in b_heads], axis=0)
    w_nat = jnp.concatenate([group_cols(0), bq_cols, wi[:, 3 * a_w + B_Q_WIDTH:]], axis=1).astype(BF16)
    w_g1 = group_cols(1).astype(BF16)
    w_g2 = group_cols(2).astype(BF16)
    pad = LANES - N_EXPERTS - N_GROUPS
    w_router = jnp.concatenate([w_router_expert[0], w_router_group[0],
                                jnp.zeros((D_MODEL, pad), F32)], axis=1)
    b_router = jnp.concatenate([b_router_expert[0].reshape(-1), b_router_group[0],
                                jnp.zeros((pad,), F32)]).reshape(1, LANES)
    return dict(
        norm_mix=norm_mix[0].reshape(1, D_MODEL), w_nat=w_nat, w_g1=w_g1, w_g2=w_g2, sink=attn_sink[0],
        w_a=w_branch_a[0].astype(BF16), w_b=w_b_rows.astype(BF16), w_gate=w_gate[0].astype(BF16),
        w_out=w_out[0].astype(BF16), norm_moe=norm_moe[0].reshape(1, D_MODEL),
        w_router=w_router, b_router=b_router,
        w1=w1[0].astype(BF16), w3=w3[0].astype(BF16), w2=w2[0].astype(BF16),
        norm_final=norm_final.reshape(1, D_MODEL))


def _trunk(x, p):
    b, s, _ = x.shape
    outs = _in_proj(x, p["norm_mix"], p["w_nat"], p["w_g1"], p["w_g2"], p["cos"], p["sin"])
    oa = _mixer_a(outs[:9], s)
    ob = _mixer_b(outs[9], outs[10], outs[11], p["sink"], s)
    t = b * s
    x1, h2, gate = _post_attn(x.reshape(t, D_MODEL), oa.reshape(t, A_WIDTH), ob.reshape(t, B_Q_WIDTH),
                              p["norm_mix"], p["w_gate"], p["w_a"], p["w_b"], p["w_out"], p["norm_moe"],
                              p["w_router"], p["b_router"])
    y = _moe_final(x1, h2, gate, p["w1"], p["w3"], p["w2"], p["norm_final"])
    return y.reshape(b, s, D_MODEL)


def kernel(x_prompt, x_sample, norm_mix, w_in, attn_sink, w_branch_a, w_branch_b, w_gate, w_out, norm_moe,
           w_router_group, b_router_group, w_router_expert, b_router_expert, w1, w3, w2, norm_final):
    p = _prep_weights(norm_mix, w_in, attn_sink, w_branch_a, w_branch_b, w_gate, w_out, norm_moe,
                      w_router_group, b_router_group, w_router_expert, b_router_expert, w1, w3, w2, norm_final)
    p["cos"], p["sin"] = _rope_tables(max(x_prompt.shape[1], x_sample.shape[1]))
    return (_trunk(x_prompt, p), _trunk(x_sample, p))
```

```python
import functools

import jax
import jax.numpy as jnp
from jax import lax
from jax.experimental import pallas as pl
from jax.experimental.pallas import tpu as pltpu

D_MODEL = 1024
HEAD_DIM = 64
ROT_DIM = HEAD_DIM // 4
ROT_HALF = ROT_DIM // 2
ROPE_THETA = 500000.0
DILATIONS = (1, 4, 16)
A_HALF = 64
A_BLOCK = 64
A_QBLOCK = 128
A_HEADS = 8
A_WIDTH = A_HEADS * HEAD_DIM
B_Q_HEADS = 8
B_KV_HEADS = 2
B_REP = B_Q_HEADS // B_KV_HEADS
B_WINDOW = 128
B_BLOCK = 128
B_Q_WIDTH = B_Q_HEADS * HEAD_DIM
B_KV_WIDTH = B_KV_HEADS * HEAD_DIM
N_GROUPS = 4
EXPERTS_PER_GROUP = 8
N_EXPERTS = N_GROUPS * EXPERTS_PER_GROUP
D_FF = 256
RMS_EPS = 1e-6
Q_SCALE = HEAD_DIM ** -0.5

LANES = 128
NEG_BIG = -1e30
VMEM_LIMIT = 58 * 1024 * 1024

IN_TILE = 512
ATT_TILE = 1024
A_TILE = 1024
A_STEP_WIDTH = 512
POST_TILE = 1024
POST_SLAB = 256
MOE_TILE = 1024
SEG_ALIGN = 16
MOE_CHUNK = 128
MOE_COPY = 32
MOE_SUB = 256
MOE_SUB_SLOTS = 2 * MOE_SUB + N_EXPERTS * SEG_ALIGN
MOE_ESTEP = 4

F32 = jnp.float32
BF16 = jnp.bfloat16


def _rms(x, g):
    ms = jnp.mean(x * x, axis=-1, keepdims=True)
    return x * lax.rsqrt(ms + RMS_EPS) * g


def _rope(a, cos, sin, low_half):
    outs = []
    for k in range(a.shape[1] // LANES):
        ak = a[:, k * LANES:(k + 1) * LANES]
        up = pltpu.roll(ak, LANES - ROT_HALF, axis=1)
        dn = pltpu.roll(ak, ROT_HALF, axis=1)
        outs.append(ak * cos + jnp.where(low_half, up, dn) * sin)
    return outs[0] if len(outs) == 1 else jnp.concatenate(outs, axis=1)


def _in_proj_kernel(x_ref, g_ref, cos_ref, sin_ref, wn_ref, w1_ref, w2_ref,
                    q0, k0, v0, q1, k1, v1, q2, k2, v2, bq, bk, bv,
                    xs, h1_s, h2_s, c1_s, s1_s, c2_s, s2_s):
    tm = x_ref.shape[1]
    n_chunks = D_MODEL // LANES
    for c in range(n_chunks):
        xs[c] = x_ref[0, :, c * LANES:(c + 1) * LANES]
    g = g_ref[...]
    low_half = (lax.broadcasted_iota(jnp.int32, (tm, LANES), 1) % HEAD_DIM) < ROT_HALF

    def mm(h, w_ref, c0, n):
        return jnp.dot(h, w_ref[:, c0:c0 + n], preferred_element_type=F32)

    h0 = _rms(x_ref[0], g).astype(BF16)
    cos0, sin0 = cos_ref[...], sin_ref[...]
    q0[0, 0] = (_rope(mm(h0, wn_ref, 0, A_WIDTH), cos0, sin0, low_half) * Q_SCALE).astype(BF16)
    k0[0, 0] = _rope(mm(h0, wn_ref, A_WIDTH, A_WIDTH), cos0, sin0, low_half).astype(BF16)
    v0[0, 0] = mm(h0, wn_ref, 2 * A_WIDTH, A_WIDTH).astype(BF16)
    c = 3 * A_WIDTH
    bq[0] = (_rope(mm(h0, wn_ref, c, B_Q_WIDTH), cos0, sin0, low_half) * Q_SCALE).astype(BF16)
    bk[0] = _rope(mm(h0, wn_ref, c + B_Q_WIDTH, B_KV_WIDTH), cos0, sin0, low_half).astype(BF16)
    bv[0] = mm(h0, wn_ref, c + B_Q_WIDTH + B_KV_WIDTH, B_KV_WIDTH).astype(BF16)

    for d, hs, cs, ss, w_ref, qo, ko, vo in ((DILATIONS[1], h1_s, c1_s, s1_s, w1_ref, q1, k1, v1),
                                             (DILATIONS[2], h2_s, c2_s, s2_s, w2_ref, q2, k2, v2)):
        n = tm // d
        for r in range(d):
            rows = pl.ds(r, n, stride=d)
            xr = jnp.concatenate([xs[c, rows, :] for c in range(n_chunks)], axis=1)
            hs[r * n:(r + 1) * n, :] = _rms(xr, g).astype(BF16)
            cs[r * n:(r + 1) * n, :] = cos_ref[rows, :]
            ss[r * n:(r + 1) * n, :] = sin_ref[rows, :]
        h = hs[...]
        cos, sin = cs[...], ss[...]
        qv = (_rope(mm(h, w_ref, 0, A_WIDTH), cos, sin, low_half) * Q_SCALE).astype(BF16)
        kv = _rope(mm(h, w_ref, A_WIDTH, A_WIDTH), cos, sin, low_half).astype(BF16)
        vv = mm(h, w_ref, 2 * A_WIDTH, A_WIDTH).astype(BF16)
        for r in range(d):
            qo[0, r] = qv[r * n:(r + 1) * n]
            ko[0, r] = kv[r * n:(r + 1) * n]
            vo[0, r] = vv[r * n:(r + 1) * n]


def _rope_tables(s):
    inv = jnp.power(jnp.float32(ROPE_THETA), -jnp.arange(ROT_HALF, dtype=F32) / ROT_HALF)
    ang = jnp.arange(s).astype(F32)[:, None] * inv[None, :]
    cos, sin = jnp.cos(ang), jnp.sin(ang)
    src = jnp.arange(LANES) % HEAD_DIM
    spread = (jnp.arange(ROT_DIM)[:, None] == src[None, :]).astype(F32)
    hi = lax.Precision.HIGHEST
    cos_t = jnp.dot(jnp.concatenate([cos, cos], axis=1), spread, precision=hi) + (src >= ROT_DIM).astype(F32)
    sin_t = jnp.dot(jnp.concatenate([-sin, sin], axis=1), spread, precision=hi)
    return cos_t, sin_t


def _in_proj(x, norm_g, w_nat, w_g1, w_g2, cos_t, sin_t):
    b, s, _ = x.shape
    tm = IN_TILE
    const = lambda t, bb: (0, 0)
    one_buf = pl.Buffered(1)
    in_specs = [
        pl.BlockSpec((1, tm, D_MODEL), lambda t, bb: (bb, t, 0)),
        pl.BlockSpec((1, D_MODEL), const),
        pl.BlockSpec((tm, LANES), lambda t, bb: (t, 0)),
        pl.BlockSpec((tm, LANES), lambda t, bb: (t, 0)),
        pl.BlockSpec(w_nat.shape, const, pipeline_mode=one_buf),
        pl.BlockSpec(w_g1.shape, const, pipeline_mode=one_buf),
        pl.BlockSpec(w_g2.shape, const, pipeline_mode=one_buf),
    ]
    out_shape, out_specs = [], []
    for d in DILATIONS:
        for _ in range(3):
            out_shape.append(jax.ShapeDtypeStruct((b, d, s // d, A_WIDTH), BF16))
            out_specs.append(pl.BlockSpec((1, d, tm // d, A_WIDTH), lambda t, bb: (bb, 0, t, 0)))
    for w in (B_Q_WIDTH, B_KV_WIDTH, B_KV_WIDTH):
        out_shape.append(jax.ShapeDtypeStruct((b, s, w), BF16))
        out_specs.append(pl.BlockSpec((1, tm, w), lambda t, bb: (bb, t, 0)))
    scratch = [pltpu.VMEM((D_MODEL // LANES, tm, LANES), F32),
               pltpu.VMEM((tm, D_MODEL), BF16), pltpu.VMEM((tm, D_MODEL), BF16)] + \
              [pltpu.VMEM((tm, LANES), F32) for _ in range(4)]
    return pl.pallas_call(
        _in_proj_kernel,
        grid=(s // tm, b),
        in_specs=in_specs,
        out_specs=out_specs,
        out_shape=out_shape,
        scratch_shapes=scratch,
        compiler_params=pltpu.CompilerParams(
            dimension_semantics=("arbitrary", "arbitrary"), vmem_limit_bytes=VMEM_LIMIT),
        name="in_proj",
    )(x, norm_g.reshape(1, D_MODEL), cos_t, sin_t, w_nat, w_g1, w_g2)


def _attend_heads(blocks, valid, first_half):
    second_half = jnp.logical_not(first_half)
    n_heads = blocks[0][0].shape[1] // HEAD_DIM
    scores = []
    for qb, kw, _ in blocks:
        for h in range(n_heads):
            cs = slice((h // 2) * LANES, (h // 2 + 1) * LANES)
            qp = qb[:, cs]
            qm = jnp.where(first_half if h % 2 == 0 else second_half, qp, jnp.zeros_like(qp))
            scores.append(lax.dot_general(qm, kw[:, cs], (((1,), (1,)), ((), ())),
                                          preferred_element_type=F32))
    sc = jnp.where(valid[None], jnp.stack(scores), NEG_BIG)
    m = jnp.max(sc, axis=-1, keepdims=True)
    p = jnp.exp(sc - m)
    l = jnp.sum(p, axis=-1, keepdims=True)
    pb = p.astype(BF16)
    results = []
    for bi, (_, _, vw) in enumerate(blocks):
        out = []
        for hp in range(n_heads // 2):
            i0 = bi * n_heads + 2 * hp
            vp = vw[:, hp * LANES:(hp + 1) * LANES]
            o0 = jnp.dot(pb[i0], vp, preferred_element_type=F32)
            o1 = jnp.dot(pb[i0 + 1], vp, preferred_element_type=F32)
            out.append((jnp.where(first_half, o0, o1), jnp.where(first_half, m[i0], m[i0 + 1]),
                        jnp.where(first_half, l[i0], l[i0 + 1])))
        results.append(out)
    return results


def _mixer_a_kernel(*refs, seq):
    ins, (o_ref, kext, vext, acc_o, acc_m, acc_l) = refs[:21], refs[21:]
    t = pl.program_id(1)
    tq = A_TILE
    halo = A_HALF

    first_group = len(DILATIONS) - 1
    for gi in reversed(range(len(DILATIONS))):
        d = DILATIONS[gi]
        q, kc, kp, kn, vc, vp, vn = ins[7 * gi:7 * gi + 7]
        n = tq // d
        blk = min(n, A_QBLOCK)
        nblk = n // blk
        win = -(-(blk + 2 * halo) // LANES) * LANES
        tail = win - (blk + 2 * halo) if n == blk else 0
        ld = seq // d
        col = lax.broadcasted_iota(jnp.int32, (blk, win), 1)
        diff = col - lax.broadcasted_iota(jnp.int32, (blk, win), 0)
        band = (diff >= 0) & (diff <= 2 * halo)
        first_half = lax.broadcasted_iota(jnp.int32, (blk, LANES), 1) < HEAD_DIM

        pair = A_QBLOCK // blk

        def residue_body(rp, carry, q=q, kc=kc, kp=kp, kn=kn, vc=vc, vp=vp, vn=vn, d=d, n=n, nblk=nblk,
                         ld=ld, blk=blk, win=win, col=col, band=band, first_half=first_half,
                         first=(gi == first_group), pair=pair, tail=tail):
            for u in range(pair):
                r = rp * pair + u
                base = u * win
                kext[base:base + halo, :] = kp[0, r]
                kext[base + halo:base + halo + n, :] = kc[0, r]
                kext[base + halo + n:base + 2 * halo + n, :] = kn[0, r]
                vext[base:base + halo, :] = vp[0, r]
                vext[base + halo:base + halo + n, :] = vc[0, r]
                vext[base + halo + n:base + 2 * halo + n, :] = vn[0, r]
                if tail:
                    kext[base + 2 * halo + n:base + win, :] = jnp.zeros((tail, kext.shape[1]), BF16)
                    vext[base + 2 * halo + n:base + win, :] = jnp.zeros((tail, vext.shape[1]), BF16)

            def block_body(j, carry2):
                j0 = pl.multiple_of(j * blk, blk)
                qbase = t * n + j * blk
                valid = band & (col >= halo - qbase) & (col < ld - qbase + halo)
                blocks = [(q[0, rp * pair + u, pl.ds(j0, blk), :],
                           kext[pl.ds(u * win + j0, win), :],
                           vext[pl.ds(u * win + j0, win), :]) for u in range(pair)]
                results = _attend_heads(blocks, valid, first_half)
                for u, parts in enumerate(results):
                    if d == 1:
                        rows = pl.ds(j0, blk)
                    else:
                        rows = pl.ds(j * (blk * d) + rp * pair + u, blk, stride=d)
                    for hp, (o_b, m_b, l_b) in enumerate(parts):
                        if first:
                            acc_o[hp, rows, :] = o_b
                            acc_l[hp, rows, :] = l_b
                            acc_m[hp, rows, :] = m_b
                        else:
                            m_old = acc_m[hp, rows, :]
                            m_new = jnp.maximum(m_old, m_b)
                            a_old = jnp.exp(m_old - m_new)
                            a_new = jnp.exp(m_b - m_new)
                            acc_o[hp, rows, :] = acc_o[hp, rows, :] * a_old + o_b * a_new
                            acc_l[hp, rows, :] = acc_l[hp, rows, :] * a_old + l_b * a_new
                            acc_m[hp, rows, :] = m_new
                return carry2

            lax.fori_loop(0, nblk, block_body, 0)
            return carry

        lax.fori_loop(0, d // pair, residue_body, 0)

    o_ref[0] = jnp.concatenate([acc_o[hp] / acc_l[hp] for hp in range(A_STEP_WIDTH // LANES)],
                               axis=1).astype(BF16)


def _mixer_a(qkv, seq):
    b = qkv[0].shape[0]
    tq = A_TILE
    blk = A_BLOCK
    aw = A_STEP_WIDTH
    args, in_specs = [], []
    for gi, d in enumerate(DILATIONS):
        q, k, v = qkv[3 * gi:3 * gi + 3]
        n = tq // d
        nblk = n // blk
        last = seq // d // blk - 1
        cur = pl.BlockSpec((1, d, n, aw), lambda bb, t, hh: (bb, 0, t, hh))
        prev = pl.BlockSpec((1, d, blk, aw),
                            lambda bb, t, hh, nblk=nblk: (bb, 0, jnp.maximum(t * nblk - 1, 0), hh))
        nxt = pl.BlockSpec((1, d, blk, aw),
                           lambda bb, t, hh, nblk=nblk, last=last: (bb, 0, jnp.minimum((t + 1) * nblk, last), hh))
        args += [q, k, k, k, v, v, v]
        in_specs += [cur, cur, prev, nxt, cur, prev, nxt]
    return pl.pallas_call(
        functools.partial(_mixer_a_kernel, seq=seq),
        grid=(b, seq // tq, A_WIDTH // aw),
        in_specs=in_specs,
        out_specs=pl.BlockSpec((1, tq, aw), lambda bb, t, hh: (bb, t, hh)),
        out_shape=jax.ShapeDtypeStruct((b, seq, A_WIDTH), BF16),
        scratch_shapes=[pltpu.VMEM((tq + 2 * blk, aw), BF16), pltpu.VMEM((tq + 2 * blk, aw), BF16),
                        pltpu.VMEM((aw // LANES, tq, LANES), F32),
                        pltpu.VMEM((aw // LANES, tq, LANES), F32),
                        pltpu.VMEM((aw // LANES, tq, LANES), F32)],
        compiler_params=pltpu.CompilerParams(
            dimension_semantics=("arbitrary", "arbitrary", "arbitrary"), vmem_limit_bytes=VMEM_LIMIT),
        name="mixer_a",
    )(*args)


def _mixer_b_kernel(sink_ref, q, kc, kp, kn, vc, vp, vn, o_ref, kext, vext, *, seq):
    t = pl.program_id(1)
    tq = ATT_TILE
    blk = B_BLOCK
    win = 3 * blk
    kext[0:blk, :] = kp[0]
    kext[blk:blk + tq, :] = kc[0]
    kext[blk + tq:2 * blk + tq, :] = kn[0]
    vext[0:blk, :] = vp[0]
    vext[blk:blk + tq, :] = vc[0]
    vext[blk + tq:2 * blk + tq, :] = vn[0]
    col = lax.broadcasted_iota(jnp.int32, (blk, win), 1)
    row = lax.broadcasted_iota(jnp.int32, (blk, win), 0)
    diff = col - row
    first_half = lax.broadcasted_iota(jnp.int32, (blk, LANES), 1) < HEAD_DIM
    second_half = jnp.logical_not(first_half)

    def block_body(j, carry):
        j0 = pl.multiple_of(j * blk, blk)
        qb = q[0, pl.ds(j0, blk), :]
        kw = kext[pl.ds(j0, win), :]
        vw = vext[pl.ds(j0, win), :]
        qbase = t * tq + j * blk
        valid = ((diff >= 0) & (diff <= 2 * B_WINDOW)
                 & (col >= blk - qbase) & (col < seq - qbase + blk))
        scores, sinks = [], []
        for c in range(B_Q_HEADS // 2):
            qp = qb[:, c * LANES:(c + 1) * LANES]
            for g, keep in enumerate((first_half, second_half)):
                qm = jnp.where(keep, qp, jnp.zeros_like(qp))
                scores.append(lax.dot_general(qm, kw, (((1,), (1,)), ((), ())), preferred_element_type=F32))
                sinks.append(jnp.full((1, LANES), sink_ref[g * B_REP + c], F32))
        sc = jnp.where(valid[None], jnp.stack(scores), NEG_BIG)
        sk = jnp.stack(sinks)
        part = sc[..., 0:LANES]
        for k in range(1, win // LANES):
            part = jnp.maximum(part, sc[..., k * LANES:(k + 1) * LANES])
        m = jnp.max(jnp.maximum(part, sk), axis=-1, keepdims=True)
        p = jnp.exp(sc - m)
        denom = jnp.sum(p, axis=-1, keepdims=True) + jnp.exp(sk - m)
        pb = p.astype(BF16)
        parts = []
        for c in range(B_Q_HEADS // 2):
            o0 = jnp.dot(pb[2 * c], vw, preferred_element_type=F32) / denom[2 * c]
            o1 = jnp.dot(pb[2 * c + 1], vw, preferred_element_type=F32) / denom[2 * c + 1]
            parts.append(jnp.where(first_half, o0, o1))
        o_ref[0, pl.ds(j0, blk), :] = jnp.concatenate(parts, axis=1).astype(BF16)
        return carry

    lax.fori_loop(0, tq // blk, block_body, 0)


def _mixer_b(q, k, v, sink, seq):
    b = q.shape[0]
    tq = ATT_TILE
    blk = B_BLOCK
    nblk = tq // blk
    last = seq // blk - 1
    cur = pl.BlockSpec((1, tq, B_KV_WIDTH), lambda bb, t: (bb, t, 0))
    prev = pl.BlockSpec((1, blk, B_KV_WIDTH), lambda bb, t: (bb, jnp.maximum(t * nblk - 1, 0), 0))
    nxt = pl.BlockSpec((1, blk, B_KV_WIDTH), lambda bb, t: (bb, jnp.minimum((t + 1) * nblk, last), 0))
    return pl.pallas_call(
        functools.partial(_mixer_b_kernel, seq=seq),
        grid=(b, seq // tq),
        in_specs=[pl.BlockSpec(memory_space=pltpu.SMEM),
                  pl.BlockSpec((1, tq, B_Q_WIDTH), lambda bb, t: (bb, t, 0)),
                  cur, prev, nxt, cur, prev, nxt],
        out_specs=pl.BlockSpec((1, tq, B_Q_WIDTH), lambda bb, t: (bb, t, 0)),
        out_shape=jax.ShapeDtypeStruct((b, seq, B_Q_WIDTH), BF16),
        scratch_shapes=[pltpu.VMEM((tq + 2 * blk, B_KV_WIDTH), BF16),
                        pltpu.VMEM((tq + 2 * blk, B_KV_WIDTH), BF16)],
        compiler_params=pltpu.CompilerParams(
            dimension_semantics=("arbitrary", "arbitrary"), vmem_limit_bytes=VMEM_LIMIT),
        name="mixer_b",
    )(sink, q, k, k, k, v, v, v)


def _post_attn_kernel(x_ref, oa_ref, ob_ref, gmix_ref, wg_ref, wa_ref, wb_ref, wo_ref, gmoe_ref,
                      wr_ref, br_ref, x1_ref, h2_ref, gate_ref):
    wr = wr_ref[...]
    wr_hi = wr.astype(BF16)
    wr_lo = (wr - wr_hi.astype(F32)).astype(BF16)
    wr3 = jnp.concatenate([wr_hi, wr_hi, wr_lo], axis=0)
    for s in range(x_ref.shape[0] // POST_SLAB):
        rows = slice(s * POST_SLAB, (s + 1) * POST_SLAB)
        _post_attn_rows(x_ref, oa_ref, ob_ref, gmix_ref, wg_ref, wa_ref, wb_ref, wo_ref, gmoe_ref, wr3,
                        br_ref, x1_ref, h2_ref, gate_ref, rows)


def _post_attn_rows(x_ref, oa_ref, ob_ref, gmix_ref, wg_ref, wa_ref, wb_ref, wo_ref, gmoe_ref, wr3,
                    br_ref, x1_ref, h2_ref, gate_ref, rows):
    x = x_ref[rows, :]
    h = _rms(x, gmix_ref[...]).astype(BF16)
    gates = jax.nn.sigmoid(jnp.dot(h, wg_ref[...], preferred_element_type=F32))
    ya = jnp.dot(oa_ref[rows, :], wa_ref[...], preferred_element_type=F32)
    yb = jnp.dot(ob_ref[rows, :], wb_ref[...], preferred_element_type=F32)
    mix = gates[:, :D_MODEL] * ya + gates[:, D_MODEL:] * yb
    x1 = x + jnp.dot(mix.astype(BF16), wo_ref[...], preferred_element_type=F32)
    x1_ref[rows, :] = x1
    h2 = _rms(x1, gmoe_ref[...])
    h2_hi = h2.astype(BF16)
    h2_ref[rows, :] = h2_hi

    h2_lo = (h2 - h2_hi.astype(F32)).astype(BF16)
    logits = jnp.dot(jnp.concatenate([h2_hi, h2_lo, h2_hi], axis=1), wr3,
                     preferred_element_type=F32) + br_ref[...]
    tm = logits.shape[0]
    lane = lax.broadcasted_iota(jnp.int32, (tm, LANES), 1)
    big = jnp.int32(LANES)
    is_group = (lane >= N_EXPERTS) & (lane < N_EXPERTS + N_GROUPS)
    gl = jnp.where(is_group, logits, NEG_BIG)
    gmax = jnp.max(gl, axis=-1, keepdims=True)
    g_idx = jnp.min(jnp.where(gl == gmax, lane, big), axis=-1, keepdims=True) - N_EXPERTS
    g_w = 1.0 / jnp.sum(jnp.exp(gl - gmax), axis=-1, keepdims=True)
    in_group = (lane < N_EXPERTS) & (lax.shift_right_logical(lane, 3) == g_idx)
    sel = jnp.where(in_group, logits, NEG_BIG)
    v1 = jnp.max(sel, axis=-1, keepdims=True)
    i1 = jnp.min(jnp.where(sel == v1, lane, big), axis=-1, keepdims=True)
    sel2 = jnp.where(lane == i1, NEG_BIG, sel)
    v2 = jnp.max(sel2, axis=-1, keepdims=True)
    i2 = jnp.min(jnp.where(sel2 == v2, lane, big), axis=-1, keepdims=True)
    e2 = jnp.exp(v2 - v1)
    w1 = g_w / (1.0 + e2)
    w2 = g_w * e2 / (1.0 + e2)
    gate_ref[rows, :] = jnp.where(lane == i1, w1, 0.0) + jnp.where(lane == i2, w2, 0.0)


def _post_attn(x2d, oa, ob, norm_mix, w_gate, w_a, w_b, w_out, norm_moe, w_router, b_router):
    t = x2d.shape[0]
    tm = POST_TILE
    const = lambda i: (0, 0)
    row = lambda w: pl.BlockSpec((tm, w), lambda i: (i, 0))
    full = lambda a: pl.BlockSpec(a.shape, const, pipeline_mode=pl.Buffered(1))
    return pl.pallas_call(
        _post_attn_kernel,
        grid=(t // tm,),
        in_specs=[row(D_MODEL), row(A_WIDTH), row(B_Q_WIDTH), full(norm_mix), full(w_gate), full(w_a),
                  full(w_b), full(w_out), full(norm_moe), full(w_router), full(b_router)],
        out_specs=[row(D_MODEL), row(D_MODEL), row(LANES)],
        out_shape=[jax.ShapeDtypeStruct((t, D_MODEL), F32), jax.ShapeDtypeStruct((t, D_MODEL), BF16),
                   jax.ShapeDtypeStruct((t, LANES), F32)],
        compiler_params=pltpu.CompilerParams(
            dimension_semantics=("arbitrary",), vmem_limit_bytes=VMEM_LIMIT),
        name="post_attn",
    )(x2d, oa, ob, norm_mix, w_gate, w_a, w_b, w_out, norm_moe, w_router, b_router)


def _moe_final_kernel(x1_ref, h2_ref, gate_ref, tri_ref, w1_ref, w3_ref, w2_ref, gf_ref, o_ref,
                      xs, stage, out_stage, seg_s, slo_s, shi_s, wlo_s, whi_s):
    step = pl.program_id(1)
    first_step = (pl.program_id(0) == 0) & (step == 0)
    n_sub = x1_ref.shape[0] // MOE_SUB

    @pl.when(first_step)
    def _init():
        xs[n_sub * MOE_SUB_SLOTS:, :] = jnp.zeros((MOE_CHUNK, D_MODEL), BF16)
        stage[...] = jnp.zeros(stage.shape, BF16)

    @pl.when(step == 0)
    def _sort():
        r_i = lax.broadcasted_iota(jnp.int32, (LANES, LANES), 0)
        c_i = lax.broadcasted_iota(jnp.int32, (LANES, LANES), 1)
        upper = jnp.where(r_i < c_i, 1.0, 0.0)
        col = lax.broadcasted_iota(jnp.int32, (MOE_SUB, MOE_SUB_SLOTS), 1)
        for s in range(n_sub):
            rows = slice(s * MOE_SUB, (s + 1) * MOE_SUB)
            g = gate_ref[rows, :]
            hit = g > 0.0
            onehot = jnp.where(hit, 1.0, 0.0)
            count = jnp.sum(onehot, axis=0, keepdims=True)
            padded = jnp.ceil(count * (1.0 / SEG_ALIGN)) * SEG_ALIGN
            seg = jnp.dot(jnp.broadcast_to(padded, (8, LANES)), upper, preferred_element_type=F32,
                          precision=lax.Precision.HIGHEST)[0:1]
            seg_i = seg.astype(jnp.int32)
            padded_i = padded.astype(jnp.int32)
            for ex in range(N_EXPERTS):
                seg_s[2 * s, ex] = seg_i[0, ex] + s * MOE_SUB_SLOTS
                seg_s[2 * s + 1, ex] = padded_i[0, ex]
            rank = jnp.dot(tri_ref[...], onehot.astype(BF16), preferred_element_type=F32)
            slot = seg + rank
            s_lo = jnp.min(jnp.where(hit, slot, 1e9), axis=-1, keepdims=True)
            s_hi = jnp.max(jnp.where(hit, slot, -1.0), axis=-1, keepdims=True)
            wlo_s[rows, :] = jnp.sum(jnp.where(hit & (slot == s_lo), g, 0.0), axis=-1, keepdims=True)
            whi_s[rows, :] = jnp.sum(jnp.where(hit & (slot == s_hi) & (s_hi > s_lo), g, 0.0),
                                     axis=-1, keepdims=True)
            s_lo_i = s_lo.astype(jnp.int32)
            s_hi_i = s_hi.astype(jnp.int32)
            slo_s[rows, :] = s_lo_i
            shi_s[rows, :] = s_hi_i
            perm_t = jnp.where((col == s_lo_i) | (col == s_hi_i), 1.0, 0.0).astype(BF16)
            sorted_rows = lax.dot_general(perm_t, h2_ref[rows, :], (((0,), (0,)), ((), ())),
                                          preferred_element_type=F32)
            xs[s * MOE_SUB_SLOTS:(s + 1) * MOE_SUB_SLOTS, :] = sorted_rows.astype(BF16)

    def ffn(x, j):
        a = jnp.dot(x, w1_ref[j], preferred_element_type=F32)
        b = jnp.dot(x, w3_ref[j], preferred_element_type=F32)
        hid = (a * jax.nn.sigmoid(a)) * b
        return jnp.dot(hid.astype(BF16), w2_ref[j], preferred_element_type=F32).astype(BF16)

    bases, totals, seg_starts, seg_lens, seg_dsts = [], [], [], [], []
    base = jnp.int32(0)
    for j in range(MOE_ESTEP):
        e = step * MOE_ESTEP + j
        starts = [seg_s[2 * s, e] for s in range(n_sub)]
        lens = [seg_s[2 * s + 1, e] for s in range(n_sub)]
        dst = base
        dsts = []
        for s in range(n_sub):
            dsts.append(dst)
            dst = dst + lens[s]
        bases.append(base)
        totals.append(dst - base)
        seg_starts.append(starts)
        seg_lens.append(lens)
        seg_dsts.append(dsts)
        base = base + ((dst - base + MOE_CHUNK - 1) // MOE_CHUNK) * MOE_CHUNK

    for j in range(MOE_ESTEP):
        for s in range(n_sub):
            def gather_body(k, carry, src0=seg_starts[j][s], dst0=seg_dsts[j][s]):
                src = pl.multiple_of(src0 + k * MOE_COPY, SEG_ALIGN)
                to = pl.multiple_of(dst0 + k * MOE_COPY, SEG_ALIGN)
                stage[pl.ds(to, MOE_COPY), :] = xs[pl.ds(src, MOE_COPY), :]
                return carry

            lax.fori_loop(1, (seg_lens[j][s] + MOE_COPY - 1) // MOE_COPY, gather_body, 0)
    for j in range(MOE_ESTEP):
        for s in range(n_sub):
            stage[pl.ds(pl.multiple_of(seg_dsts[j][s], SEG_ALIGN), MOE_COPY), :] = \
                xs[pl.ds(pl.multiple_of(seg_starts[j][s], SEG_ALIGN), MOE_COPY), :]

    for j in range(MOE_ESTEP):
        def extra_body(k, carry, j=j):
            r0 = pl.multiple_of(bases[j] + k * MOE_CHUNK, MOE_CHUNK)
            out_stage[pl.ds(r0, MOE_CHUNK), :] = ffn(stage[pl.ds(r0, MOE_CHUNK), :], j)
            return carry

        lax.fori_loop(1, (totals[j] + MOE_CHUNK - 1) // MOE_CHUNK, extra_body, 0)

    firsts = [ffn(stage[pl.ds(pl.multiple_of(bases[j], MOE_CHUNK), MOE_CHUNK), :], j) for j in range(MOE_ESTEP)]
    for j in range(MOE_ESTEP):
        out_stage[pl.ds(pl.multiple_of(bases[j], MOE_CHUNK), MOE_CHUNK), :] = firsts[j]

    for j in range(MOE_ESTEP):
        for s in range(n_sub):
            def scatter_body(k, carry, src0=seg_dsts[j][s], dst0=seg_starts[j][s]):
                src = pl.multiple_of(src0 + k * SEG_ALIGN, SEG_ALIGN)
                to = pl.multiple_of(dst0 + k * SEG_ALIGN, SEG_ALIGN)
                xs[pl.ds(to, SEG_ALIGN), :] = out_stage[pl.ds(src, SEG_ALIGN), :]
                return carry

            lax.fori_loop(0, seg_lens[j][s] // SEG_ALIGN, scatter_body, 0)

    @pl.when(step == N_EXPERTS // MOE_ESTEP - 1)
    def _combine():
        col = lax.broadcasted_iota(jnp.int32, (MOE_SUB, MOE_SUB_SLOTS), 1)
        for s in range(n_sub):
            rows = slice(s * MOE_SUB, (s + 1) * MOE_SUB)
            wperm = (jnp.where(col == slo_s[rows, :], wlo_s[rows, :], 0.0)
                     + jnp.where(col == shi_s[rows, :], whi_s[rows, :], 0.0)).astype(BF16)
            moe = jnp.dot(wperm, xs[s * MOE_SUB_SLOTS:(s + 1) * MOE_SUB_SLOTS, :], preferred_element_type=F32)
            o_ref[rows, :] = _rms(x1_ref[rows, :] + moe, gf_ref[...])


def _moe_final(x1, h2, gate, w1, w3, w2, norm_final):
    t = x1.shape[0]
    tm = MOE_TILE
    n_sub = tm // MOE_SUB
    row = lambda w: pl.BlockSpec((tm, w), lambda i, e: (i, 0))
    tri = jnp.tril(jnp.ones((MOE_SUB, MOE_SUB), BF16), -1)
    return pl.pallas_call(
        _moe_final_kernel,
        grid=(t // tm, N_EXPERTS // MOE_ESTEP),
        in_specs=[row(D_MODEL), row(D_MODEL), row(LANES),
                  pl.BlockSpec((MOE_SUB, MOE_SUB), lambda i, e: (0, 0), pipeline_mode=pl.Buffered(1)),
                  pl.BlockSpec((MOE_ESTEP, D_MODEL, D_FF), lambda i, e: (e, 0, 0)),
                  pl.BlockSpec((MOE_ESTEP, D_MODEL, D_FF), lambda i, e: (e, 0, 0)),
                  pl.BlockSpec((MOE_ESTEP, D_FF, D_MODEL), lambda i, e: (e, 0, 0)),
                  pl.BlockSpec((1, D_MODEL), lambda i, e: (0, 0))],
        out_specs=row(D_MODEL),
        out_shape=jax.ShapeDtypeStruct((t, D_MODEL), F32),
        scratch_shapes=[pltpu.VMEM((n_sub * MOE_SUB_SLOTS + MOE_CHUNK, D_MODEL), BF16),
                        pltpu.VMEM((2 * tm + (MOE_ESTEP + 1) * MOE_CHUNK, D_MODEL), BF16),
                        pltpu.VMEM((2 * tm + (MOE_ESTEP + 1) * MOE_CHUNK, D_MODEL), BF16),
                        pltpu.SMEM((2 * n_sub, N_EXPERTS), jnp.int32),
                        pltpu.VMEM((tm, 1), jnp.int32), pltpu.VMEM((tm, 1), jnp.int32),
                        pltpu.VMEM((tm, 1), F32), pltpu.VMEM((tm, 1), F32)],
        compiler_params=pltpu.CompilerParams(
            dimension_semantics=("arbitrary", "arbitrary"), vmem_limit_bytes=VMEM_LIMIT),
        name="moe_final",
    )(x1, h2, gate, tri, w1, w3, w2, norm_final)


def _prep_weights(norm_mix, w_in, attn_sink, w_branch_a, w_branch_b, w_gate, w_out, norm_moe,
                  w_router_group, b_router_group, w_router_expert, b_router_expert, w1, w3, w2, norm_final):
    wi = w_in[0]
    a_w = 3 * A_WIDTH

    def group_cols(gi):
        return jnp.concatenate([wi[:, role * a_w + gi * A_WIDTH: role * a_w + (gi + 1) * A_WIDTH]
                                for role in range(3)], axis=1)

    b_heads = [g * B_REP + c for c in range(B_REP) for g in range(B_KV_HEADS)]
    bq_cols = jnp.concatenate([wi[:, 3 * a_w + h * HEAD_DIM: 3 * a_w + (h + 1) * HEAD_DIM] for h in b_heads],
                              axis=1)
    w_b_rows = jnp.concatenate([w_branch_b[0][h * HEAD_DIM:(h + 1) * HEAD_DIM] for h in b_heads], axis=0)
    w_nat = jnp.concatenate([group_cols(0), bq_cols, wi[:, 3 * a_w + B_Q_WIDTH:]], axis=1).astype(BF16)
    w_g1 = group_cols(1).astype(BF16)
    w_g2 = group_cols(2).astype(BF16)
    pad = LANES - N_EXPERTS - N_GROUPS
    w_router = jnp.concatenate([w_router_expert[0], w_router_group[0],
                                jnp.zeros((D_MODEL, pad), F32)], axis=1)
    b_router = jnp.concatenate([b_router_expert[0].reshape(-1), b_router_group[0],
                                jnp.zeros((pad,), F32)]).reshape(1, LANES)
    return dict(
        norm_mix=norm_mix[0].reshape(1, D_MODEL), w_nat=w_nat, w_g1=w_g1, w_g2=w_g2, sink=attn_sink[0],
        w_a=w_branch_a[0].astype(BF16), w_b=w_b_rows.astype(BF16), w_gate=w_gate[0].astype(BF16),
        w_out=w_out[0].astype(BF16), norm_moe=norm_moe[0].reshape(1, D_MODEL),
        w_router=w_router, b_router=b_router,
        w1=w1[0].astype(BF16), w3=w3[0].astype(BF16), w2=w2[0].astype(BF16),
        norm_final=norm_final.reshape(1, D_MODEL))


def _trunk(x, p):
    b, s, _ = x.shape
    outs = _in_proj(x, p["norm_mix"], p["w_nat"], p["w_g1"], p["w_g2"], p["cos"], p["sin"])
    oa = _mixer_a(outs[:9], s)
    ob = _mixer_b(outs[9], outs[10], outs[11], p["sink"], s)
    t = b * s
    x1, h2, gate = _post_attn(x.reshape(t, D_MODEL), oa.reshape(t, A_WIDTH), ob.reshape(t, B_Q_WIDTH),
                              p["norm_mix"], p["w_gate"], p["w_a"], p["w_b"], p["w_out"], p["norm_moe"],
                              p["w_router"], p["b_router"])
    y = _moe_final(x1, h2, gate, p["w1"], p["w3"], p["w2"], p["norm_final"])
    return y.reshape(b, s, D_MODEL)


def kernel(x_prompt, x_sample, norm_mix, w_in, attn_sink, w_branch_a, w_branch_b, w_gate, w_out, norm_moe,
           w_router_group, b_router_group, w_router_expert, b_router_expert, w1, w3, w2, norm_final):
    p = _prep_weights(norm_mix, w_in, attn_sink, w_branch_a, w_branch_b, w_gate, w_out, norm_moe,
                      w_router_group, b_router_group, w_router_expert, b_router_expert, w1, w3, w2, norm_final)
    p["cos"], p["sin"] = _rope_tables(max(x_prompt.shape[1], x_sample.shape[1]))
    return (_trunk(x_prompt, p), _trunk(x_sample, p))
```

```python
import functools

import jax
import jax.numpy as jnp
from jax import lax
from jax.experimental import pallas as pl
from jax.experimental.pallas import tpu as pltpu

D_MODEL = 1024
HEAD_DIM = 64
ROT_DIM = HEAD_DIM // 4
ROT_HALF = ROT_DIM // 2
ROPE_THETA = 500000.0
DILATIONS = (1, 4, 16)
A_HALF = 64
A_BLOCK = 64
A_QBLOCK = 128
A_HEADS = 8
A_WIDTH = A_HEADS * HEAD_DIM
B_Q_HEADS = 8
B_KV_HEADS = 2
B_REP = B_Q_HEADS // B_KV_HEADS
B_WINDOW = 128
B_BLOCK = 128
B_Q_WIDTH = B_Q_HEADS * HEAD_DIM
B_KV_WIDTH = B_KV_HEADS * HEAD_DIM
N_GROUPS = 4
EXPERTS_PER_GROUP = 8
N_EXPERTS = N_GROUPS * EXPERTS_PER_GROUP
D_FF = 256
RMS_EPS = 1e-6
Q_SCALE = HEAD_DIM ** -0.5

LANES = 128
NEG_BIG = -1e30
VMEM_LIMIT = 58 * 1024 * 1024

IN_TILE = 512
ATT_TILE = 1024
A_TILE = 1024
A_STEP_WIDTH = 512
POST_TILE = 1024
POST_SLAB = 256
MOE_TILE = 1024
SEG_ALIGN = 16
MOE_CHUNK = 128
MOE_COPY = 32
MOE_SUB = 256
MOE_SUB_SLOTS = 2 * MOE_SUB + N_EXPERTS * SEG_ALIGN
MOE_ESTEP = 4

F32 = jnp.float32
BF16 = jnp.bfloat16


def _rms(x, g):
    ms = jnp.mean(x * x, axis=-1, keepdims=True)
    return x * lax.rsqrt(ms + RMS_EPS) * g


def _rope(a, cos, sin, low_half):
    outs = []
    for k in range(a.shape[1] // LANES):
        ak = a[:, k * LANES:(k + 1) * LANES]
        up = pltpu.roll(ak, LANES - ROT_HALF, axis=1)
        dn = pltpu.roll(ak, ROT_HALF, axis=1)
        outs.append(ak * cos + jnp.where(low_half, up, dn) * sin)
    return outs[0] if len(outs) == 1 else jnp.concatenate(outs, axis=1)


def _in_proj_kernel(x_ref, g_ref, cos_ref, sin_ref, wn_ref, w1_ref, w2_ref,
                    q0, k0, v0, q1, k1, v1, q2, k2, v2, bq, bk, bv,
                    xs, h1_s, h2_s, c1_s, s1_s, c2_s, s2_s):
    tm = x_ref.shape[1]
    n_chunks = D_MODEL // LANES
    for c in range(n_chunks):
        xs[c] = x_ref[0, :, c * LANES:(c + 1) * LANES]
    g = g_ref[...]
    low_half = (lax.broadcasted_iota(jnp.int32, (tm, LANES), 1) % HEAD_DIM) < ROT_HALF

    def mm(h, w_ref, c0, n):
        return jnp.dot(h, w_ref[:, c0:c0 + n], preferred_element_type=F32)

    h0 = _rms(x_ref[0], g).astype(BF16)
    cos0, sin0 = cos_ref[...], sin_ref[...]
    q0[0, 0] = (_rope(mm(h0, wn_ref, 0, A_WIDTH), cos0, sin0, low_half) * Q_SCALE).astype(BF16)
    k0[0, 0] = _rope(mm(h0, wn_ref, A_WIDTH, A_WIDTH), cos0, sin0, low_half).astype(BF16)
    v0[0, 0] = mm(h0, wn_ref, 2 * A_WIDTH, A_WIDTH).astype(BF16)
    c = 3 * A_WIDTH
    bq[0] = (_rope(mm(h0, wn_ref, c, B_Q_WIDTH), cos0, sin0, low_half) * Q_SCALE).astype(BF16)
    bk[0] = _rope(mm(h0, wn_ref, c + B_Q_WIDTH, B_KV_WIDTH), cos0, sin0, low_half).astype(BF16)
    bv[0] = mm(h0, wn_ref, c + B_Q_WIDTH + B_KV_WIDTH, B_KV_WIDTH).astype(BF16)

    for d, hs, cs, ss, w_ref, qo, ko, vo in ((DILATIONS[1], h1_s, c1_s, s1_s, w1_ref, q1, k1, v1),
                                             (DILATIONS[2], h2_s, c2_s, s2_s, w2_ref, q2, k2, v2)):
        n = tm // d
        for r in range(d):
            rows = pl.ds(r, n, stride=d)
            xr = jnp.concatenate([xs[c, rows, :] for c in range(n_chunks)], axis=1)
            hs[r * n:(r + 1) * n, :] = _rms(xr, g).astype(BF16)
            cs[r * n:(r + 1) * n, :] = cos_ref[rows, :]
            ss[r * n:(r + 1) * n, :] = sin_ref[rows, :]
        h = hs[...]
        cos, sin = cs[...], ss[...]
        qv = (_rope(mm(h, w_ref, 0, A_WIDTH), cos, sin, low_half) * Q_SCALE).astype(BF16)
        kv = _rope(mm(h, w_ref, A_WIDTH, A_WIDTH), cos, sin, low_half).astype(BF16)
        vv = mm(h, w_ref, 2 * A_WIDTH, A_WIDTH).astype(BF16)
        for r in range(d):
            qo[0, r] = qv[r * n:(r + 1) * n]
            ko[0, r] = kv[r * n:(r + 1) * n]
            vo[0, r] = vv[r * n:(r + 1) * n]


def _rope_tables(s):
    inv = jnp.power(jnp.float32(ROPE_THETA), -jnp.arange(ROT_HALF, dtype=F32) / ROT_HALF)
    ang = jnp.arange(s).astype(F32)[:, None] * inv[None, :]
    cos, sin = jnp.cos(ang), jnp.sin(ang)
    src = jnp.arange(LANES) % HEAD_DIM
    spread = (jnp.arange(ROT_DIM)[:, None] == src[None, :]).astype(F32)
    hi = lax.Precision.HIGHEST
    cos_t = jnp.dot(jnp.concatenate([cos, cos], axis=1), spread, precision=hi) + (src >= ROT_DIM).astype(F32)
    sin_t = jnp.dot(jnp.concatenate([-sin, sin], axis=1), spread, precision=hi)
    return cos_t, sin_t


def _in_proj(x, norm_g, w_nat, w_g1, w_g2, cos_t, sin_t):
    b, s, _ = x.shape
    tm = IN_TILE
    const = lambda t, bb: (0, 0)
    one_buf = pl.Buffered(1)
    in_specs = [
        pl.BlockSpec((1, tm, D_MODEL), lambda t, bb: (bb, t, 0)),
        pl.BlockSpec((1, D_MODEL), const),
        pl.BlockSpec((tm, LANES), lambda t, bb: (t, 0)),
        pl.BlockSpec((tm, LANES), lambda t, bb: (t, 0)),
        pl.BlockSpec(w_nat.shape, const, pipeline_mode=one_buf),
        pl.BlockSpec(w_g1.shape, const, pipeline_mode=one_buf),
        pl.BlockSpec(w_g2.shape, const, pipeline_mode=one_buf),
    ]
    out_shape, out_specs = [], []
    for d in DILATIONS:
        for _ in range(3):
            out_shape.append(jax.ShapeDtypeStruct((b, d, s // d, A_WIDTH), BF16))
            out_specs.append(pl.BlockSpec((1, d, tm // d, A_WIDTH), lambda t, bb: (bb, 0, t, 0)))
    for w in (B_Q_WIDTH, B_KV_WIDTH, B_KV_WIDTH):
        out_shape.append(jax.ShapeDtypeStruct((b, s, w), BF16))
        out_specs.append(pl.BlockSpec((1, tm, w), lambda t, bb: (bb, t, 0)))
    scratch = [pltpu.VMEM((D_MODEL // LANES, tm, LANES), F32),
               pltpu.VMEM((tm, D_MODEL), BF16), pltpu.VMEM((tm, D_MODEL), BF16)] + \
              [pltpu.VMEM((tm, LANES), F32) for _ in range(4)]
    return pl.pallas_call(
        _in_proj_kernel,
        grid=(s // tm, b),
        in_specs=in_specs,
        out_specs=out_specs,
        out_shape=out_shape,
        scratch_shapes=scratch,
        compiler_params=pltpu.CompilerParams(
            dimension_semantics=("arbitrary", "arbitrary"), vmem_limit_bytes=VMEM_LIMIT),
        name="in_proj",
    )(x, norm_g.reshape(1, D_MODEL), cos_t, sin_t, w_nat, w_g1, w_g2)


def _attend_heads(blocks, valid, first_half):
    second_half = jnp.logical_not(first_half)
    n_heads = blocks[0][0].shape[1] // HEAD_DIM
    scores = []
    for qb, kw, _ in blocks:
        for h in range(n_heads):
            cs = slice((h // 2) * LANES, (h // 2 + 1) * LANES)
            qp = qb[:, cs]
            qm = jnp.where(first_half if h % 2 == 0 else second_half, qp, jnp.zeros_like(qp))
            scores.append(lax.dot_general(qm, kw[:, cs], (((1,), (1,)), ((), ())),
                                          preferred_element_type=F32))
    sc = jnp.where(valid[None], jnp.stack(scores), NEG_BIG)
    m = jnp.max(sc, axis=-1, keepdims=True)
    p = jnp.exp(sc - m)
    l = jnp.sum(p, axis=-1, keepdims=True)
    pb = p.astype(BF16)
    results = []
    for bi, (_, _, vw) in enumerate(blocks):
        out = []
        for hp in range(n_heads // 2):
            i0 = bi * n_heads + 2 * hp
            vp = vw[:, hp * LANES:(hp + 1) * LANES]
            o0 = jnp.dot(pb[i0], vp, preferred_element_type=F32)
            o1 = jnp.dot(pb[i0 + 1], vp, preferred_element_type=F32)
            out.append((jnp.where(first_half, o0, o1), jnp.where(first_half, m[i0], m[i0 + 1]),
                        jnp.where(first_half, l[i0], l[i0 + 1])))
        results.append(out)
    return results


def _mixer_a_kernel(*refs, seq):
    ins, (o_ref, kext, vext, acc_o, acc_m, acc_l) = refs[:21], refs[21:]
    t = pl.program_id(1)
    tq = A_TILE
    halo = A_HALF

    first_group = len(DILATIONS) - 1
    for gi in reversed(range(len(DILATIONS))):
        d = DILATIONS[gi]
        q, kc, kp, kn, vc, vp, vn = ins[7 * gi:7 * gi + 7]
        n = tq // d
        blk = min(n, A_QBLOCK)
        nblk = n // blk
        win = -(-(blk + 2 * halo) // LANES) * LANES
        tail = win - (blk + 2 * halo) if n == blk else 0
        ld = seq // d
        col = lax.broadcasted_iota(jnp.int32, (blk, win), 1)
        diff = col - lax.broadcasted_iota(jnp.int32, (blk, win), 0)
        band = (diff >= 0) & (diff <= 2 * halo)
        first_half = lax.broadcasted_iota(jnp.int32, (blk, LANES), 1) < HEAD_DIM

        pair = A_QBLOCK // blk

        def residue_body(rp, carry, q=q, kc=kc, kp=kp, kn=kn, vc=vc, vp=vp, vn=vn, d=d, n=n, nblk=nblk,
                         ld=ld, blk=blk, win=win, col=col, band=band, first_half=first_half,
                         first=(gi == first_group), last=(gi == 0), pair=pair, tail=tail):
            for u in range(pair):
                r = rp * pair + u
                base = u * win
                kext[base:base + halo, :] = kp[0, r]
                kext[base + halo:base + halo + n, :] = kc[0, r]
                kext[base + halo + n:base + 2 * halo + n, :] = kn[0, r]
                vext[base:base + halo, :] = vp[0, r]
                vext[base + halo:base + halo + n, :] = vc[0, r]
                vext[base + halo + n:base + 2 * halo + n, :] = vn[0, r]
                if tail:
                    kext[base + 2 * halo + n:base + win, :] = jnp.zeros((tail, kext.shape[1]), BF16)
                    vext[base + 2 * halo + n:base + win, :] = jnp.zeros((tail, vext.shape[1]), BF16)

            def block_body(j, carry2):
                j0 = pl.multiple_of(j * blk, blk)
                qbase = t * n + j * blk
                valid = band & (col >= halo - qbase) & (col < ld - qbase + halo)
                blocks = [(q[0, rp * pair + u, pl.ds(j0, blk), :],
                           kext[pl.ds(u * win + j0, win), :],
                           vext[pl.ds(u * win + j0, win), :]) for u in range(pair)]
                results = _attend_heads(blocks, valid, first_half)
                for u, parts in enumerate(results):
                    if d == 1:
                        rows = pl.ds(j0, blk)
                    else:
                        rows = pl.ds(j * (blk * d) + rp * pair + u, blk, stride=d)
                    for hp, (o_b, m_b, l_b) in enumerate(parts):
                        if first:
                            acc_o[hp, rows, :] = o_b
                            acc_l[hp, rows, :] = l_b
                            acc_m[hp, rows, :] = m_b
                        else:
                            m_old = acc_m[hp, rows, :]
                            m_new = jnp.maximum(m_old, m_b)
                            a_old = jnp.exp(m_old - m_new)
                            a_new = jnp.exp(m_b - m_new)
                            o_new = acc_o[hp, rows, :] * a_old + o_b * a_new
                            l_new = acc_l[hp, rows, :] * a_old + l_b * a_new
                            if last:
                                o_ref[0, rows, hp * LANES:(hp + 1) * LANES] = (o_new / l_new).astype(BF16)
                            else:
                                acc_o[hp, rows, :] = o_new
                                acc_l[hp, rows, :] = l_new
                                acc_m[hp, rows, :] = m_new
                return carry2

            lax.fori_loop(0, nblk, block_body, 0)
            return carry

        lax.fori_loop(0, d // pair, residue_body, 0)


def _mixer_a(qkv, seq):
    b = qkv[0].shape[0]
    tq = A_TILE
    blk = A_BLOCK
    aw = A_STEP_WIDTH
    args, in_specs = [], []
    for gi, d in enumerate(DILATIONS):
        q, k, v = qkv[3 * gi:3 * gi + 3]
        n = tq // d
        nblk = n // blk
        last = seq // d // blk - 1
        cur = pl.BlockSpec((1, d, n, aw), lambda bb, t, hh: (bb, 0, t, hh))
        prev = pl.BlockSpec((1, d, blk, aw),
                            lambda bb, t, hh, nblk=nblk: (bb, 0, jnp.maximum(t * nblk - 1, 0), hh))
        nxt = pl.BlockSpec((1, d, blk, aw),
                           lambda bb, t, hh, nblk=nblk, last=last: (bb, 0, jnp.minimum((t + 1) * nblk, last), hh))
        args += [q, k, k, k, v, v, v]
        in_specs += [cur, cur, prev, nxt, cur, prev, nxt]
    return pl.pallas_call(
        functools.partial(_mixer_a_kernel, seq=seq),
        grid=(b, seq // tq, A_WIDTH // aw),
        in_specs=in_specs,
        out_specs=pl.BlockSpec((1, tq, aw), lambda bb, t, hh: (bb, t, hh)),
        out_shape=jax.ShapeDtypeStruct((b, seq, A_WIDTH), BF16),
        scratch_shapes=[pltpu.VMEM((tq + 2 * blk, aw), BF16), pltpu.VMEM((tq + 2 * blk, aw), BF16),
                        pltpu.VMEM((aw // LANES, tq, LANES), F32),
                        pltpu.VMEM((aw // LANES, tq, LANES), F32),
                        pltpu.VMEM((aw // LANES, tq, LANES), F32)],
        compiler_params=pltpu.CompilerParams(
            dimension_semantics=("arbitrary", "arbitrary", "arbitrary"), vmem_limit_bytes=VMEM_LIMIT),
        name="mixer_a",
    )(*args)


def _mixer_b_kernel(sink_ref, q, kc, kp, kn, vc, vp, vn, o_ref, kext, vext, *, seq):
    t = pl.program_id(1)
    tq = ATT_TILE
    blk = B_BLOCK
    win = 3 * blk
    kext[0:blk, :] = kp[0]
    kext[blk:blk + tq, :] = kc[0]
    kext[blk + tq:2 * blk + tq, :] = kn[0]
    vext[0:blk, :] = vp[0]
    vext[blk:blk + tq, :] = vc[0]
    vext[blk + tq:2 * blk + tq, :] = vn[0]
    col = lax.broadcasted_iota(jnp.int32, (blk, win), 1)
    row = lax.broadcasted_iota(jnp.int32, (blk, win), 0)
    diff = col - row
    first_half = lax.broadcasted_iota(jnp.int32, (blk, LANES), 1) < HEAD_DIM
    second_half = jnp.logical_not(first_half)

    def block_body(j, carry):
        j0 = pl.multiple_of(j * blk, blk)
        qb = q[0, pl.ds(j0, blk), :]
        kw = kext[pl.ds(j0, win), :]
        vw = vext[pl.ds(j0, win), :]
        qbase = t * tq + j * blk
        valid = ((diff >= 0) & (diff <= 2 * B_WINDOW)
                 & (col >= blk - qbase) & (col < seq - qbase + blk))
        scores, sinks = [], []
        for c in range(B_Q_HEADS // 2):
            qp = qb[:, c * LANES:(c + 1) * LANES]
            for g, keep in enumerate((first_half, second_half)):
                qm = jnp.where(keep, qp, jnp.zeros_like(qp))
                scores.append(lax.dot_general(qm, kw, (((1,), (1,)), ((), ())), preferred_element_type=F32))
                sinks.append(jnp.full((1, LANES), sink_ref[g * B_REP + c], F32))
        sc = jnp.where(valid[None], jnp.stack(scores), NEG_BIG)
        sk = jnp.stack(sinks)
        part = sc[..., 0:LANES]
        for k in range(1, win // LANES):
            part = jnp.maximum(part, sc[..., k * LANES:(k + 1) * LANES])
        m = jnp.max(jnp.maximum(part, sk), axis=-1, keepdims=True)
        p = jnp.exp(sc - m)
        denom = jnp.sum(p, axis=-1, keepdims=True) + jnp.exp(sk - m)
        pb = p.astype(BF16)
        parts = []
        for c in range(B_Q_HEADS // 2):
            o0 = jnp.dot(pb[2 * c], vw, preferred_element_type=F32) / denom[2 * c]
            o1 = jnp.dot(pb[2 * c + 1], vw, preferred_element_type=F32) / denom[2 * c + 1]
            parts.append(jnp.where(first_half, o0, o1))
        o_ref[0, pl.ds(j0, blk), :] = jnp.concatenate(parts, axis=1).astype(BF16)
        return carry

    lax.fori_loop(0, tq // blk, block_body, 0)


def _mixer_b(q, k, v, sink, seq):
    b = q.shape[0]
    tq = ATT_TILE
    blk = B_BLOCK
    nblk = tq // blk
    last = seq // blk - 1
    cur = pl.BlockSpec((1, tq, B_KV_WIDTH), lambda bb, t: (bb, t, 0))
    prev = pl.BlockSpec((1, blk, B_KV_WIDTH), lambda bb, t: (bb, jnp.maximum(t * nblk - 1, 0), 0))
    nxt = pl.BlockSpec((1, blk, B_KV_WIDTH), lambda bb, t: (bb, jnp.minimum((t + 1) * nblk, last), 0))
    return pl.pallas_call(
        functools.partial(_mixer_b_kernel, seq=seq),
        grid=(b, seq // tq),
        in_specs=[pl.BlockSpec(memory_space=pltpu.SMEM),
                  pl.BlockSpec((1, tq, B_Q_WIDTH), lambda bb, t: (bb, t, 0)),
                  cur, prev, nxt, cur, prev, nxt],
        out_specs=pl.BlockSpec((1, tq, B_Q_WIDTH), lambda bb, t: (bb, t, 0)),
        out_shape=jax.ShapeDtypeStruct((b, seq, B_Q_WIDTH), BF16),
        scratch_shapes=[pltpu.VMEM((tq + 2 * blk, B_KV_WIDTH), BF16),
                        pltpu.VMEM((tq + 2 * blk, B_KV_WIDTH), BF16)],
        compiler_params=pltpu.CompilerParams(
            dimension_semantics=("arbitrary", "arbitrary"), vmem_limit_bytes=VMEM_LIMIT),
        name="mixer_b",
    )(sink, q, k, k, k, v, v, v)


def _post_attn_kernel(x_ref, oa_ref, ob_ref, gmix_ref, wg_ref, wa_ref, wb_ref, wo_ref, gmoe_ref,
                      wr_ref, br_ref, x1_ref, h2_ref, gate_ref):
    wr = wr_ref[...]
    wr_hi = wr.astype(BF16)
    wr_lo = (wr - wr_hi.astype(F32)).astype(BF16)
    wr3 = jnp.concatenate([wr_hi, wr_hi, wr_lo], axis=0)
    for s in range(x_ref.shape[0] // POST_SLAB):
        rows = slice(s * POST_SLAB, (s + 1) * POST_SLAB)
        _post_attn_rows(x_ref, oa_ref, ob_ref, gmix_ref, wg_ref, wa_ref, wb_ref, wo_ref, gmoe_ref, wr3,
                        br_ref, x1_ref, h2_ref, gate_ref, rows)


def _post_attn_rows(x_ref, oa_ref, ob_ref, gmix_ref, wg_ref, wa_ref, wb_ref, wo_ref, gmoe_ref, wr3,
                    br_ref, x1_ref, h2_ref, gate_ref, rows):
    x = x_ref[rows, :]
    h = _rms(x, gmix_ref[...]).astype(BF16)
    gates = jax.nn.sigmoid(jnp.dot(h, wg_ref[...], preferred_element_type=F32))
    ya = jnp.dot(oa_ref[rows, :], wa_ref[...], preferred_element_type=F32)
    yb = jnp.dot(ob_ref[rows, :], wb_ref[...], preferred_element_type=F32)
    mix = gates[:, :D_MODEL] * ya + gates[:, D_MODEL:] * yb
    x1 = x + jnp.dot(mix.astype(BF16), wo_ref[...], preferred_element_type=F32)
    x1_ref[rows, :] = x1
    h2 = _rms(x1, gmoe_ref[...])
    h2_hi = h2.astype(BF16)
    h2_ref[rows, :] = h2_hi

    h2_lo = (h2 - h2_hi.astype(F32)).astype(BF16)
    logits = jnp.dot(jnp.concatenate([h2_hi, h2_lo, h2_hi], axis=1), wr3,
                     preferred_element_type=F32) + br_ref[...]
    tm = logits.shape[0]
    lane = lax.broadcasted_iota(jnp.int32, (tm, LANES), 1)
    big = jnp.int32(LANES)
    is_group = (lane >= N_EXPERTS) & (lane < N_EXPERTS + N_GROUPS)
    gl = jnp.where(is_group, logits, NEG_BIG)
    gmax = jnp.max(gl, axis=-1, keepdims=True)
    g_idx = jnp.min(jnp.where(gl == gmax, lane, big), axis=-1, keepdims=True) - N_EXPERTS
    g_w = 1.0 / jnp.sum(jnp.exp(gl - gmax), axis=-1, keepdims=True)
    in_group = (lane < N_EXPERTS) & (lax.shift_right_logical(lane, 3) == g_idx)
    sel = jnp.where(in_group, logits, NEG_BIG)
    v1 = jnp.max(sel, axis=-1, keepdims=True)
    i1 = jnp.min(jnp.where(sel == v1, lane, big), axis=-1, keepdims=True)
    sel2 = jnp.where(lane == i1, NEG_BIG, sel)
    v2 = jnp.max(sel2, axis=-1, keepdims=True)
    i2 = jnp.min(jnp.where(sel2 == v2, lane, big), axis=-1, keepdims=True)
    e2 = jnp.exp(v2 - v1)
    w1 = g_w / (1.0 + e2)
    w2 = g_w * e2 / (1.0 + e2)
    gate_ref[rows, :] = jnp.where(lane == i1, w1, 0.0) + jnp.where(lane == i2, w2, 0.0)


def _post_attn(x2d, oa, ob, norm_mix, w_gate, w_a, w_b, w_out, norm_moe, w_router, b_router):
    t = x2d.shape[0]
    tm = POST_TILE
    const = lambda i: (0, 0)
    row = lambda w: pl.BlockSpec((tm, w), lambda i: (i, 0))
    full = lambda a: pl.BlockSpec(a.shape, const, pipeline_mode=pl.Buffered(1))
    return pl.pallas_call(
        _post_attn_kernel,
        grid=(t // tm,),
        in_specs=[row(D_MODEL), row(A_WIDTH), row(B_Q_WIDTH), full(norm_mix), full(w_gate), full(w_a),
                  full(w_b), full(w_out), full(norm_moe), full(w_router), full(b_router)],
        out_specs=[row(D_MODEL), row(D_MODEL), row(LANES)],
        out_shape=[jax.ShapeDtypeStruct((t, D_MODEL), F32), jax.ShapeDtypeStruct((t, D_MODEL), BF16),
                   jax.ShapeDtypeStruct((t, LANES), F32)],
        compiler_params=pltpu.CompilerParams(
            dimension_semantics=("arbitrary",), vmem_limit_bytes=VMEM_LIMIT),
        name="post_attn",
    )(x2d, oa, ob, norm_mix, w_gate, w_a, w_b, w_out, norm_moe, w_router, b_router)


def _moe_final_kernel(x1_ref, h2_ref, gate_ref, tri_ref, w1_ref, w3_ref, w2_ref, gf_ref, o_ref,
                      xs, stage, out_stage, seg_s, slo_s, shi_s, wlo_s, whi_s):
    step = pl.program_id(1)
    first_step = (pl.program_id(0) == 0) & (step == 0)
    n_sub = x1_ref.shape[0] // MOE_SUB

    @pl.when(first_step)
    def _init():
        xs[n_sub * MOE_SUB_SLOTS:, :] = jnp.zeros((MOE_CHUNK, D_MODEL), BF16)
        stage[...] = jnp.zeros(stage.shape, BF16)

    @pl.when(step == 0)
    def _sort():
        r_i = lax.broadcasted_iota(jnp.int32, (LANES, LANES), 0)
        c_i = lax.broadcasted_iota(jnp.int32, (LANES, LANES), 1)
        upper = jnp.where(r_i < c_i, 1.0, 0.0)
        col = lax.broadcasted_iota(jnp.int32, (MOE_SUB, MOE_SUB_SLOTS), 1)
        for s in range(n_sub):
            rows = slice(s * MOE_SUB, (s + 1) * MOE_SUB)
            g = gate_ref[rows, :]
            hit = g > 0.0
            onehot = jnp.where(hit, 1.0, 0.0)
            count = jnp.sum(onehot, axis=0, keepdims=True)
            padded = jnp.ceil(count * (1.0 / SEG_ALIGN)) * SEG_ALIGN
            seg = jnp.dot(jnp.broadcast_to(padded, (8, LANES)), upper, preferred_element_type=F32,
                          precision=lax.Precision.HIGHEST)[0:1]
            seg_i = seg.astype(jnp.int32)
            padded_i = padded.astype(jnp.int32)
            for ex in range(N_EXPERTS):
                seg_s[2 * s, ex] = seg_i[0, ex] + s * MOE_SUB_SLOTS
                seg_s[2 * s + 1, ex] = padded_i[0, ex]
            rank = jnp.dot(tri_ref[...], onehot.astype(BF16), preferred_element_type=F32)
            slot = seg + rank
            s_lo = jnp.min(jnp.where(hit, slot, 1e9), axis=-1, keepdims=True)
            s_hi = jnp.max(jnp.where(hit, slot, -1.0), axis=-1, keepdims=True)
            wlo_s[rows, :] = jnp.sum(jnp.where(hit & (slot == s_lo), g, 0.0), axis=-1, keepdims=True)
            whi_s[rows, :] = jnp.sum(jnp.where(hit & (slot == s_hi) & (s_hi > s_lo), g, 0.0),
                                     axis=-1, keepdims=True)
            s_lo_i = s_lo.astype(jnp.int32)
            s_hi_i = s_hi.astype(jnp.int32)
            slo_s[rows, :] = s_lo_i
            shi_s[rows, :] = s_hi_i
            perm_t = jnp.where((col == s_lo_i) | (col == s_hi_i), 1.0, 0.0).astype(BF16)
            sorted_rows = lax.dot_general(perm_t, h2_ref[rows, :], (((0,), (0,)), ((), ())),
                                          preferred_element_type=F32)
            xs[s * MOE_SUB_SLOTS:(s + 1) * MOE_SUB_SLOTS, :] = sorted_rows.astype(BF16)

    def ffn(x, j):
        a = jnp.dot(x, w1_ref[j], preferred_element_type=F32)
        b = jnp.dot(x, w3_ref[j], preferred_element_type=F32)
        hid = (a * jax.nn.sigmoid(a)) * b
        return jnp.dot(hid.astype(BF16), w2_ref[j], preferred_element_type=F32).astype(BF16)

    bases, totals, seg_starts, seg_lens, seg_dsts = [], [], [], [], []
    base = jnp.int32(0)
    for j in range(MOE_ESTEP):
        e = step * MOE_ESTEP + j
        starts = [seg_s[2 * s, e] for s in range(n_sub)]
        lens = [seg_s[2 * s + 1, e] for s in range(n_sub)]
        dst = base
        dsts = []
        for s in range(n_sub):
            dsts.append(dst)
            dst = dst + lens[s]
        bases.append(base)
        totals.append(dst - base)
        seg_starts.append(starts)
        seg_lens.append(lens)
        seg_dsts.append(dsts)
        base = base + ((dst - base + MOE_CHUNK - 1) // MOE_CHUNK) * MOE_CHUNK

    for j in range(MOE_ESTEP):
        for s in range(n_sub):
            def gather_body(k, carry, src0=seg_starts[j][s], dst0=seg_dsts[j][s]):
                src = pl.multiple_of(src0 + k * MOE_COPY, SEG_ALIGN)
                to = pl.multiple_of(dst0 + k * MOE_COPY, SEG_ALIGN)
                stage[pl.ds(to, MOE_COPY), :] = xs[pl.ds(src, MOE_COPY), :]
                return carry

            lax.fori_loop(1, (seg_lens[j][s] + MOE_COPY - 1) // MOE_COPY, gather_body, 0)
    for j in range(MOE_ESTEP):
        for s in range(n_sub):
            stage[pl.ds(pl.multiple_of(seg_dsts[j][s], SEG_ALIGN), MOE_COPY), :] = \
                xs[pl.ds(pl.multiple_of(seg_starts[j][s], SEG_ALIGN), MOE_COPY), :]

    for j in range(MOE_ESTEP):
        def extra_body(k, carry, j=j):
            r0 = pl.multiple_of(bases[j] + k * MOE_CHUNK, MOE_CHUNK)
            out_stage[pl.ds(r0, MOE_CHUNK), :] = ffn(stage[pl.ds(r0, MOE_CHUNK), :], j)
            return carry

        lax.fori_loop(1, (totals[j] + MOE_CHUNK - 1) // MOE_CHUNK, extra_body, 0)

    firsts = [ffn(stage[pl.ds(pl.multiple_of(bases[j], MOE_CHUNK), MOE_CHUNK), :], j) for j in range(MOE_ESTEP)]
    for j in range(MOE_ESTEP):
        out_stage[pl.ds(pl.multiple_of(bases[j], MOE_CHUNK), MOE_CHUNK), :] = firsts[j]

    for j in range(MOE_ESTEP):
        for s in range(n_sub):
            def scatter_body(k, carry, src0=seg_dsts[j][s], dst0=seg_starts[j][s]):
                src = pl.multiple_of(src0 + k * SEG_ALIGN, SEG_ALIGN)
                to = pl.multiple_of(dst0 + k * SEG_ALIGN, SEG_ALIGN)
                xs[pl.ds(to, SEG_ALIGN), :] = out_stage[pl.ds(src, SEG_ALIGN), :]
                return carry

            lax.fori_loop(0, seg_lens[j][s] // SEG_ALIGN, scatter_body, 0)

    @pl.when(step == N_EXPERTS // MOE_ESTEP - 1)
    def _combine():
        col = lax.broadcasted_iota(jnp.int32, (MOE_SUB, MOE_SUB_SLOTS), 1)
        for s in range(n_sub):
            rows = slice(s * MOE_SUB, (s + 1) * MOE_SUB)
            wperm = (jnp.where(col == slo_s[rows, :], wlo_s[rows, :], 0.0)
                     + jnp.where(col == shi_s[rows, :], whi_s[rows, :], 0.0)).astype(BF16)
            moe = jnp.dot(wperm, xs[s * MOE_SUB_SLOTS:(s + 1) * MOE_SUB_SLOTS, :], preferred_element_type=F32)
            o_ref[rows, :] = _rms(x1_ref[rows, :] + moe, gf_ref[...])


def _moe_final(x1, h2, gate, w1, w3, w2, norm_final):
    t = x1.shape[0]
    tm = MOE_TILE
    n_sub = tm // MOE_SUB
    row = lambda w: pl.BlockSpec((tm, w), lambda i, e: (i, 0))
    tri = jnp.tril(jnp.ones((MOE_SUB, MOE_SUB), BF16), -1)
    return pl.pallas_call(
        _moe_final_kernel,
        grid=(t // tm, N_EXPERTS // MOE_ESTEP),
        in_specs=[row(D_MODEL), row(D_MODEL), row(LANES),
                  pl.BlockSpec((MOE_SUB, MOE_SUB), lambda i, e: (0, 0), pipeline_mode=pl.Buffered(1)),
                  pl.BlockSpec((MOE_ESTEP, D_MODEL, D_FF), lambda i, e: (e, 0, 0)),
                  pl.BlockSpec((MOE_ESTEP, D_MODEL, D_FF), lambda i, e: (e, 0, 0)),
                  pl.BlockSpec((MOE_ESTEP, D_FF, D_MODEL), lambda i, e: (e, 0, 0)),
                  pl.BlockSpec((1, D_MODEL), lambda i, e: (0, 0))],
        out_specs=row(D_MODEL),
        out_shape=jax.ShapeDtypeStruct((t, D_MODEL), F32),
        scratch_shapes=[pltpu.VMEM((n_sub * MOE_SUB_SLOTS + MOE_CHUNK, D_MODEL), BF16),
                        pltpu.VMEM((2 * tm + (MOE_ESTEP + 1) * MOE_CHUNK, D_MODEL), BF16),
                        pltpu.VMEM((2 * tm + (MOE_ESTEP + 1) * MOE_CHUNK, D_MODEL), BF16),
                        pltpu.SMEM((2 * n_sub, N_EXPERTS), jnp.int32),
                        pltpu.VMEM((tm, 1), jnp.int32), pltpu.VMEM((tm, 1), jnp.int32),
                        pltpu.VMEM((tm, 1), F32), pltpu.VMEM((tm, 1), F32)],
        compiler_params=pltpu.CompilerParams(
            dimension_semantics=("arbitrary", "arbitrary"), vmem_limit_bytes=VMEM_LIMIT),
        name="moe_final",
    )(x1, h2, gate, tri, w1, w3, w2, norm_final)


def _prep_weights(norm_mix, w_in, attn_sink, w_branch_a, w_branch_b, w_gate, w_out, norm_moe,
                  w_router_group, b_router_group, w_router_expert, b_router_expert, w1, w3, w2, norm_final):
    wi = w_in[0]
    a_w = 3 * A_WIDTH

    def group_cols(gi):
        return jnp.concatenate([wi[:, role * a_w + gi * A_WIDTH: role * a_w + (gi + 1) * A_WIDTH]
                                for role in range(3)], axis=1)

    b_heads = [g * B_REP + c for c in range(B_REP) for g in range(B_KV_HEADS)]
    bq_cols = jnp.concatenate([wi[:, 3 * a_w + h * HEAD_DIM: 3 * a_w + (h + 1) * HEAD_DIM] for h in b_heads],
                              axis=1)
    w_b_rows = jnp.concatenate([w_branch_b[0][h * HEAD_DIM:(h + 1) * HEAD_DIM] for h in b_heads], axis=0)
    w_nat = jnp.concatenate([group_cols(0), bq_cols, wi[:, 3 * a_w + B_Q_WIDTH:]], axis=1).astype(BF16)
    w_g1 = group_cols(1).astype(BF16)
    w_g2 = group_cols(2).astype(BF16)
    pad = LANES - N_EXPERTS - N_GROUPS
    w_router = jnp.concatenate([w_router_expert[0], w_router_group[0],
                                jnp.zeros((D_MODEL, pad), F32)], axis=1)
    b_router = jnp.concatenate([b_router_expert[0].reshape(-1), b_router_group[0],
                                jnp.zeros((pad,), F32)]).reshape(1, LANES)
    return dict(
        norm_mix=norm_mix[0].reshape(1, D_MODEL), w_nat=w_nat, w_g1=w_g1, w_g2=w_g2, sink=attn_sink[0],
        w_a=w_branch_a[0].astype(BF16), w_b=w_b_rows.astype(BF16), w_gate=w_gate[0].astype(BF16),
        w_out=w_out[0].astype(BF16), norm_moe=norm_moe[0].reshape(1, D_MODEL),
        w_router=w_router, b_router=b_router,
        w1=w1[0].astype(BF16), w3=w3[0].astype(BF16), w2=w2[0].astype(BF16),
        norm_final=norm_final.reshape(1, D_MODEL))


def _trunk(x, p):
    b, s, _ = x.shape
    outs = _in_proj(x, p["norm_mix"], p["w_nat"], p["w_g1"], p["w_g2"], p["cos"], p["sin"])
    oa = _mixer_a(outs[:9], s)
    ob = _mixer_b(outs[9], outs[10], outs[11], p["sink"], s)
    t = b * s
    x1, h2, gate = _post_attn(x.reshape(t, D_MODEL), oa.reshape(t, A_WIDTH), ob.reshape(t, B_Q_WIDTH),
                              p["norm_mix"], p["w_gate"], p["w_a"], p["w_b"], p["w_out"], p["norm_moe"],
                              p["w_router"], p["b_router"])
    y = _moe_final(x1, h2, gate, p["w1"], p["w3"], p["w2"], p["norm_final"])
    return y.reshape(b, s, D_MODEL)


def kernel(x_prompt, x_sample, norm_mix, w_in, attn_sink, w_branch_a, w_branch_b, w_gate, w_out, norm_moe,
           w_router_group, b_router_group, w_router_expert, b_router_expert, w1, w3, w2, norm_final):
    p = _prep_weights(norm_mix, w_in, attn_sink, w_branch_a, w_branch_b, w_gate, w_out, norm_moe,
                      w_router_group, b_router_group, w_router_expert, b_router_expert, w1, w3, w2, norm_final)
    p["cos"], p["sin"] = _rope_tables(max(x_prompt.shape[1], x_sample.shape[1]))
    return (_trunk(x_prompt, p), _trunk(x_sample, p))
```

```python
import functools

import jax
import jax.numpy as jnp
from jax import lax
from jax.experimental import pallas as pl
from jax.experimental.pallas import tpu as pltpu

D_MODEL = 1024
HEAD_DIM = 64
ROT_DIM = HEAD_DIM // 4
ROT_HALF = ROT_DIM // 2
ROPE_THETA = 500000.0
DILATIONS = (1, 4, 16)
A_HALF = 64
A_BLOCK = 64
A_QBLOCK = 128
A_HEADS = 8
A_WIDTH = A_HEADS * HEAD_DIM
B_Q_HEADS = 8
B_KV_HEADS = 2
B_REP = B_Q_HEADS // B_KV_HEADS
B_WINDOW = 128
B_BLOCK = 128
B_Q_WIDTH = B_Q_HEADS * HEAD_DIM
B_KV_WIDTH = B_KV_HEADS * HEAD_DIM
N_GROUPS = 4
EXPERTS_PER_GROUP = 8
N_EXPERTS = N_GROUPS * EXPERTS_PER_GROUP
D_FF = 256
RMS_EPS = 1e-6
Q_SCALE = HEAD_DIM ** -0.5

LANES = 128
NEG_BIG = -1e30
VMEM_LIMIT = 58 * 1024 * 1024

IN_TILE = 512
ATT_TILE = 1024
A_TILE = 1024
A_STEP_WIDTH = 512
POST_TILE = 1024
POST_SLAB = 256
MOE_TILE = 1024
SEG_ALIGN = 16
MOE_CHUNK = 128
MOE_COPY = 32
MOE_SUB = 256
MOE_SUB_SLOTS = 2 * MOE_SUB + N_EXPERTS * SEG_ALIGN
MOE_ESTEP = 4

F32 = jnp.float32
BF16 = jnp.bfloat16


def _rms(x, g):
    ms = jnp.mean(x * x, axis=-1, keepdims=True)
    return x * lax.rsqrt(ms + RMS_EPS) * g


def _rope(a, cos, sin, low_half):
    outs = []
    for k in range(a.shape[1] // LANES):
        ak = a[:, k * LANES:(k + 1) * LANES]
        up = pltpu.roll(ak, LANES - ROT_HALF, axis=1)
        dn = pltpu.roll(ak, ROT_HALF, axis=1)
        outs.append(ak * cos + jnp.where(low_half, up, dn) * sin)
    return outs[0] if len(outs) == 1 else jnp.concatenate(outs, axis=1)


def _in_proj_kernel(x_ref, g_ref, cos_ref, sin_ref, wn_ref, w1_ref, w2_ref,
                    q0, k0, v0, q1, k1, v1, q2, k2, v2, bq, bk, bv,
                    xs, h1_s, h2_s, c1_s, s1_s, c2_s, s2_s):
    tm = x_ref.shape[1]
    n_chunks = D_MODEL // LANES
    for c in range(n_chunks):
        xs[c] = x_ref[0, :, c * LANES:(c + 1) * LANES]
    g = g_ref[...]
    low_half = (lax.broadcasted_iota(jnp.int32, (tm, LANES), 1) % HEAD_DIM) < ROT_HALF

    def mm(h, w_ref, c0, n):
        return jnp.dot(h, w_ref[:, c0:c0 + n], preferred_element_type=F32)

    h0 = _rms(x_ref[0], g).astype(BF16)
    cos0, sin0 = cos_ref[...], sin_ref[...]
    q0[0, 0] = (_rope(mm(h0, wn_ref, 0, A_WIDTH), cos0, sin0, low_half) * Q_SCALE).astype(BF16)
    k0[0, 0] = _rope(mm(h0, wn_ref, A_WIDTH, A_WIDTH), cos0, sin0, low_half).astype(BF16)
    v0[0, 0] = mm(h0, wn_ref, 2 * A_WIDTH, A_WIDTH).astype(BF16)
    c = 3 * A_WIDTH
    bq[0] = (_rope(mm(h0, wn_ref, c, B_Q_WIDTH), cos0, sin0, low_half) * Q_SCALE).astype(BF16)
    bk[0] = _rope(mm(h0, wn_ref, c + B_Q_WIDTH, B_KV_WIDTH), cos0, sin0, low_half).astype(BF16)
    bv[0] = mm(h0, wn_ref, c + B_Q_WIDTH + B_KV_WIDTH, B_KV_WIDTH).astype(BF16)

    for d, hs, cs, ss, w_ref, qo, ko, vo in ((DILATIONS[1], h1_s, c1_s, s1_s, w1_ref, q1, k1, v1),
                                             (DILATIONS[2], h2_s, c2_s, s2_s, w2_ref, q2, k2, v2)):
        n = tm // d
        for r in range(d):
            rows = pl.ds(r, n, stride=d)
            xr = jnp.concatenate([xs[c, rows, :] for c in range(n_chunks)], axis=1)
            hs[r * n:(r + 1) * n, :] = _rms(xr, g).astype(BF16)
            cs[r * n:(r + 1) * n, :] = cos_ref[rows, :]
            ss[r * n:(r + 1) * n, :] = sin_ref[rows, :]
        h = hs[...]
        cos, sin = cs[...], ss[...]
        qv = (_rope(mm(h, w_ref, 0, A_WIDTH), cos, sin, low_half) * Q_SCALE).astype(BF16)
        kv = _rope(mm(h, w_ref, A_WIDTH, A_WIDTH), cos, sin, low_half).astype(BF16)
        vv = mm(h, w_ref, 2 * A_WIDTH, A_WIDTH).astype(BF16)
        for r in range(d):
            qo[0, r] = qv[r * n:(r + 1) * n]
            ko[0, r] = kv[r * n:(r + 1) * n]
            vo[0, r] = vv[r * n:(r + 1) * n]


def _rope_tables(s):
    inv = jnp.power(jnp.float32(ROPE_THETA), -jnp.arange(ROT_HALF, dtype=F32) / ROT_HALF)
    ang = jnp.arange(s).astype(F32)[:, None] * inv[None, :]
    cos, sin = jnp.cos(ang), jnp.sin(ang)
    src = jnp.arange(LANES) % HEAD_DIM
    spread = (jnp.arange(ROT_DIM)[:, None] == src[None, :]).astype(F32)
    hi = lax.Precision.HIGHEST
    cos_t = jnp.dot(jnp.concatenate([cos, cos], axis=1), spread, precision=hi) + (src >= ROT_DIM).astype(F32)
    sin_t = jnp.dot(jnp.concatenate([-sin, sin], axis=1), spread, precision=hi)
    return cos_t, sin_t


def _in_proj(x, norm_g, w_nat, w_g1, w_g2, cos_t, sin_t):
    b, s, _ = x.shape
    tm = IN_TILE
    const = lambda t, bb: (0, 0)
    one_buf = pl.Buffered(1)
    in_specs = [
        pl.BlockSpec((1, tm, D_MODEL), lambda t, bb: (bb, t, 0)),
        pl.BlockSpec((1, D_MODEL), const),
        pl.BlockSpec((tm, LANES), lambda t, bb: (t, 0)),
        pl.BlockSpec((tm, LANES), lambda t, bb: (t, 0)),
        pl.BlockSpec(w_nat.shape, const, pipeline_mode=one_buf),
        pl.BlockSpec(w_g1.shape, const, pipeline_mode=one_buf),
        pl.BlockSpec(w_g2.shape, const, pipeline_mode=one_buf),
    ]
    out_shape, out_specs = [], []
    for d in DILATIONS:
        for _ in range(3):
            out_shape.append(jax.ShapeDtypeStruct((b, d, s // d, A_WIDTH), BF16))
            out_specs.append(pl.BlockSpec((1, d, tm // d, A_WIDTH), lambda t, bb: (bb, 0, t, 0)))
    for w in (B_Q_WIDTH, B_KV_WIDTH, B_KV_WIDTH):
        out_shape.append(jax.ShapeDtypeStruct((b, s, w), BF16))
        out_specs.append(pl.BlockSpec((1, tm, w), lambda t, bb: (bb, t, 0)))
    scratch = [pltpu.VMEM((D_MODEL // LANES, tm, LANES), F32),
               pltpu.VMEM((tm, D_MODEL), BF16), pltpu.VMEM((tm, D_MODEL), BF16)] + \
              [pltpu.VMEM((tm, LANES), F32) for _ in range(4)]
    return pl.pallas_call(
        _in_proj_kernel,
        grid=(s // tm, b),
        in_specs=in_specs,
        out_specs=out_specs,
        out_shape=out_shape,
        scratch_shapes=scratch,
        compiler_params=pltpu.CompilerParams(
            dimension_semantics=("arbitrary", "arbitrary"), vmem_limit_bytes=VMEM_LIMIT),
        name="in_proj",
    )(x, norm_g.reshape(1, D_MODEL), cos_t, sin_t, w_nat, w_g1, w_g2)


def _attend_heads(blocks, valid, first_half):
    second_half = jnp.logical_not(first_half)
    n_heads = blocks[0][0].shape[1] // HEAD_DIM
    scores = []
    for qb, kw, _ in blocks:
        for h in range(n_heads):
            cs = slice((h // 2) * LANES, (h // 2 + 1) * LANES)
            qp = qb[:, cs]
            qm = jnp.where(first_half if h % 2 == 0 else second_half, qp, jnp.zeros_like(qp))
            scores.append(lax.dot_general(qm, kw[:, cs], (((1,), (1,)), ((), ())),
                                          preferred_element_type=F32))
    sc = jnp.where(valid[None], jnp.stack(scores), NEG_BIG)
    m = jnp.max(sc, axis=-1, keepdims=True)
    pb = jnp.exp((sc - m).astype(BF16))
    ones = jnp.ones((pb.shape[2], LANES), BF16)
    results = []
    for bi, (_, _, vw) in enumerate(blocks):
        out = []
        for hp in range(n_heads // 2):
            i0 = bi * n_heads + 2 * hp
            v1 = jnp.concatenate([vw[:, hp * LANES:(hp + 1) * LANES], ones], axis=1)
            r0 = jnp.dot(pb[i0], v1, preferred_element_type=F32)
            r1 = jnp.dot(pb[i0 + 1], v1, preferred_element_type=F32)
            out.append((jnp.where(first_half, r0[:, :LANES], r1[:, :LANES]),
                        jnp.where(first_half, m[i0], m[i0 + 1]),
                        jnp.where(first_half, r0[:, LANES:], r1[:, LANES:])))
        results.append(out)
    return results


def _mixer_a_kernel(*refs, seq):
    ins, (o_ref, kext, vext, acc_o, acc_m, acc_l) = refs[:21], refs[21:]
    t = pl.program_id(1)
    tq = A_TILE
    halo = A_HALF

    first_group = len(DILATIONS) - 1
    for gi in reversed(range(len(DILATIONS))):
        d = DILATIONS[gi]
        q, kc, kp, kn, vc, vp, vn = ins[7 * gi:7 * gi + 7]
        n = tq // d
        blk = min(n, A_QBLOCK)
        nblk = n // blk
        win = -(-(blk + 2 * halo) // LANES) * LANES
        tail = win - (blk + 2 * halo) if n == blk else 0
        ld = seq // d
        col = lax.broadcasted_iota(jnp.int32, (blk, win), 1)
        diff = col - lax.broadcasted_iota(jnp.int32, (blk, win), 0)
        band = (diff >= 0) & (diff <= 2 * halo)
        first_half = lax.broadcasted_iota(jnp.int32, (blk, LANES), 1) < HEAD_DIM

        pair = A_QBLOCK // blk

        def residue_body(rp, carry, q=q, kc=kc, kp=kp, kn=kn, vc=vc, vp=vp, vn=vn, d=d, n=n, nblk=nblk,
                         ld=ld, blk=blk, win=win, col=col, band=band, first_half=first_half,
                         first=(gi == first_group), last=(gi == 0), pair=pair, tail=tail):
            for u in range(pair):
                r = rp * pair + u
                base = u * win
                kext[base:base + halo, :] = kp[0, r]
                kext[base + halo:base + halo + n, :] = kc[0, r]
                kext[base + halo + n:base + 2 * halo + n, :] = kn[0, r]
                vext[base:base + halo, :] = vp[0, r]
                vext[base + halo:base + halo + n, :] = vc[0, r]
                vext[base + halo + n:base + 2 * halo + n, :] = vn[0, r]
                if tail:
                    kext[base + 2 * halo + n:base + win, :] = jnp.zeros((tail, kext.shape[1]), BF16)
                    vext[base + 2 * halo + n:base + win, :] = jnp.zeros((tail, vext.shape[1]), BF16)

            def block_body(j, carry2):
                j0 = pl.multiple_of(j * blk, blk)
                qbase = t * n + j * blk
                valid = band & (col >= halo - qbase) & (col < ld - qbase + halo)
                blocks = [(q[0, rp * pair + u, pl.ds(j0, blk), :],
                           kext[pl.ds(u * win + j0, win), :],
                           vext[pl.ds(u * win + j0, win), :]) for u in range(pair)]
                results = _attend_heads(blocks, valid, first_half)
                for u, parts in enumerate(results):
                    if d == 1:
                        rows = pl.ds(j0, blk)
                    else:
                        rows = pl.ds(j * (blk * d) + rp * pair + u, blk, stride=d)
                    for hp, (o_b, m_b, l_b) in enumerate(parts):
                        if first:
                            acc_o[hp, rows, :] = o_b
                            acc_l[hp, rows, :] = l_b
                            acc_m[hp, rows, :] = m_b
                        else:
                            m_old = acc_m[hp, rows, :]
                            m_new = jnp.maximum(m_old, m_b)
                            a_old = jnp.exp(m_old - m_new)
                            a_new = jnp.exp(m_b - m_new)
                            o_new = acc_o[hp, rows, :] * a_old + o_b * a_new
                            l_new = acc_l[hp, rows, :] * a_old + l_b * a_new
                            if last:
                                o_ref[0, rows, hp * LANES:(hp + 1) * LANES] = (o_new / l_new).astype(BF16)
                            else:
                                acc_o[hp, rows, :] = o_new
                                acc_l[hp, rows, :] = l_new
                                acc_m[hp, rows, :] = m_new
                return carry2

            lax.fori_loop(0, nblk, block_body, 0)
            return carry

        lax.fori_loop(0, d // pair, residue_body, 0)


def _mixer_a(qkv, seq):
    b = qkv[0].shape[0]
    tq = A_TILE
    blk = A_BLOCK
    aw = A_STEP_WIDTH
    args, in_specs = [], []
    for gi, d in enumerate(DILATIONS):
        q, k, v = qkv[3 * gi:3 * gi + 3]
        n = tq // d
        nblk = n // blk
        last = seq // d // blk - 1
        cur = pl.BlockSpec((1, d, n, aw), lambda bb, t, hh: (bb, 0, t, hh))
        prev = pl.BlockSpec((1, d, blk, aw),
                            lambda bb, t, hh, nblk=nblk: (bb, 0, jnp.maximum(t * nblk - 1, 0), hh))
        nxt = pl.BlockSpec((1, d, blk, aw),
                           lambda bb, t, hh, nblk=nblk, last=last: (bb, 0, jnp.minimum((t + 1) * nblk, last), hh))
        args += [q, k, k, k, v, v, v]
        in_specs += [cur, cur, prev, nxt, cur, prev, nxt]
    return pl.pallas_call(
        functools.partial(_mixer_a_kernel, seq=seq),
        grid=(b, seq // tq, A_WIDTH // aw),
        in_specs=in_specs,
        out_specs=pl.BlockSpec((1, tq, aw), lambda bb, t, hh: (bb, t, hh)),
        out_shape=jax.ShapeDtypeStruct((b, seq, A_WIDTH), BF16),
        scratch_shapes=[pltpu.VMEM((tq + 2 * blk, aw), BF16), pltpu.VMEM((tq + 2 * blk, aw), BF16),
                        pltpu.VMEM((aw // LANES, tq, LANES), F32),
                        pltpu.VMEM((aw // LANES, tq, LANES), F32),
                        pltpu.VMEM((aw // LANES, tq, LANES), F32)],
        compiler_params=pltpu.CompilerParams(
            dimension_semantics=("arbitrary", "arbitrary", "arbitrary"), vmem_limit_bytes=VMEM_LIMIT),
        name="mixer_a",
    )(*args)


def _mixer_b_kernel(sink_ref, q, kc, kp, kn, vc, vp, vn, o_ref, kext, vext, *, seq):
    t = pl.program_id(1)
    tq = ATT_TILE
    blk = B_BLOCK
    win = 3 * blk
    kext[0:blk, :] = kp[0]
    kext[blk:blk + tq, :] = kc[0]
    kext[blk + tq:2 * blk + tq, :] = kn[0]
    vext[0:blk, :] = vp[0]
    vext[blk:blk + tq, :] = vc[0]
    vext[blk + tq:2 * blk + tq, :] = vn[0]
    col = lax.broadcasted_iota(jnp.int32, (blk, win), 1)
    row = lax.broadcasted_iota(jnp.int32, (blk, win), 0)
    diff = col - row
    first_half = lax.broadcasted_iota(jnp.int32, (blk, LANES), 1) < HEAD_DIM
    second_half = jnp.logical_not(first_half)

    def block_body(j, carry):
        j0 = pl.multiple_of(j * blk, blk)
        qb = q[0, pl.ds(j0, blk), :]
        kw = kext[pl.ds(j0, win), :]
        vw = vext[pl.ds(j0, win), :]
        qbase = t * tq + j * blk
        valid = ((diff >= 0) & (diff <= 2 * B_WINDOW)
                 & (col >= blk - qbase) & (col < seq - qbase + blk))
        scores, sinks = [], []
        for c in range(B_Q_HEADS // 2):
            qp = qb[:, c * LANES:(c + 1) * LANES]
            for g, keep in enumerate((first_half, second_half)):
                qm = jnp.where(keep, qp, jnp.zeros_like(qp))
                scores.append(lax.dot_general(qm, kw, (((1,), (1,)), ((), ())), preferred_element_type=F32))
                sinks.append(jnp.full((1, LANES), sink_ref[g * B_REP + c], F32))
        sc = jnp.where(valid[None], jnp.stack(scores), NEG_BIG)
        sk = jnp.stack(sinks)
        part = sc[..., 0:LANES]
        for k in range(1, win // LANES):
            part = jnp.maximum(part, sc[..., k * LANES:(k + 1) * LANES])
        m = jnp.max(jnp.maximum(part, sk), axis=-1, keepdims=True)
        pb = jnp.exp((sc - m).astype(BF16))
        sink_term = jnp.exp(sk - m)
        v1 = jnp.concatenate([vw, jnp.ones((win, LANES), BF16)], axis=1)
        parts = []
        for c in range(B_Q_HEADS // 2):
            r0 = jnp.dot(pb[2 * c], v1, preferred_element_type=F32)
            r1 = jnp.dot(pb[2 * c + 1], v1, preferred_element_type=F32)
            o0 = r0[:, :LANES] / (r0[:, LANES:] + sink_term[2 * c])
            o1 = r1[:, :LANES] / (r1[:, LANES:] + sink_term[2 * c + 1])
            parts.append(jnp.where(first_half, o0, o1))
        o_ref[0, pl.ds(j0, blk), :] = jnp.concatenate(parts, axis=1).astype(BF16)
        return carry

    lax.fori_loop(0, tq // blk, block_body, 0)


def _mixer_b(q, k, v, sink, seq):
    b = q.shape[0]
    tq = ATT_TILE
    blk = B_BLOCK
    nblk = tq // blk
    last = seq // blk - 1
    cur = pl.BlockSpec((1, tq, B_KV_WIDTH), lambda bb, t: (bb, t, 0))
    prev = pl.BlockSpec((1, blk, B_KV_WIDTH), lambda bb, t: (bb, jnp.maximum(t * nblk - 1, 0), 0))
    nxt = pl.BlockSpec((1, blk, B_KV_WIDTH), lambda bb, t: (bb, jnp.minimum((t + 1) * nblk, last), 0))
    return pl.pallas_call(
        functools.partial(_mixer_b_kernel, seq=seq),
        grid=(b, seq // tq),
        in_specs=[pl.BlockSpec(memory_space=pltpu.SMEM),
                  pl.BlockSpec((1, tq, B_Q_WIDTH), lambda bb, t: (bb, t, 0)),
                  cur, prev, nxt, cur, prev, nxt],
        out_specs=pl.BlockSpec((1, tq, B_Q_WIDTH), lambda bb, t: (bb, t, 0)),
        out_shape=jax.ShapeDtypeStruct((b, seq, B_Q_WIDTH), BF16),
        scratch_shapes=[pltpu.VMEM((tq + 2 * blk, B_KV_WIDTH), BF16),
                        pltpu.VMEM((tq + 2 * blk, B_KV_WIDTH), BF16)],
        compiler_params=pltpu.CompilerParams(
            dimension_semantics=("arbitrary", "arbitrary"), vmem_limit_bytes=VMEM_LIMIT),
        name="mixer_b",
    )(sink, q, k, k, k, v, v, v)


def _post_attn_kernel(x_ref, oa_ref, ob_ref, gmix_ref, wg_ref, wa_ref, wb_ref, wo_ref, gmoe_ref,
                      wr_ref, br_ref, x1_ref, h2_ref, gate_ref):
    wr = wr_ref[...]
    wr_hi = wr.astype(BF16)
    wr_lo = (wr - wr_hi.astype(F32)).astype(BF16)
    wr3 = jnp.concatenate([wr_hi, wr_hi, wr_lo], axis=0)
    for s in range(x_ref.shape[0] // POST_SLAB):
        rows = slice(s * POST_SLAB, (s + 1) * POST_SLAB)
        _post_attn_rows(x_ref, oa_ref, ob_ref, gmix_ref, wg_ref, wa_ref, wb_ref, wo_ref, gmoe_ref, wr3,
                        br_ref, x1_ref, h2_ref, gate_ref, rows)


def _post_attn_rows(x_ref, oa_ref, ob_ref, gmix_ref, wg_ref, wa_ref, wb_ref, wo_ref, gmoe_ref, wr3,
                    br_ref, x1_ref, h2_ref, gate_ref, rows):
    x = x_ref[rows, :]
    h = _rms(x, gmix_ref[...]).astype(BF16)
    gates = jax.nn.sigmoid(jnp.dot(h, wg_ref[...], preferred_element_type=F32))
    ya = jnp.dot(oa_ref[rows, :], wa_ref[...], preferred_element_type=F32)
    yb = jnp.dot(ob_ref[rows, :], wb_ref[...], preferred_element_type=F32)
    mix = gates[:, :D_MODEL] * ya + gates[:, D_MODEL:] * yb
    x1 = x + jnp.dot(mix.astype(BF16), wo_ref[...], preferred_element_type=F32)
    x1_ref[rows, :] = x1
    h2 = _rms(x1, gmoe_ref[...])
    h2_hi = h2.astype(BF16)
    h2_ref[rows, :] = h2_hi

    h2_lo = (h2 - h2_hi.astype(F32)).astype(BF16)
    logits = jnp.dot(jnp.concatenate([h2_hi, h2_lo, h2_hi], axis=1), wr3,
                     preferred_element_type=F32) + br_ref[...]
    tm = logits.shape[0]
    lane = lax.broadcasted_iota(jnp.int32, (tm, LANES), 1)
    big = jnp.int32(LANES)
    is_group = (lane >= N_EXPERTS) & (lane < N_EXPERTS + N_GROUPS)
    gl = jnp.where(is_group, logits, NEG_BIG)
    gmax = jnp.max(gl, axis=-1, keepdims=True)
    g_idx = jnp.min(jnp.where(gl == gmax, lane, big), axis=-1, keepdims=True) - N_EXPERTS
    g_w = 1.0 / jnp.sum(jnp.exp(gl - gmax), axis=-1, keepdims=True)
    in_group = (lane < N_EXPERTS) & (lax.shift_right_logical(lane, 3) == g_idx)
    sel = jnp.where(in_group, logits, NEG_BIG)
    v1 = jnp.max(sel, axis=-1, keepdims=True)
    i1 = jnp.min(jnp.where(sel == v1, lane, big), axis=-1, keepdims=True)
    sel2 = jnp.where(lane == i1, NEG_BIG, sel)
    v2 = jnp.max(sel2, axis=-1, keepdims=True)
    i2 = jnp.min(jnp.where(sel2 == v2, lane, big), axis=-1, keepdims=True)
    e2 = jnp.exp(v2 - v1)
    w1 = g_w / (1.0 + e2)
    w2 = g_w * e2 / (1.0 + e2)
    gate_ref[rows, :] = jnp.where(lane == i1, w1, 0.0) + jnp.where(lane == i2, w2, 0.0)


def _post_attn(x2d, oa, ob, norm_mix, w_gate, w_a, w_b, w_out, norm_moe, w_router, b_router):
    t = x2d.shape[0]
    tm = POST_TILE
    const = lambda i: (0, 0)
    row = lambda w: pl.BlockSpec((tm, w), lambda i: (i, 0))
    full = lambda a: pl.BlockSpec(a.shape, const, pipeline_mode=pl.Buffered(1))
    return pl.pallas_call(
        _post_attn_kernel,
        grid=(t // tm,),
        in_specs=[row(D_MODEL), row(A_WIDTH), row(B_Q_WIDTH), full(norm_mix), full(w_gate), full(w_a),
                  full(w_b), full(w_out), full(norm_moe), full(w_router), full(b_router)],
        out_specs=[row(D_MODEL), row(D_MODEL), row(LANES)],
        out_shape=[jax.ShapeDtypeStruct((t, D_MODEL), F32), jax.ShapeDtypeStruct((t, D_MODEL), BF16),
                   jax.ShapeDtypeStruct((t, LANES), F32)],
        compiler_params=pltpu.CompilerParams(
            dimension_semantics=("arbitrary",), vmem_limit_bytes=VMEM_LIMIT),
        name="post_attn",
    )(x2d, oa, ob, norm_mix, w_gate, w_a, w_b, w_out, norm_moe, w_router, b_router)


def _moe_final_kernel(x1_ref, h2_ref, gate_ref, tri_ref, w1_ref, w3_ref, w2_ref, gf_ref, o_ref,
                      xs, stage, out_stage, seg_s, slo_s, shi_s, wlo_s, whi_s):
    step = pl.program_id(1)
    first_step = (pl.program_id(0) == 0) & (step == 0)
    n_sub = x1_ref.shape[0] // MOE_SUB

    @pl.when(first_step)
    def _init():
        xs[n_sub * MOE_SUB_SLOTS:, :] = jnp.zeros((MOE_CHUNK, D_MODEL), BF16)
        stage[...] = jnp.zeros(stage.shape, BF16)

    @pl.when(step == 0)
    def _sort():
        r_i = lax.broadcasted_iota(jnp.int32, (LANES, LANES), 0)
        c_i = lax.broadcasted_iota(jnp.int32, (LANES, LANES), 1)
        upper = jnp.where(r_i < c_i, 1.0, 0.0)
        col = lax.broadcasted_iota(jnp.int32, (MOE_SUB, MOE_SUB_SLOTS), 1)
        for s in range(n_sub):
            rows = slice(s * MOE_SUB, (s + 1) * MOE_SUB)
            g = gate_ref[rows, :]
            hit = g > 0.0
            onehot = jnp.where(hit, 1.0, 0.0)
            count = jnp.sum(onehot, axis=0, keepdims=True)
            padded = jnp.ceil(count * (1.0 / SEG_ALIGN)) * SEG_ALIGN
            seg = jnp.dot(jnp.broadcast_to(padded, (8, LANES)), upper, preferred_element_type=F32,
                          precision=lax.Precision.HIGHEST)[0:1]
            seg_i = seg.astype(jnp.int32)
            padded_i = padded.astype(jnp.int32)
            for ex in range(N_EXPERTS):
                seg_s[2 * s, ex] = seg_i[0, ex] + s * MOE_SUB_SLOTS
                seg_s[2 * s + 1, ex] = padded_i[0, ex]
            rank = jnp.dot(tri_ref[...], onehot.astype(BF16), preferred_element_type=F32)
            slot = seg + rank
            s_lo = jnp.min(jnp.where(hit, slot, 1e9), axis=-1, keepdims=True)
            s_hi = jnp.max(jnp.where(hit, slot, -1.0), axis=-1, keepdims=True)
            wlo_s[rows, :] = jnp.sum(jnp.where(hit & (slot == s_lo), g, 0.0), axis=-1, keepdims=True)
            whi_s[rows, :] = jnp.sum(jnp.where(hit & (slot == s_hi) & (s_hi > s_lo), g, 0.0),
                                     axis=-1, keepdims=True)
            s_lo_i = s_lo.astype(jnp.int32)
            s_hi_i = s_hi.astype(jnp.int32)
            slo_s[rows, :] = s_lo_i
            shi_s[rows, :] = s_hi_i
            perm_t = jnp.where((col == s_lo_i) | (col == s_hi_i), 1.0, 0.0).astype(BF16)
            sorted_rows = lax.dot_general(perm_t, h2_ref[rows, :], (((0,), (0,)), ((), ())),
                                          preferred_element_type=F32)
            xs[s * MOE_SUB_SLOTS:(s + 1) * MOE_SUB_SLOTS, :] = sorted_rows.astype(BF16)

    def ffn(x, j):
        a = jnp.dot(x, w1_ref[j], preferred_element_type=F32)
        b = jnp.dot(x, w3_ref[j], preferred_element_type=F32)
        hid = (a * jax.nn.sigmoid(a)) * b
        return jnp.dot(hid.astype(BF16), w2_ref[j], preferred_element_type=F32).astype(BF16)

    bases, totals, seg_starts, seg_lens, seg_dsts = [], [], [], [], []
    base = jnp.int32(0)
    for j in range(MOE_ESTEP):
        e = step * MOE_ESTEP + j
        starts = [seg_s[2 * s, e] for s in range(n_sub)]
        lens = [seg_s[2 * s + 1, e] for s in range(n_sub)]
        dst = base
        dsts = []
        for s in range(n_sub):
            dsts.append(dst)
            dst = dst + lens[s]
        bases.append(base)
        totals.append(dst - base)
        seg_starts.append(starts)
        seg_lens.append(lens)
        seg_dsts.append(dsts)
        base = base + ((dst - base + MOE_CHUNK - 1) // MOE_CHUNK) * MOE_CHUNK

    for j in range(MOE_ESTEP):
        for s in range(n_sub):
            def gather_body(k, carry, src0=seg_starts[j][s], dst0=seg_dsts[j][s]):
                src = pl.multiple_of(src0 + k * MOE_COPY, SEG_ALIGN)
                to = pl.multiple_of(dst0 + k * MOE_COPY, SEG_ALIGN)
                stage[pl.ds(to, MOE_COPY), :] = xs[pl.ds(src, MOE_COPY), :]
                return carry

            lax.fori_loop(1, (seg_lens[j][s] + MOE_COPY - 1) // MOE_COPY, gather_body, 0)
    for j in range(MOE_ESTEP):
        for s in range(n_sub):
            stage[pl.ds(pl.multiple_of(seg_dsts[j][s], SEG_ALIGN), MOE_COPY), :] = \
                xs[pl.ds(pl.multiple_of(seg_starts[j][s], SEG_ALIGN), MOE_COPY), :]

    for j in range(MOE_ESTEP):
        def extra_body(k, carry, j=j):
            r0 = pl.multiple_of(bases[j] + k * MOE_CHUNK, MOE_CHUNK)
            out_stage[pl.ds(r0, MOE_CHUNK), :] = ffn(stage[pl.ds(r0, MOE_CHUNK), :], j)
            return carry

        lax.fori_loop(1, (totals[j] + MOE_CHUNK - 1) // MOE_CHUNK, extra_body, 0)

    firsts = [ffn(stage[pl.ds(pl.multiple_of(bases[j], MOE_CHUNK), MOE_CHUNK), :], j) for j in range(MOE_ESTEP)]
    for j in range(MOE_ESTEP):
        out_stage[pl.ds(pl.multiple_of(bases[j], MOE_CHUNK), MOE_CHUNK), :] = firsts[j]

    for j in range(MOE_ESTEP):
        for s in range(n_sub):
            def scatter_body(k, carry, src0=seg_dsts[j][s], dst0=seg_starts[j][s]):
                src = pl.multiple_of(src0 + k * SEG_ALIGN, SEG_ALIGN)
                to = pl.multiple_of(dst0 + k * SEG_ALIGN, SEG_ALIGN)
                xs[pl.ds(to, SEG_ALIGN), :] = out_stage[pl.ds(src, SEG_ALIGN), :]
                return carry

            lax.fori_loop(0, seg_lens[j][s] // SEG_ALIGN, scatter_body, 0)

    @pl.when(step == N_EXPERTS // MOE_ESTEP - 1)
    def _combine():
        col = lax.broadcasted_iota(jnp.int32, (MOE_SUB, MOE_SUB_SLOTS), 1)
        for s in range(n_sub):
            rows = slice(s * MOE_SUB, (s + 1) * MOE_SUB)
            wperm = (jnp.where(col == slo_s[rows, :], wlo_s[rows, :], 0.0)
                     + jnp.where(col == shi_s[rows, :], whi_s[rows, :], 0.0)).astype(BF16)
            moe = jnp.dot(wperm, xs[s * MOE_SUB_SLOTS:(s + 1) * MOE_SUB_SLOTS, :], preferred_element_type=F32)
            o_ref[rows, :] = _rms(x1_ref[rows, :] + moe, gf_ref[...])


def _moe_final(x1, h2, gate, w1, w3, w2, norm_final):
    t = x1.shape[0]
    tm = MOE_TILE
    n_sub = tm // MOE_SUB
    row = lambda w: pl.BlockSpec((tm, w), lambda i, e: (i, 0))
    tri = jnp.tril(jnp.ones((MOE_SUB, MOE_SUB), BF16), -1)
    return pl.pallas_call(
        _moe_final_kernel,
        grid=(t // tm, N_EXPERTS // MOE_ESTEP),
        in_specs=[row(D_MODEL), row(D_MODEL), row(LANES),
                  pl.BlockSpec((MOE_SUB, MOE_SUB), lambda i, e: (0, 0), pipeline_mode=pl.Buffered(1)),
                  pl.BlockSpec((MOE_ESTEP, D_MODEL, D_FF), lambda i, e: (e, 0, 0)),
                  pl.BlockSpec((MOE_ESTEP, D_MODEL, D_FF), lambda i, e: (e, 0, 0)),
                  pl.BlockSpec((MOE_ESTEP, D_FF, D_MODEL), lambda i, e: (e, 0, 0)),
                  pl.BlockSpec((1, D_MODEL), lambda i, e: (0, 0))],
        out_specs=row(D_MODEL),
        out_shape=jax.ShapeDtypeStruct((t, D_MODEL), F32),
        scratch_shapes=[pltpu.VMEM((n_sub * MOE_SUB_SLOTS + MOE_CHUNK, D_MODEL), BF16),
                        pltpu.VMEM((2 * tm + (MOE_ESTEP + 1) * MOE_CHUNK, D_MODEL), BF16),
                        pltpu.VMEM((2 * tm + (MOE_ESTEP + 1) * MOE_CHUNK, D_MODEL), BF16),
                        pltpu.SMEM((2 * n_sub, N_EXPERTS), jnp.int32),
                        pltpu.VMEM((tm, 1), jnp.int32), pltpu.VMEM((tm, 1), jnp.int32),
                        pltpu.VMEM((tm, 1), F32), pltpu.VMEM((tm, 1), F32)],
        compiler_params=pltpu.CompilerParams(
            dimension_semantics=("arbitrary", "arbitrary"), vmem_limit_bytes=VMEM_LIMIT),
        name="moe_final",
    )(x1, h2, gate, tri, w1, w3, w2, norm_final)


def _prep_weights(norm_mix, w_in, attn_sink, w_branch_a, w_branch_b, w_gate, w_out, norm_moe,
                  w_router_group, b_router_group, w_router_expert, b_router_expert, w1, w3, w2, norm_final):
    wi = w_in[0]
    a_w = 3 * A_WIDTH

    def group_cols(gi):
        return jnp.concatenate([wi[:, role * a_w + gi * A_WIDTH: role * a_w + (gi + 1) * A_WIDTH]
                                for role in range(3)], axis=1)

    b_heads = [g * B_REP + c for c in range(B_REP) for g in range(B_KV_HEADS)]
    bq_cols = jnp.concatenate([wi[:, 3 * a_w + h * HEAD_DIM: 3 * a_w + (h + 1) * HEAD_DIM] for h in b_heads],
                              axis=1)
    w_b_rows = jnp.concatenate([w_branch_b[0][h * HEAD_DIM:(h + 1) * HEAD_DIM] for h in b_heads], axis=0)
    w_nat = jnp.concatenate([group_cols(0), bq_cols, wi[:, 3 * a_w + B_Q_WIDTH:]], axis=1).astype(BF16)
    w_g1 = group_cols(1).astype(BF16)
    w_g2 = group_cols(2).astype(BF16)
    pad = LANES - N_EXPERTS - N_GROUPS
    w_router = jnp.concatenate([w_router_expert[0], w_router_group[0],
                                jnp.zeros((D_MODEL, pad), F32)], axis=1)
    b_router = jnp.concatenate([b_router_expert[0].reshape(-1), b_router_group[0],
                                jnp.zeros((pad,), F32)]).reshape(1, LANES)
    return dict(
        norm_mix=norm_mix[0].reshape(1, D_MODEL), w_nat=w_nat, w_g1=w_g1, w_g2=w_g2, sink=attn_sink[0],
        w_a=w_branch_a[0].astype(BF16), w_b=w_b_rows.astype(BF16), w_gate=w_gate[0].astype(BF16),
        w_out=w_out[0].astype(BF16), norm_moe=norm_moe[0].reshape(1, D_MODEL),
        w_router=w_router, b_router=b_router,
        w1=w1[0].astype(BF16), w3=w3[0].astype(BF16), w2=w2[0].astype(BF16),
        norm_final=norm_final.reshape(1, D_MODEL))


def _trunk(x, p):
    b, s, _ = x.shape
    outs = _in_proj(x, p["norm_mix"], p["w_nat"], p["w_g1"], p["w_g2"], p["cos"], p["sin"])
    oa = _mixer_a(outs[:9], s)
    ob = _mixer_b(outs[9], outs[10], outs[11], p["sink"], s)
    t = b * s
    x1, h2, gate = _post_attn(x.reshape(t, D_MODEL), oa.reshape(t, A_WIDTH), ob.reshape(t, B_Q_WIDTH),
                              p["norm_mix"], p["w_gate"], p["w_a"], p["w_b"], p["w_out"], p["norm_moe"],
                              p["w_router"], p["b_router"])
    y = _moe_final(x1, h2, gate, p["w1"], p["w3"], p["w2"], p["norm_final"])
    return y.reshape(b, s, D_MODEL)


def kernel(x_prompt, x_sample, norm_mix, w_in, attn_sink, w_branch_a, w_branch_b, w_gate, w_out, norm_moe,
           w_router_group, b_router_group, w_router_expert, b_router_expert, w1, w3, w2, norm_final):
    p = _prep_weights(norm_mix, w_in, attn_sink, w_branch_a, w_branch_b, w_gate, w_out, norm_moe,
                      w_router_group, b_router_group, w_router_expert, b_router_expert, w1, w3, w2, norm_final)
    p["cos"], p["sin"] = _rope_tables(max(x_prompt.shape[1], x_sample.shape[1]))
    return (_trunk(x_prompt, p), _trunk(x_sample, p))
```

```python
import functools

import jax
import jax.numpy as jnp
from jax import lax
from jax.experimental import pallas as pl
from jax.experimental.pallas import tpu as pltpu

D_MODEL = 1024
HEAD_DIM = 64
ROT_DIM = HEAD_DIM // 4
ROT_HALF = ROT_DIM // 2
ROPE_THETA = 500000.0
DILATIONS = (1, 4, 16)
A_HALF = 64
A_BLOCK = 64
A_QBLOCK = 128
A_HEADS = 8
A_WIDTH = A_HEADS * HEAD_DIM
B_Q_HEADS = 8
B_KV_HEADS = 2
B_REP = B_Q_HEADS // B_KV_HEADS
B_WINDOW = 128
B_BLOCK = 128
B_Q_WIDTH = B_Q_HEADS * HEAD_DIM
B_KV_WIDTH = B_KV_HEADS * HEAD_DIM
N_GROUPS = 4
EXPERTS_PER_GROUP = 8
N_EXPERTS = N_GROUPS * EXPERTS_PER_GROUP
D_FF = 256
RMS_EPS = 1e-6
Q_SCALE = HEAD_DIM ** -0.5

LANES = 128
NEG_BIG = -1e30
VMEM_LIMIT = 58 * 1024 * 1024

IN_TILE = 512
ATT_TILE = 1024
A_TILE = 1024
A_STEP_WIDTH = 512
POST_TILE = 1024
POST_SLAB = 256
MOE_TILE = 1024
SEG_ALIGN = 16
MOE_CHUNK = 128
MOE_COPY = 32
MOE_SUB = 256
MOE_SUB_SLOTS = 2 * MOE_SUB + N_EXPERTS * SEG_ALIGN
MOE_SUB_SLOTS_COMMON = 768
MOE_ESTEP = 4

F32 = jnp.float32
BF16 = jnp.bfloat16


def _rms(x, g):
    ms = jnp.mean(x * x, axis=-1, keepdims=True)
    return x * lax.rsqrt(ms + RMS_EPS) * g


def _rope(a, cos, sin, low_half):
    outs = []
    for k in range(a.shape[1] // LANES):
        ak = a[:, k * LANES:(k + 1) * LANES]
        up = pltpu.roll(ak, LANES - ROT_HALF, axis=1)
        dn = pltpu.roll(ak, ROT_HALF, axis=1)
        outs.append(ak * cos + jnp.where(low_half, up, dn) * sin)
    return outs[0] if len(outs) == 1 else jnp.concatenate(outs, axis=1)


def _in_proj_kernel(x_ref, g_ref, cos_ref, sin_ref, wn_ref, w1_ref, w2_ref,
                    q0, k0, v0, q1, k1, v1, q2, k2, v2, bq, bk, bv,
                    xs, h1_s, h2_s, c1_s, s1_s, c2_s, s2_s):
    tm = x_ref.shape[1]
    n_chunks = D_MODEL // LANES
    for c in range(n_chunks):
        xs[c] = x_ref[0, :, c * LANES:(c + 1) * LANES]
    g = g_ref[...]
    low_half = (lax.broadcasted_iota(jnp.int32, (tm, LANES), 1) % HEAD_DIM) < ROT_HALF

    def mm(h, w_ref, c0, n):
        return jnp.dot(h, w_ref[:, c0:c0 + n], preferred_element_type=F32)

    h0 = _rms(x_ref[0], g).astype(BF16)
    cos0, sin0 = cos_ref[...], sin_ref[...]
    q0[0, 0] = (_rope(mm(h0, wn_ref, 0, A_WIDTH), cos0, sin0, low_half) * Q_SCALE).astype(BF16)
    k0[0, 0] = _rope(mm(h0, wn_ref, A_WIDTH, A_WIDTH), cos0, sin0, low_half).astype(BF16)
    v0[0, 0] = mm(h0, wn_ref, 2 * A_WIDTH, A_WIDTH).astype(BF16)
    c = 3 * A_WIDTH
    bq[0] = (_rope(mm(h0, wn_ref, c, B_Q_WIDTH), cos0, sin0, low_half) * Q_SCALE).astype(BF16)
    bk[0] = _rope(mm(h0, wn_ref, c + B_Q_WIDTH, B_KV_WIDTH), cos0, sin0, low_half).astype(BF16)
    bv[0] = mm(h0, wn_ref, c + B_Q_WIDTH + B_KV_WIDTH, B_KV_WIDTH).astype(BF16)

    for d, hs, cs, ss, w_ref, qo, ko, vo in ((DILATIONS[1], h1_s, c1_s, s1_s, w1_ref, q1, k1, v1),
                                             (DILATIONS[2], h2_s, c2_s, s2_s, w2_ref, q2, k2, v2)):
        n = tm // d
        for r in range(d):
            rows = pl.ds(r, n, stride=d)
            xr = jnp.concatenate([xs[c, rows, :] for c in range(n_chunks)], axis=1)
            hs[r * n:(r + 1) * n, :] = _rms(xr, g).astype(BF16)
            cs[r * n:(r + 1) * n, :] = cos_ref[rows, :]
            ss[r * n:(r + 1) * n, :] = sin_ref[rows, :]
        h = hs[...]
        cos, sin = cs[...], ss[...]
        qv = (_rope(mm(h, w_ref, 0, A_WIDTH), cos, sin, low_half) * Q_SCALE).astype(BF16)
        kv = _rope(mm(h, w_ref, A_WIDTH, A_WIDTH), cos, sin, low_half).astype(BF16)
        vv = mm(h, w_ref, 2 * A_WIDTH, A_WIDTH).astype(BF16)
        for r in range(d):
            qo[0, r] = qv[r * n:(r + 1) * n]
            ko[0, r] = kv[r * n:(r + 1) * n]
            vo[0, r] = vv[r * n:(r + 1) * n]


def _rope_tables(s):
    inv = jnp.power(jnp.float32(ROPE_THETA), -jnp.arange(ROT_HALF, dtype=F32) / ROT_HALF)
    ang = jnp.arange(s).astype(F32)[:, None] * inv[None, :]
    cos, sin = jnp.cos(ang), jnp.sin(ang)
    src = jnp.arange(LANES) % HEAD_DIM
    spread = (jnp.arange(ROT_DIM)[:, None] == src[None, :]).astype(F32)
    hi = lax.Precision.HIGHEST
    cos_t = jnp.dot(jnp.concatenate([cos, cos], axis=1), spread, precision=hi) + (src >= ROT_DIM).astype(F32)
    sin_t = jnp.dot(jnp.concatenate([-sin, sin], axis=1), spread, precision=hi)
    return cos_t, sin_t


def _in_proj(x, norm_g, w_nat, w_g1, w_g2, cos_t, sin_t):
    b, s, _ = x.shape
    tm = IN_TILE
    const = lambda t, bb: (0, 0)
    one_buf = pl.Buffered(1)
    in_specs = [
        pl.BlockSpec((1, tm, D_MODEL), lambda t, bb: (bb, t, 0)),
        pl.BlockSpec((1, D_MODEL), const),
        pl.BlockSpec((tm, LANES), lambda t, bb: (t, 0)),
        pl.BlockSpec((tm, LANES), lambda t, bb: (t, 0)),
        pl.BlockSpec(w_nat.shape, const, pipeline_mode=one_buf),
        pl.BlockSpec(w_g1.shape, const, pipeline_mode=one_buf),
        pl.BlockSpec(w_g2.shape, const, pipeline_mode=one_buf),
    ]
    out_shape, out_specs = [], []
    for d in DILATIONS:
        for _ in range(3):
            out_shape.append(jax.ShapeDtypeStruct((b, d, s // d, A_WIDTH), BF16))
            out_specs.append(pl.BlockSpec((1, d, tm // d, A_WIDTH), lambda t, bb: (bb, 0, t, 0)))
    for w in (B_Q_WIDTH, B_KV_WIDTH, B_KV_WIDTH):
        out_shape.append(jax.ShapeDtypeStruct((b, s, w), BF16))
        out_specs.append(pl.BlockSpec((1, tm, w), lambda t, bb: (bb, t, 0)))
    scratch = [pltpu.VMEM((D_MODEL // LANES, tm, LANES), F32),
               pltpu.VMEM((tm, D_MODEL), BF16), pltpu.VMEM((tm, D_MODEL), BF16)] + \
              [pltpu.VMEM((tm, LANES), F32) for _ in range(4)]
    return pl.pallas_call(
        _in_proj_kernel,
        grid=(s // tm, b),
        in_specs=in_specs,
        out_specs=out_specs,
        out_shape=out_shape,
        scratch_shapes=scratch,
        compiler_params=pltpu.CompilerParams(
            dimension_semantics=("arbitrary", "arbitrary"), vmem_limit_bytes=VMEM_LIMIT),
        name="in_proj",
    )(x, norm_g.reshape(1, D_MODEL), cos_t, sin_t, w_nat, w_g1, w_g2)


def _attend_heads(blocks, valid, first_half):
    second_half = jnp.logical_not(first_half)
    n_heads = blocks[0][0].shape[1] // HEAD_DIM
    scores = []
    for qb, kw, _ in blocks:
        for h in range(n_heads):
            cs = slice((h // 2) * LANES, (h // 2 + 1) * LANES)
            qp = qb[:, cs]
            qm = jnp.where(first_half if h % 2 == 0 else second_half, qp, jnp.zeros_like(qp))
            scores.append(lax.dot_general(qm, kw[:, cs], (((1,), (1,)), ((), ())),
                                          preferred_element_type=F32))
    sc = jnp.where(valid[None], jnp.stack(scores), NEG_BIG)
    m = jnp.max(sc, axis=-1, keepdims=True)
    p = jnp.exp(sc - m)
    l = jnp.sum(p, axis=-1, keepdims=True)
    pb = p.astype(BF16)
    results = []
    for bi, (_, _, vw) in enumerate(blocks):
        out = []
        for hp in range(n_heads // 2):
            i0 = bi * n_heads + 2 * hp
            vp = vw[:, hp * LANES:(hp + 1) * LANES]
            o0 = jnp.dot(pb[i0], vp, preferred_element_type=F32)
            o1 = jnp.dot(pb[i0 + 1], vp, preferred_element_type=F32)
            out.append((jnp.where(first_half, o0, o1), jnp.where(first_half, m[i0], m[i0 + 1]),
                        jnp.where(first_half, l[i0], l[i0 + 1])))
        results.append(out)
    return results


def _mixer_a_kernel(*refs, seq):
    ins, (o_ref, kext, vext, acc_o, acc_m, acc_l) = refs[:21], refs[21:]
    t = pl.program_id(1)
    tq = A_TILE
    halo = A_HALF

    first_group = len(DILATIONS) - 1
    for gi in reversed(range(len(DILATIONS))):
        d = DILATIONS[gi]
        q, kc, kp, kn, vc, vp, vn = ins[7 * gi:7 * gi + 7]
        n = tq // d
        blk = min(n, A_QBLOCK)
        nblk = n // blk
        win = -(-(blk + 2 * halo) // LANES) * LANES
        tail = win - (blk + 2 * halo) if n == blk else 0
        ld = seq // d
        col = lax.broadcasted_iota(jnp.int32, (blk, win), 1)
        diff = col - lax.broadcasted_iota(jnp.int32, (blk, win), 0)
        band = (diff >= 0) & (diff <= 2 * halo)
        first_half = lax.broadcasted_iota(jnp.int32, (blk, LANES), 1) < HEAD_DIM

        pair = A_QBLOCK // blk

        def residue_body(rp, carry, q=q, kc=kc, kp=kp, kn=kn, vc=vc, vp=vp, vn=vn, d=d, n=n, nblk=nblk,
                         ld=ld, blk=blk, win=win, col=col, band=band, first_half=first_half,
                         first=(gi == first_group), last=(gi == 0), pair=pair, tail=tail):
            for u in range(pair):
                r = rp * pair + u
                base = u * win
                kext[base:base + halo, :] = kp[0, r]
                kext[base + halo:base + halo + n, :] = kc[0, r]
                kext[base + halo + n:base + 2 * halo + n, :] = kn[0, r]
                vext[base:base + halo, :] = vp[0, r]
                vext[base + halo:base + halo + n, :] = vc[0, r]
                vext[base + halo + n:base + 2 * halo + n, :] = vn[0, r]
                if tail:
                    kext[base + 2 * halo + n:base + win, :] = jnp.zeros((tail, kext.shape[1]), BF16)
                    vext[base + 2 * halo + n:base + win, :] = jnp.zeros((tail, vext.shape[1]), BF16)

            def block_body(j, carry2):
                j0 = pl.multiple_of(j * blk, blk)
                qbase = t * n + j * blk
                valid = band & (col >= halo - qbase) & (col < ld - qbase + halo)
                blocks = [(q[0, rp * pair + u, pl.ds(j0, blk), :],
                           kext[pl.ds(u * win + j0, win), :],
                           vext[pl.ds(u * win + j0, win), :]) for u in range(pair)]
                results = _attend_heads(blocks, valid, first_half)
                for u, parts in enumerate(results):
                    if d == 1:
                        rows = pl.ds(j0, blk)
                    else:
                        rows = pl.ds(j * (blk * d) + rp * pair + u, blk, stride=d)
                    for hp, (o_b, m_b, l_b) in enumerate(parts):
                        if first:
                            acc_o[hp, rows, :] = o_b
                            acc_l[hp, rows, :] = l_b
                            acc_m[hp, rows, :] = m_b
                        else:
                            m_old = acc_m[hp, rows, :]
                            m_new = jnp.maximum(m_old, m_b)
                            a_old = jnp.exp(m_old - m_new)
                            a_new = jnp.exp(m_b - m_new)
                            o_new = acc_o[hp, rows, :] * a_old + o_b * a_new
                            l_new = acc_l[hp, rows, :] * a_old + l_b * a_new
                            if last:
                                o_ref[0, rows, hp * LANES:(hp + 1) * LANES] = (o_new / l_new).astype(BF16)
                            else:
                                acc_o[hp, rows, :] = o_new
                                acc_l[hp, rows, :] = l_new
                                acc_m[hp, rows, :] = m_new
                return carry2

            lax.fori_loop(0, nblk, block_body, 0)
            return carry

        lax.fori_loop(0, d // pair, residue_body, 0)


def _mixer_a(qkv, seq):
    b = qkv[0].shape[0]
    tq = A_TILE
    blk = A_BLOCK
    aw = A_STEP_WIDTH
    args, in_specs = [], []
    for gi, d in enumerate(DILATIONS):
        q, k, v = qkv[3 * gi:3 * gi + 3]
        n = tq // d
        nblk = n // blk
        last = seq // d // blk - 1
        cur = pl.BlockSpec((1, d, n, aw), lambda bb, t, hh: (bb, 0, t, hh))
        prev = pl.BlockSpec((1, d, blk, aw),
                            lambda bb, t, hh, nblk=nblk: (bb, 0, jnp.maximum(t * nblk - 1, 0), hh))
        nxt = pl.BlockSpec((1, d, blk, aw),
                           lambda bb, t, hh, nblk=nblk, last=last: (bb, 0, jnp.minimum((t + 1) * nblk, last), hh))
        args += [q, k, k, k, v, v, v]
        in_specs += [cur, cur, prev, nxt, cur, prev, nxt]
    return pl.pallas_call(
        functools.partial(_mixer_a_kernel, seq=seq),
        grid=(b, seq // tq, A_WIDTH // aw),
        in_specs=in_specs,
        out_specs=pl.BlockSpec((1, tq, aw), lambda bb, t, hh: (bb, t, hh)),
        out_shape=jax.ShapeDtypeStruct((b, seq, A_WIDTH), BF16),
        scratch_shapes=[pltpu.VMEM((tq + 2 * blk, aw), BF16), pltpu.VMEM((tq + 2 * blk, aw), BF16),
                        pltpu.VMEM((aw // LANES, tq, LANES), F32),
                        pltpu.VMEM((aw // LANES, tq, LANES), F32),
                        pltpu.VMEM((aw // LANES, tq, LANES), F32)],
        compiler_params=pltpu.CompilerParams(
            dimension_semantics=("arbitrary", "arbitrary", "arbitrary"), vmem_limit_bytes=VMEM_LIMIT),
        name="mixer_a",
    )(*args)


def _mixer_b_kernel(sink_ref, q, kc, kp, kn, vc, vp, vn, o_ref, kext, vext, *, seq):
    t = pl.program_id(1)
    tq = ATT_TILE
    blk = B_BLOCK
    win = 3 * blk
    kext[0:blk, :] = kp[0]
    kext[blk:blk + tq, :] = kc[0]
    kext[blk + tq:2 * blk + tq, :] = kn[0]
    vext[0:blk, :] = vp[0]
    vext[blk:blk + tq, :] = vc[0]
    vext[blk + tq:2 * blk + tq, :] = vn[0]
    col = lax.broadcasted_iota(jnp.int32, (blk, win), 1)
    row = lax.broadcasted_iota(jnp.int32, (blk, win), 0)
    diff = col - row
    first_half = lax.broadcasted_iota(jnp.int32, (blk, LANES), 1) < HEAD_DIM
    second_half = jnp.logical_not(first_half)

    def block_body(j, carry):
        j0 = pl.multiple_of(j * blk, blk)
        qb = q[0, pl.ds(j0, blk), :]
        kw = kext[pl.ds(j0, win), :]
        vw = vext[pl.ds(j0, win), :]
        qbase = t * tq + j * blk
        valid = ((diff >= 0) & (diff <= 2 * B_WINDOW)
                 & (col >= blk - qbase) & (col < seq - qbase + blk))
        scores, sinks = [], []
        for c in range(B_Q_HEADS // 2):
            qp = qb[:, c * LANES:(c + 1) * LANES]
            for g, keep in enumerate((first_half, second_half)):
                qm = jnp.where(keep, qp, jnp.zeros_like(qp))
                scores.append(lax.dot_general(qm, kw, (((1,), (1,)), ((), ())), preferred_element_type=F32))
                sinks.append(jnp.full((1, LANES), sink_ref[g * B_REP + c], F32))
        sc = jnp.where(valid[None], jnp.stack(scores), NEG_BIG)
        sk = jnp.stack(sinks)
        part = sc[..., 0:LANES]
        for k in range(1, win // LANES):
            part = jnp.maximum(part, sc[..., k * LANES:(k + 1) * LANES])
        m = jnp.max(jnp.maximum(part, sk), axis=-1, keepdims=True)
        p = jnp.exp(sc - m)
        denom = jnp.sum(p, axis=-1, keepdims=True) + jnp.exp(sk - m)
        pb = p.astype(BF16)
        parts = []
        for c in range(B_Q_HEADS // 2):
            o0 = jnp.dot(pb[2 * c], vw, preferred_element_type=F32) / denom[2 * c]
            o1 = jnp.dot(pb[2 * c + 1], vw, preferred_element_type=F32) / denom[2 * c + 1]
            parts.append(jnp.where(first_half, o0, o1))
        o_ref[0, pl.ds(j0, blk), :] = jnp.concatenate(parts, axis=1).astype(BF16)
        return carry

    lax.fori_loop(0, tq // blk, block_body, 0)


def _mixer_b(q, k, v, sink, seq):
    b = q.shape[0]
    tq = ATT_TILE
    blk = B_BLOCK
    nblk = tq // blk
    last = seq // blk - 1
    cur = pl.BlockSpec((1, tq, B_KV_WIDTH), lambda bb, t: (bb, t, 0))
    prev = pl.BlockSpec((1, blk, B_KV_WIDTH), lambda bb, t: (bb, jnp.maximum(t * nblk - 1, 0), 0))
    nxt = pl.BlockSpec((1, blk, B_KV_WIDTH), lambda bb, t: (bb, jnp.minimum((t + 1) * nblk, last), 0))
    return pl.pallas_call(
        functools.partial(_mixer_b_kernel, seq=seq),
        grid=(b, seq // tq),
        in_specs=[pl.BlockSpec(memory_space=pltpu.SMEM),
                  pl.BlockSpec((1, tq, B_Q_WIDTH), lambda bb, t: (bb, t, 0)),
                  cur, prev, nxt, cur, prev, nxt],
        out_specs=pl.BlockSpec((1, tq, B_Q_WIDTH), lambda bb, t: (bb, t, 0)),
        out_shape=jax.ShapeDtypeStruct((b, seq, B_Q_WIDTH), BF16),
        scratch_shapes=[pltpu.VMEM((tq + 2 * blk, B_KV_WIDTH), BF16),
                        pltpu.VMEM((tq + 2 * blk, B_KV_WIDTH), BF16)],
        compiler_params=pltpu.CompilerParams(
            dimension_semantics=("arbitrary", "arbitrary"), vmem_limit_bytes=VMEM_LIMIT),
        name="mixer_b",
    )(sink, q, k, k, k, v, v, v)


def _post_attn_kernel(x_ref, oa_ref, ob_ref, gmix_ref, wg_ref, wa_ref, wb_ref, wo_ref, gmoe_ref,
                      wr_ref, br_ref, x1_ref, h2_ref, gate_ref):
    wr = wr_ref[...]
    wr_hi = wr.astype(BF16)
    wr_lo = (wr - wr_hi.astype(F32)).astype(BF16)
    wr3 = jnp.concatenate([wr_hi, wr_hi, wr_lo], axis=0)
    for s in range(x_ref.shape[0] // POST_SLAB):
        rows = slice(s * POST_SLAB, (s + 1) * POST_SLAB)
        _post_attn_rows(x_ref, oa_ref, ob_ref, gmix_ref, wg_ref, wa_ref, wb_ref, wo_ref, gmoe_ref, wr3,
                        br_ref, x1_ref, h2_ref, gate_ref, rows)


def _post_attn_rows(x_ref, oa_ref, ob_ref, gmix_ref, wg_ref, wa_ref, wb_ref, wo_ref, gmoe_ref, wr3,
                    br_ref, x1_ref, h2_ref, gate_ref, rows):
    x = x_ref[rows, :]
    h = _rms(x, gmix_ref[...]).astype(BF16)
    gates = jax.nn.sigmoid(jnp.dot(h, wg_ref[...], preferred_element_type=F32))
    ya = jnp.dot(oa_ref[rows, :], wa_ref[...], preferred_element_type=F32)
    yb = jnp.dot(ob_ref[rows, :], wb_ref[...], preferred_element_type=F32)
    mix = gates[:, :D_MODEL] * ya + gates[:, D_MODEL:] * yb
    x1 = x + jnp.dot(mix.astype(BF16), wo_ref[...], preferred_element_type=F32)
    x1_ref[rows, :] = x1
    h2 = _rms(x1, gmoe_ref[...])
    h2_hi = h2.astype(BF16)
    h2_ref[rows, :] = h2_hi

    h2_lo = (h2 - h2_hi.astype(F32)).astype(BF16)
    logits = jnp.dot(jnp.concatenate([h2_hi, h2_lo, h2_hi], axis=1), wr3,
                     preferred_element_type=F32) + br_ref[...]
    tm = logits.shape[0]
    lane = lax.broadcasted_iota(jnp.int32, (tm, LANES), 1)
    big = jnp.int32(LANES)
    is_group = (lane >= N_EXPERTS) & (lane < N_EXPERTS + N_GROUPS)
    gl = jnp.where(is_group, logits, NEG_BIG)
    gmax = jnp.max(gl, axis=-1, keepdims=True)
    g_idx = jnp.min(jnp.where(gl == gmax, lane, big), axis=-1, keepdims=True) - N_EXPERTS
    g_w = 1.0 / jnp.sum(jnp.exp(gl - gmax), axis=-1, keepdims=True)
    in_group = (lane < N_EXPERTS) & (lax.shift_right_logical(lane, 3) == g_idx)
    sel = jnp.where(in_group, logits, NEG_BIG)
    v1 = jnp.max(sel, axis=-1, keepdims=True)
    i1 = jnp.min(jnp.where(sel == v1, lane, big), axis=-1, keepdims=True)
    sel2 = jnp.where(lane == i1, NEG_BIG, sel)
    v2 = jnp.max(sel2, axis=-1, keepdims=True)
    i2 = jnp.min(jnp.where(sel2 == v2, lane, big), axis=-1, keepdims=True)
    e2 = jnp.exp(v2 - v1)
    w1 = g_w / (1.0 + e2)
    w2 = g_w * e2 / (1.0 + e2)
    gate_ref[rows, :] = jnp.where(lane == i1, w1, 0.0) + jnp.where(lane == i2, w2, 0.0)


def _post_attn(x2d, oa, ob, norm_mix, w_gate, w_a, w_b, w_out, norm_moe, w_router, b_router):
    t = x2d.shape[0]
    tm = POST_TILE
    const = lambda i: (0, 0)
    row = lambda w: pl.BlockSpec((tm, w), lambda i: (i, 0))
    full = lambda a: pl.BlockSpec(a.shape, const, pipeline_mode=pl.Buffered(1))
    return pl.pallas_call(
        _post_attn_kernel,
        grid=(t // tm,),
        in_specs=[row(D_MODEL), row(A_WIDTH), row(B_Q_WIDTH), full(norm_mix), full(w_gate), full(w_a),
                  full(w_b), full(w_out), full(norm_moe), full(w_router), full(b_router)],
        out_specs=[row(D_MODEL), row(D_MODEL), row(LANES)],
        out_shape=[jax.ShapeDtypeStruct((t, D_MODEL), F32), jax.ShapeDtypeStruct((t, D_MODEL), BF16),
                   jax.ShapeDtypeStruct((t, LANES), F32)],
        compiler_params=pltpu.CompilerParams(
            dimension_semantics=("arbitrary",), vmem_limit_bytes=VMEM_LIMIT),
        name="post_attn",
    )(x2d, oa, ob, norm_mix, w_gate, w_a, w_b, w_out, norm_moe, w_router, b_router)


def _moe_final_kernel(x1_ref, h2_ref, gate_ref, tri_ref, w1_ref, w3_ref, w2_ref, gf_ref, o_ref,
                      xs, stage, out_stage, seg_s, slo_s, shi_s, wlo_s, whi_s):
    step = pl.program_id(1)
    first_step = (pl.program_id(0) == 0) & (step == 0)
    n_sub = x1_ref.shape[0] // MOE_SUB

    @pl.when(first_step)
    def _init():
        xs[...] = jnp.zeros(xs.shape, BF16)
        stage[...] = jnp.zeros(stage.shape, BF16)

    @pl.when(step == 0)
    def _sort():
        r_i = lax.broadcasted_iota(jnp.int32, (LANES, LANES), 0)
        c_i = lax.broadcasted_iota(jnp.int32, (LANES, LANES), 1)
        upper = jnp.where(r_i < c_i, 1.0, 0.0)
        col = lax.broadcasted_iota(jnp.int32, (MOE_SUB, MOE_SUB_SLOTS), 1)
        for s in range(n_sub):
            rows = slice(s * MOE_SUB, (s + 1) * MOE_SUB)
            g = gate_ref[rows, :]
            hit = g > 0.0
            onehot = jnp.where(hit, 1.0, 0.0)
            count = jnp.sum(onehot, axis=0, keepdims=True)
            padded = jnp.ceil(count * (1.0 / SEG_ALIGN)) * SEG_ALIGN
            seg = jnp.dot(jnp.broadcast_to(padded, (8, LANES)), upper, preferred_element_type=F32,
                          precision=lax.Precision.HIGHEST)[0:1]
            seg_i = seg.astype(jnp.int32)
            padded_i = padded.astype(jnp.int32)
            for ex in range(N_EXPERTS):
                seg_s[2 * s, ex] = seg_i[0, ex] + s * MOE_SUB_SLOTS
                seg_s[2 * s + 1, ex] = padded_i[0, ex]
            rank = jnp.dot(tri_ref[...], onehot.astype(BF16), preferred_element_type=F32)
            slot = seg + rank
            s_lo = jnp.min(jnp.where(hit, slot, 1e9), axis=-1, keepdims=True)
            s_hi = jnp.max(jnp.where(hit, slot, -1.0), axis=-1, keepdims=True)
            wlo_s[rows, :] = jnp.sum(jnp.where(hit & (slot == s_lo), g, 0.0), axis=-1, keepdims=True)
            whi_s[rows, :] = jnp.sum(jnp.where(hit & (slot == s_hi) & (s_hi > s_lo), g, 0.0),
                                     axis=-1, keepdims=True)
            s_lo_i = s_lo.astype(jnp.int32)
            s_hi_i = s_hi.astype(jnp.int32)
            slo_s[rows, :] = s_lo_i
            shi_s[rows, :] = s_hi_i
            def sort_slots(n_slots, s=s, rows=rows):
                c = lax.broadcasted_iota(jnp.int32, (MOE_SUB, n_slots), 1)
                perm_t = jnp.where((c == slo_s[rows, :]) | (c == shi_s[rows, :]), 1.0, 0.0).astype(BF16)
                sorted_rows = lax.dot_general(perm_t, h2_ref[rows, :], (((0,), (0,)), ((), ())),
                                              preferred_element_type=F32)
                xs[s * MOE_SUB_SLOTS:s * MOE_SUB_SLOTS + n_slots, :] = sorted_rows.astype(BF16)

            used = seg_s[2 * s, N_EXPERTS - 1] + seg_s[2 * s + 1, N_EXPERTS - 1] - s * MOE_SUB_SLOTS
            pl.when(used <= MOE_SUB_SLOTS_COMMON)(functools.partial(sort_slots, MOE_SUB_SLOTS_COMMON))
            pl.when(used > MOE_SUB_SLOTS_COMMON)(functools.partial(sort_slots, MOE_SUB_SLOTS))

    def ffn(x, j):
        a = jnp.dot(x, w1_ref[j], preferred_element_type=F32)
        b = jnp.dot(x, w3_ref[j], preferred_element_type=F32)
        hid = (a * jax.nn.sigmoid(a)) * b
        return jnp.dot(hid.astype(BF16), w2_ref[j], preferred_element_type=F32).astype(BF16)

    bases, totals, seg_starts, seg_lens, seg_dsts = [], [], [], [], []
    base = jnp.int32(0)
    for j in range(MOE_ESTEP):
        e = step * MOE_ESTEP + j
        starts = [seg_s[2 * s, e] for s in range(n_sub)]
        lens = [seg_s[2 * s + 1, e] for s in range(n_sub)]
        dst = base
        dsts = []
        for s in range(n_sub):
            dsts.append(dst)
            dst = dst + lens[s]
        bases.append(base)
        totals.append(dst - base)
        seg_starts.append(starts)
        seg_lens.append(lens)
        seg_dsts.append(dsts)
        base = base + ((dst - base + MOE_CHUNK - 1) // MOE_CHUNK) * MOE_CHUNK

    for j in range(MOE_ESTEP):
        for s in range(n_sub):
            def gather_body(k, carry, src0=seg_starts[j][s], dst0=seg_dsts[j][s]):
                src = pl.multiple_of(src0 + k * MOE_COPY, SEG_ALIGN)
                to = pl.multiple_of(dst0 + k * MOE_COPY, SEG_ALIGN)
                stage[pl.ds(to, MOE_COPY), :] = xs[pl.ds(src, MOE_COPY), :]
                return carry

            lax.fori_loop(1, (seg_lens[j][s] + MOE_COPY - 1) // MOE_COPY, gather_body, 0)
    for j in range(MOE_ESTEP):
        for s in range(n_sub):
            stage[pl.ds(pl.multiple_of(seg_dsts[j][s], SEG_ALIGN), MOE_COPY), :] = \
                xs[pl.ds(pl.multiple_of(seg_starts[j][s], SEG_ALIGN), MOE_COPY), :]

    for j in range(MOE_ESTEP):
        def extra_body(k, carry, j=j):
            r0 = pl.multiple_of(bases[j] + k * MOE_CHUNK, MOE_CHUNK)
            out_stage[pl.ds(r0, MOE_CHUNK), :] = ffn(stage[pl.ds(r0, MOE_CHUNK), :], j)
            return carry

        lax.fori_loop(1, (totals[j] + MOE_CHUNK - 1) // MOE_CHUNK, extra_body, 0)

    firsts = [ffn(stage[pl.ds(pl.multiple_of(bases[j], MOE_CHUNK), MOE_CHUNK), :], j) for j in range(MOE_ESTEP)]
    for j in range(MOE_ESTEP):
        out_stage[pl.ds(pl.multiple_of(bases[j], MOE_CHUNK), MOE_CHUNK), :] = firsts[j]

    for j in range(MOE_ESTEP):
        for s in range(n_sub):
            def scatter_body(k, carry, src0=seg_dsts[j][s], dst0=seg_starts[j][s]):
                src = pl.multiple_of(src0 + k * SEG_ALIGN, SEG_ALIGN)
                to = pl.multiple_of(dst0 + k * SEG_ALIGN, SEG_ALIGN)
                xs[pl.ds(to, SEG_ALIGN), :] = out_stage[pl.ds(src, SEG_ALIGN), :]
                return carry

            lax.fori_loop(0, seg_lens[j][s] // SEG_ALIGN, scatter_body, 0)

    @pl.when(step == N_EXPERTS // MOE_ESTEP - 1)
    def _combine():
        col = lax.broadcasted_iota(jnp.int32, (MOE_SUB, MOE_SUB_SLOTS), 1)
        for s in range(n_sub):
            rows = slice(s * MOE_SUB, (s + 1) * MOE_SUB)

            def combine_slots(n_slots, s=s, rows=rows):
                c = lax.broadcasted_iota(jnp.int32, (MOE_SUB, n_slots), 1)
                wperm = (jnp.where(c == slo_s[rows, :], wlo_s[rows, :], 0.0)
                         + jnp.where(c == shi_s[rows, :], whi_s[rows, :], 0.0)).astype(BF16)
                moe = jnp.dot(wperm, xs[s * MOE_SUB_SLOTS:s * MOE_SUB_SLOTS + n_slots, :],
                              preferred_element_type=F32)
                o_ref[rows, :] = _rms(x1_ref[rows, :] + moe, gf_ref[...])

            used = seg_s[2 * s, N_EXPERTS - 1] + seg_s[2 * s + 1, N_EXPERTS - 1] - s * MOE_SUB_SLOTS
            pl.when(used <= MOE_SUB_SLOTS_COMMON)(functools.partial(combine_slots, MOE_SUB_SLOTS_COMMON))
            pl.when(used > MOE_SUB_SLOTS_COMMON)(functools.partial(combine_slots, MOE_SUB_SLOTS))


def _moe_final(x1, h2, gate, w1, w3, w2, norm_final):
    t = x1.shape[0]
    tm = MOE_TILE
    n_sub = tm // MOE_SUB
    row = lambda w: pl.BlockSpec((tm, w), lambda i, e: (i, 0))
    tri = jnp.tril(jnp.ones((MOE_SUB, MOE_SUB), BF16), -1)
    return pl.pallas_call(
        _moe_final_kernel,
        grid=(t // tm, N_EXPERTS // MOE_ESTEP),
        in_specs=[row(D_MODEL), row(D_MODEL), row(LANES),
                  pl.BlockSpec((MOE_SUB, MOE_SUB), lambda i, e: (0, 0), pipeline_mode=pl.Buffered(1)),
                  pl.BlockSpec((MOE_ESTEP, D_MODEL, D_FF), lambda i, e: (e, 0, 0)),
                  pl.BlockSpec((MOE_ESTEP, D_MODEL, D_FF), lambda i, e: (e, 0, 0)),
                  pl.BlockSpec((MOE_ESTEP, D_FF, D_MODEL), lambda i, e: (e, 0, 0)),
                  pl.BlockSpec((1, D_MODEL), lambda i, e: (0, 0))],
        out_specs=row(D_MODEL),
        out_shape=jax.ShapeDtypeStruct((t, D_MODEL), F32),
        scratch_shapes=[pltpu.VMEM((n_sub * MOE_SUB_SLOTS + MOE_CHUNK, D_MODEL), BF16),
                        pltpu.VMEM((2 * tm + (MOE_ESTEP + 1) * MOE_CHUNK, D_MODEL), BF16),
                        pltpu.VMEM((2 * tm + (MOE_ESTEP + 1) * MOE_CHUNK, D_MODEL), BF16),
                        pltpu.SMEM((2 * n_sub, N_EXPERTS), jnp.int32),
                        pltpu.VMEM((tm, 1), jnp.int32), pltpu.VMEM((tm, 1), jnp.int32),
                        pltpu.VMEM((tm, 1), F32), pltpu.VMEM((tm, 1), F32)],
        compiler_params=pltpu.CompilerParams(
            dimension_semantics=("arbitrary", "arbitrary"), vmem_limit_bytes=VMEM_LIMIT),
        name="moe_final",
    )(x1, h2, gate, tri, w1, w3, w2, norm_final)


def _prep_weights(norm_mix, w_in, attn_sink, w_branch_a, w_branch_b, w_gate, w_out, norm_moe,
                  w_router_group, b_router_group, w_router_expert, b_router_expert, w1, w3, w2, norm_final):
    wi = w_in[0]
    a_w = 3 * A_WIDTH

    def group_cols(gi):
        return jnp.concatenate([wi[:, role * a_w + gi * A_WIDTH: role * a_w + (gi + 1) * A_WIDTH]
                                for role in range(3)], axis=1)

    b_heads = [g * B_REP + c for c in range(B_REP) for g in range(B_KV_HEADS)]
    bq_cols = jnp.concatenate([wi[:, 3 * a_w + h * HEAD_DIM: 3 * a_w + (h + 1) * HEAD_DIM] for h in b_heads],
                              axis=1)
    w_b_rows = jnp.concatenate([w_branch_b[0][h * HEAD_DIM:(h + 1) * HEAD_DIM] for h in b_heads], axis=0)
    w_nat = jnp.concatenate([group_cols(0), bq_cols, wi[:, 3 * a_w + B_Q_WIDTH:]], axis=1).astype(BF16)
    w_g1 = group_cols(1).astype(BF16)
    w_g2 = group_cols(2).astype(BF16)
    pad = LANES - N_EXPERTS - N_GROUPS
    w_router = jnp.concatenate([w_router_expert[0], w_router_group[0],
                                jnp.zeros((D_MODEL, pad), F32)], axis=1)
    b_router = jnp.concatenate([b_router_expert[0].reshape(-1), b_router_group[0],
                                jnp.zeros((pad,), F32)]).reshape(1, LANES)
    return dict(
        norm_mix=norm_mix[0].reshape(1, D_MODEL), w_nat=w_nat, w_g1=w_g1, w_g2=w_g2, sink=attn_sink[0],
        w_a=w_branch_a[0].astype(BF16), w_b=w_b_rows.astype(BF16), w_gate=w_gate[0].astype(BF16),
        w_out=w_out[0].astype(BF16), norm_moe=norm_moe[0].reshape(1, D_MODEL),
        w_router=w_router, b_router=b_router,
        w1=w1[0].astype(BF16), w3=w3[0].astype(BF16), w2=w2[0].astype(BF16),
        norm_final=norm_final.reshape(1, D_MODEL))


def _trunk(x, p):
    b, s, _ = x.shape
    outs = _in_proj(x, p["norm_mix"], p["w_nat"], p["w_g1"], p["w_g2"], p["cos"], p["sin"])
    oa = _mixer_a(outs[:9], s)
    ob = _mixer_b(outs[9], outs[10], outs[11], p["sink"], s)
    t = b * s
    x1, h2, gate = _post_attn(x.reshape(t, D_MODEL), oa.reshape(t, A_WIDTH), ob.reshape(t, B_Q_WIDTH),
                              p["norm_mix"], p["w_gate"], p["w_a"], p["w_b"], p["w_out"], p["norm_moe"],
                              p["w_router"], p["b_router"])
    y = _moe_final(x1, h2, gate, p["w1"], p["w3"], p["w2"], p["norm_final"])
    return y.reshape(b, s, D_MODEL)


def kernel(x_prompt, x_sample, norm_mix, w_in, attn_sink, w_branch_a, w_branch_b, w_gate, w_out, norm_moe,
           w_router_group, b_router_group, w_router_expert, b_router_expert, w1, w3, w2, norm_final):
    p = _prep_weights(norm_mix, w_in, attn_sink, w_branch_a, w_branch_b, w_gate, w_out, norm_moe,
                      w_router_group, b_router_group, w_router_expert, b_router_expert, w1, w3, w2, norm_final)
    p["cos"], p["sin"] = _rope_tables(max(x_prompt.shape[1], x_sample.shape[1]))
    return (_trunk(x_prompt, p), _trunk(x_sample, p))
```

```python
import functools

import jax
import jax.numpy as jnp
from jax import lax
from jax.experimental import pallas as pl
from jax.experimental.pallas import tpu as pltpu

D_MODEL = 1024
HEAD_DIM = 64
ROT_DIM = HEAD_DIM // 4
ROT_HALF = ROT_DIM // 2
ROPE_THETA = 500000.0
DILATIONS = (1, 4, 16)
A_HALF = 64
A_BLOCK = 64
A_QBLOCK = 128
A_HEADS = 8
A_WIDTH = A_HEADS * HEAD_DIM
B_Q_HEADS = 8
B_KV_HEADS = 2
B_REP = B_Q_HEADS // B_KV_HEADS
B_WINDOW = 128
B_BLOCK = 128
B_Q_WIDTH = B_Q_HEADS * HEAD_DIM
B_KV_WIDTH = B_KV_HEADS * HEAD_DIM
N_GROUPS = 4
EXPERTS_PER_GROUP = 8
N_EXPERTS = N_GROUPS * EXPERTS_PER_GROUP
D_FF = 256
RMS_EPS = 1e-6
Q_SCALE = HEAD_DIM ** -0.5

LANES = 128
NEG_BIG = -1e30
VMEM_LIMIT = 58 * 1024 * 1024

IN_TILE = 512
ATT_TILE = 1024
A_TILE = 1024
A_STEP_WIDTH = 512
POST_TILE = 1024
POST_SLAB = 256
MOE_TILE = 1024
SEG_ALIGN = 16
MOE_CHUNK = 128
MOE_COPY = 32
MOE_SUB = 256
MOE_SUB_SLOTS = 2 * MOE_SUB + N_EXPERTS * SEG_ALIGN
MOE_ESTEP = 4

F32 = jnp.float32
BF16 = jnp.bfloat16


def _rms(x, g):
    ms = jnp.mean(x * x, axis=-1, keepdims=True)
    return x * lax.rsqrt(ms + RMS_EPS) * g


def _rope(a, cos, sin, low_half):
    outs = []
    for k in range(a.shape[1] // LANES):
        ak = a[:, k * LANES:(k + 1) * LANES]
        up = pltpu.roll(ak, LANES - ROT_HALF, axis=1)
        dn = pltpu.roll(ak, ROT_HALF, axis=1)
        outs.append(ak * cos + jnp.where(low_half, up, dn) * sin)
    return outs[0] if len(outs) == 1 else jnp.concatenate(outs, axis=1)


def _in_proj_kernel(x_ref, g_ref, cos_ref, sin_ref, wn_ref, w1_ref, w2_ref,
                    q0, k0, v0, q1, k1, v1, q2, k2, v2, bq, bk, bv,
                    xs, h1_s, h2_s, c1_s, s1_s, c2_s, s2_s):
    tm = x_ref.shape[1]
    n_chunks = D_MODEL // LANES
    for c in range(n_chunks):
        xs[c] = x_ref[0, :, c * LANES:(c + 1) * LANES]
    g = g_ref[...]
    low_half = (lax.broadcasted_iota(jnp.int32, (tm, LANES), 1) % HEAD_DIM) < ROT_HALF

    def mm(h, w_ref, c0, n):
        return jnp.dot(h, w_ref[:, c0:c0 + n], preferred_element_type=F32)

    h0 = _rms(x_ref[0], g).astype(BF16)
    cos0, sin0 = cos_ref[...], sin_ref[...]
    q0[0, 0] = (_rope(mm(h0, wn_ref, 0, A_WIDTH), cos0, sin0, low_half) * Q_SCALE).astype(BF16)
    k0[0, 0] = _rope(mm(h0, wn_ref, A_WIDTH, A_WIDTH), cos0, sin0, low_half).astype(BF16)
    v0[0, 0] = mm(h0, wn_ref, 2 * A_WIDTH, A_WIDTH).astype(BF16)
    c = 3 * A_WIDTH
    bq[0] = (_rope(mm(h0, wn_ref, c, B_Q_WIDTH), cos0, sin0, low_half) * Q_SCALE).astype(BF16)
    bk[0] = _rope(mm(h0, wn_ref, c + B_Q_WIDTH, B_KV_WIDTH), cos0, sin0, low_half).astype(BF16)
    bv[0] = mm(h0, wn_ref, c + B_Q_WIDTH + B_KV_WIDTH, B_KV_WIDTH).astype(BF16)

    for d, hs, cs, ss, w_ref, qo, ko, vo in ((DILATIONS[1], h1_s, c1_s, s1_s, w1_ref, q1, k1, v1),
                                             (DILATIONS[2], h2_s, c2_s, s2_s, w2_ref, q2, k2, v2)):
        n = tm // d
        for r in range(d):
            rows = pl.ds(r, n, stride=d)
            xr = jnp.concatenate([xs[c, rows, :] for c in range(n_chunks)], axis=1)
            hs[r * n:(r + 1) * n, :] = _rms(xr, g).astype(BF16)
            cs[r * n:(r + 1) * n, :] = cos_ref[rows, :]
            ss[r * n:(r + 1) * n, :] = sin_ref[rows, :]
        h = hs[...]
        cos, sin = cs[...], ss[...]
        qv = (_rope(mm(h, w_ref, 0, A_WIDTH), cos, sin, low_half) * Q_SCALE).astype(BF16)
        kv = _rope(mm(h, w_ref, A_WIDTH, A_WIDTH), cos, sin, low_half).astype(BF16)
        vv = mm(h, w_ref, 2 * A_WIDTH, A_WIDTH).astype(BF16)
        for r in range(d):
            qo[0, r] = qv[r * n:(r + 1) * n]
            ko[0, r] = kv[r * n:(r + 1) * n]
            vo[0, r] = vv[r * n:(r + 1) * n]


def _rope_tables(s):
    inv = jnp.power(jnp.float32(ROPE_THETA), -jnp.arange(ROT_HALF, dtype=F32) / ROT_HALF)
    ang = jnp.arange(s).astype(F32)[:, None] * inv[None, :]
    cos, sin = jnp.cos(ang), jnp.sin(ang)
    src = jnp.arange(LANES) % HEAD_DIM
    spread = (jnp.arange(ROT_DIM)[:, None] == src[None, :]).astype(F32)
    hi = lax.Precision.HIGHEST
    cos_t = jnp.dot(jnp.concatenate([cos, cos], axis=1), spread, precision=hi) + (src >= ROT_DIM).astype(F32)
    sin_t = jnp.dot(jnp.concatenate([-sin, sin], axis=1), spread, precision=hi)
    return cos_t, sin_t


def _in_proj(x, norm_g, w_nat, w_g1, w_g2, cos_t, sin_t):
    b, s, _ = x.shape
    tm = IN_TILE
    const = lambda t, bb: (0, 0)
    one_buf = pl.Buffered(1)
    in_specs = [
        pl.BlockSpec((1, tm, D_MODEL), lambda t, bb: (bb, t, 0)),
        pl.BlockSpec((1, D_MODEL), const),
        pl.BlockSpec((tm, LANES), lambda t, bb: (t, 0)),
        pl.BlockSpec((tm, LANES), lambda t, bb: (t, 0)),
        pl.BlockSpec(w_nat.shape, const, pipeline_mode=one_buf),
        pl.BlockSpec(w_g1.shape, const, pipeline_mode=one_buf),
        pl.BlockSpec(w_g2.shape, const, pipeline_mode=one_buf),
    ]
    out_shape, out_specs = [], []
    for d in DILATIONS:
        for _ in range(3):
            out_shape.append(jax.ShapeDtypeStruct((b, d, s // d, A_WIDTH), BF16))
            out_specs.append(pl.BlockSpec((1, d, tm // d, A_WIDTH), lambda t, bb: (bb, 0, t, 0)))
    for w in (B_Q_WIDTH, B_KV_WIDTH, B_KV_WIDTH):
        out_shape.append(jax.ShapeDtypeStruct((b, s, w), BF16))
        out_specs.append(pl.BlockSpec((1, tm, w), lambda t, bb: (bb, t, 0)))
    scratch = [pltpu.VMEM((D_MODEL // LANES, tm, LANES), F32),
               pltpu.VMEM((tm, D_MODEL), BF16), pltpu.VMEM((tm, D_MODEL), BF16)] + \
              [pltpu.VMEM((tm, LANES), F32) for _ in range(4)]
    return pl.pallas_call(
        _in_proj_kernel,
        grid=(s // tm, b),
        in_specs=in_specs,
        out_specs=out_specs,
        out_shape=out_shape,
        scratch_shapes=scratch,
        compiler_params=pltpu.CompilerParams(
            dimension_semantics=("arbitrary", "arbitrary"), vmem_limit_bytes=VMEM_LIMIT),
        name="in_proj",
    )(x, norm_g.reshape(1, D_MODEL), cos_t, sin_t, w_nat, w_g1, w_g2)


def _attend_heads(blocks, valid, first_half):
    second_half = jnp.logical_not(first_half)
    n_heads = blocks[0][0].shape[1] // HEAD_DIM
    scores = []
    for qb, kw, _ in blocks:
        for h in range(n_heads):
            cs = slice((h // 2) * LANES, (h // 2 + 1) * LANES)
            qp = qb[:, cs]
            qm = jnp.where(first_half if h % 2 == 0 else second_half, qp, jnp.zeros_like(qp))
            scores.append(lax.dot_general(qm, kw[:, cs], (((1,), (1,)), ((), ())),
                                          preferred_element_type=F32))
    sc = jnp.where(valid[None], jnp.stack(scores), NEG_BIG)
    m = jnp.max(sc, axis=-1, keepdims=True)
    p = jnp.exp(sc - m)
    l = jnp.sum(p, axis=-1, keepdims=True)
    pb = p.astype(BF16)
    results = []
    for bi, (_, _, vw) in enumerate(blocks):
        out = []
        for hp in range(n_heads // 2):
            i0 = bi * n_heads + 2 * hp
            vp = vw[:, hp * LANES:(hp + 1) * LANES]
            o0 = jnp.dot(pb[i0], vp, preferred_element_type=F32)
            o1 = jnp.dot(pb[i0 + 1], vp, preferred_element_type=F32)
            out.append((jnp.where(first_half, o0, o1), jnp.where(first_half, m[i0], m[i0 + 1]),
                        jnp.where(first_half, l[i0], l[i0 + 1])))
        results.append(out)
    return results


def _mixer_a_kernel(*refs, seq):
    ins, (o_ref, kext, vext, acc_o, acc_m, acc_l) = refs[:21], refs[21:]
    t = pl.program_id(1)
    tq = A_TILE
    halo = A_HALF

    first_group = len(DILATIONS) - 1
    for gi in reversed(range(len(DILATIONS))):
        d = DILATIONS[gi]
        q, kc, kp, kn, vc, vp, vn = ins[7 * gi:7 * gi + 7]
        n = tq // d
        blk = min(n, A_QBLOCK)
        nblk = n // blk
        win = -(-(blk + 2 * halo) // LANES) * LANES
        tail = win - (blk + 2 * halo) if n == blk else 0
        ld = seq // d
        col = lax.broadcasted_iota(jnp.int32, (blk, win), 1)
        diff = col - lax.broadcasted_iota(jnp.int32, (blk, win), 0)
        band = (diff >= 0) & (diff <= 2 * halo)
        first_half = lax.broadcasted_iota(jnp.int32, (blk, LANES), 1) < HEAD_DIM

        pair = 4 if n == blk and blk < A_QBLOCK else 1

        def residue_body(rp, carry, q=q, kc=kc, kp=kp, kn=kn, vc=vc, vp=vp, vn=vn, d=d, n=n, nblk=nblk,
                         ld=ld, blk=blk, win=win, col=col, band=band, first_half=first_half,
                         first=(gi == first_group), last=(gi == 0), pair=pair, tail=tail):
            for u in range(pair):
                r = rp * pair + u
                base = u * win
                kext[base:base + halo, :] = kp[0, r]
                kext[base + halo:base + halo + n, :] = kc[0, r]
                kext[base + halo + n:base + 2 * halo + n, :] = kn[0, r]
                vext[base:base + halo, :] = vp[0, r]
                vext[base + halo:base + halo + n, :] = vc[0, r]
                vext[base + halo + n:base + 2 * halo + n, :] = vn[0, r]
                if tail:
                    kext[base + 2 * halo + n:base + win, :] = jnp.zeros((tail, kext.shape[1]), BF16)
                    vext[base + 2 * halo + n:base + win, :] = jnp.zeros((tail, vext.shape[1]), BF16)

            def block_body(j, carry2):
                j0 = pl.multiple_of(j * blk, blk)
                qbase = t * n + j * blk
                valid = band & (col >= halo - qbase) & (col < ld - qbase + halo)
                blocks = [(q[0, rp * pair + u, pl.ds(j0, blk), :],
                           kext[pl.ds(u * win + j0, win), :],
                           vext[pl.ds(u * win + j0, win), :]) for u in range(pair)]
                results = _attend_heads(blocks, valid, first_half)
                for u, parts in enumerate(results):
                    if d == 1:
                        rows = pl.ds(j0, blk)
                    else:
                        rows = pl.ds(j * (blk * d) + rp * pair + u, blk, stride=d)
                    for hp, (o_b, m_b, l_b) in enumerate(parts):
                        if first:
                            acc_o[hp, rows, :] = o_b
                            acc_l[hp, rows, :] = l_b
                            acc_m[hp, rows, :] = m_b
                        else:
                            m_old = acc_m[hp, rows, :]
                            m_new = jnp.maximum(m_old, m_b)
                            a_old = jnp.exp(m_old - m_new)
                            a_new = jnp.exp(m_b - m_new)
                            o_new = acc_o[hp, rows, :] * a_old + o_b * a_new
                            l_new = acc_l[hp, rows, :] * a_old + l_b * a_new
                            if last:
                                o_ref[0, rows, hp * LANES:(hp + 1) * LANES] = (o_new / l_new).astype(BF16)
                            else:
                                acc_o[hp, rows, :] = o_new
                                acc_l[hp, rows, :] = l_new
                                acc_m[hp, rows, :] = m_new
                return carry2

            lax.fori_loop(0, nblk, block_body, 0)
            return carry

        lax.fori_loop(0, d // pair, residue_body, 0)


def _mixer_a(qkv, seq):
    b = qkv[0].shape[0]
    tq = A_TILE
    blk = A_BLOCK
    aw = A_STEP_WIDTH
    args, in_specs = [], []
    for gi, d in enumerate(DILATIONS):
        q, k, v = qkv[3 * gi:3 * gi + 3]
        n = tq // d
        nblk = n // blk
        last = seq // d // blk - 1
        cur = pl.BlockSpec((1, d, n, aw), lambda bb, t, hh: (bb, 0, t, hh))
        prev = pl.BlockSpec((1, d, blk, aw),
                            lambda bb, t, hh, nblk=nblk: (bb, 0, jnp.maximum(t * nblk - 1, 0), hh))
        nxt = pl.BlockSpec((1, d, blk, aw),
                           lambda bb, t, hh, nblk=nblk, last=last: (bb, 0, jnp.minimum((t + 1) * nblk, last), hh))
        args += [q, k, k, k, v, v, v]
        in_specs += [cur, cur, prev, nxt, cur, prev, nxt]
    return pl.pallas_call(
        functools.partial(_mixer_a_kernel, seq=seq),
        grid=(b, seq // tq, A_WIDTH // aw),
        in_specs=in_specs,
        out_specs=pl.BlockSpec((1, tq, aw), lambda bb, t, hh: (bb, t, hh)),
        out_shape=jax.ShapeDtypeStruct((b, seq, A_WIDTH), BF16),
        scratch_shapes=[pltpu.VMEM((tq + 2 * blk, aw), BF16), pltpu.VMEM((tq + 2 * blk, aw), BF16),
                        pltpu.VMEM((aw // LANES, tq, LANES), F32),
                        pltpu.VMEM((aw // LANES, tq, LANES), F32),
                        pltpu.VMEM((aw // LANES, tq, LANES), F32)],
        compiler_params=pltpu.CompilerParams(
            dimension_semantics=("arbitrary", "arbitrary", "arbitrary"), vmem_limit_bytes=VMEM_LIMIT),
        name="mixer_a",
    )(*args)


def _mixer_b_kernel(sink_ref, q, kc, kp, kn, vc, vp, vn, o_ref, kext, vext, *, seq):
    t = pl.program_id(1)
    tq = ATT_TILE
    blk = B_BLOCK
    win = 3 * blk
    kext[0:blk, :] = kp[0]
    kext[blk:blk + tq, :] = kc[0]
    kext[blk + tq:2 * blk + tq, :] = kn[0]
    vext[0:blk, :] = vp[0]
    vext[blk:blk + tq, :] = vc[0]
    vext[blk + tq:2 * blk + tq, :] = vn[0]
    col = lax.broadcasted_iota(jnp.int32, (blk, win), 1)
    row = lax.broadcasted_iota(jnp.int32, (blk, win), 0)
    diff = col - row
    first_half = lax.broadcasted_iota(jnp.int32, (blk, LANES), 1) < HEAD_DIM
    second_half = jnp.logical_not(first_half)

    def block_body(j, carry):
        j0 = pl.multiple_of(j * blk, blk)
        qb = q[0, pl.ds(j0, blk), :]
        kw = kext[pl.ds(j0, win), :]
        vw = vext[pl.ds(j0, win), :]
        qbase = t * tq + j * blk
        valid = ((diff >= 0) & (diff <= 2 * B_WINDOW)
                 & (col >= blk - qbase) & (col < seq - qbase + blk))
        scores, sinks = [], []
        for c in range(B_Q_HEADS // 2):
            qp = qb[:, c * LANES:(c + 1) * LANES]
            for g, keep in enumerate((first_half, second_half)):
                qm = jnp.where(keep, qp, jnp.zeros_like(qp))
                scores.append(lax.dot_general(qm, kw, (((1,), (1,)), ((), ())), preferred_element_type=F32))
                sinks.append(jnp.full((1, LANES), sink_ref[g * B_REP + c], F32))
        sc = jnp.where(valid[None], jnp.stack(scores), NEG_BIG)
        sk = jnp.stack(sinks)
        part = sc[..., 0:LANES]
        for k in range(1, win // LANES):
            part = jnp.maximum(part, sc[..., k * LANES:(k + 1) * LANES])
        m = jnp.max(jnp.maximum(part, sk), axis=-1, keepdims=True)
        p = jnp.exp(sc - m)
        denom = jnp.sum(p, axis=-1, keepdims=True) + jnp.exp(sk - m)
        pb = p.astype(BF16)
        parts = []
        for c in range(B_Q_HEADS // 2):
            o0 = jnp.dot(pb[2 * c], vw, preferred_element_type=F32) / denom[2 * c]
            o1 = jnp.dot(pb[2 * c + 1], vw, preferred_element_type=F32) / denom[2 * c + 1]
            parts.append(jnp.where(first_half, o0, o1))
        o_ref[0, pl.ds(j0, blk), :] = jnp.concatenate(parts, axis=1).astype(BF16)
        return carry

    lax.fori_loop(0, tq // blk, block_body, 0)


def _mixer_b(q, k, v, sink, seq):
    b = q.shape[0]
    tq = ATT_TILE
    blk = B_BLOCK
    nblk = tq // blk
    last = seq // blk - 1
    cur = pl.BlockSpec((1, tq, B_KV_WIDTH), lambda bb, t: (bb, t, 0))
    prev = pl.BlockSpec((1, blk, B_KV_WIDTH), lambda bb, t: (bb, jnp.maximum(t * nblk - 1, 0), 0))
    nxt = pl.BlockSpec((1, blk, B_KV_WIDTH), lambda bb, t: (bb, jnp.minimum((t + 1) * nblk, last), 0))
    return pl.pallas_call(
        functools.partial(_mixer_b_kernel, seq=seq),
        grid=(b, seq // tq),
        in_specs=[pl.BlockSpec(memory_space=pltpu.SMEM),
                  pl.BlockSpec((1, tq, B_Q_WIDTH), lambda bb, t: (bb, t, 0)),
                  cur, prev, nxt, cur, prev, nxt],
        out_specs=pl.BlockSpec((1, tq, B_Q_WIDTH), lambda bb, t: (bb, t, 0)),
        out_shape=jax.ShapeDtypeStruct((b, seq, B_Q_WIDTH), BF16),
        scratch_shapes=[pltpu.VMEM((tq + 2 * blk, B_KV_WIDTH), BF16),
                        pltpu.VMEM((tq + 2 * blk, B_KV_WIDTH), BF16)],
        compiler_params=pltpu.CompilerParams(
            dimension_semantics=("arbitrary", "arbitrary"), vmem_limit_bytes=VMEM_LIMIT),
        name="mixer_b",
    )(sink, q, k, k, k, v, v, v)


def _post_attn_kernel(x_ref, oa_ref, ob_ref, gmix_ref, wg_ref, wa_ref, wb_ref, wo_ref, gmoe_ref,
                      wr_ref, br_ref, x1_ref, h2_ref, gate_ref):
    wr = wr_ref[...]
    wr_hi = wr.astype(BF16)
    wr_lo = (wr - wr_hi.astype(F32)).astype(BF16)
    wr3 = jnp.concatenate([wr_hi, wr_hi, wr_lo], axis=0)
    for s in range(x_ref.shape[0] // POST_SLAB):
        rows = slice(s * POST_SLAB, (s + 1) * POST_SLAB)
        _post_attn_rows(x_ref, oa_ref, ob_ref, gmix_ref, wg_ref, wa_ref, wb_ref, wo_ref, gmoe_ref, wr3,
                        br_ref, x1_ref, h2_ref, gate_ref, rows)


def _post_attn_rows(x_ref, oa_ref, ob_ref, gmix_ref, wg_ref, wa_ref, wb_ref, wo_ref, gmoe_ref, wr3,
                    br_ref, x1_ref, h2_ref, gate_ref, rows):
    x = x_ref[rows, :]
    h = _rms(x, gmix_ref[...]).astype(BF16)
    gates = jax.nn.sigmoid(jnp.dot(h, wg_ref[...], preferred_element_type=F32))
    ya = jnp.dot(oa_ref[rows, :], wa_ref[...], preferred_element_type=F32)
    yb = jnp.dot(ob_ref[rows, :], wb_ref[...], preferred_element_type=F32)
    mix = gates[:, :D_MODEL] * ya + gates[:, D_MODEL:] * yb
    x1 = x + jnp.dot(mix.astype(BF16), wo_ref[...], preferred_element_type=F32)
    x1_ref[rows, :] = x1
    h2 = _rms(x1, gmoe_ref[...])
    h2_hi = h2.astype(BF16)
    h2_ref[rows, :] = h2_hi

    h2_lo = (h2 - h2_hi.astype(F32)).astype(BF16)
    logits = jnp.dot(jnp.concatenate([h2_hi, h2_lo, h2_hi], axis=1), wr3,
                     preferred_element_type=F32) + br_ref[...]
    tm = logits.shape[0]
    lane = lax.broadcasted_iota(jnp.int32, (tm, LANES), 1)
    big = jnp.int32(LANES)
    is_group = (lane >= N_EXPERTS) & (lane < N_EXPERTS + N_GROUPS)
    gl = jnp.where(is_group, logits, NEG_BIG)
    gmax = jnp.max(gl, axis=-1, keepdims=True)
    g_idx = jnp.min(jnp.where(gl == gmax, lane, big), axis=-1, keepdims=True) - N_EXPERTS
    g_w = 1.0 / jnp.sum(jnp.exp(gl - gmax), axis=-1, keepdims=True)
    in_group = (lane < N_EXPERTS) & (lax.shift_right_logical(lane, 3) == g_idx)
    sel = jnp.where(in_group, logits, NEG_BIG)
    v1 = jnp.max(sel, axis=-1, keepdims=True)
    i1 = jnp.min(jnp.where(sel == v1, lane, big), axis=-1, keepdims=True)
    sel2 = jnp.where(lane == i1, NEG_BIG, sel)
    v2 = jnp.max(sel2, axis=-1, keepdims=True)
    i2 = jnp.min(jnp.where(sel2 == v2, lane, big), axis=-1, keepdims=True)
    e2 = jnp.exp(v2 - v1)
    w1 = g_w / (1.0 + e2)
    w2 = g_w * e2 / (1.0 + e2)
    gate_ref[rows, :] = jnp.where(lane == i1, w1, 0.0) + jnp.where(lane == i2, w2, 0.0)


def _post_attn(x2d, oa, ob, norm_mix, w_gate, w_a, w_b, w_out, norm_moe, w_router, b_router):
    t = x2d.shape[0]
    tm = POST_TILE
    const = lambda i: (0, 0)
    row = lambda w: pl.BlockSpec((tm, w), lambda i: (i, 0))
    full = lambda a: pl.BlockSpec(a.shape, const, pipeline_mode=pl.Buffered(1))
    return pl.pallas_call(
        _post_attn_kernel,
        grid=(t // tm,),
        in_specs=[row(D_MODEL), row(A_WIDTH), row(B_Q_WIDTH), full(norm_mix), full(w_gate), full(w_a),
                  full(w_b), full(w_out), full(norm_moe), full(w_router), full(b_router)],
        out_specs=[row(D_MODEL), row(D_MODEL), row(LANES)],
        out_shape=[jax.ShapeDtypeStruct((t, D_MODEL), F32), jax.ShapeDtypeStruct((t, D_MODEL), BF16),
                   jax.ShapeDtypeStruct((t, LANES), F32)],
        compiler_params=pltpu.CompilerParams(
            dimension_semantics=("arbitrary",), vmem_limit_bytes=VMEM_LIMIT),
        name="post_attn",
    )(x2d, oa, ob, norm_mix, w_gate, w_a, w_b, w_out, norm_moe, w_router, b_router)


def _moe_final_kernel(x1_ref, h2_ref, gate_ref, tri_ref, w1_ref, w3_ref, w2_ref, gf_ref, o_ref,
                      xs, stage, out_stage, seg_s, slo_s, shi_s, wlo_s, whi_s):
    step = pl.program_id(1)
    first_step = (pl.program_id(0) == 0) & (step == 0)
    n_sub = x1_ref.shape[0] // MOE_SUB

    @pl.when(first_step)
    def _init():
        xs[n_sub * MOE_SUB_SLOTS:, :] = jnp.zeros((MOE_CHUNK, D_MODEL), BF16)
        stage[...] = jnp.zeros(stage.shape, BF16)

    @pl.when(step == 0)
    def _sort():
        r_i = lax.broadcasted_iota(jnp.int32, (LANES, LANES), 0)
        c_i = lax.broadcasted_iota(jnp.int32, (LANES, LANES), 1)
        upper = jnp.where(r_i < c_i, 1.0, 0.0)
        col = lax.broadcasted_iota(jnp.int32, (MOE_SUB, MOE_SUB_SLOTS), 1)
        for s in range(n_sub):
            rows = slice(s * MOE_SUB, (s + 1) * MOE_SUB)
            g = gate_ref[rows, :]
            hit = g > 0.0
            onehot = jnp.where(hit, 1.0, 0.0)
            count = jnp.sum(onehot, axis=0, keepdims=True)
            padded = jnp.ceil(count * (1.0 / SEG_ALIGN)) * SEG_ALIGN
            seg = jnp.dot(jnp.broadcast_to(padded, (8, LANES)), upper, preferred_element_type=F32,
                          precision=lax.Precision.HIGHEST)[0:1]
            seg_i = seg.astype(jnp.int32)
            padded_i = padded.astype(jnp.int32)
            for ex in range(N_EXPERTS):
                seg_s[2 * s, ex] = seg_i[0, ex] + s * MOE_SUB_SLOTS
                seg_s[2 * s + 1, ex] = padded_i[0, ex]
            rank = jnp.dot(tri_ref[...], onehot.astype(BF16), preferred_element_type=F32)
            slot = seg + rank
            s_lo = jnp.min(jnp.where(hit, slot, 1e9), axis=-1, keepdims=True)
            s_hi = jnp.max(jnp.where(hit, slot, -1.0), axis=-1, keepdims=True)
            wlo_s[rows, :] = jnp.sum(jnp.where(hit & (slot == s_lo), g, 0.0), axis=-1, keepdims=True)
            whi_s[rows, :] = jnp.sum(jnp.where(hit & (slot == s_hi) & (s_hi > s_lo), g, 0.0),
                                     axis=-1, keepdims=True)
            s_lo_i = s_lo.astype(jnp.int32)
            s_hi_i = s_hi.astype(jnp.int32)
            slo_s[rows, :] = s_lo_i
            shi_s[rows, :] = s_hi_i
            perm_t = jnp.where((col == s_lo_i) | (col == s_hi_i), 1.0, 0.0).astype(BF16)
            sorted_rows = lax.dot_general(perm_t, h2_ref[rows, :], (((0,), (0,)), ((), ())),
                                          preferred_element_type=F32)
            xs[s * MOE_SUB_SLOTS:(s + 1) * MOE_SUB_SLOTS, :] = sorted_rows.astype(BF16)

    def ffn(x, j):
        a = jnp.dot(x, w1_ref[j], preferred_element_type=F32)
        b = jnp.dot(x, w3_ref[j], preferred_element_type=F32)
        hid = (a * jax.nn.sigmoid(a)) * b
        return jnp.dot(hid.astype(BF16), w2_ref[j], preferred_element_type=F32).astype(BF16)

    bases, totals, seg_starts, seg_lens, seg_dsts = [], [], [], [], []
    base = jnp.int32(0)
    for j in range(MOE_ESTEP):
        e = step * MOE_ESTEP + j
        starts = [seg_s[2 * s, e] for s in range(n_sub)]
        lens = [seg_s[2 * s + 1, e] for s in range(n_sub)]
        dst = base
        dsts = []
        for s in range(n_sub):
            dsts.append(dst)
            dst = dst + lens[s]
        bases.append(base)
        totals.append(dst - base)
        seg_starts.append(starts)
        seg_lens.append(lens)
        seg_dsts.append(dsts)
        base = base + ((dst - base + MOE_CHUNK - 1) // MOE_CHUNK) * MOE_CHUNK

    for j in range(MOE_ESTEP):
        for s in range(n_sub):
            def gather_body(k, carry, src0=seg_starts[j][s], dst0=seg_dsts[j][s]):
                src = pl.multiple_of(src0 + k * MOE_COPY, SEG_ALIGN)
                to = pl.multiple_of(dst0 + k * MOE_COPY, SEG_ALIGN)
                stage[pl.ds(to, MOE_COPY), :] = xs[pl.ds(src, MOE_COPY), :]
                return carry

            lax.fori_loop(1, (seg_lens[j][s] + MOE_COPY - 1) // MOE_COPY, gather_body, 0)
    for j in range(MOE_ESTEP):
        for s in range(n_sub):
            stage[pl.ds(pl.multiple_of(seg_dsts[j][s], SEG_ALIGN), MOE_COPY), :] = \
                xs[pl.ds(pl.multiple_of(seg_starts[j][s], SEG_ALIGN), MOE_COPY), :]

    for j in range(MOE_ESTEP):
        def extra_body(k, carry, j=j):
            r0 = pl.multiple_of(bases[j] + k * MOE_CHUNK, MOE_CHUNK)
            out_stage[pl.ds(r0, MOE_CHUNK), :] = ffn(stage[pl.ds(r0, MOE_CHUNK), :], j)
            return carry

        lax.fori_loop(1, (totals[j] + MOE_CHUNK - 1) // MOE_CHUNK, extra_body, 0)

    firsts = [ffn(stage[pl.ds(pl.multiple_of(bases[j], MOE_CHUNK), MOE_CHUNK), :], j) for j in range(MOE_ESTEP)]
    for j in range(MOE_ESTEP):
        out_stage[pl.ds(pl.multiple_of(bases[j], MOE_CHUNK), MOE_CHUNK), :] = firsts[j]

    for j in range(MOE_ESTEP):
        for s in range(n_sub):
            def scatter_body(k, carry, src0=seg_dsts[j][s], dst0=seg_starts[j][s]):
                src = pl.multiple_of(src0 + k * SEG_ALIGN, SEG_ALIGN)
                to = pl.multiple_of(dst0 + k * SEG_ALIGN, SEG_ALIGN)
                xs[pl.ds(to, SEG_ALIGN), :] = out_stage[pl.ds(src, SEG_ALIGN), :]
                return carry

            lax.fori_loop(0, seg_lens[j][s] // SEG_ALIGN, scatter_body, 0)

    @pl.when(step == N_EXPERTS // MOE_ESTEP - 1)
    def _combine():
        col = lax.broadcasted_iota(jnp.int32, (MOE_SUB, MOE_SUB_SLOTS), 1)
        for s in range(n_sub):
            rows = slice(s * MOE_SUB, (s + 1) * MOE_SUB)
            wperm = (jnp.where(col == slo_s[rows, :], wlo_s[rows, :], 0.0)
                     + jnp.where(col == shi_s[rows, :], whi_s[rows, :], 0.0)).astype(BF16)
            moe = jnp.dot(wperm, xs[s * MOE_SUB_SLOTS:(s + 1) * MOE_SUB_SLOTS, :], preferred_element_type=F32)
            o_ref[rows, :] = _rms(x1_ref[rows, :] + moe, gf_ref[...])


def _moe_final(x1, h2, gate, w1, w3, w2, norm_final):
    t = x1.shape[0]
    tm = MOE_TILE
    n_sub = tm // MOE_SUB
    row = lambda w: pl.BlockSpec((tm, w), lambda i, e: (i, 0))
    tri = jnp.tril(jnp.ones((MOE_SUB, MOE_SUB), BF16), -1)
    return pl.pallas_call(
        _moe_final_kernel,
        grid=(t // tm, N_EXPERTS // MOE_ESTEP),
        in_specs=[row(D_MODEL), row(D_MODEL), row(LANES),
                  pl.BlockSpec((MOE_SUB, MOE_SUB), lambda i, e: (0, 0), pipeline_mode=pl.Buffered(1)),
                  pl.BlockSpec((MOE_ESTEP, D_MODEL, D_FF), lambda i, e: (e, 0, 0)),
                  pl.BlockSpec((MOE_ESTEP, D_MODEL, D_FF), lambda i, e: (e, 0, 0)),
                  pl.BlockSpec((MOE_ESTEP, D_FF, D_MODEL), lambda i, e: (e, 0, 0)),
                  pl.BlockSpec((1, D_MODEL), lambda i, e: (0, 0))],
        out_specs=row(D_MODEL),
        out_shape=jax.ShapeDtypeStruct((t, D_MODEL), F32),
        scratch_shapes=[pltpu.VMEM((n_sub * MOE_SUB_SLOTS + MOE_CHUNK, D_MODEL), BF16),
                        pltpu.VMEM((2 * tm + (MOE_ESTEP + 1) * MOE_CHUNK, D_MODEL), BF16),
                        pltpu.VMEM((2 * tm + (MOE_ESTEP + 1) * MOE_CHUNK, D_MODEL), BF16),
                        pltpu.SMEM((2 * n_sub, N_EXPERTS), jnp.int32),
                        pltpu.VMEM((tm, 1), jnp.int32), pltpu.VMEM((tm, 1), jnp.int32),
                        pltpu.VMEM((tm, 1), F32), pltpu.VMEM((tm, 1), F32)],
        compiler_params=pltpu.CompilerParams(
            dimension_semantics=("arbitrary", "arbitrary"), vmem_limit_bytes=VMEM_LIMIT),
        name="moe_final",
    )(x1, h2, gate, tri, w1, w3, w2, norm_final)


def _prep_weights(norm_mix, w_in, attn_sink, w_branch_a, w_branch_b, w_gate, w_out, norm_moe,
                  w_router_group, b_router_group, w_router_expert, b_router_expert, w1, w3, w2, norm_final):
    wi = w_in[0]
    a_w = 3 * A_WIDTH

    def group_cols(gi):
        return jnp.concatenate([wi[:, role * a_w + gi * A_WIDTH: role * a_w + (gi + 1) * A_WIDTH]
                                for role in range(3)], axis=1)

    b_heads = [g * B_REP + c for c in range(B_REP) for g in range(B_KV_HEADS)]
    bq_cols = jnp.concatenate([wi[:, 3 * a_w + h * HEAD_DIM: 3 * a_w + (h + 1) * HEAD_DIM] for h in b_heads],
                              axis=1)
    w_b_rows = jnp.concatenate([w_branch_b[0][h * HEAD_DIM:(h + 1) * HEAD_DIM] for h in b_heads], axis=0)
    w_nat = jnp.concatenate([group_cols(0), bq_cols, wi[:, 3 * a_w + B_Q_WIDTH:]], axis=1).astype(BF16)
    w_g1 = group_cols(1).astype(BF16)
    w_g2 = group_cols(2).astype(BF16)
    pad = LANES - N_EXPERTS - N_GROUPS
    w_router = jnp.concatenate([w_router_expert[0], w_router_group[0],
                                jnp.zeros((D_MODEL, pad), F32)], axis=1)
    b_router = jnp.concatenate([b_router_expert[0].reshape(-1), b_router_group[0],
                                jnp.zeros((pad,), F32)]).reshape(1, LANES)
    return dict(
        norm_mix=norm_mix[0].reshape(1, D_MODEL), w_nat=w_nat, w_g1=w_g1, w_g2=w_g2, sink=attn_sink[0],
        w_a=w_branch_a[0].astype(BF16), w_b=w_b_rows.astype(BF16), w_gate=w_gate[0].astype(BF16),
        w_out=w_out[0].astype(BF16), norm_moe=norm_moe[0].reshape(1, D_MODEL),
        w_router=w_router, b_router=b_router,
        w1=w1[0].astype(BF16), w3=w3[0].astype(BF16), w2=w2[0].astype(BF16),
        norm_final=norm_final.reshape(1, D_MODEL))


def _trunk(x, p):
    b, s, _ = x.shape
    outs = _in_proj(x, p["norm_mix"], p["w_nat"], p["w_g1"], p["w_g2"], p["cos"], p["sin"])
    oa = _mixer_a(outs[:9], s)
    ob = _mixer_b(outs[9], outs[10], outs[11], p["sink"], s)
    t = b * s
    x1, h2, gate = _post_attn(x.reshape(t, D_MODEL), oa.reshape(t, A_WIDTH), ob.reshape(t, B_Q_WIDTH),
                              p["norm_mix"], p["w_gate"], p["w_a"], p["w_b"], p["w_out"], p["norm_moe"],
                              p["w_router"], p["b_router"])
    y = _moe_final(x1, h2, gate, p["w1"], p["w3"], p["w2"], p["norm_final"])
    return y.reshape(b, s, D_MODEL)


def kernel(x_prompt, x_sample, norm_mix, w_in, attn_sink, w_branch_a, w_branch_b, w_gate, w_out, norm_moe,
           w_router_group, b_router_group, w_router_expert, b_router_expert, w1, w3, w2, norm_final):
    p = _prep_weights(norm_mix, w_in, attn_sink, w_branch_a, w_branch_b, w_gate, w_out, norm_moe,
                      w_router_group, b_router_group, w_router_expert, b_router_expert, w1, w3, w2, norm_final)
    p["cos"], p["sin"] = _rope_tables(max(x_prompt.shape[1], x_sample.shape[1]))
    return (_trunk(x_prompt, p), _trunk(x_sample, p))
```
